```python
import math
import jax
import jax.numpy as jnp
from jax import lax
import numpy as np

D_MODEL = 1024
BATCH = 8
SEQ = 4096
DEPTH = 1

D_INNER = 2 * D_MODEL
D_ATTN = D_INNER // 2
D_SSM = D_INNER - D_ATTN
SB_HEAD_DIM = 64
SB_HEADS = D_ATTN // SB_HEAD_DIM
SSM_HEAD_DIM = 64
SSM_HEADS = D_SSM // SSM_HEAD_DIM
SSM_GROUPS = 2
SSM_STATE = 128
CONV_WIDTH = 4
SSD_CHUNK = 128
Q_BLOCK = 128
NORM_EPS = 1e-6
D_XBC = D_SSM + 2 * SSM_GROUPS * SSM_STATE
D_PROJ = 4 * D_ATTN + D_XBC + SSM_HEADS + D_SSM
DT_MIN = 1e-3
DT_MAX = 1e-1

kernel_name = "hybrid_stickbreak_ssd_layer"


def rms_norm(x, gain):
    xf = x.astype(jnp.float32)
    y = xf * lax.rsqrt(jnp.mean(xf * xf, axis=-1, keepdims=True) + NORM_EPS)
    return (y * gain.astype(jnp.float32)).astype(x.dtype)


def stick_breaking_attention(q, k, v):
    seq = q.shape[1]
    scale = q.shape[-1] ** -0.5
    outs = []
    for blk in range(seq // Q_BLOCK):
        start = blk * Q_BLOCK
        end = start + Q_BLOCK
        z = jnp.einsum("bqhd,bkhd->bhqk", q[:, start:end], k[:, :end]).astype(jnp.float32) * scale
        t_idx = start + jnp.arange(Q_BLOCK)[:, None]
        s_idx = jnp.arange(end)[None, :]
        mask = s_idx < t_idx
        log_beta = jax.nn.log_sigmoid(z)
        log_one_minus = jnp.where(mask, jax.nn.log_sigmoid(-z), 0.0)
        tail = lax.cumsum(log_one_minus, axis=3, reverse=True) - log_one_minus
        weights = jnp.exp(jnp.where(mask, log_beta + tail, -jnp.inf))
        outs.append(jnp.einsum("bhqk,bkhd->bqhd", weights.astype(v.dtype), v[:, :end]))
    return jnp.concatenate(outs, axis=1)


def causal_depthwise_conv(u, w, b):
    ch = u.shape[-1]
    y = lax.conv_general_dilated(
        u, w[:, None, :].astype(u.dtype), window_strides=(1,),
        padding=[(CONV_WIDTH - 1, 0)], dimension_numbers=("NWC", "WIO", "NWC"),
        feature_group_count=ch)
    return y + b.astype(u.dtype)


def ssd_scan(xs, dt, a, b_in, c_in, d_skip):
    bsz, seq, nh, hp = xs.shape
    ng, ns = b_in.shape[-2:]
    hpg = nh // ng
    nc = seq // SSD_CHUNK
    cl = SSD_CHUNK
    xf = xs.astype(jnp.float32)
    xdt = (xf * dt[..., None]).reshape(bsz, nc, cl, ng, hpg, hp)
    bc = b_in.astype(jnp.float32).reshape(bsz, nc, cl, ng, ns)
    cc = c_in.astype(jnp.float32).reshape(bsz, nc, cl, ng, ns)
    log_decay = (dt * a).reshape(bsz, nc, cl, ng, hpg).transpose(0, 1, 3, 4, 2)
    a_cum = jnp.cumsum(log_decay, axis=-1)
    causal = jnp.tril(jnp.ones((cl, cl), dtype=bool))
    seg = a_cum[..., :, None] - a_cum[..., None, :]
    decay = jnp.exp(jnp.where(causal, seg, -jnp.inf))
    cb = jnp.einsum("bclgn,bcsgn->bcgls", cc, bc)
    y_diag = jnp.einsum("bcgrls,bcsgrp->bclgrp", cb[:, :, :, None] * decay, xdt)
    decay_to_end = jnp.exp(a_cum[..., -1:] - a_cum)
    chunk_states = jnp.einsum("bclgn,bcgrl,bclgrp->bcgrpn", bc, decay_to_end, xdt)
    chunk_decay = jnp.exp(a_cum[..., -1])

    def step(h, inp):
        st, dec = inp
        return h * dec[..., None, None] + st, h

    h0 = jnp.zeros((bsz, ng, hpg, hp, ns), jnp.float32)
    _, prev = lax.scan(step, h0, (jnp.moveaxis(chunk_states, 1, 0), jnp.moveaxis(chunk_decay, 1, 0)))
    prev = jnp.moveaxis(prev, 0, 1)
    y_off = jnp.einsum("bclgn,bcgrpn,bcgrl->bclgrp", cc, prev, jnp.exp(a_cum))
    y = (y_diag + y_off).reshape(bsz, seq, nh, hp)
    return y + xf * d_skip.astype(jnp.float32)[:, None]


def _fwd_setup_inputs(seed: int = 0) -> dict:
    key = jax.random.key(seed)
    ks = jax.random.split(key, 16)
    f = jnp.float32
    nrm = jax.random.normal
    x = nrm(ks[0], (BATCH, SEQ, D_MODEL), f)
    c = nrm(ks[1], (BATCH, D_MODEL), f)
    w_ada = nrm(ks[2], (DEPTH, D_MODEL, 3 * D_MODEL), f) * D_MODEL ** -0.5
    b_ada = 0.02 * nrm(ks[3], (DEPTH, 3 * D_MODEL), f)
    norm_in_gain = 1.0 + 0.1 * nrm(ks[4], (DEPTH, D_MODEL), f)
    w_in = nrm(ks[5], (DEPTH, D_MODEL, D_PROJ), f) * D_MODEL ** -0.5
    conv_w = nrm(ks[6], (DEPTH, CONV_WIDTH, D_XBC), f) * CONV_WIDTH ** -0.5
    conv_b = 0.02 * nrm(ks[7], (DEPTH, D_XBC), f)
    dt0 = jnp.exp(jax.random.uniform(ks[8], (DEPTH, SSM_HEADS), f, math.log(DT_MIN), math.log(DT_MAX)))
    dt_bias = dt0 + jnp.log(-jnp.expm1(-dt0))
    a_log = jnp.log(jax.random.uniform(ks[9], (DEPTH, SSM_HEADS), f, 1.0, 16.0))
    d_skip = 1.0 + 0.1 * nrm(ks[10], (DEPTH, SSM_HEADS), f)
    sb_norm_gain = 1.0 + 0.1 * nrm(ks[11], (DEPTH, D_ATTN), f)
    ssm_norm_gain = 1.0 + 0.1 * nrm(ks[12], (DEPTH, D_SSM), f)
    w_out = nrm(ks[13], (DEPTH, D_INNER, D_MODEL), f) * D_INNER ** -0.5
    norm_f_gain = 1.0 + 0.1 * nrm(ks[14], (D_MODEL,), f)
    return {"x": x, "c": c, "w_ada": w_ada, "b_ada": b_ada, "norm_in_gain": norm_in_gain,
            "w_in": w_in, "conv_w": conv_w, "conv_b": conv_b, "dt_bias": dt_bias,
            "a_log": a_log, "d_skip": d_skip, "sb_norm_gain": sb_norm_gain,
            "ssm_norm_gain": ssm_norm_gain, "w_out": w_out, "norm_f_gain": norm_f_gain}


def _fwd_reference(x, c, w_ada, b_ada, norm_in_gain, w_in, conv_w, conv_b, dt_bias, a_log,
              d_skip, sb_norm_gain, ssm_norm_gain, w_out, norm_f_gain):
    bsz, seq, _ = x.shape
    splits = [D_ATTN, 2 * D_ATTN, 3 * D_ATTN, 4 * D_ATTN, 4 * D_ATTN + D_XBC,
              4 * D_ATTN + D_XBC + SSM_HEADS]
    c_act = jax.nn.silu(c)
    for layer in range(DEPTH):
        mod = c_act @ w_ada[layer] + b_ada[layer]
        shift, scale, gate = jnp.split(mod, 3, axis=-1)
        h = rms_norm(x, norm_in_gain[layer]) * (1.0 + scale[:, None, :]) + shift[:, None, :]

        proj = h @ w_in[layer]
        q, k, v, z_attn, xbc, dt_raw, z_ssm = jnp.split(proj, splits, axis=-1)

        o = stick_breaking_attention(
            q.reshape(bsz, seq, SB_HEADS, SB_HEAD_DIM),
            k.reshape(bsz, seq, SB_HEADS, SB_HEAD_DIM),
            v.reshape(bsz, seq, SB_HEADS, SB_HEAD_DIM)).reshape(bsz, seq, D_ATTN)
        y_attn = rms_norm(o, sb_norm_gain[layer]) * jax.nn.silu(z_attn)

        xbc = jax.nn.silu(causal_depthwise_conv(xbc, conv_w[layer], conv_b[layer]))
        xs, b_ssm, c_ssm = jnp.split(xbc, [D_SSM, D_SSM + SSM_GROUPS * SSM_STATE], axis=-1)
        dt = jax.nn.softplus((dt_raw + dt_bias[layer]).astype(jnp.float32))
        a = -jnp.exp(a_log[layer].astype(jnp.float32))
        y = ssd_scan(xs.reshape(bsz, seq, SSM_HEADS, SSM_HEAD_DIM), dt, a,
                     b_ssm.reshape(bsz, seq, SSM_GROUPS, SSM_STATE),
                     c_ssm.reshape(bsz, seq, SSM_GROUPS, SSM_STATE), d_skip[layer])
        y = y.reshape(bsz, seq, D_SSM).astype(x.dtype)
        y_ssm = rms_norm(y * jax.nn.silu(z_ssm), ssm_norm_gain[layer])

        mixed = jnp.concatenate([y_attn, y_ssm], axis=-1) @ w_out[layer]
        x = x + gate[:, None, :] * mixed
    return rms_norm(x, norm_f_gain)


import jax as _jax
import jax.numpy as _jnp

TWIN_FORMAT = 'train_step'
FWD_PARAMS = ['x', 'c', 'w_ada', 'b_ada', 'norm_in_gain', 'w_in', 'conv_w', 'conv_b', 'dt_bias', 'a_log', 'd_skip', 'sb_norm_gain', 'ssm_norm_gain', 'w_out', 'norm_f_gain']
TWIN_WEIGHTS = ['w_ada', 'b_ada', 'norm_in_gain', 'w_in', 'conv_w', 'conv_b', 'dt_bias', 'a_log', 'd_skip', 'sb_norm_gain', 'ssm_norm_gain', 'w_out', 'norm_f_gain']
TWIN_DIFF_INPUT = 'x'
TWIN_INPUTS = ['x', 'c', 'w_ada', 'b_ada', 'norm_in_gain', 'w_in', 'conv_w', 'conv_b', 'dt_bias', 'a_log', 'd_skip', 'sb_norm_gain', 'ssm_norm_gain', 'w_out', 'norm_f_gain', 'loss_target', 'm_w_ada', 'm_b_ada', 'm_norm_in_gain', 'm_w_in', 'm_conv_w', 'm_conv_b', 'm_dt_bias', 'm_a_log', 'm_d_skip', 'm_sb_norm_gain', 'm_ssm_norm_gain', 'm_w_out', 'm_norm_f_gain', 'v_w_ada', 'v_b_ada', 'v_norm_in_gain', 'v_w_in', 'v_conv_w', 'v_conv_b', 'v_dt_bias', 'v_a_log', 'v_d_skip', 'v_sb_norm_gain', 'v_ssm_norm_gain', 'v_w_out', 'v_norm_f_gain']
TWIN_OUTPUTS = ['loss', 'grad_x', 'grad_w_ada', 'grad_b_ada', 'grad_norm_in_gain', 'grad_w_in', 'grad_conv_w', 'grad_conv_b', 'grad_dt_bias', 'grad_a_log', 'grad_d_skip', 'grad_sb_norm_gain', 'grad_ssm_norm_gain', 'grad_w_out', 'grad_norm_f_gain', 'delta_w_ada', 'delta_b_ada', 'delta_norm_in_gain', 'delta_w_in', 'delta_conv_w', 'delta_conv_b', 'delta_dt_bias', 'delta_a_log', 'delta_d_skip', 'delta_sb_norm_gain', 'delta_ssm_norm_gain', 'delta_w_out', 'delta_norm_f_gain', 'new_m_w_ada', 'new_m_b_ada', 'new_m_norm_in_gain', 'new_m_w_in', 'new_m_conv_w', 'new_m_conv_b', 'new_m_dt_bias', 'new_m_a_log', 'new_m_d_skip', 'new_m_sb_norm_gain', 'new_m_ssm_norm_gain', 'new_m_w_out', 'new_m_norm_f_gain', 'new_v_w_ada', 'new_v_b_ada', 'new_v_norm_in_gain', 'new_v_w_in', 'new_v_conv_w', 'new_v_conv_b', 'new_v_dt_bias', 'new_v_a_log', 'new_v_d_skip', 'new_v_sb_norm_gain', 'new_v_ssm_norm_gain', 'new_v_w_out', 'new_v_norm_f_gain']
TWIN_LEAF_KINDS = {'loss': 'loss', 'grad_x': 'grad_x', 'grad_w_ada': 'grad_w', 'grad_b_ada': 'grad_w', 'grad_norm_in_gain': 'grad_w', 'grad_w_in': 'grad_w', 'grad_conv_w': 'grad_w', 'grad_conv_b': 'grad_w', 'grad_dt_bias': 'grad_w', 'grad_a_log': 'grad_w', 'grad_d_skip': 'grad_w', 'grad_sb_norm_gain': 'grad_w', 'grad_ssm_norm_gain': 'grad_w', 'grad_w_out': 'grad_w', 'grad_norm_f_gain': 'grad_w', 'delta_w_ada': 'delta_w', 'delta_b_ada': 'delta_w', 'delta_norm_in_gain': 'delta_w', 'delta_w_in': 'delta_w', 'delta_conv_w': 'delta_w', 'delta_conv_b': 'delta_w', 'delta_dt_bias': 'delta_w', 'delta_a_log': 'delta_w', 'delta_d_skip': 'delta_w', 'delta_sb_norm_gain': 'delta_w', 'delta_ssm_norm_gain': 'delta_w', 'delta_w_out': 'delta_w', 'delta_norm_f_gain': 'delta_w', 'new_m_w_ada': 'new_m', 'new_m_b_ada': 'new_m', 'new_m_norm_in_gain': 'new_m', 'new_m_w_in': 'new_m', 'new_m_conv_w': 'new_m', 'new_m_conv_b': 'new_m', 'new_m_dt_bias': 'new_m', 'new_m_a_log': 'new_m', 'new_m_d_skip': 'new_m', 'new_m_sb_norm_gain': 'new_m', 'new_m_ssm_norm_gain': 'new_m', 'new_m_w_out': 'new_m', 'new_m_norm_f_gain': 'new_m', 'new_v_w_ada': 'new_v', 'new_v_b_ada': 'new_v', 'new_v_norm_in_gain': 'new_v', 'new_v_w_in': 'new_v', 'new_v_conv_w': 'new_v', 'new_v_conv_b': 'new_v', 'new_v_dt_bias': 'new_v', 'new_v_a_log': 'new_v', 'new_v_d_skip': 'new_v', 'new_v_sb_norm_gain': 'new_v', 'new_v_ssm_norm_gain': 'new_v', 'new_v_w_out': 'new_v', 'new_v_norm_f_gain': 'new_v'}


def _forward(args):
    return _fwd_reference(*[args[k] for k in FWD_PARAMS])


def _output_shape():
    out = _jax.eval_shape(lambda: _forward(_fwd_setup_inputs(0)))
    return out.shape, out.dtype

N_MICROBATCH = 1
ADAM_LR = 0.001
ADAM_B1 = 0.9
ADAM_B2 = 0.999
ADAM_EPS = 1e-08
ADAM_WD = 0.01
ADAM_STEP = 10
PER_EXAMPLE_BATCH_AXIS = {'x': 0, 'c': 0, 'loss_target': 0}
SHARED_INPUTS = []
_WEIGHT_DTYPES = {'w_ada': _jnp.float32, 'b_ada': _jnp.float32, 'norm_in_gain': _jnp.float32, 'w_in': _jnp.float32, 'conv_w': _jnp.float32, 'conv_b': _jnp.float32, 'dt_bias': _jnp.float32, 'a_log': _jnp.float32, 'd_skip': _jnp.float32, 'sb_norm_gain': _jnp.float32, 'ssm_norm_gain': _jnp.float32, 'w_out': _jnp.float32, 'norm_f_gain': _jnp.float32}
MOMENT_SCALE = {'w_ada': 3.709873e-01, 'b_ada': 6.968652e-01, 'norm_in_gain': 1.394317e-01, 'w_in': 7.031879e-02, 'conv_w': 1.087871e-01, 'conv_b': 1.268602e-01, 'dt_bias': 3.347294e-01, 'a_log': 8.540482e-01, 'd_skip': 2.946604e-01, 'sb_norm_gain': 7.686524e-02, 'ssm_norm_gain': 1.393635e-01, 'w_out': 1.609077e-01, 'norm_f_gain': 3.268178e+01}


def _to_microbatches(a, axis):
    t = _jnp.moveaxis(a, axis, 0)
    t = t.reshape((N_MICROBATCH, t.shape[0] // N_MICROBATCH) + t.shape[1:])
    return _jnp.moveaxis(t, 1, axis + 1)


def setup_inputs(seed: int = 0) -> dict:
    inp = _fwd_setup_inputs(seed)
    key = _jax.random.fold_in(_jax.random.key(seed), 7919)
    shape, _ = _output_shape()
    out = dict(inp)
    out["loss_target"] = _jax.random.normal(_jax.random.fold_in(key, 0), shape, _jnp.float32)
    for i, name in enumerate(TWIN_WEIGHTS):
        w = inp[name].astype(_jnp.float32)
        if MOMENT_SCALE is None:
            s = _jnp.sqrt(_jnp.mean(_jnp.square(w)) + 1e-30)
        else:
            s = MOMENT_SCALE[name]
        km, kv = _jax.random.split(_jax.random.fold_in(key, i + 1))
        out[name] = w
        out["m_" + name] = s * _jax.random.normal(km, w.shape, _jnp.float32)
        out["v_" + name] = (s * s) * _jax.random.uniform(kv, w.shape, _jnp.float32, 0.5, 1.5)
    if N_MICROBATCH > 1:
        for name, axis in PER_EXAMPLE_BATCH_AXIS.items():
            out[name] = _to_microbatches(out[name], axis)
    return {'x': out['x'], 'c': out['c'], 'w_ada': out['w_ada'], 'b_ada': out['b_ada'], 'norm_in_gain': out['norm_in_gain'], 'w_in': out['w_in'], 'conv_w': out['conv_w'], 'conv_b': out['conv_b'], 'dt_bias': out['dt_bias'], 'a_log': out['a_log'], 'd_skip': out['d_skip'], 'sb_norm_gain': out['sb_norm_gain'], 'ssm_norm_gain': out['ssm_norm_gain'], 'w_out': out['w_out'], 'norm_f_gain': out['norm_f_gain'], 'loss_target': out['loss_target'], 'm_w_ada': out['m_w_ada'], 'm_b_ada': out['m_b_ada'], 'm_norm_in_gain': out['m_norm_in_gain'], 'm_w_in': out['m_w_in'], 'm_conv_w': out['m_conv_w'], 'm_conv_b': out['m_conv_b'], 'm_dt_bias': out['m_dt_bias'], 'm_a_log': out['m_a_log'], 'm_d_skip': out['m_d_skip'], 'm_sb_norm_gain': out['m_sb_norm_gain'], 'm_ssm_norm_gain': out['m_ssm_norm_gain'], 'm_w_out': out['m_w_out'], 'm_norm_f_gain': out['m_norm_f_gain'], 'v_w_ada': out['v_w_ada'], 'v_b_ada': out['v_b_ada'], 'v_norm_in_gain': out['v_norm_in_gain'], 'v_w_in': out['v_w_in'], 'v_conv_w': out['v_conv_w'], 'v_conv_b': out['v_conv_b'], 'v_dt_bias': out['v_dt_bias'], 'v_a_log': out['v_a_log'], 'v_d_skip': out['v_d_skip'], 'v_sb_norm_gain': out['v_sb_norm_gain'], 'v_ssm_norm_gain': out['v_ssm_norm_gain'], 'v_w_out': out['v_w_out'], 'v_norm_f_gain': out['v_norm_f_gain']}


def _loss(weights, diff, rest, loss_target):
    with _jax.named_scope("forward"):
        args = {**rest, TWIN_DIFF_INPUT: diff, **{k: w.astype(_WEIGHT_DTYPES[k]) for k, w in weights.items()}}
        y = _forward(args)
    with _jax.named_scope("loss_head"):
        err = _jnp.square(y.astype(_jnp.float32) - loss_target)
        return 0.5 * _jnp.sum(_jnp.mean(err, axis=-1)) if err.ndim else 0.5 * err


def _adamw(w, g, m, v):
    m = ADAM_B1 * m + (1.0 - ADAM_B1) * g
    v = ADAM_B2 * v + (1.0 - ADAM_B2) * _jnp.square(g)
    m_hat = m / (1.0 - ADAM_B1 ** ADAM_STEP)
    v_hat = v / (1.0 - ADAM_B2 ** ADAM_STEP)
    delta = -ADAM_LR * (m_hat / (_jnp.sqrt(v_hat) + ADAM_EPS) + ADAM_WD * w)
    return delta, m, v


def reference(x, c, w_ada, b_ada, norm_in_gain, w_in, conv_w, conv_b, dt_bias, a_log, d_skip, sb_norm_gain, ssm_norm_gain, w_out, norm_f_gain, loss_target, m_w_ada, m_b_ada, m_norm_in_gain, m_w_in, m_conv_w, m_conv_b, m_dt_bias, m_a_log, m_d_skip, m_sb_norm_gain, m_ssm_norm_gain, m_w_out, m_norm_f_gain, v_w_ada, v_b_ada, v_norm_in_gain, v_w_in, v_conv_w, v_conv_b, v_dt_bias, v_a_log, v_d_skip, v_sb_norm_gain, v_ssm_norm_gain, v_w_out, v_norm_f_gain):
    given = dict(x=x, c=c, w_ada=w_ada, b_ada=b_ada, norm_in_gain=norm_in_gain, w_in=w_in, conv_w=conv_w, conv_b=conv_b, dt_bias=dt_bias, a_log=a_log, d_skip=d_skip, sb_norm_gain=sb_norm_gain, ssm_norm_gain=ssm_norm_gain, w_out=w_out, norm_f_gain=norm_f_gain, loss_target=loss_target, m_w_ada=m_w_ada, m_b_ada=m_b_ada, m_norm_in_gain=m_norm_in_gain, m_w_in=m_w_in, m_conv_w=m_conv_w, m_conv_b=m_conv_b, m_dt_bias=m_dt_bias, m_a_log=m_a_log, m_d_skip=m_d_skip, m_sb_norm_gain=m_sb_norm_gain, m_ssm_norm_gain=m_ssm_norm_gain, m_w_out=m_w_out, m_norm_f_gain=m_norm_f_gain, v_w_ada=v_w_ada, v_b_ada=v_b_ada, v_norm_in_gain=v_norm_in_gain, v_w_in=v_w_in, v_conv_w=v_conv_w, v_conv_b=v_conv_b, v_dt_bias=v_dt_bias, v_a_log=v_a_log, v_d_skip=v_d_skip, v_sb_norm_gain=v_sb_norm_gain, v_ssm_norm_gain=v_ssm_norm_gain, v_w_out=v_w_out, v_norm_f_gain=v_norm_f_gain)
    weights = {n: given[n] for n in TWIN_WEIGHTS}
    shared = {n: given[n] for n in SHARED_INPUTS}
    per_example = {n: given[n] for n in ['x', 'c']}
    grad_fn = _jax.value_and_grad(_loss, argnums=(0, 1))

    def one_microbatch(ex, loss_target):
        ex = dict(ex)
        diff = ex.pop(TWIN_DIFF_INPUT)
        return grad_fn(weights, diff, {**shared, **ex}, loss_target)

    if N_MICROBATCH == 1:
        loss, (grad_w, grad_x) = one_microbatch(per_example, given["loss_target"])
    else:
        def body(carry, xs):
            loss_sum, grad_sum = carry
            l_k, (gw_k, gx_k) = one_microbatch(xs[0], xs[1])
            with _jax.named_scope("update"):
                return (loss_sum + l_k, _jax.tree.map(_jnp.add, grad_sum, gw_k)), gx_k

        init = (_jnp.zeros((), _jnp.float32), _jax.tree.map(_jnp.zeros_like, weights))
        (loss, grad_w), grad_x = _jax.lax.scan(body, init, (per_example, given["loss_target"]))
    with _jax.named_scope("update"):
        delta_w, new_m, new_v = {}, {}, {}
        for n in TWIN_WEIGHTS:
            delta_w[n], new_m[n], new_v[n] = _adamw(weights[n], grad_w[n], given["m_" + n], given["v_" + n])
    return (loss, grad_x, *[grad_w[n] for n in TWIN_WEIGHTS], *[delta_w[n] for n in TWIN_WEIGHTS],
            *[new_m[n] for n in TWIN_WEIGHTS], *[new_v[n] for n in TWIN_WEIGHTS])
```

```python
import functools

import jax
import jax.numpy as jnp
from jax import lax
from jax.experimental import pallas as pl
from jax.experimental.pallas import tpu as pltpu

f32 = jnp.float32
bf16 = jnp.bfloat16
MESH = pl.DeviceIdType.MESH
HI = lax.Precision.HIGHEST

N_DEV = 8
D_MODEL = 1024
D_ATTN = 1024
D_SSM = 1024
N_HEADS = 16
HEAD_DIM = 64
N_GROUPS = 2
HEADS_PER_GROUP = 8
N_STATE = 128
D_XBC = D_SSM + 2 * N_GROUPS * N_STATE
D_PROJ = 4 * D_ATTN + D_XBC + N_HEADS + D_SSM
W_IN_SHARD = D_PROJ // N_DEV
CONV_K = 4
CHUNK = 128
LANES = 128
EPS = 1e-6
OFF_ZA = 3072
OFF_ZS = 4096
OFF_XBC = 5120
OFF_DT = 6656
D_PROJ_P = 6784
VMEM_LIMIT_BYTES = 56 * 1024 * 1024

ADAM_LR = 0.001
ADAM_B1 = 0.9
ADAM_B2 = 0.999
ADAM_EPS = 1e-08
ADAM_WD = 0.01
ADAM_STEP = 10

P_LOSS = 0
P_DMOD = 128
P_GIN = 3200
P_CB = 4224
P_DTB = 5760
P_ALOG = 5888
P_DSK = 6016
P_GSB = 6144
P_GSS = 7168
P_GNF = 8192
P_CW = 9216
N_PACK = 15360


def _params(sem=None):
    return pltpu.CompilerParams(dimension_semantics=sem, vmem_limit_bytes=VMEM_LIMIT_BYTES)


def _sigmoid(v):
    return 1.0 / (1.0 + jnp.exp(-v))


def _softplus(v):
    return jnp.maximum(v, 0.0) + jnp.log(1.0 + jnp.exp(-jnp.abs(v)))


def _nt(a, b, precision=None):
    return lax.dot_general(a, b, (((1,), (1,)), ((), ())), preferred_element_type=f32, precision=precision)


def _tn(a, b, precision=None):
    return lax.dot_general(a, b, (((0,), (0,)), ((), ())), preferred_element_type=f32, precision=precision)


def _nn(a, b, precision=None):
    return lax.dot_general(a, b, (((1,), (0,)), ((), ())), preferred_element_type=f32, precision=precision)


def _me():
    x, y, c = lax.axis_index("x"), lax.axis_index("y"), lax.axis_index("c")
    return (x, y, c), 4 * x + 2 * y + c


def _peer(k):
    x, y, c = lax.axis_index("x"), lax.axis_index("y"), lax.axis_index("c")
    px = 1 - x if (k >> 2) & 1 else x
    py = 1 - y if (k >> 1) & 1 else y
    pc = 1 - c if k & 1 else c
    return (px, py, pc), 4 * px + 2 * py + pc


def _all_gather(arrs, name):
    n = len(arrs)

    def body(*refs):
        ins, outs = refs[:n], refs[n:2 * n]
        send_sems, recv_sems, local_sems = refs[2 * n:]
        _, my_slot = _me()
        sends = []
        locals_ = []
        for a in range(n):
            loc = pltpu.make_async_copy(ins[a], outs[a].at[my_slot], local_sems.at[a])
            loc.start()
            locals_.append(loc)
            for k in range(1, N_DEV):
                peer, _ = _peer(k)
                cp = pltpu.make_async_remote_copy(src_ref=ins[a], dst_ref=outs[a].at[my_slot], send_sem=send_sems.at[a, k - 1],
                                                  recv_sem=recv_sems.at[a, k - 1], device_id=peer, device_id_type=MESH)
                cp.start()
                sends.append(cp)
        for a in range(n):
            for k in range(1, N_DEV):
                peer, peer_slot = _peer(k)
                pltpu.make_async_remote_copy(src_ref=ins[a], dst_ref=outs[a].at[peer_slot], send_sem=send_sems.at[a, k - 1],
                                             recv_sem=recv_sems.at[a, k - 1], device_id=peer, device_id_type=MESH).wait_recv()
        for cp in sends:
            cp.wait_send()
        for loc in locals_:
            loc.wait()

    any_spec = pl.BlockSpec(memory_space=pl.ANY)
    return pl.pallas_call(
        body, name=name,
        out_shape=tuple(jax.ShapeDtypeStruct((N_DEV,) + a.shape, a.dtype) for a in arrs),
        in_specs=[any_spec] * n, out_specs=tuple([any_spec] * n),
        scratch_shapes=[pltpu.SemaphoreType.DMA((n, N_DEV - 1)), pltpu.SemaphoreType.DMA((n, N_DEV - 1)),
                        pltpu.SemaphoreType.DMA((n,))],
    )(*arrs)


def _all_to_all(arrs, name):
    n = len(arrs)

    def body(*refs):
        ins, outs = refs[:n], refs[n:2 * n]
        send_sems, recv_sems, local_sems = refs[2 * n:]
        _, my_slot = _me()
        sends = []
        locals_ = []
        for a in range(n):
            loc = pltpu.make_async_copy(ins[a].at[my_slot], outs[a].at[my_slot], local_sems.at[a])
            loc.start()
            locals_.append(loc)
            for k in range(1, N_DEV):
                peer, peer_slot = _peer(k)
                cp = pltpu.make_async_remote_copy(src_ref=ins[a].at[peer_slot], dst_ref=outs[a].at[my_slot],
                                                  send_sem=send_sems.at[a, k - 1], recv_sem=recv_sems.at[a, k - 1],
                                                  device_id=peer, device_id_type=MESH)
                cp.start()
                sends.append(cp)
        for a in range(n):
            for k in range(1, N_DEV):
                peer, peer_slot = _peer(k)
                pltpu.make_async_remote_copy(src_ref=ins[a].at[peer_slot], dst_ref=outs[a].at[peer_slot],
                                             send_sem=send_sems.at[a, k - 1], recv_sem=recv_sems.at[a, k - 1],
                                             device_id=peer, device_id_type=MESH).wait_recv()
        for cp in sends:
            cp.wait_send()
        for loc in locals_:
            loc.wait()

    any_spec = pl.BlockSpec(memory_space=pl.ANY)
    return pl.pallas_call(
        body, name=name,
        out_shape=tuple(jax.ShapeDtypeStruct(a.shape, a.dtype) for a in arrs),
        in_specs=[any_spec] * n, out_specs=tuple([any_spec] * n),
        scratch_shapes=[pltpu.SemaphoreType.DMA((n, N_DEV - 1)), pltpu.SemaphoreType.DMA((n, N_DEV - 1)),
                        pltpu.SemaphoreType.DMA((n,))],
    )(*arrs)


def _mod_exchange(c_row, w_ada, b_ada):
    n_col = w_ada.shape[1]

    def body(c_ref, w_ref, b_ref, mod_ref, call_ref, part, modp, send_sems, recv_sems):
        _, my_slot = _me()
        call_ref[my_slot] = c_ref[...]
        sends = []
        for k in range(1, N_DEV):
            peer, _ = _peer(k)
            cp = pltpu.make_async_remote_copy(src_ref=c_ref, dst_ref=call_ref.at[my_slot], send_sem=send_sems.at[0, k - 1],
                                              recv_sem=recv_sems.at[0, k - 1], device_id=peer, device_id_type=MESH)
            cp.start()
            sends.append(cp)
        for k in range(1, N_DEV):
            peer, peer_slot = _peer(k)
            pltpu.make_async_remote_copy(src_ref=c_ref, dst_ref=call_ref.at[peer_slot], send_sem=send_sems.at[0, k - 1],
                                         recv_sem=recv_sems.at[0, k - 1], device_id=peer, device_id_type=MESH).wait_recv()
        for cp in sends:
            cp.wait_send()
        w = w_ref[...]
        for b in range(N_DEV):
            cb = call_ref[b]
            part[b] = _nn(cb * _sigmoid(cb), w, HI)
        modp[my_slot] = part[my_slot]
        sends = []
        for k in range(1, N_DEV):
            peer, peer_slot = _peer(k)
            cp = pltpu.make_async_remote_copy(src_ref=part.at[peer_slot], dst_ref=modp.at[my_slot], send_sem=send_sems.at[1, k - 1],
                                              recv_sem=recv_sems.at[1, k - 1], device_id=peer, device_id_type=MESH)
            cp.start()
            sends.append(cp)
        for k in range(1, N_DEV):
            peer, peer_slot = _peer(k)
            pltpu.make_async_remote_copy(src_ref=part.at[peer_slot], dst_ref=modp.at[peer_slot], send_sem=send_sems.at[1, k - 1],
                                         recv_sem=recv_sems.at[1, k - 1], device_id=peer, device_id_type=MESH).wait_recv()
        for cp in sends:
            cp.wait_send()
        for j in range(N_DEV):
            mod_ref[:, j * n_col:(j + 1) * n_col] = modp[j] + b_ref[:, j * n_col:(j + 1) * n_col]

    vmem = pl.BlockSpec(memory_space=pltpu.VMEM)
    return pl.pallas_call(
        body, name="mod_exchange",
        out_shape=(jax.ShapeDtypeStruct((1, N_DEV * n_col), f32), jax.ShapeDtypeStruct((N_DEV, 1, D_MODEL), f32)),
        in_specs=[vmem, vmem, vmem], out_specs=(vmem, vmem),
        scratch_shapes=[pltpu.VMEM((N_DEV, 1, n_col), f32), pltpu.VMEM((N_DEV, 1, n_col), f32),
                        pltpu.SemaphoreType.DMA((2, N_DEV - 1)), pltpu.SemaphoreType.DMA((2, N_DEV - 1))],
        compiler_params=_params(),
    )(c_row, w_ada, b_ada)


def _cast_bf16(a, rows):
    r, c = a.shape

    def body(a_ref, o_ref):
        o_ref[...] = a_ref[...].astype(bf16)

    return pl.pallas_call(
        body, name="cast_bf16", grid=(r // rows,),
        in_specs=[pl.BlockSpec((rows, c), lambda i: (i, 0))], out_specs=pl.BlockSpec((rows, c), lambda i: (i, 0)),
        out_shape=jax.ShapeDtypeStruct((r, c), bf16), compiler_params=_params(("parallel",)),
    )(a)


def _proj(x, gain, scale, shift, wp, seq):
    ts = 256

    def body(x_ref, g_ref, sc_ref, sh_ref, w_ref, h_ref, qkv_ref, za_ref, zs_ref, xbc_ref, dt_ref):
        xv = x_ref[...]
        r = lax.rsqrt(jnp.mean(xv * xv, axis=-1, keepdims=True) + EPS)
        hb = ((xv * r * g_ref[...]) * (1.0 + sc_ref[...]) + sh_ref[...]).astype(bf16)
        h_ref[...] = hb
        for cb in range(3 * D_ATTN // 256):
            res = jnp.dot(hb, w_ref[:, cb * 256:(cb + 1) * 256], preferred_element_type=f32)
            for u in range(4):
                qkv_ref[cb * 4 + u] = res[:, u * HEAD_DIM:(u + 1) * HEAD_DIM].astype(bf16)
        for out_ref, off, width in ((za_ref, OFF_ZA, D_ATTN), (zs_ref, OFF_ZS, D_SSM), (xbc_ref, OFF_XBC, D_XBC), (dt_ref, OFF_DT, LANES)):
            for cc in range(0, width, 512):
                wd = min(512, width - cc)
                out_ref[:, cc:cc + wd] = jnp.dot(hb, w_ref[:, off + cc:off + cc + wd], preferred_element_type=f32)

    row = lambda i: (i, 0)
    fixed = lambda i: (0, 0)
    return pl.pallas_call(
        body, name="proj", grid=(seq // ts,),
        in_specs=[pl.BlockSpec((ts, D_MODEL), row), pl.BlockSpec((1, D_MODEL), fixed), pl.BlockSpec((1, D_MODEL), fixed),
                  pl.BlockSpec((1, D_MODEL), fixed), pl.BlockSpec((D_MODEL, D_PROJ_P), fixed)],
        out_specs=(pl.BlockSpec((ts, D_MODEL), row), pl.BlockSpec((3 * N_HEADS, ts, HEAD_DIM), lambda i: (0, i, 0)),
                   pl.BlockSpec((ts, D_ATTN), row), pl.BlockSpec((ts, D_SSM), row), pl.BlockSpec((ts, D_XBC), row),
                   pl.BlockSpec((ts, LANES), row)),
        out_shape=(jax.ShapeDtypeStruct((seq, D_MODEL), bf16), jax.ShapeDtypeStruct((3 * N_HEADS, seq, HEAD_DIM), bf16),
                   jax.ShapeDtypeStruct((seq, D_ATTN), f32), jax.ShapeDtypeStruct((seq, D_SSM), f32),
                   jax.ShapeDtypeStruct((seq, D_XBC), f32), jax.ShapeDtypeStruct((seq, LANES), f32)),
        compiler_params=_params(("arbitrary",)),
    )(x, gain, scale, shift, wp)


def _log_sigmoids(z):
    l1p = jnp.log(1.0 + jnp.exp(-jnp.abs(z)))
    return jnp.minimum(z, 0.0) - l1p, -jnp.maximum(z, 0.0) - l1p


def _split_bf16(v):
    hi = v.astype(bf16)
    return hi, (v - hi.astype(f32)).astype(bf16)


def _attn_fwd(qkv, seq):
    t = CHUNK

    def body(q_ref, k_ref, v_ref, o_ref):
        i = pl.program_id(1)
        q = q_ref[0] * 0.125
        row = lax.broadcasted_iota(jnp.int32, (t, t), 0)
        col = lax.broadcasted_iota(jnp.int32, (t, t), 1)
        upper = (row > col).astype(bf16)

        def tile(j, carry, masked):
            acc, run = carry
            start = pl.multiple_of(j * t, t)
            k = k_ref[0, pl.ds(start, t), :]
            v = v_ref[0, pl.ds(start, t), :]
            z = _nt(q, k)
            lb, lom = _log_sigmoids(z)
            if masked:
                keep = col < row
                lom = jnp.where(keep, lom, 0.0)
            hi, lo = _split_bf16(lom)
            tail = jnp.dot(hi, upper, preferred_element_type=f32) + jnp.dot(lo, upper, preferred_element_type=f32)
            a = lb + tail + run
            if masked:
                a = jnp.where(keep, a, -jnp.inf)
            w = jnp.exp(a)
            acc = acc + jnp.dot(w.astype(bf16), v, preferred_element_type=f32)
            run = run + tail[:, 0:1] + lom[:, 0:1]
            return acc, run

        carry = tile(i, (jnp.zeros((t, HEAD_DIM), f32), jnp.zeros((t, 1), f32)), True)
        acc, run = lax.fori_loop(0, i, lambda n, cr: tile(i - 1 - n, cr, False), carry)
        o_ref[0] = jnp.concatenate([acc, jnp.broadcast_to(run, (t, HEAD_DIM))], axis=1)

    return pl.pallas_call(
        body, name="attn_fwd", grid=(N_HEADS, seq // t),
        in_specs=[pl.BlockSpec((1, t, HEAD_DIM), lambda h, i: (h, i, 0)),
                  pl.BlockSpec((1, seq, HEAD_DIM), lambda h, i: (N_HEADS + h, 0, 0)),
                  pl.BlockSpec((1, seq, HEAD_DIM), lambda h, i: (2 * N_HEADS + h, 0, 0))],
        out_specs=pl.BlockSpec((1, t, 2 * HEAD_DIM), lambda h, i: (h, i, 0)),
        out_shape=jax.ShapeDtypeStruct((N_HEADS, seq, 2 * HEAD_DIM), f32),
        compiler_params=_params(("parallel", "arbitrary")),
    )(qkv, qkv, qkv)


def _attn_bwd(qkv, o_tot, d_o, seq):
    t = CHUNK

    def body(q_ref, k_ref, v_ref, ot_ref, do_ref, dq_ref, dk_ref, dv_ref):
        i = pl.program_id(1)

        @pl.when(i == 0)
        def _():
            dk_ref[...] = jnp.zeros_like(dk_ref)
            dv_ref[...] = jnp.zeros_like(dv_ref)

        q = q_ref[0] * 0.125
        d_out = do_ref[0]
        total = ot_ref[0][:, HEAD_DIM:HEAD_DIM + 1]
        row = lax.broadcasted_iota(jnp.int32, (t, t), 0)
        col = lax.broadcasted_iota(jnp.int32, (t, t), 1)
        incl = (row <= col).astype(bf16)
        before = (row < col).astype(bf16)

        def tile(j, carry, masked):
            dq, pre, dpre = carry
            start = pl.multiple_of(j * t, t)
            k = k_ref[0, pl.ds(start, t), :]
            v = v_ref[0, pl.ds(start, t), :]
            z = _nt(q, k)
            lb, lom = _log_sigmoids(z)
            if masked:
                keep = col < row
                lom = jnp.where(keep, lom, 0.0)
            hi, lo = _split_bf16(lom)
            pin = jnp.dot(hi, incl, preferred_element_type=f32) + jnp.dot(lo, incl, preferred_element_type=f32)
            a = lb + ((total - pre) - pin)
            if masked:
                a = jnp.where(keep, a, -jnp.inf)
            w = jnp.exp(a)
            d_a = _nt(d_out, v) * w
            d_lom_local = jnp.dot(d_a.astype(bf16), before, preferred_element_type=f32)
            d_lom = d_lom_local + dpre
            sig = jnp.exp(lb)
            dz = d_a * (1.0 - sig) - d_lom * sig
            if masked:
                dz = jnp.where(keep, dz, 0.0)
            dzb = dz.astype(bf16)
            dq = dq + jnp.dot(dzb, k, preferred_element_type=f32)
            dk_ref[0, pl.ds(start, t), :] += _tn(dzb, q)
            dv_ref[0, pl.ds(start, t), :] += _tn(w.astype(bf16), d_out)
            pre = pre + pin[:, t - 1:t]
            dpre = dpre + d_lom_local[:, t - 1:t] + d_a[:, t - 1:t]
            return dq, pre, dpre

        carry = (jnp.zeros((t, HEAD_DIM), f32), jnp.zeros((t, 1), f32), jnp.zeros((t, 1), f32))
        carry = lax.fori_loop(0, i, lambda n, cr: tile(n, cr, False), carry)
        dq, _, _ = tile(i, carry, True)
        dq_ref[0] = dq * 0.125

    blk = pl.BlockSpec((1, t, HEAD_DIM), lambda h, i: (h, i, 0))
    full = pl.BlockSpec((1, seq, HEAD_DIM), lambda h, i: (h, 0, 0))
    return pl.pallas_call(
        body, name="attn_bwd", grid=(N_HEADS, seq // t),
        in_specs=[blk, pl.BlockSpec((1, seq, HEAD_DIM), lambda h, i: (N_HEADS + h, 0, 0)),
                  pl.BlockSpec((1, seq, HEAD_DIM), lambda h, i: (2 * N_HEADS + h, 0, 0)),
                  pl.BlockSpec((1, t, 2 * HEAD_DIM), lambda h, i: (h, i, 0)), blk],
        out_specs=(blk, full, full),
        out_shape=(jax.ShapeDtypeStruct((N_HEADS, seq, HEAD_DIM), f32),) * 3,
        compiler_params=_params(("parallel", "arbitrary")),
    )(qkv, qkv, qkv, o_tot, d_o)


def _ssd_common(conv, dt_raw, dtb, alog):
    t = CHUNK
    sg = _sigmoid(conv)
    act = conv * sg
    dt_pre = dt_raw + dtb
    dt = _softplus(dt_pre)
    a = -jnp.exp(alog)
    row = lax.broadcasted_iota(jnp.int32, (t, t), 0)
    col = lax.broadcasted_iota(jnp.int32, (t, t), 1)
    causal = row >= col
    ac = _nn(causal.astype(f32), dt * a, HI)
    ac_t = _nt((row == col).astype(f32), ac, HI)
    ac_last = ac[t - 1:t, :]
    return sg, act, dt_pre, dt, a, causal, ac, ac_t, ac_last, jnp.exp(ac), jnp.exp(ac_last - ac), jnp.exp(ac_last)


def _ssd_fwd(xbc, dt_raw, conv_w, conv_b, dtb, alog, dsk, seq):
    t = CHUNK
    n_chunks = seq // t

    def body(x_ref, dt_ref, cw_ref, cb_ref, dtb_ref, al_ref, dsk_ref, conv_ref, y_ref, st_ref, prev, state):
        c = pl.program_id(0)

        @pl.when(c == 0)
        def _():
            prev[...] = jnp.zeros_like(prev)
            state[...] = jnp.zeros_like(state)

        cur = x_ref[...]
        pv = prev[...]
        rows = lax.broadcasted_iota(jnp.int32, (t, D_XBC), 0)
        conv = cur * cw_ref[CONV_K - 1:CONV_K, :] + cb_ref[...]
        for m in range(1, CONV_K):
            shifted = jnp.where(rows < m, pltpu.roll(pv, m, 0), pltpu.roll(cur, m, 0))
            conv = conv + shifted * cw_ref[CONV_K - 1 - m:CONV_K - m, :]
        prev[...] = cur
        conv_ref[...] = conv
        _, act, _, dt, _, causal, ac, ac_t, _, e_ac, dte, cdec = _ssd_common(conv, dt_ref[...], dtb_ref[...], al_ref[...])
        dskv = dsk_ref[...]
        ys = []
        for g in range(N_GROUPS):
            bg = act[:, D_SSM + g * N_STATE:D_SSM + (g + 1) * N_STATE].astype(bf16)
            cg = act[:, D_SSM + (N_GROUPS + g) * N_STATE:D_SSM + (N_GROUPS + g + 1) * N_STATE].astype(bf16)
            gm = _nt(cg, bg)
            for r in range(HEADS_PER_GROUP):
                h = g * HEADS_PER_GROUP + r
                xh = act[:, h * HEAD_DIM:(h + 1) * HEAD_DIM]
                xd = xh * dt[:, h:h + 1]
                seg = ac[:, h:h + 1] - ac_t[h:h + 1, :]
                lm = jnp.exp(jnp.where(causal, seg, -jnp.inf))
                y_diag = jnp.dot((gm * lm).astype(bf16), xd.astype(bf16), preferred_element_type=f32)
                hp = state[h]
                st_ref[0, h] = hp
                zo = _nt(cg, hp.astype(bf16))
                ys.append(y_diag + zo * e_ac[:, h:h + 1] + xh * dskv[:, h:h + 1])
                sc = _tn((xd * dte[:, h:h + 1]).astype(bf16), bg)
                state[h] = hp * cdec[:, h:h + 1] + sc
        y_ref[...] = jnp.concatenate(ys, axis=1)

    row = lambda c: (c, 0)
    fixed = lambda c: (0, 0)
    return pl.pallas_call(
        body, name="ssd_fwd", grid=(n_chunks,),
        in_specs=[pl.BlockSpec((t, D_XBC), row), pl.BlockSpec((t, LANES), row), pl.BlockSpec((CONV_K, D_XBC), fixed),
                  pl.BlockSpec((1, D_XBC), fixed), pl.BlockSpec((1, LANES), fixed), pl.BlockSpec((1, LANES), fixed),
                  pl.BlockSpec((1, LANES), fixed)],
        out_specs=(pl.BlockSpec((t, D_XBC), row), pl.BlockSpec((t, D_SSM), row),
                   pl.BlockSpec((1, N_HEADS, HEAD_DIM, N_STATE), lambda c: (c, 0, 0, 0))),
        out_shape=(jax.ShapeDtypeStruct((seq, D_XBC), f32), jax.ShapeDtypeStruct((seq, D_SSM), f32),
                   jax.ShapeDtypeStruct((n_chunks, N_HEADS, HEAD_DIM, N_STATE), f32)),
        scratch_shapes=[pltpu.VMEM((t, D_XBC), f32), pltpu.VMEM((N_HEADS, HEAD_DIM, N_STATE), f32)],
        compiler_params=_params(("arbitrary",)),
    )(xbc, dt_raw, conv_w, conv_b, dtb, alog, dsk)


def _ssd_bwd(dy, conv, xbc, dt_raw, states, conv_w, dtb, alog, dsk, seq):
    t = CHUNK
    n_chunks = seq // t

    def body(dy_ref, conv_ref, x_ref, dt_ref, st_ref, cw_ref, dtb_ref, al_ref, dsk_ref,
             dx_ref, ddt_ref, gcw_ref, gcb_ref, gdtb_ref, gal_ref, gdsk_ref, d_state, d_conv_next):
        c = pl.program_id(0)

        @pl.when(c == 0)
        def _():
            d_state[...] = jnp.zeros_like(d_state)
            d_conv_next[...] = jnp.zeros_like(d_conv_next)
            gcw_ref[...] = jnp.zeros_like(gcw_ref)
            gcb_ref[...] = jnp.zeros_like(gcb_ref)
            gdtb_ref[...] = jnp.zeros_like(gdtb_ref)
            gal_ref[...] = jnp.zeros_like(gal_ref)
            gdsk_ref[...] = jnp.zeros_like(gdsk_ref)

        conv = conv_ref[...]
        sg, act, dt_pre, dt, a, causal, ac, ac_t, _, e_ac, dte, cdec = _ssd_common(conv, dt_ref[...], dtb_ref[...], al_ref[...])
        dskv = dsk_ref[...]
        dyv = dy_ref[...]
        lane = lax.broadcasted_iota(jnp.int32, (1, LANES), 1)
        last_row = (lax.broadcasted_iota(jnp.int32, (t, 1), 0) == t - 1).astype(f32)
        ones = jnp.ones((t, LANES), f32)
        d_ac = jnp.zeros((t, LANES), f32)
        d_dt = jnp.zeros((t, LANES), f32)
        g_dsk = jnp.zeros((1, LANES), f32)
        dxs = []
        dbs = []
        dcs = []
        for g in range(N_GROUPS):
            bg = act[:, D_SSM + g * N_STATE:D_SSM + (g + 1) * N_STATE].astype(bf16)
            cg = act[:, D_SSM + (N_GROUPS + g) * N_STATE:D_SSM + (N_GROUPS + g + 1) * N_STATE].astype(bf16)
            gm = _nt(cg, bg)
            d_gm = jnp.zeros((t, t), f32)
            d_b = jnp.zeros((t, N_STATE), f32)
            d_c = jnp.zeros((t, N_STATE), f32)
            for r in range(HEADS_PER_GROUP):
                h = g * HEADS_PER_GROUP + r
                onehot = (lane == h).astype(f32)
                xh = act[:, h * HEAD_DIM:(h + 1) * HEAD_DIM]
                dth = dt[:, h:h + 1]
                xd = xh * dth
                xdb = xd.astype(bf16)
                seg = ac[:, h:h + 1] - ac_t[h:h + 1, :]
                lm = jnp.exp(jnp.where(causal, seg, -jnp.inf))
                mm = gm * lm
                hp = st_ref[0, h]
                hpb = hp.astype(bf16)
                d_hn = d_state[h]
                d_hnb = d_hn.astype(bf16)
                d_yh = dyv[:, h * HEAD_DIM:(h + 1) * HEAD_DIM]
                d_yb = d_yh.astype(bf16)
                g_dsk = g_dsk + jnp.sum(d_yh * xh) * onehot
                d_mm = _nt(d_yb, xdb)
                d_xd = _tn(mm.astype(bf16), d_yb)
                d_gm = d_gm + d_mm * lm
                d_seg = d_mm * mm
                d_ac_h = jnp.sum(d_seg, axis=1, keepdims=True) - _tn(d_seg, ones, HI)[:, 0:1]
                e_h = e_ac[:, h:h + 1]
                zo = _nt(cg, hpb)
                d_zo = d_yh * e_h
                d_zob = d_zo.astype(bf16)
                d_ac_h = d_ac_h + jnp.sum(d_yh * zo, axis=1, keepdims=True) * e_h
                d_c = d_c + jnp.dot(d_zob, hpb, preferred_element_type=f32)
                cd = cdec[:, h:h + 1]
                d_hp = _tn(d_zob, cg) + d_hn * cd
                d_last = jnp.sum(d_hn * hp) * cd
                dte_h = dte[:, h:h + 1]
                d_w = _nt(bg, d_hnb)
                d_b = d_b + jnp.dot((xd * dte_h).astype(bf16), d_hnb, preferred_element_type=f32)
                d_xd = d_xd + d_w * dte_h
                d_dte = jnp.sum(d_w * xd, axis=1, keepdims=True) * dte_h
                d_last = d_last + jnp.sum(d_dte)
                d_ac_h = d_ac_h - d_dte + d_last * last_row
                d_state[h] = d_hp
                dxs.append(d_yh * dskv[:, h:h + 1] + d_xd * dth)
                d_dt = d_dt + jnp.sum(d_xd * xh, axis=1, keepdims=True) * onehot
                d_ac = d_ac + d_ac_h * onehot
            d_gmb = d_gm.astype(bf16)
            dcs.append(d_c + jnp.dot(d_gmb, bg, preferred_element_type=f32))
            dbs.append(d_b + _tn(d_gmb, cg))
        d_ld = _nn((lax.broadcasted_iota(jnp.int32, (t, t), 1) >= lax.broadcasted_iota(jnp.int32, (t, t), 0)).astype(f32), d_ac, HI)
        d_dt = d_dt + d_ld * a
        gal_ref[...] += jnp.sum(d_ld * dt, axis=0, keepdims=True) * a
        gdsk_ref[...] += g_dsk
        d_dt_raw = d_dt * _sigmoid(dt_pre)
        ddt_ref[...] = d_dt_raw.astype(bf16)
        gdtb_ref[...] += jnp.sum(d_dt_raw, axis=0, keepdims=True)
        d_act = jnp.concatenate(dxs + dbs + dcs, axis=1)
        d_conv = d_act * (sg * (1.0 + conv * (1.0 - sg)))
        gcb_ref[...] += jnp.sum(d_conv, axis=0, keepdims=True)
        nxt = d_conv_next[...]
        rows = lax.broadcasted_iota(jnp.int32, (t, D_XBC), 0)
        xraw = x_ref[...]
        d_x = d_conv * cw_ref[CONV_K - 1:CONV_K, :]
        gcw_ref[pl.ds(CONV_K - 1, 1), :] += jnp.sum(xraw * d_conv, axis=0, keepdims=True)
        for m in range(1, CONV_K):
            ahead = jnp.where(rows >= t - m, pltpu.roll(nxt, t - m, 0), pltpu.roll(d_conv, t - m, 0))
            d_x = d_x + ahead * cw_ref[CONV_K - 1 - m:CONV_K - m, :]
            gcw_ref[pl.ds(CONV_K - 1 - m, 1), :] += jnp.sum(xraw * ahead, axis=0, keepdims=True)
        d_conv_next[...] = d_conv
        dx_ref[...] = d_x.astype(bf16)

    rev = lambda c: (n_chunks - 1 - c, 0)
    fixed = lambda c: (0, 0)
    return pl.pallas_call(
        body, name="ssd_bwd", grid=(n_chunks,),
        in_specs=[pl.BlockSpec((t, D_SSM), rev), pl.BlockSpec((t, D_XBC), rev), pl.BlockSpec((t, D_XBC), rev),
                  pl.BlockSpec((t, LANES), rev), pl.BlockSpec((1, N_HEADS, HEAD_DIM, N_STATE), lambda c: (n_chunks - 1 - c, 0, 0, 0)),
                  pl.BlockSpec((CONV_K, D_XBC), fixed), pl.BlockSpec((1, LANES), fixed), pl.BlockSpec((1, LANES), fixed),
                  pl.BlockSpec((1, LANES), fixed)],
        out_specs=(pl.BlockSpec((t, D_XBC), rev), pl.BlockSpec((t, LANES), rev), pl.BlockSpec((CONV_K, D_XBC), fixed),
                   pl.BlockSpec((1, D_XBC), fixed), pl.BlockSpec((1, LANES), fixed), pl.BlockSpec((1, LANES), fixed),
                   pl.BlockSpec((1, LANES), fixed)),
        out_shape=(jax.ShapeDtypeStruct((seq, D_XBC), bf16), jax.ShapeDtypeStruct((seq, LANES), bf16),
                   jax.ShapeDtypeStruct((CONV_K, D_XBC), f32), jax.ShapeDtypeStruct((1, D_XBC), f32),
                   jax.ShapeDtypeStruct((1, LANES), f32), jax.ShapeDtypeStruct((1, LANES), f32), jax.ShapeDtypeStruct((1, LANES), f32)),
        scratch_shapes=[pltpu.VMEM((N_HEADS, HEAD_DIM, N_STATE), f32), pltpu.VMEM((t, D_XBC), f32)],
        compiler_params=_params(("arbitrary",)),
    )(dy, conv, xbc, dt_raw, states, conv_w, dtb, alog, dsk)


def _heads_to_cols(ref, width=HEAD_DIM):
    return jnp.concatenate([ref[h][:, :width] for h in range(N_HEADS)], axis=1)


def _silu_and_grad(z):
    sg = _sigmoid(z)
    return z * sg, sg * (1.0 + z * (1.0 - sg))


def _rms(v):
    return lax.rsqrt(jnp.mean(v * v, axis=-1, keepdims=True) + EPS)


def _rms_bwd(d_hat, hat, r):
    return r * (d_hat - hat * jnp.mean(d_hat * hat, axis=-1, keepdims=True))


def _post(x, target, o_tot, y, za, zs, w_out, gate, g_sb, g_ssm, g_f, seq):
    ts = 256

    def body(x_ref, t_ref, o_ref, y_ref, za_ref, zs_ref, w_ref, gate_ref, gsb_ref, gss_ref, gf_ref,
             ycat_ref, dmix_ref, dx2_ref, loss_ref, gnf_ref, dgate_ref):
        @pl.when(pl.program_id(0) == 0)
        def _():
            loss_ref[...] = jnp.zeros_like(loss_ref)
            gnf_ref[...] = jnp.zeros_like(gnf_ref)
            dgate_ref[...] = jnp.zeros_like(dgate_ref)

        o = _heads_to_cols(o_ref)
        zav = za_ref[...]
        ya = (o * _rms(o) * gsb_ref[...]) * (zav * _sigmoid(zav))
        zsv = zs_ref[...]
        u = y_ref[...] * (zsv * _sigmoid(zsv))
        ys = u * _rms(u) * gss_ref[...]
        yab, ysb = ya.astype(bf16), ys.astype(bf16)
        ycat_ref[:, :D_ATTN] = yab
        ycat_ref[:, D_ATTN:] = ysb
        mixed = (jnp.dot(yab, w_ref[:D_ATTN, :], preferred_element_type=f32)
                 + jnp.dot(ysb, w_ref[D_ATTN:, :], preferred_element_type=f32))
        gate_v = gate_ref[...]
        x2 = x_ref[...] + gate_v * mixed
        r2 = _rms(x2)
        xh = x2 * r2
        gf = gf_ref[...]
        diff = xh * gf - t_ref[...]
        loss_ref[...] += jnp.sum(diff * diff) * (0.5 / D_MODEL)
        d_out = diff * (1.0 / D_MODEL)
        gnf_ref[...] += jnp.sum(d_out * xh, axis=0, keepdims=True)
        dx2 = _rms_bwd(d_out * gf, xh, r2)
        dx2_ref[...] = dx2
        dgate_ref[...] += jnp.sum(dx2 * mixed, axis=0, keepdims=True)
        dmix_ref[...] = (dx2 * gate_v).astype(bf16)

    row = lambda i: (i, 0)
    fixed = lambda i: (0, 0)
    vec = pl.BlockSpec((1, D_MODEL), fixed)
    return pl.pallas_call(
        body, name="post", grid=(seq // ts,),
        in_specs=[pl.BlockSpec((ts, D_MODEL), row), pl.BlockSpec((ts, D_MODEL), row),
                  pl.BlockSpec((N_HEADS, ts, 2 * HEAD_DIM), lambda i: (0, i, 0)), pl.BlockSpec((ts, D_SSM), row),
                  pl.BlockSpec((ts, D_ATTN), row), pl.BlockSpec((ts, D_SSM), row), pl.BlockSpec((D_ATTN + D_SSM, D_MODEL), fixed),
                  vec, vec, vec, vec],
        out_specs=(pl.BlockSpec((ts, D_ATTN + D_SSM), row), pl.BlockSpec((ts, D_MODEL), row), pl.BlockSpec((ts, D_MODEL), row),
                   pl.BlockSpec((1, LANES), fixed), vec, vec),
        out_shape=(jax.ShapeDtypeStruct((seq, D_ATTN + D_SSM), bf16), jax.ShapeDtypeStruct((seq, D_MODEL), bf16),
                   jax.ShapeDtypeStruct((seq, D_MODEL), f32), jax.ShapeDtypeStruct((1, LANES), f32),
                   jax.ShapeDtypeStruct((1, D_MODEL), f32), jax.ShapeDtypeStruct((1, D_MODEL), f32)),
        compiler_params=_params(("arbitrary",)),
    )(x, target, o_tot, y, za, zs, w_out, gate, g_sb, g_ssm, g_f)


def _bwd_out(dmix, w_out, o_tot, y, za, zs, g_sb, g_ssm, seq):
    ts = 256

    def body(dm_ref, w_ref, o_ref, y_ref, za_ref, zs_ref, gsb_ref, gss_ref, do_ref, dza_ref, dzs_ref, dy_ref, ggsb_ref, ggss_ref):
        @pl.when(pl.program_id(0) == 0)
        def _():
            ggsb_ref[...] = jnp.zeros_like(ggsb_ref)
            ggss_ref[...] = jnp.zeros_like(ggss_ref)

        dm = dm_ref[...]
        d_ya = _nt(dm, w_ref[:D_ATTN, :])
        d_ys = _nt(dm, w_ref[D_ATTN:, :])
        o = _heads_to_cols(o_ref)
        ro = _rms(o)
        oh = o * ro
        sa, dsa = _silu_and_grad(za_ref[...])
        gsb = gsb_ref[...]
        dza_ref[...] = (d_ya * oh * gsb * dsa).astype(bf16)
        ggsb_ref[...] += jnp.sum(d_ya * oh * sa, axis=0, keepdims=True)
        d_o = _rms_bwd(d_ya * gsb * sa, oh, ro)
        for h in range(N_HEADS):
            do_ref[h] = d_o[:, h * HEAD_DIM:(h + 1) * HEAD_DIM].astype(bf16)
        yv = y_ref[...]
        sz, dsz = _silu_and_grad(zs_ref[...])
        u = yv * sz
        ru = _rms(u)
        uh = u * ru
        ggss_ref[...] += jnp.sum(d_ys * uh, axis=0, keepdims=True)
        du = _rms_bwd(d_ys * gss_ref[...], uh, ru)
        dy_ref[...] = du * sz
        dzs_ref[...] = (du * yv * dsz).astype(bf16)

    row = lambda i: (i, 0)
    fixed = lambda i: (0, 0)
    vec = pl.BlockSpec((1, D_MODEL), fixed)
    return pl.pallas_call(
        body, name="bwd_out", grid=(seq // ts,),
        in_specs=[pl.BlockSpec((ts, D_MODEL), row), pl.BlockSpec((D_ATTN + D_SSM, D_MODEL), fixed),
                  pl.BlockSpec((N_HEADS, ts, 2 * HEAD_DIM), lambda i: (0, i, 0)), pl.BlockSpec((ts, D_SSM), row),
                  pl.BlockSpec((ts, D_ATTN), row), pl.BlockSpec((ts, D_SSM), row), vec, vec],
        out_specs=(pl.BlockSpec((N_HEADS, ts, HEAD_DIM), lambda i: (0, i, 0)), pl.BlockSpec((ts, D_ATTN), row),
                   pl.BlockSpec((ts, D_SSM), row), pl.BlockSpec((ts, D_SSM), row), vec, vec),
        out_shape=(jax.ShapeDtypeStruct((N_HEADS, seq, HEAD_DIM), bf16), jax.ShapeDtypeStruct((seq, D_ATTN), bf16),
                   jax.ShapeDtypeStruct((seq, D_SSM), bf16), jax.ShapeDtypeStruct((seq, D_SSM), f32),
                   jax.ShapeDtypeStruct((1, D_MODEL), f32), jax.ShapeDtypeStruct((1, D_MODEL), f32)),
        compiler_params=_params(("arbitrary",)),
    )(dmix, w_out, o_tot, y, za, zs, g_sb, g_ssm)


def _qkv_grads_to_cols(dq, dk, dv, seq):
    ts = 256

    def body(dq_ref, dk_ref, dv_ref, out_ref):
        for p, ref in enumerate((dq_ref, dk_ref, dv_ref)):
            out_ref[:, p * D_ATTN:(p + 1) * D_ATTN] = _heads_to_cols(ref).astype(bf16)

    blk = pl.BlockSpec((N_HEADS, ts, HEAD_DIM), lambda i: (0, i, 0))
    return pl.pallas_call(
        body, name="qkv_grads_to_cols", grid=(seq // ts,), in_specs=[blk, blk, blk],
        out_specs=pl.BlockSpec((ts, 3 * D_ATTN), lambda i: (i, 0)),
        out_shape=jax.ShapeDtypeStruct((seq, 3 * D_ATTN), bf16), compiler_params=_params(("parallel",)),
    )(dq, dk, dv)


def _bwd_in(dqkv, dza, dzs, dxbc, ddt, wp, x, dx2, gain, scale, seq):
    ts = 256
    pieces = ((0, 0, 3 * D_ATTN), (1, OFF_ZA, D_ATTN), (2, OFF_ZS, D_SSM), (3, OFF_XBC, D_XBC), (4, OFF_DT, LANES))

    def body(dqkv_ref, dza_ref, dzs_ref, dxbc_ref, ddt_ref, w_ref, x_ref, dx2_ref, g_ref, sc_ref,
             gx_ref, dshift_ref, dscale_ref, ggain_ref):
        @pl.when(pl.program_id(0) == 0)
        def _():
            dshift_ref[...] = jnp.zeros_like(dshift_ref)
            dscale_ref[...] = jnp.zeros_like(dscale_ref)
            ggain_ref[...] = jnp.zeros_like(ggain_ref)

        refs = (dqkv_ref, dza_ref, dzs_ref, dxbc_ref, ddt_ref)
        dh = jnp.zeros((ts, D_MODEL), f32)
        for idx, off, width in pieces:
            for cc in range(0, width, 512):
                wd = min(512, width - cc)
                dh = dh + _nt(refs[idx][:, cc:cc + wd], w_ref[:, off + cc:off + cc + wd])
        xv = x_ref[...]
        r = _rms(xv)
        xh = xv * r
        g = g_ref[...]
        dshift_ref[...] += jnp.sum(dh, axis=0, keepdims=True)
        dscale_ref[...] += jnp.sum(dh * xh * g, axis=0, keepdims=True)
        tt = dh * (1.0 + sc_ref[...])
        ggain_ref[...] += jnp.sum(tt * xh, axis=0, keepdims=True)
        gx_ref[...] = dx2_ref[...] + _rms_bwd(tt * g, xh, r)

    row = lambda i: (i, 0)
    fixed = lambda i: (0, 0)
    vec = pl.BlockSpec((1, D_MODEL), fixed)
    return pl.pallas_call(
        body, name="bwd_in", grid=(seq // ts,),
        in_specs=[pl.BlockSpec((ts, 3 * D_ATTN), row), pl.BlockSpec((ts, D_ATTN), row), pl.BlockSpec((ts, D_SSM), row),
                  pl.BlockSpec((ts, D_XBC), row), pl.BlockSpec((ts, LANES), row), pl.BlockSpec((D_MODEL, D_PROJ_P), fixed),
                  pl.BlockSpec((ts, D_MODEL), row), pl.BlockSpec((ts, D_MODEL), row), vec, vec],
        out_specs=(pl.BlockSpec((ts, D_MODEL), row), vec, vec, vec),
        out_shape=(jax.ShapeDtypeStruct((seq, D_MODEL), f32), jax.ShapeDtypeStruct((1, D_MODEL), f32),
                   jax.ShapeDtypeStruct((1, D_MODEL), f32), jax.ShapeDtypeStruct((1, D_MODEL), f32)),
        compiler_params=_params(("arbitrary",)),
    )(dqkv, dza, dzs, dxbc, ddt, wp, x, dx2, gain, scale)


def _grad_w(a, b, tn, name):
    seq, m = a.shape
    n = b.shape[1]
    tk = min(512, seq)

    def body(a_ref, b_ref, o_ref):
        @pl.when(pl.program_id(1) == 0)
        def _():
            o_ref[...] = jnp.zeros_like(o_ref)

        o_ref[...] += _tn(a_ref[...], b_ref[...])

    return pl.pallas_call(
        body, name=name, grid=(n // tn, seq // tk),
        in_specs=[pl.BlockSpec((tk, m), lambda j, k: (k, 0)), pl.BlockSpec((tk, tn), lambda j, k: (k, j))],
        out_specs=pl.BlockSpec((m, tn), lambda j, k: (0, j)),
        out_shape=jax.ShapeDtypeStruct((m, n), f32), compiler_params=_params(("parallel", "arbitrary")),
    )(a, b)


def _small_finish(g_all, c_all, dmod_mine):
    def body(g_ref, c_ref, dm_ref, tot_ref, gwada_ref):
        tot = g_ref[0:1, :]
        for j in range(1, N_DEV):
            tot = tot + g_ref[j:j + 1, :]
        tot_ref[...] = tot
        cv = c_ref[...]
        gwada_ref[...] = _tn(cv * _sigmoid(cv), dm_ref[...], HI)

    vmem = pl.BlockSpec(memory_space=pltpu.VMEM)
    return pl.pallas_call(
        body, name="small_finish", in_specs=[vmem, vmem, vmem], out_specs=(vmem, vmem),
        out_shape=(jax.ShapeDtypeStruct((1, N_PACK), f32), jax.ShapeDtypeStruct((D_MODEL, dmod_mine.shape[1]), f32)),
        compiler_params=_params(),
    )(g_all, c_all, dmod_mine)


def _adamw(w, g_parts, m, v, rows, name):
    r, c = w.shape
    n_parts = g_parts.shape[0]
    bc1 = 1.0 - ADAM_B1 ** ADAM_STEP
    bc2 = 1.0 - ADAM_B2 ** ADAM_STEP

    def body(w_ref, g_ref, m_ref, v_ref, go_ref, d_ref, mo_ref, vo_ref):
        g = g_ref[0]
        for j in range(1, n_parts):
            g = g + g_ref[j]
        go_ref[...] = g
        mn = ADAM_B1 * m_ref[...] + (1.0 - ADAM_B1) * g
        vn = ADAM_B2 * v_ref[...] + (1.0 - ADAM_B2) * (g * g)
        mo_ref[...] = mn
        vo_ref[...] = vn
        d_ref[...] = -ADAM_LR * ((mn / bc1) / (jnp.sqrt(vn / bc2) + ADAM_EPS) + ADAM_WD * w_ref[...])

    blk = pl.BlockSpec((rows, c), lambda i: (i, 0))
    return pl.pallas_call(
        body, name=name, grid=(r // rows,),
        in_specs=[blk, pl.BlockSpec((n_parts, rows, c), lambda i: (0, i, 0)), blk, blk],
        out_specs=(blk, blk, blk, blk), out_shape=(jax.ShapeDtypeStruct((r, c), f32),) * 4,
        compiler_params=_params(("parallel",)),
    )(w, g_parts, m, v)


def _pad_lanes(v):
    return jnp.pad(v, ((0, 0), (0, LANES - v.shape[1])))


def kernel(x, c, w_ada, b_ada, norm_in_gain, w_in, conv_w, conv_b, dt_bias, a_log, d_skip, sb_norm_gain, ssm_norm_gain, w_out, norm_f_gain, loss_target, m_w_ada, m_b_ada, m_norm_in_gain, m_w_in, m_conv_w, m_conv_b, m_dt_bias, m_a_log, m_d_skip, m_sb_norm_gain, m_ssm_norm_gain, m_w_out, m_norm_f_gain, v_w_ada, v_b_ada, v_norm_in_gain, v_w_in, v_conv_w, v_conv_b, v_dt_bias, v_a_log, v_d_skip, v_sb_norm_gain, v_ssm_norm_gain, v_w_out, v_norm_f_gain):
    seq = x.shape[1]
    xs = x[0]
    tgt = loss_target[0]
    _, my_slot = _me()

    mod, c_all = _mod_exchange(c, w_ada[0], b_ada)
    shift, scale, gate = mod[:, :D_MODEL], mod[:, D_MODEL:2 * D_MODEL], mod[:, 2 * D_MODEL:]
    w_in_g, w_out_g, conv_w_g = _all_gather(
        [_cast_bf16(w_in[0], 128), _cast_bf16(w_out[0], 128), conv_w[0]], "gather_weights")
    w_full = jnp.transpose(w_in_g, (1, 0, 2)).reshape(D_MODEL, D_PROJ)
    wp = jnp.concatenate([w_full[:, :4 * D_ATTN], w_full[:, D_PROJ - D_SSM:], w_full[:, 4 * D_ATTN:4 * D_ATTN + D_XBC],
                          _pad_lanes(w_full[:, 4 * D_ATTN + D_XBC:4 * D_ATTN + D_XBC + N_HEADS])], axis=1)
    w_out_full = w_out_g.reshape(D_ATTN + D_SSM, D_MODEL)
    conv_w_full = jnp.transpose(conv_w_g, (1, 0, 2)).reshape(CONV_K, D_XBC)
    dtb, alog, dsk = _pad_lanes(dt_bias), _pad_lanes(a_log), _pad_lanes(d_skip)

    h, qkv, za, zs, xbc, dt_raw = _proj(xs, norm_in_gain, scale, shift, wp, seq)
    o_tot = _attn_fwd(qkv, seq)
    conv, y, states = _ssd_fwd(xbc, dt_raw, conv_w_full, conv_b, dtb, alog, dsk, seq)
    ycat, dmix, dx2, loss_p, g_nf, d_gate = _post(xs, tgt, o_tot, y, za, zs, w_out_full, gate, sb_norm_gain, ssm_norm_gain,
                                                  norm_f_gain.reshape(1, D_MODEL), seq)

    d_o, dza, dzs, dy, g_sb, g_ss = _bwd_out(dmix, w_out_full, o_tot, y, za, zs, sb_norm_gain, ssm_norm_gain, seq)
    dq, dk, dv = _attn_bwd(qkv, o_tot, d_o, seq)
    dxbc, ddt, g_cw, g_cb, g_dtb, g_al, g_dsk = _ssd_bwd(dy, conv, xbc, dt_raw, states, conv_w_full, dtb, alog, dsk, seq)
    dqkv = _qkv_grads_to_cols(dq, dk, dv, seq)
    grad_x, d_shift, d_scale, g_in = _bwd_in(dqkv, dza, dzs, dxbc, ddt, wp, xs, dx2, norm_in_gain, scale, seq)
    gw_qkv = _grad_w(h, dqkv, 512, "grad_w_qkv")
    gw_za = _grad_w(h, dza, 512, "grad_w_za")
    gw_zs = _grad_w(h, dzs, 512, "grad_w_zs")
    gw_xbc = _grad_w(h, dxbc, 512, "grad_w_xbc")
    gw_dt = _grad_w(h, ddt, LANES, "grad_w_dt")
    gw_out = _grad_w(ycat, dmix, 512, "grad_w_out")
    gw_in = jnp.concatenate([gw_qkv, gw_za, gw_xbc, gw_dt[:, :N_HEADS], gw_zs], axis=1)

    gw_in_parts, gw_out_parts = _all_to_all(
        [jnp.transpose(gw_in.reshape(D_MODEL, N_DEV, W_IN_SHARD), (1, 0, 2)),
         gw_out.reshape(N_DEV, (D_ATTN + D_SSM) // N_DEV, D_MODEL)], "scatter_grads")
    packed = jnp.concatenate([loss_p, d_shift, d_scale, d_gate, g_in, g_cb, g_dtb, g_al, g_dsk, g_sb, g_ss, g_nf,
                              g_cw.reshape(1, CONV_K * D_XBC)], axis=1)
    (packed_all,) = _all_gather([packed], "gather_small")
    packed_all = packed_all.reshape(N_DEV, N_PACK)
    n_ada = w_ada.shape[2]
    dmod_mine = lax.dynamic_slice(packed_all, (0, P_DMOD + my_slot * n_ada), (N_DEV, n_ada))
    tot, g_w_ada = _small_finish(packed_all, c_all.reshape(N_DEV, D_MODEL), dmod_mine)

    def small(w, g, m, v):
        shape = w.shape
        w2, g2, m2, v2 = (t.reshape(1, -1) for t in (w, g, m, v))
        outs = _adamw(w2, g2[None], m2, v2, 1, "adamw_small")
        return tuple(t.reshape(shape) for t in outs)

    n_cw = conv_w.shape[2]
    g_cw_tot = tot[:, P_CW:].reshape(CONV_K, D_XBC)
    g_cw_mine = lax.dynamic_slice(g_cw_tot, (0, my_slot * n_cw), (CONV_K, n_cw))
    res = {
        "w_ada": tuple(t[None] for t in _adamw(w_ada[0], g_w_ada[None], m_w_ada[0], v_w_ada[0], 128, "adamw_w_ada")),
        "b_ada": small(b_ada, tot[:, P_DMOD:P_DMOD + 3 * D_MODEL], m_b_ada, v_b_ada),
        "norm_in_gain": small(norm_in_gain, tot[:, P_GIN:P_GIN + D_MODEL], m_norm_in_gain, v_norm_in_gain),
        "w_in": tuple(t[None] for t in _adamw(w_in[0], gw_in_parts, m_w_in[0], v_w_in[0], 128, "adamw_w_in")),
        "conv_w": small(conv_w, g_cw_mine[None], m_conv_w, v_conv_w),
        "conv_b": small(conv_b, tot[:, P_CB:P_CB + D_XBC], m_conv_b, v_conv_b),
        "dt_bias": small(dt_bias, tot[:, P_DTB:P_DTB + N_HEADS], m_dt_bias, v_dt_bias),
        "a_log": small(a_log, tot[:, P_ALOG:P_ALOG + N_HEADS], m_a_log, v_a_log),
        "d_skip": small(d_skip, tot[:, P_DSK:P_DSK + N_HEADS], m_d_skip, v_d_skip),
        "sb_norm_gain": small(sb_norm_gain, tot[:, P_GSB:P_GSB + D_MODEL], m_sb_norm_gain, v_sb_norm_gain),
        "ssm_norm_gain": small(ssm_norm_gain, tot[:, P_GSS:P_GSS + D_MODEL], m_ssm_norm_gain, v_ssm_norm_gain),
        "w_out": tuple(t[None] for t in _adamw(w_out[0], gw_out_parts, m_w_out[0], v_w_out[0], 64, "adamw_w_out")),
        "norm_f_gain": small(norm_f_gain, tot[0, P_GNF:P_GNF + D_MODEL], m_norm_f_gain, v_norm_f_gain),
    }
    names = ["w_ada", "b_ada", "norm_in_gain", "w_in", "conv_w", "conv_b", "dt_bias", "a_log", "d_skip", "sb_norm_gain",
             "ssm_norm_gain", "w_out", "norm_f_gain"]
    loss = tot[0, P_LOSS]
    return (loss, grad_x[None], *[res[n][0] for n in names], *[res[n][1] for n in names],
            *[res[n][2] for n in names], *[res[n][3] for n in names])
```

```python
import functools

import jax
import jax.numpy as jnp
from jax import lax
from jax.experimental import pallas as pl
from jax.experimental.pallas import tpu as pltpu

f32 = jnp.float32
bf16 = jnp.bfloat16
MESH = pl.DeviceIdType.MESH
HI = lax.Precision.HIGHEST

N_DEV = 8
D_MODEL = 1024
D_ATTN = 1024
D_SSM = 1024
N_HEADS = 16
HEAD_DIM = 64
N_GROUPS = 2
HEADS_PER_GROUP = 8
N_STATE = 128
D_XBC = D_SSM + 2 * N_GROUPS * N_STATE
D_PROJ = 4 * D_ATTN + D_XBC + N_HEADS + D_SSM
W_IN_SHARD = D_PROJ // N_DEV
CONV_K = 4
CHUNK = 128
ATTN_Q_ROWS = 1024
LANES = 128
EPS = 1e-6
OFF_ZA = 3072
OFF_ZS = 4096
OFF_XBC = 5120
OFF_DT = 6656
D_PROJ_P = 6784
VMEM_LIMIT_BYTES = 56 * 1024 * 1024

ADAM_LR = 0.001
ADAM_B1 = 0.9
ADAM_B2 = 0.999
ADAM_EPS = 1e-08
ADAM_WD = 0.01
ADAM_STEP = 10

P_LOSS = 0
P_DMOD = 128
P_GIN = 3200
P_CB = 4224
P_DTB = 5760
P_ALOG = 5888
P_DSK = 6016
P_GSB = 6144
P_GSS = 7168
P_GNF = 8192
P_CW = 9216
N_PACK = 15360


def _params(sem=None):
    return pltpu.CompilerParams(dimension_semantics=sem, vmem_limit_bytes=VMEM_LIMIT_BYTES)


def _sigmoid(v):
    return 1.0 / (1.0 + jnp.exp(-v))


def _softplus(v):
    return jnp.maximum(v, 0.0) + jnp.log(1.0 + jnp.exp(-jnp.abs(v)))


def _nt(a, b, precision=None):
    return lax.dot_general(a, b, (((1,), (1,)), ((), ())), preferred_element_type=f32, precision=precision)


def _tn(a, b, precision=None):
    return lax.dot_general(a, b, (((0,), (0,)), ((), ())), preferred_element_type=f32, precision=precision)


def _nn(a, b, precision=None):
    return lax.dot_general(a, b, (((1,), (0,)), ((), ())), preferred_element_type=f32, precision=precision)


def _me():
    x, y, c = lax.axis_index("x"), lax.axis_index("y"), lax.axis_index("c")
    return (x, y, c), 4 * x + 2 * y + c


def _peer(k):
    x, y, c = lax.axis_index("x"), lax.axis_index("y"), lax.axis_index("c")
    px = 1 - x if (k >> 2) & 1 else x
    py = 1 - y if (k >> 1) & 1 else y
    pc = 1 - c if k & 1 else c
    return (px, py, pc), 4 * px + 2 * py + pc


def _all_gather(arrs, name):
    n = len(arrs)

    def body(*refs):
        ins, outs = refs[:n], refs[n:2 * n]
        send_sems, recv_sems, local_sems = refs[2 * n:]
        _, my_slot = _me()
        sends = []
        locals_ = []
        for a in range(n):
            loc = pltpu.make_async_copy(ins[a], outs[a].at[my_slot], local_sems.at[a])
            loc.start()
            locals_.append(loc)
            for k in range(1, N_DEV):
                peer, _ = _peer(k)
                cp = pltpu.make_async_remote_copy(src_ref=ins[a], dst_ref=outs[a].at[my_slot], send_sem=send_sems.at[a, k - 1],
                                                  recv_sem=recv_sems.at[a, k - 1], device_id=peer, device_id_type=MESH)
                cp.start()
                sends.append(cp)
        for a in range(n):
            for k in range(1, N_DEV):
                peer, peer_slot = _peer(k)
                pltpu.make_async_remote_copy(src_ref=ins[a], dst_ref=outs[a].at[peer_slot], send_sem=send_sems.at[a, k - 1],
                                             recv_sem=recv_sems.at[a, k - 1], device_id=peer, device_id_type=MESH).wait_recv()
        for cp in sends:
            cp.wait_send()
        for loc in locals_:
            loc.wait()

    any_spec = pl.BlockSpec(memory_space=pl.ANY)
    return pl.pallas_call(
        body, name=name,
        out_shape=tuple(jax.ShapeDtypeStruct((N_DEV,) + a.shape, a.dtype) for a in arrs),
        in_specs=[any_spec] * n, out_specs=tuple([any_spec] * n),
        scratch_shapes=[pltpu.SemaphoreType.DMA((n, N_DEV - 1)), pltpu.SemaphoreType.DMA((n, N_DEV - 1)),
                        pltpu.SemaphoreType.DMA((n,))],
    )(*arrs)


def _all_to_all(arrs, name):
    n = len(arrs)

    def body(*refs):
        ins, outs = refs[:n], refs[n:2 * n]
        send_sems, recv_sems, local_sems = refs[2 * n:]
        _, my_slot = _me()
        sends = []
        locals_ = []
        for a in range(n):
            loc = pltpu.make_async_copy(ins[a].at[my_slot], outs[a].at[my_slot], local_sems.at[a])
            loc.start()
            locals_.append(loc)
            for k in range(1, N_DEV):
                peer, peer_slot = _peer(k)
                cp = pltpu.make_async_remote_copy(src_ref=ins[a].at[peer_slot], dst_ref=outs[a].at[my_slot],
                                                  send_sem=send_sems.at[a, k - 1], recv_sem=recv_sems.at[a, k - 1],
                                                  device_id=peer, device_id_type=MESH)
                cp.start()
                sends.append(cp)
        for a in range(n):
            for k in range(1, N_DEV):
                peer, peer_slot = _peer(k)
                pltpu.make_async_remote_copy(src_ref=ins[a].at[peer_slot], dst_ref=outs[a].at[peer_slot],
                                             send_sem=send_sems.at[a, k - 1], recv_sem=recv_sems.at[a, k - 1],
                                             device_id=peer, device_id_type=MESH).wait_recv()
        for cp in sends:
            cp.wait_send()
        for loc in locals_:
            loc.wait()

    any_spec = pl.BlockSpec(memory_space=pl.ANY)
    return pl.pallas_call(
        body, name=name,
        out_shape=tuple(jax.ShapeDtypeStruct(a.shape, a.dtype) for a in arrs),
        in_specs=[any_spec] * n, out_specs=tuple([any_spec] * n),
        scratch_shapes=[pltpu.SemaphoreType.DMA((n, N_DEV - 1)), pltpu.SemaphoreType.DMA((n, N_DEV - 1)),
                        pltpu.SemaphoreType.DMA((n,))],
    )(*arrs)


def _mod_exchange(c_row, w_ada, b_ada):
    n_col = w_ada.shape[1]

    def body(c_ref, w_ref, b_ref, mod_ref, call_ref, part, modp, send_sems, recv_sems):
        _, my_slot = _me()
        call_ref[my_slot] = c_ref[...]
        sends = []
        for k in range(1, N_DEV):
            peer, _ = _peer(k)
            cp = pltpu.make_async_remote_copy(src_ref=c_ref, dst_ref=call_ref.at[my_slot], send_sem=send_sems.at[0, k - 1],
                                              recv_sem=recv_sems.at[0, k - 1], device_id=peer, device_id_type=MESH)
            cp.start()
            sends.append(cp)
        for k in range(1, N_DEV):
            peer, peer_slot = _peer(k)
            pltpu.make_async_remote_copy(src_ref=c_ref, dst_ref=call_ref.at[peer_slot], send_sem=send_sems.at[0, k - 1],
                                         recv_sem=recv_sems.at[0, k - 1], device_id=peer, device_id_type=MESH).wait_recv()
        for cp in sends:
            cp.wait_send()
        w = w_ref[...]
        for b in range(N_DEV):
            cb = call_ref[b]
            part[b] = _nn(cb * _sigmoid(cb), w, HI)
        modp[my_slot] = part[my_slot]
        sends = []
        for k in range(1, N_DEV):
            peer, peer_slot = _peer(k)
            cp = pltpu.make_async_remote_copy(src_ref=part.at[peer_slot], dst_ref=modp.at[my_slot], send_sem=send_sems.at[1, k - 1],
                                              recv_sem=recv_sems.at[1, k - 1], device_id=peer, device_id_type=MESH)
            cp.start()
            sends.append(cp)
        for k in range(1, N_DEV):
            peer, peer_slot = _peer(k)
            pltpu.make_async_remote_copy(src_ref=part.at[peer_slot], dst_ref=modp.at[peer_slot], send_sem=send_sems.at[1, k - 1],
                                         recv_sem=recv_sems.at[1, k - 1], device_id=peer, device_id_type=MESH).wait_recv()
        for cp in sends:
            cp.wait_send()
        for j in range(N_DEV):
            mod_ref[:, j * n_col:(j + 1) * n_col] = modp[j] + b_ref[:, j * n_col:(j + 1) * n_col]

    vmem = pl.BlockSpec(memory_space=pltpu.VMEM)
    return pl.pallas_call(
        body, name="mod_exchange",
        out_shape=(jax.ShapeDtypeStruct((1, N_DEV * n_col), f32), jax.ShapeDtypeStruct((N_DEV, 1, D_MODEL), f32)),
        in_specs=[vmem, vmem, vmem], out_specs=(vmem, vmem),
        scratch_shapes=[pltpu.VMEM((N_DEV, 1, n_col), f32), pltpu.VMEM((N_DEV, 1, n_col), f32),
                        pltpu.SemaphoreType.DMA((2, N_DEV - 1)), pltpu.SemaphoreType.DMA((2, N_DEV - 1))],
        compiler_params=_params(),
    )(c_row, w_ada, b_ada)


def _cast_bf16(a, rows):
    r, c = a.shape

    def body(a_ref, o_ref):
        o_ref[...] = a_ref[...].astype(bf16)

    return pl.pallas_call(
        body, name="cast_bf16", grid=(r // rows,),
        in_specs=[pl.BlockSpec((rows, c), lambda i: (i, 0))], out_specs=pl.BlockSpec((rows, c), lambda i: (i, 0)),
        out_shape=jax.ShapeDtypeStruct((r, c), bf16), compiler_params=_params(("parallel",)),
    )(a)


def _proj(x, gain, scale, shift, wp, seq):
    ts = 256

    def body(x_ref, g_ref, sc_ref, sh_ref, w_ref, h_ref, qkv_ref, za_ref, zs_ref, xbc_ref, dt_ref):
        xv = x_ref[...]
        r = lax.rsqrt(jnp.mean(xv * xv, axis=-1, keepdims=True) + EPS)
        hb = ((xv * r * g_ref[...]) * (1.0 + sc_ref[...]) + sh_ref[...]).astype(bf16)
        h_ref[...] = hb
        for cb in range(3 * D_ATTN // 256):
            res = jnp.dot(hb, w_ref[:, cb * 256:(cb + 1) * 256], preferred_element_type=f32)
            for u in range(4):
                qkv_ref[cb * 4 + u] = res[:, u * HEAD_DIM:(u + 1) * HEAD_DIM].astype(bf16)
        for out_ref, off, width in ((za_ref, OFF_ZA, D_ATTN), (zs_ref, OFF_ZS, D_SSM), (xbc_ref, OFF_XBC, D_XBC), (dt_ref, OFF_DT, LANES)):
            for cc in range(0, width, 512):
                wd = min(512, width - cc)
                out_ref[:, cc:cc + wd] = jnp.dot(hb, w_ref[:, off + cc:off + cc + wd], preferred_element_type=f32)

    row = lambda i: (i, 0)
    fixed = lambda i: (0, 0)
    return pl.pallas_call(
        body, name="proj", grid=(seq // ts,),
        in_specs=[pl.BlockSpec((ts, D_MODEL), row), pl.BlockSpec((1, D_MODEL), fixed), pl.BlockSpec((1, D_MODEL), fixed),
                  pl.BlockSpec((1, D_MODEL), fixed), pl.BlockSpec((D_MODEL, D_PROJ_P), fixed)],
        out_specs=(pl.BlockSpec((ts, D_MODEL), row), pl.BlockSpec((3 * N_HEADS, ts, HEAD_DIM), lambda i: (0, i, 0)),
                   pl.BlockSpec((ts, D_ATTN), row), pl.BlockSpec((ts, D_SSM), row), pl.BlockSpec((ts, D_XBC), row),
                   pl.BlockSpec((ts, LANES), row)),
        out_shape=(jax.ShapeDtypeStruct((seq, D_MODEL), bf16), jax.ShapeDtypeStruct((3 * N_HEADS, seq, HEAD_DIM), bf16),
                   jax.ShapeDtypeStruct((seq, D_ATTN), f32), jax.ShapeDtypeStruct((seq, D_SSM), f32),
                   jax.ShapeDtypeStruct((seq, D_XBC), f32), jax.ShapeDtypeStruct((seq, LANES), f32)),
        compiler_params=_params(("arbitrary",)),
    )(x, gain, scale, shift, wp)


def _log_sigmoids(z):
    l1p = jnp.log(1.0 + jnp.exp(-jnp.abs(z)))
    return jnp.minimum(z, 0.0) - l1p, -jnp.maximum(z, 0.0) - l1p


def _split_bf16(v):
    hi = v.astype(bf16)
    return hi, (v - hi.astype(f32)).astype(bf16)


def _attn_fwd(qkv, seq):
    t = CHUNK
    tq = min(ATTN_Q_ROWS, seq)
    nd = tq // t

    def body(q_ref, k_ref, v_ref, o_ref):
        i = pl.program_id(1)
        q = q_ref[0] * 0.125
        upper = (lax.broadcasted_iota(jnp.int32, (t, t), 0) > lax.broadcasted_iota(jnp.int32, (t, t), 1)).astype(bf16)

        def tile(j, q_s, acc, run, masked):
            n = q_s.shape[0]
            start = pl.multiple_of(j * t, t)
            k = k_ref[0, pl.ds(start, t), :]
            v = v_ref[0, pl.ds(start, t), :]
            z = _nt(q_s, k)
            lb, lom = _log_sigmoids(z)
            if masked:
                keep = lax.broadcasted_iota(jnp.int32, (n, t), 1) < lax.broadcasted_iota(jnp.int32, (n, t), 0)
                lom = jnp.where(keep, lom, 0.0)
            hi, lo = _split_bf16(lom)
            tail = jnp.dot(hi, upper, preferred_element_type=f32) + jnp.dot(lo, upper, preferred_element_type=f32)
            a = lb + tail + run
            if masked:
                a = jnp.where(keep, a, -jnp.inf)
            w = jnp.exp(a)
            acc = acc + jnp.dot(w.astype(bf16), v, preferred_element_type=f32)
            run = run + tail[:, 0:1] + lom[:, 0:1]
            return acc, run

        acc, run = jnp.zeros((tq, HEAD_DIM), f32), jnp.zeros((tq, 1), f32)
        for jj in reversed(range(nd)):
            r0 = jj * t
            acc_s, run_s = tile(i * nd + jj, q[r0:], acc[r0:], run[r0:], True)
            acc = acc_s if r0 == 0 else jnp.concatenate([acc[:r0], acc_s], axis=0)
            run = run_s if r0 == 0 else jnp.concatenate([run[:r0], run_s], axis=0)
        acc, run = lax.fori_loop(0, i * nd, lambda n, cr: tile(i * nd - 1 - n, q, cr[0], cr[1], False), (acc, run))
        o_ref[0] = jnp.concatenate([acc, jnp.broadcast_to(run, (tq, HEAD_DIM))], axis=1)

    return pl.pallas_call(
        body, name="attn_fwd", grid=(N_HEADS, seq // tq),
        in_specs=[pl.BlockSpec((1, tq, HEAD_DIM), lambda h, i: (h, i, 0)),
                  pl.BlockSpec((1, seq, HEAD_DIM), lambda h, i: (N_HEADS + h, 0, 0)),
                  pl.BlockSpec((1, seq, HEAD_DIM), lambda h, i: (2 * N_HEADS + h, 0, 0))],
        out_specs=pl.BlockSpec((1, tq, 2 * HEAD_DIM), lambda h, i: (h, i, 0)),
        out_shape=jax.ShapeDtypeStruct((N_HEADS, seq, 2 * HEAD_DIM), f32),
        compiler_params=_params(("parallel", "arbitrary")),
    )(qkv, qkv, qkv)


def _attn_bwd(qkv, o_tot, d_o, seq):
    t = CHUNK
    tq = min(ATTN_Q_ROWS, seq)
    nd = tq // t
    nk = seq // t

    def body(q_ref, k_ref, v_ref, ot_ref, do_ref, dq_ref, dkt_ref, dvt_ref):
        i = pl.program_id(1)

        @pl.when(i == 0)
        def _():
            dkt_ref[...] = jnp.zeros_like(dkt_ref)
            dvt_ref[...] = jnp.zeros_like(dvt_ref)

        q = q_ref[0] * 0.125
        d_out = do_ref[0]
        total = ot_ref[0][:, HEAD_DIM:HEAD_DIM + 1]
        eye = (lax.broadcasted_iota(jnp.int32, (HEAD_DIM, HEAD_DIM), 0)
               == lax.broadcasted_iota(jnp.int32, (HEAD_DIM, HEAD_DIM), 1)).astype(bf16)
        q_t = _nt(eye, q).astype(bf16)
        do_t = _nt(eye, d_out).astype(bf16)
        ur = lax.broadcasted_iota(jnp.int32, (t, t), 0)
        uc = lax.broadcasted_iota(jnp.int32, (t, t), 1)
        incl = (ur <= uc).astype(bf16)
        before = (ur < uc).astype(bf16)

        def tile(j, rows, dq, pre, dpre, masked):
            q_s, do_s, tot_s = q[rows:], d_out[rows:], total[rows:]
            n = q_s.shape[0]
            start = pl.multiple_of(j * t, t)
            k = k_ref[0, pl.ds(start, t), :]
            v = v_ref[0, pl.ds(start, t), :]
            z = _nt(q_s, k)
            lb, lom = _log_sigmoids(z)
            if masked:
                keep = lax.broadcasted_iota(jnp.int32, (n, t), 1) < lax.broadcasted_iota(jnp.int32, (n, t), 0)
                lom = jnp.where(keep, lom, 0.0)
            hi, lo = _split_bf16(lom)
            pin = jnp.dot(hi, incl, preferred_element_type=f32) + jnp.dot(lo, incl, preferred_element_type=f32)
            a = lb + ((tot_s - pre) - pin)
            if masked:
                a = jnp.where(keep, a, -jnp.inf)
            w = jnp.exp(a)
            d_a = _nt(do_s, v) * w
            d_lom_local = jnp.dot(d_a.astype(bf16), before, preferred_element_type=f32)
            d_lom = d_lom_local + dpre
            sig = jnp.exp(lb)
            dz = d_a * (1.0 - sig) - d_lom * sig
            if masked:
                dz = jnp.where(keep, dz, 0.0)
            dzb = dz.astype(bf16)
            dq = dq + jnp.dot(dzb, k, preferred_element_type=f32)
            dkt_ref[0, j] += jnp.dot(q_t[:, rows:], dzb, preferred_element_type=f32)
            dvt_ref[0, j] += jnp.dot(do_t[:, rows:], w.astype(bf16), preferred_element_type=f32)
            pre = pre + pin[:, t - 1:t]
            dpre = dpre + d_lom_local[:, t - 1:t] + d_a[:, t - 1:t]
            return dq, pre, dpre

        carry = (jnp.zeros((tq, HEAD_DIM), f32), jnp.zeros((tq, 1), f32), jnp.zeros((tq, 1), f32))
        carry = lax.fori_loop(0, i * nd, lambda n, cr: tile(n, 0, cr[0], cr[1], cr[2], False), carry)
        for jj in range(nd):
            r0 = jj * t
            part = tile(i * nd + jj, r0, carry[0][r0:], carry[1][r0:], carry[2][r0:], True)
            carry = part if r0 == 0 else tuple(jnp.concatenate([c[:r0], p], axis=0) for c, p in zip(carry, part))
        dq_ref[0] = carry[0] * 0.125

    blk = pl.BlockSpec((1, tq, HEAD_DIM), lambda h, i: (h, i, 0))
    full_t = pl.BlockSpec((1, nk, HEAD_DIM, t), lambda h, i: (h, 0, 0, 0))
    return pl.pallas_call(
        body, name="attn_bwd", grid=(N_HEADS, seq // tq),
        in_specs=[blk, pl.BlockSpec((1, seq, HEAD_DIM), lambda h, i: (N_HEADS + h, 0, 0)),
                  pl.BlockSpec((1, seq, HEAD_DIM), lambda h, i: (2 * N_HEADS + h, 0, 0)),
                  pl.BlockSpec((1, tq, 2 * HEAD_DIM), lambda h, i: (h, i, 0)), blk],
        out_specs=(blk, full_t, full_t),
        out_shape=(jax.ShapeDtypeStruct((N_HEADS, seq, HEAD_DIM), f32),
                   jax.ShapeDtypeStruct((N_HEADS, nk, HEAD_DIM, t), f32), jax.ShapeDtypeStruct((N_HEADS, nk, HEAD_DIM, t), f32)),
        compiler_params=_params(("parallel", "arbitrary")),
    )(qkv, qkv, qkv, o_tot, d_o)


def _ssd_common(conv, dt_raw, dtb, alog):
    t = CHUNK
    sg = _sigmoid(conv)
    act = conv * sg
    dt_pre = dt_raw + dtb
    dt = _softplus(dt_pre)
    a = -jnp.exp(alog)
    row = lax.broadcasted_iota(jnp.int32, (t, t), 0)
    col = lax.broadcasted_iota(jnp.int32, (t, t), 1)
    causal = row >= col
    ac = _nn(causal.astype(f32), dt * a, HI)
    ac_t = _nt((row == col).astype(f32), ac, HI)
    ac_last = ac[t - 1:t, :]
    return sg, act, dt_pre, dt, a, causal, ac, ac_t, ac_last, jnp.exp(ac), jnp.exp(ac_last - ac), jnp.exp(ac_last)


def _ssd_fwd(xbc, dt_raw, conv_w, conv_b, dtb, alog, dsk, seq):
    t = CHUNK
    n_chunks = seq // t

    def body(x_ref, dt_ref, cw_ref, cb_ref, dtb_ref, al_ref, dsk_ref, conv_ref, y_ref, st_ref, prev, state):
        c = pl.program_id(0)

        @pl.when(c == 0)
        def _():
            prev[...] = jnp.zeros_like(prev)
            state[...] = jnp.zeros_like(state)

        cur = x_ref[...]
        pv = prev[...]
        rows = lax.broadcasted_iota(jnp.int32, (t, D_XBC), 0)
        conv = cur * cw_ref[CONV_K - 1:CONV_K, :] + cb_ref[...]
        for m in range(1, CONV_K):
            shifted = jnp.where(rows < m, pltpu.roll(pv, m, 0), pltpu.roll(cur, m, 0))
            conv = conv + shifted * cw_ref[CONV_K - 1 - m:CONV_K - m, :]
        prev[...] = cur
        conv_ref[...] = conv
        _, act, _, dt, _, causal, ac, ac_t, _, e_ac, dte, cdec = _ssd_common(conv, dt_ref[...], dtb_ref[...], al_ref[...])
        dskv = dsk_ref[...]
        ys = []
        for g in range(N_GROUPS):
            bg = act[:, D_SSM + g * N_STATE:D_SSM + (g + 1) * N_STATE].astype(bf16)
            cg = act[:, D_SSM + (N_GROUPS + g) * N_STATE:D_SSM + (N_GROUPS + g + 1) * N_STATE].astype(bf16)
            gm = _nt(cg, bg)
            for r in range(HEADS_PER_GROUP):
                h = g * HEADS_PER_GROUP + r
                xh = act[:, h * HEAD_DIM:(h + 1) * HEAD_DIM]
                xd = xh * dt[:, h:h + 1]
                seg = ac[:, h:h + 1] - ac_t[h:h + 1, :]
                lm = jnp.exp(jnp.where(causal, seg, -jnp.inf))
                y_diag = jnp.dot((gm * lm).astype(bf16), xd.astype(bf16), preferred_element_type=f32)
                hp = state[h]
                st_ref[0, h] = hp
                zo = _nt(cg, hp.astype(bf16))
                ys.append(y_diag + zo * e_ac[:, h:h + 1] + xh * dskv[:, h:h + 1])
                sc = _tn((xd * dte[:, h:h + 1]).astype(bf16), bg)
                state[h] = hp * cdec[:, h:h + 1] + sc
        y_ref[...] = jnp.concatenate(ys, axis=1)

    row = lambda c: (c, 0)
    fixed = lambda c: (0, 0)
    return pl.pallas_call(
        body, name="ssd_fwd", grid=(n_chunks,),
        in_specs=[pl.BlockSpec((t, D_XBC), row), pl.BlockSpec((t, LANES), row), pl.BlockSpec((CONV_K, D_XBC), fixed),
                  pl.BlockSpec((1, D_XBC), fixed), pl.BlockSpec((1, LANES), fixed), pl.BlockSpec((1, LANES), fixed),
                  pl.BlockSpec((1, LANES), fixed)],
        out_specs=(pl.BlockSpec((t, D_XBC), row), pl.BlockSpec((t, D_SSM), row),
                   pl.BlockSpec((1, N_HEADS, HEAD_DIM, N_STATE), lambda c: (c, 0, 0, 0))),
        out_shape=(jax.ShapeDtypeStruct((seq, D_XBC), f32), jax.ShapeDtypeStruct((seq, D_SSM), f32),
                   jax.ShapeDtypeStruct((n_chunks, N_HEADS, HEAD_DIM, N_STATE), f32)),
        scratch_shapes=[pltpu.VMEM((t, D_XBC), f32), pltpu.VMEM((N_HEADS, HEAD_DIM, N_STATE), f32)],
        compiler_params=_params(("arbitrary",)),
    )(xbc, dt_raw, conv_w, conv_b, dtb, alog, dsk)


def _ssd_bwd(dy, conv, xbc, dt_raw, states, conv_w, dtb, alog, dsk, seq):
    t = CHUNK
    n_chunks = seq // t

    def body(dy_ref, conv_ref, x_ref, dt_ref, st_ref, cw_ref, dtb_ref, al_ref, dsk_ref,
             dx_ref, ddt_ref, gcw_ref, gcb_ref, gdtb_ref, gal_ref, gdsk_ref, d_state, d_conv_next):
        c = pl.program_id(0)

        @pl.when(c == 0)
        def _():
            d_state[...] = jnp.zeros_like(d_state)
            d_conv_next[...] = jnp.zeros_like(d_conv_next)
            gcw_ref[...] = jnp.zeros_like(gcw_ref)
            gcb_ref[...] = jnp.zeros_like(gcb_ref)
            gdtb_ref[...] = jnp.zeros_like(gdtb_ref)
            gal_ref[...] = jnp.zeros_like(gal_ref)
            gdsk_ref[...] = jnp.zeros_like(gdsk_ref)

        conv = conv_ref[...]
        sg, act, dt_pre, dt, a, causal, ac, ac_t, _, e_ac, dte, cdec = _ssd_common(conv, dt_ref[...], dtb_ref[...], al_ref[...])
        dskv = dsk_ref[...]
        dyv = dy_ref[...]
        lane = lax.broadcasted_iota(jnp.int32, (1, LANES), 1)
        last_row = (lax.broadcasted_iota(jnp.int32, (t, 1), 0) == t - 1).astype(f32)
        ones = jnp.ones((t, LANES), f32)
        d_ac = jnp.zeros((t, LANES), f32)
        d_dt = jnp.zeros((t, LANES), f32)
        g_dsk = jnp.zeros((1, LANES), f32)
        dxs = []
        dbs = []
        dcs = []
        for g in range(N_GROUPS):
            bg = act[:, D_SSM + g * N_STATE:D_SSM + (g + 1) * N_STATE].astype(bf16)
            cg = act[:, D_SSM + (N_GROUPS + g) * N_STATE:D_SSM + (N_GROUPS + g + 1) * N_STATE].astype(bf16)
            gm = _nt(cg, bg)
            d_gm = jnp.zeros((t, t), f32)
            d_b = jnp.zeros((t, N_STATE), f32)
            d_c = jnp.zeros((t, N_STATE), f32)
            for r in range(HEADS_PER_GROUP):
                h = g * HEADS_PER_GROUP + r
                onehot = (lane == h).astype(f32)
                xh = act[:, h * HEAD_DIM:(h + 1) * HEAD_DIM]
                dth = dt[:, h:h + 1]
                xd = xh * dth
                xdb = xd.astype(bf16)
                seg = ac[:, h:h + 1] - ac_t[h:h + 1, :]
                lm = jnp.exp(jnp.where(causal, seg, -jnp.inf))
                mm = gm * lm
                hp = st_ref[0, h]
                hpb = hp.astype(bf16)
                d_hn = d_state[h]
                d_hnb = d_hn.astype(bf16)
                d_yh = dyv[:, h * HEAD_DIM:(h + 1) * HEAD_DIM]
                d_yb = d_yh.astype(bf16)
                g_dsk = g_dsk + jnp.sum(d_yh * xh) * onehot
                d_mm = _nt(d_yb, xdb)
                d_xd = _tn(mm.astype(bf16), d_yb)
                d_gm = d_gm + d_mm * lm
                d_seg = d_mm * mm
                d_ac_h = jnp.sum(d_seg, axis=1, keepdims=True) - _tn(d_seg, ones, HI)[:, 0:1]
                e_h = e_ac[:, h:h + 1]
                zo = _nt(cg, hpb)
                d_zo = d_yh * e_h
                d_zob = d_zo.astype(bf16)
                d_ac_h = d_ac_h + jnp.sum(d_yh * zo, axis=1, keepdims=True) * e_h
                d_c = d_c + jnp.dot(d_zob, hpb, preferred_element_type=f32)
                cd = cdec[:, h:h + 1]
                d_hp = _tn(d_zob, cg) + d_hn * cd
                d_last = jnp.sum(d_hn * hp) * cd
                dte_h = dte[:, h:h + 1]
                d_w = _nt(bg, d_hnb)
                d_b = d_b + jnp.dot((xd * dte_h).astype(bf16), d_hnb, preferred_element_type=f32)
                d_xd = d_xd + d_w * dte_h
                d_dte = jnp.sum(d_w * xd, axis=1, keepdims=True) * dte_h
                d_last = d_last + jnp.sum(d_dte)
                d_ac_h = d_ac_h - d_dte + d_last * last_row
                d_state[h] = d_hp
                dxs.append(d_yh * dskv[:, h:h + 1] + d_xd * dth)
                d_dt = d_dt + jnp.sum(d_xd * xh, axis=1, keepdims=True) * onehot
                d_ac = d_ac + d_ac_h * onehot
            d_gmb = d_gm.astype(bf16)
            dcs.append(d_c + jnp.dot(d_gmb, bg, preferred_element_type=f32))
            dbs.append(d_b + _tn(d_gmb, cg))
        d_ld = _nn((lax.broadcasted_iota(jnp.int32, (t, t), 1) >= lax.broadcasted_iota(jnp.int32, (t, t), 0)).astype(f32), d_ac, HI)
        d_dt = d_dt + d_ld * a
        gal_ref[...] += jnp.sum(d_ld * dt, axis=0, keepdims=True) * a
        gdsk_ref[...] += g_dsk
        d_dt_raw = d_dt * _sigmoid(dt_pre)
        ddt_ref[...] = d_dt_raw.astype(bf16)
        gdtb_ref[...] += jnp.sum(d_dt_raw, axis=0, keepdims=True)
        d_act = jnp.concatenate(dxs + dbs + dcs, axis=1)
        d_conv = d_act * (sg * (1.0 + conv * (1.0 - sg)))
        gcb_ref[...] += jnp.sum(d_conv, axis=0, keepdims=True)
        nxt = d_conv_next[...]
        rows = lax.broadcasted_iota(jnp.int32, (t, D_XBC), 0)
        xraw = x_ref[...]
        d_x = d_conv * cw_ref[CONV_K - 1:CONV_K, :]
        gcw_ref[pl.ds(CONV_K - 1, 1), :] += jnp.sum(xraw * d_conv, axis=0, keepdims=True)
        for m in range(1, CONV_K):
            ahead = jnp.where(rows >= t - m, pltpu.roll(nxt, t - m, 0), pltpu.roll(d_conv, t - m, 0))
            d_x = d_x + ahead * cw_ref[CONV_K - 1 - m:CONV_K - m, :]
            gcw_ref[pl.ds(CONV_K - 1 - m, 1), :] += jnp.sum(xraw * ahead, axis=0, keepdims=True)
        d_conv_next[...] = d_conv
        dx_ref[...] = d_x.astype(bf16)

    rev = lambda c: (n_chunks - 1 - c, 0)
    fixed = lambda c: (0, 0)
    return pl.pallas_call(
        body, name="ssd_bwd", grid=(n_chunks,),
        in_specs=[pl.BlockSpec((t, D_SSM), rev), pl.BlockSpec((t, D_XBC), rev), pl.BlockSpec((t, D_XBC), rev),
                  pl.BlockSpec((t, LANES), rev), pl.BlockSpec((1, N_HEADS, HEAD_DIM, N_STATE), lambda c: (n_chunks - 1 - c, 0, 0, 0)),
                  pl.BlockSpec((CONV_K, D_XBC), fixed), pl.BlockSpec((1, LANES), fixed), pl.BlockSpec((1, LANES), fixed),
                  pl.BlockSpec((1, LANES), fixed)],
        out_specs=(pl.BlockSpec((t, D_XBC), rev), pl.BlockSpec((t, LANES), rev), pl.BlockSpec((CONV_K, D_XBC), fixed),
                   pl.BlockSpec((1, D_XBC), fixed), pl.BlockSpec((1, LANES), fixed), pl.BlockSpec((1, LANES), fixed),
                   pl.BlockSpec((1, LANES), fixed)),
        out_shape=(jax.ShapeDtypeStruct((seq, D_XBC), bf16), jax.ShapeDtypeStruct((seq, LANES), bf16),
                   jax.ShapeDtypeStruct((CONV_K, D_XBC), f32), jax.ShapeDtypeStruct((1, D_XBC), f32),
                   jax.ShapeDtypeStruct((1, LANES), f32), jax.ShapeDtypeStruct((1, LANES), f32), jax.ShapeDtypeStruct((1, LANES), f32)),
        scratch_shapes=[pltpu.VMEM((N_HEADS, HEAD_DIM, N_STATE), f32), pltpu.VMEM((t, D_XBC), f32)],
        compiler_params=_params(("arbitrary",)),
    )(dy, conv, xbc, dt_raw, states, conv_w, dtb, alog, dsk)


def _heads_to_cols(ref, width=HEAD_DIM):
    return jnp.concatenate([ref[h][:, :width] for h in range(N_HEADS)], axis=1)


def _silu_and_grad(z):
    sg = _sigmoid(z)
    return z * sg, sg * (1.0 + z * (1.0 - sg))


def _rms(v):
    return lax.rsqrt(jnp.mean(v * v, axis=-1, keepdims=True) + EPS)


def _rms_bwd(d_hat, hat, r):
    return r * (d_hat - hat * jnp.mean(d_hat * hat, axis=-1, keepdims=True))


def _post(x, target, o_tot, y, za, zs, w_out, gate, g_sb, g_ssm, g_f, seq):
    ts = 256

    def body(x_ref, t_ref, o_ref, y_ref, za_ref, zs_ref, w_ref, gate_ref, gsb_ref, gss_ref, gf_ref,
             ycat_ref, dmix_ref, dx2_ref, loss_ref, gnf_ref, dgate_ref):
        @pl.when(pl.program_id(0) == 0)
        def _():
            loss_ref[...] = jnp.zeros_like(loss_ref)
            gnf_ref[...] = jnp.zeros_like(gnf_ref)
            dgate_ref[...] = jnp.zeros_like(dgate_ref)

        o = _heads_to_cols(o_ref)
        zav = za_ref[...]
        ya = (o * _rms(o) * gsb_ref[...]) * (zav * _sigmoid(zav))
        zsv = zs_ref[...]
        u = y_ref[...] * (zsv * _sigmoid(zsv))
        ys = u * _rms(u) * gss_ref[...]
        yab, ysb = ya.astype(bf16), ys.astype(bf16)
        ycat_ref[:, :D_ATTN] = yab
        ycat_ref[:, D_ATTN:] = ysb
        mixed = (jnp.dot(yab, w_ref[:D_ATTN, :], preferred_element_type=f32)
                 + jnp.dot(ysb, w_ref[D_ATTN:, :], preferred_element_type=f32))
        gate_v = gate_ref[...]
        x2 = x_ref[...] + gate_v * mixed
        r2 = _rms(x2)
        xh = x2 * r2
        gf = gf_ref[...]
        diff = xh * gf - t_ref[...]
        loss_ref[...] += jnp.sum(diff * diff) * (0.5 / D_MODEL)
        d_out = diff * (1.0 / D_MODEL)
        gnf_ref[...] += jnp.sum(d_out * xh, axis=0, keepdims=True)
        dx2 = _rms_bwd(d_out * gf, xh, r2)
        dx2_ref[...] = dx2
        dgate_ref[...] += jnp.sum(dx2 * mixed, axis=0, keepdims=True)
        dmix_ref[...] = (dx2 * gate_v).astype(bf16)

    row = lambda i: (i, 0)
    fixed = lambda i: (0, 0)
    vec = pl.BlockSpec((1, D_MODEL), fixed)
    return pl.pallas_call(
        body, name="post", grid=(seq // ts,),
        in_specs=[pl.BlockSpec((ts, D_MODEL), row), pl.BlockSpec((ts, D_MODEL), row),
                  pl.BlockSpec((N_HEADS, ts, 2 * HEAD_DIM), lambda i: (0, i, 0)), pl.BlockSpec((ts, D_SSM), row),
                  pl.BlockSpec((ts, D_ATTN), row), pl.BlockSpec((ts, D_SSM), row), pl.BlockSpec((D_ATTN + D_SSM, D_MODEL), fixed),
                  vec, vec, vec, vec],
        out_specs=(pl.BlockSpec((ts, D_ATTN + D_SSM), row), pl.BlockSpec((ts, D_MODEL), row), pl.BlockSpec((ts, D_MODEL), row),
                   pl.BlockSpec((1, LANES), fixed), vec, vec),
        out_shape=(jax.ShapeDtypeStruct((seq, D_ATTN + D_SSM), bf16), jax.ShapeDtypeStruct((seq, D_MODEL), bf16),
                   jax.ShapeDtypeStruct((seq, D_MODEL), f32), jax.ShapeDtypeStruct((1, LANES), f32),
                   jax.ShapeDtypeStruct((1, D_MODEL), f32), jax.ShapeDtypeStruct((1, D_MODEL), f32)),
        compiler_params=_params(("arbitrary",)),
    )(x, target, o_tot, y, za, zs, w_out, gate, g_sb, g_ssm, g_f)


def _bwd_out(dmix, w_out, o_tot, y, za, zs, g_sb, g_ssm, seq):
    ts = 256

    def body(dm_ref, w_ref, o_ref, y_ref, za_ref, zs_ref, gsb_ref, gss_ref, do_ref, dza_ref, dzs_ref, dy_ref, ggsb_ref, ggss_ref):
        @pl.when(pl.program_id(0) == 0)
        def _():
            ggsb_ref[...] = jnp.zeros_like(ggsb_ref)
            ggss_ref[...] = jnp.zeros_like(ggss_ref)

        dm = dm_ref[...]
        d_ya = _nt(dm, w_ref[:D_ATTN, :])
        d_ys = _nt(dm, w_ref[D_ATTN:, :])
        o = _heads_to_cols(o_ref)
        ro = _rms(o)
        oh = o * ro
        sa, dsa = _silu_and_grad(za_ref[...])
        gsb = gsb_ref[...]
        dza_ref[...] = (d_ya * oh * gsb * dsa).astype(bf16)
        ggsb_ref[...] += jnp.sum(d_ya * oh * sa, axis=0, keepdims=True)
        d_o = _rms_bwd(d_ya * gsb * sa, oh, ro)
        for h in range(N_HEADS):
            do_ref[h] = d_o[:, h * HEAD_DIM:(h + 1) * HEAD_DIM].astype(bf16)
        yv = y_ref[...]
        sz, dsz = _silu_and_grad(zs_ref[...])
        u = yv * sz
        ru = _rms(u)
        uh = u * ru
        ggss_ref[...] += jnp.sum(d_ys * uh, axis=0, keepdims=True)
        du = _rms_bwd(d_ys * gss_ref[...], uh, ru)
        dy_ref[...] = du * sz
        dzs_ref[...] = (du * yv * dsz).astype(bf16)

    row = lambda i: (i, 0)
    fixed = lambda i: (0, 0)
    vec = pl.BlockSpec((1, D_MODEL), fixed)
    return pl.pallas_call(
        body, name="bwd_out", grid=(seq // ts,),
        in_specs=[pl.BlockSpec((ts, D_MODEL), row), pl.BlockSpec((D_ATTN + D_SSM, D_MODEL), fixed),
                  pl.BlockSpec((N_HEADS, ts, 2 * HEAD_DIM), lambda i: (0, i, 0)), pl.BlockSpec((ts, D_SSM), row),
                  pl.BlockSpec((ts, D_ATTN), row), pl.BlockSpec((ts, D_SSM), row), vec, vec],
        out_specs=(pl.BlockSpec((N_HEADS, ts, HEAD_DIM), lambda i: (0, i, 0)), pl.BlockSpec((ts, D_ATTN), row),
                   pl.BlockSpec((ts, D_SSM), row), pl.BlockSpec((ts, D_SSM), row), vec, vec),
        out_shape=(jax.ShapeDtypeStruct((N_HEADS, seq, HEAD_DIM), bf16), jax.ShapeDtypeStruct((seq, D_ATTN), bf16),
                   jax.ShapeDtypeStruct((seq, D_SSM), bf16), jax.ShapeDtypeStruct((seq, D_SSM), f32),
                   jax.ShapeDtypeStruct((1, D_MODEL), f32), jax.ShapeDtypeStruct((1, D_MODEL), f32)),
        compiler_params=_params(("arbitrary",)),
    )(dmix, w_out, o_tot, y, za, zs, g_sb, g_ssm)


def _qkv_grads_to_cols(dq, dkt, dvt, seq):
    ts = 256
    nb = ts // CHUNK

    def body(dq_ref, dkt_ref, dvt_ref, out_ref):
        out_ref[:, :D_ATTN] = _heads_to_cols(dq_ref).astype(bf16)
        eye = (lax.broadcasted_iota(jnp.int32, (CHUNK, CHUNK), 0) == lax.broadcasted_iota(jnp.int32, (CHUNK, CHUNK), 1)).astype(bf16)
        for p, ref in ((1, dkt_ref), (2, dvt_ref)):
            for b in range(nb):
                cols = [_nt(eye, ref[h, b].astype(bf16)) for h in range(N_HEADS)]
                out_ref[b * CHUNK:(b + 1) * CHUNK, p * D_ATTN:(p + 1) * D_ATTN] = jnp.concatenate(cols, axis=1).astype(bf16)

    blk = pl.BlockSpec((N_HEADS, ts, HEAD_DIM), lambda i: (0, i, 0))
    blk_t = pl.BlockSpec((N_HEADS, nb, HEAD_DIM, CHUNK), lambda i: (0, i, 0, 0))
    return pl.pallas_call(
        body, name="qkv_grads_to_cols", grid=(seq // ts,), in_specs=[blk, blk_t, blk_t],
        out_specs=pl.BlockSpec((ts, 3 * D_ATTN), lambda i: (i, 0)),
        out_shape=jax.ShapeDtypeStruct((seq, 3 * D_ATTN), bf16), compiler_params=_params(("parallel",)),
    )(dq, dkt, dvt)


def _bwd_in(dqkv, dza, dzs, dxbc, ddt, wp, x, dx2, gain, scale, seq):
    ts = 256
    pieces = ((0, 0, 3 * D_ATTN), (1, OFF_ZA, D_ATTN), (2, OFF_ZS, D_SSM), (3, OFF_XBC, D_XBC), (4, OFF_DT, LANES))

    def body(dqkv_ref, dza_ref, dzs_ref, dxbc_ref, ddt_ref, w_ref, x_ref, dx2_ref, g_ref, sc_ref,
             gx_ref, dshift_ref, dscale_ref, ggain_ref):
        @pl.when(pl.program_id(0) == 0)
        def _():
            dshift_ref[...] = jnp.zeros_like(dshift_ref)
            dscale_ref[...] = jnp.zeros_like(dscale_ref)
            ggain_ref[...] = jnp.zeros_like(ggain_ref)

        refs = (dqkv_ref, dza_ref, dzs_ref, dxbc_ref, ddt_ref)
        dh = jnp.zeros((ts, D_MODEL), f32)
        for idx, off, width in pieces:
            for cc in range(0, width, 512):
                wd = min(512, width - cc)
                dh = dh + _nt(refs[idx][:, cc:cc + wd], w_ref[:, off + cc:off + cc + wd])
        xv = x_ref[...]
        r = _rms(xv)
        xh = xv * r
        g = g_ref[...]
        dshift_ref[...] += jnp.sum(dh, axis=0, keepdims=True)
        dscale_ref[...] += jnp.sum(dh * xh * g, axis=0, keepdims=True)
        tt = dh * (1.0 + sc_ref[...])
        ggain_ref[...] += jnp.sum(tt * xh, axis=0, keepdims=True)
        gx_ref[...] = dx2_ref[...] + _rms_bwd(tt * g, xh, r)

    row = lambda i: (i, 0)
    fixed = lambda i: (0, 0)
    vec = pl.BlockSpec((1, D_MODEL), fixed)
    return pl.pallas_call(
        body, name="bwd_in", grid=(seq // ts,),
        in_specs=[pl.BlockSpec((ts, 3 * D_ATTN), row), pl.BlockSpec((ts, D_ATTN), row), pl.BlockSpec((ts, D_SSM), row),
                  pl.BlockSpec((ts, D_XBC), row), pl.BlockSpec((ts, LANES), row), pl.BlockSpec((D_MODEL, D_PROJ_P), fixed),
                  pl.BlockSpec((ts, D_MODEL), row), pl.BlockSpec((ts, D_MODEL), row), vec, vec],
        out_specs=(pl.BlockSpec((ts, D_MODEL), row), vec, vec, vec),
        out_shape=(jax.ShapeDtypeStruct((seq, D_MODEL), f32), jax.ShapeDtypeStruct((1, D_MODEL), f32),
                   jax.ShapeDtypeStruct((1, D_MODEL), f32), jax.ShapeDtypeStruct((1, D_MODEL), f32)),
        compiler_params=_params(("arbitrary",)),
    )(dqkv, dza, dzs, dxbc, ddt, wp, x, dx2, gain, scale)


def _grad_w(a, b, tn, name):
    seq, m = a.shape
    n = b.shape[1]
    tk = min(512, seq)

    def body(a_ref, b_ref, o_ref):
        @pl.when(pl.program_id(1) == 0)
        def _():
            o_ref[...] = jnp.zeros_like(o_ref)

        o_ref[...] += _tn(a_ref[...], b_ref[...])

    return pl.pallas_call(
        body, name=name, grid=(n // tn, seq // tk),
        in_specs=[pl.BlockSpec((tk, m), lambda j, k: (k, 0)), pl.BlockSpec((tk, tn), lambda j, k: (k, j))],
        out_specs=pl.BlockSpec((m, tn), lambda j, k: (0, j)),
        out_shape=jax.ShapeDtypeStruct((m, n), f32), compiler_params=_params(("parallel", "arbitrary")),
    )(a, b)


def _small_finish(g_all, c_all, dmod_mine):
    def body(g_ref, c_ref, dm_ref, tot_ref, gwada_ref):
        tot = g_ref[0:1, :]
        for j in range(1, N_DEV):
            tot = tot + g_ref[j:j + 1, :]
        tot_ref[...] = tot
        cv = c_ref[...]
        gwada_ref[...] = _tn(cv * _sigmoid(cv), dm_ref[...], HI)

    vmem = pl.BlockSpec(memory_space=pltpu.VMEM)
    return pl.pallas_call(
        body, name="small_finish", in_specs=[vmem, vmem, vmem], out_specs=(vmem, vmem),
        out_shape=(jax.ShapeDtypeStruct((1, N_PACK), f32), jax.ShapeDtypeStruct((D_MODEL, dmod_mine.shape[1]), f32)),
        compiler_params=_params(),
    )(g_all, c_all, dmod_mine)


def _adamw(w, g_parts, m, v, rows, name):
    r, c = w.shape
    n_parts = g_parts.shape[0]
    bc1 = 1.0 - ADAM_B1 ** ADAM_STEP
    bc2 = 1.0 - ADAM_B2 ** ADAM_STEP

    def body(w_ref, g_ref, m_ref, v_ref, go_ref, d_ref, mo_ref, vo_ref):
        g = g_ref[0]
        for j in range(1, n_parts):
            g = g + g_ref[j]
        go_ref[...] = g
        mn = ADAM_B1 * m_ref[...] + (1.0 - ADAM_B1) * g
        vn = ADAM_B2 * v_ref[...] + (1.0 - ADAM_B2) * (g * g)
        mo_ref[...] = mn
        vo_ref[...] = vn
        d_ref[...] = -ADAM_LR * ((mn / bc1) / (jnp.sqrt(vn / bc2) + ADAM_EPS) + ADAM_WD * w_ref[...])

    blk = pl.BlockSpec((rows, c), lambda i: (i, 0))
    return pl.pallas_call(
        body, name=name, grid=(r // rows,),
        in_specs=[blk, pl.BlockSpec((n_parts, rows, c), lambda i: (0, i, 0)), blk, blk],
        out_specs=(blk, blk, blk, blk), out_shape=(jax.ShapeDtypeStruct((r, c), f32),) * 4,
        compiler_params=_params(("parallel",)),
    )(w, g_parts, m, v)


def _pad_lanes(v):
    return jnp.pad(v, ((0, 0), (0, LANES - v.shape[1])))


def kernel(x, c, w_ada, b_ada, norm_in_gain, w_in, conv_w, conv_b, dt_bias, a_log, d_skip, sb_norm_gain, ssm_norm_gain, w_out, norm_f_gain, loss_target, m_w_ada, m_b_ada, m_norm_in_gain, m_w_in, m_conv_w, m_conv_b, m_dt_bias, m_a_log, m_d_skip, m_sb_norm_gain, m_ssm_norm_gain, m_w_out, m_norm_f_gain, v_w_ada, v_b_ada, v_norm_in_gain, v_w_in, v_conv_w, v_conv_b, v_dt_bias, v_a_log, v_d_skip, v_sb_norm_gain, v_ssm_norm_gain, v_w_out, v_norm_f_gain):
    seq = x.shape[1]
    xs = x[0]
    tgt = loss_target[0]
    _, my_slot = _me()

    mod, c_all = _mod_exchange(c, w_ada[0], b_ada)
    shift, scale, gate = mod[:, :D_MODEL], mod[:, D_MODEL:2 * D_MODEL], mod[:, 2 * D_MODEL:]
    w_in_g, w_out_g, conv_w_g = _all_gather(
        [_cast_bf16(w_in[0], 128), _cast_bf16(w_out[0], 128), conv_w[0]], "gather_weights")
    w_full = jnp.transpose(w_in_g, (1, 0, 2)).reshape(D_MODEL, D_PROJ)
    wp = jnp.concatenate([w_full[:, :4 * D_ATTN], w_full[:, D_PROJ - D_SSM:], w_full[:, 4 * D_ATTN:4 * D_ATTN + D_XBC],
                          _pad_lanes(w_full[:, 4 * D_ATTN + D_XBC:4 * D_ATTN + D_XBC + N_HEADS])], axis=1)
    w_out_full = w_out_g.reshape(D_ATTN + D_SSM, D_MODEL)
    conv_w_full = jnp.transpose(conv_w_g, (1, 0, 2)).reshape(CONV_K, D_XBC)
    dtb, alog, dsk = _pad_lanes(dt_bias), _pad_lanes(a_log), _pad_lanes(d_skip)

    h, qkv, za, zs, xbc, dt_raw = _proj(xs, norm_in_gain, scale, shift, wp, seq)
    o_tot = _attn_fwd(qkv, seq)
    conv, y, states = _ssd_fwd(xbc, dt_raw, conv_w_full, conv_b, dtb, alog, dsk, seq)
    ycat, dmix, dx2, loss_p, g_nf, d_gate = _post(xs, tgt, o_tot, y, za, zs, w_out_full, gate, sb_norm_gain, ssm_norm_gain,
                                                  norm_f_gain.reshape(1, D_MODEL), seq)

    d_o, dza, dzs, dy, g_sb, g_ss = _bwd_out(dmix, w_out_full, o_tot, y, za, zs, sb_norm_gain, ssm_norm_gain, seq)
    dq, dk, dv = _attn_bwd(qkv, o_tot, d_o, seq)
    dxbc, ddt, g_cw, g_cb, g_dtb, g_al, g_dsk = _ssd_bwd(dy, conv, xbc, dt_raw, states, conv_w_full, dtb, alog, dsk, seq)
    dqkv = _qkv_grads_to_cols(dq, dk, dv, seq)
    grad_x, d_shift, d_scale, g_in = _bwd_in(dqkv, dza, dzs, dxbc, ddt, wp, xs, dx2, norm_in_gain, scale, seq)
    gw_qkv = _grad_w(h, dqkv, 512, "grad_w_qkv")
    gw_za = _grad_w(h, dza, 512, "grad_w_za")
    gw_zs = _grad_w(h, dzs, 512, "grad_w_zs")
    gw_xbc = _grad_w(h, dxbc, 512, "grad_w_xbc")
    gw_dt = _grad_w(h, ddt, LANES, "grad_w_dt")
    gw_out = _grad_w(ycat, dmix, 512, "grad_w_out")
    gw_in = jnp.concatenate([gw_qkv, gw_za, gw_xbc, gw_dt[:, :N_HEADS], gw_zs], axis=1)

    gw_in_parts, gw_out_parts = _all_to_all(
        [jnp.transpose(gw_in.reshape(D_MODEL, N_DEV, W_IN_SHARD), (1, 0, 2)),
         gw_out.reshape(N_DEV, (D_ATTN + D_SSM) // N_DEV, D_MODEL)], "scatter_grads")
    packed = jnp.concatenate([loss_p, d_shift, d_scale, d_gate, g_in, g_cb, g_dtb, g_al, g_dsk, g_sb, g_ss, g_nf,
                              g_cw.reshape(1, CONV_K * D_XBC)], axis=1)
    (packed_all,) = _all_gather([packed], "gather_small")
    packed_all = packed_all.reshape(N_DEV, N_PACK)
    n_ada = w_ada.shape[2]
    dmod_mine = lax.dynamic_slice(packed_all, (0, P_DMOD + my_slot * n_ada), (N_DEV, n_ada))
    tot, g_w_ada = _small_finish(packed_all, c_all.reshape(N_DEV, D_MODEL), dmod_mine)

    def small(w, g, m, v):
        shape = w.shape
        w2, g2, m2, v2 = (t.reshape(1, -1) for t in (w, g, m, v))
        outs = _adamw(w2, g2[None], m2, v2, 1, "adamw_small")
        return tuple(t.reshape(shape) for t in outs)

    n_cw = conv_w.shape[2]
    g_cw_tot = tot[:, P_CW:].reshape(CONV_K, D_XBC)
    g_cw_mine = lax.dynamic_slice(g_cw_tot, (0, my_slot * n_cw), (CONV_K, n_cw))
    res = {
        "w_ada": tuple(t[None] for t in _adamw(w_ada[0], g_w_ada[None], m_w_ada[0], v_w_ada[0], 128, "adamw_w_ada")),
        "b_ada": small(b_ada, tot[:, P_DMOD:P_DMOD + 3 * D_MODEL], m_b_ada, v_b_ada),
        "norm_in_gain": small(norm_in_gain, tot[:, P_GIN:P_GIN + D_MODEL], m_norm_in_gain, v_norm_in_gain),
        "w_in": tuple(t[None] for t in _adamw(w_in[0], gw_in_parts, m_w_in[0], v_w_in[0], 128, "adamw_w_in")),
        "conv_w": small(conv_w, g_cw_mine[None], m_conv_w, v_conv_w),
        "conv_b": small(conv_b, tot[:, P_CB:P_CB + D_XBC], m_conv_b, v_conv_b),
        "dt_bias": small(dt_bias, tot[:, P_DTB:P_DTB + N_HEADS], m_dt_bias, v_dt_bias),
        "a_log": small(a_log, tot[:, P_ALOG:P_ALOG + N_HEADS], m_a_log, v_a_log),
        "d_skip": small(d_skip, tot[:, P_DSK:P_DSK + N_HEADS], m_d_skip, v_d_skip),
        "sb_norm_gain": small(sb_norm_gain, tot[:, P_GSB:P_GSB + D_MODEL], m_sb_norm_gain, v_sb_norm_gain),
        "ssm_norm_gain": small(ssm_norm_gain, tot[:, P_GSS:P_GSS + D_MODEL], m_ssm_norm_gain, v_ssm_norm_gain),
        "w_out": tuple(t[None] for t in _adamw(w_out[0], gw_out_parts, m_w_out[0], v_w_out[0], 64, "adamw_w_out")),
        "norm_f_gain": small(norm_f_gain, tot[0, P_GNF:P_GNF + D_MODEL], m_norm_f_gain, v_norm_f_gain),
    }
    names = ["w_ada", "b_ada", "norm_in_gain", "w_in", "conv_w", "conv_b", "dt_bias", "a_log", "d_skip", "sb_norm_gain",
             "ssm_norm_gain", "w_out", "norm_f_gain"]
    loss = tot[0, P_LOSS]
    return (loss, grad_x[None], *[res[n][0] for n in names], *[res[n][1] for n in names],
            *[res[n][2] for n in names], *[res[n][3] for n in names])
```

```python
import functools

import jax
import jax.numpy as jnp
from jax import lax
from jax.experimental import pallas as pl
from jax.experimental.pallas import tpu as pltpu

f32 = jnp.float32
bf16 = jnp.bfloat16
MESH = pl.DeviceIdType.MESH
HI = lax.Precision.HIGHEST

N_DEV = 8
D_MODEL = 1024
D_ATTN = 1024
D_SSM = 1024
N_HEADS = 16
HEAD_DIM = 64
N_GROUPS = 2
HEADS_PER_GROUP = 8
N_STATE = 128
D_XBC = D_SSM + 2 * N_GROUPS * N_STATE
D_PROJ = 4 * D_ATTN + D_XBC + N_HEADS + D_SSM
W_IN_SHARD = D_PROJ // N_DEV
CONV_K = 4
CHUNK = 128
ATTN_Q_ROWS = 2048
LANES = 128
EPS = 1e-6
OFF_ZA = 3072
OFF_ZS = 4096
OFF_XBC = 5120
OFF_DT = 6656
D_PROJ_P = 6784
VMEM_LIMIT_BYTES = 56 * 1024 * 1024

ADAM_LR = 0.001
ADAM_B1 = 0.9
ADAM_B2 = 0.999
ADAM_EPS = 1e-08
ADAM_WD = 0.01
ADAM_STEP = 10

P_LOSS = 0
P_DMOD = 128
P_GIN = 3200
P_CB = 4224
P_DTB = 5760
P_ALOG = 5888
P_DSK = 6016
P_GSB = 6144
P_GSS = 7168
P_GNF = 8192
P_CW = 9216
N_PACK = 15360


def _params(sem=None):
    return pltpu.CompilerParams(dimension_semantics=sem, vmem_limit_bytes=VMEM_LIMIT_BYTES)


def _sigmoid(v):
    return 1.0 / (1.0 + jnp.exp(-v))


def _softplus(v):
    return jnp.maximum(v, 0.0) + jnp.log(1.0 + jnp.exp(-jnp.abs(v)))


def _nt(a, b, precision=None):
    return lax.dot_general(a, b, (((1,), (1,)), ((), ())), preferred_element_type=f32, precision=precision)


def _tn(a, b, precision=None):
    return lax.dot_general(a, b, (((0,), (0,)), ((), ())), preferred_element_type=f32, precision=precision)


def _nn(a, b, precision=None):
    return lax.dot_general(a, b, (((1,), (0,)), ((), ())), preferred_element_type=f32, precision=precision)


def _me():
    x, y, c = lax.axis_index("x"), lax.axis_index("y"), lax.axis_index("c")
    return (x, y, c), 4 * x + 2 * y + c


def _peer(k):
    x, y, c = lax.axis_index("x"), lax.axis_index("y"), lax.axis_index("c")
    px = 1 - x if (k >> 2) & 1 else x
    py = 1 - y if (k >> 1) & 1 else y
    pc = 1 - c if k & 1 else c
    return (px, py, pc), 4 * px + 2 * py + pc


def _all_gather(arrs, name):
    n = len(arrs)

    def body(*refs):
        ins, outs = refs[:n], refs[n:2 * n]
        send_sems, recv_sems, local_sems = refs[2 * n:]
        _, my_slot = _me()
        sends = []
        locals_ = []
        for a in range(n):
            loc = pltpu.make_async_copy(ins[a], outs[a].at[my_slot], local_sems.at[a])
            loc.start()
            locals_.append(loc)
            for k in range(1, N_DEV):
                peer, _ = _peer(k)
                cp = pltpu.make_async_remote_copy(src_ref=ins[a], dst_ref=outs[a].at[my_slot], send_sem=send_sems.at[a, k - 1],
                                                  recv_sem=recv_sems.at[a, k - 1], device_id=peer, device_id_type=MESH)
                cp.start()
                sends.append(cp)
        for a in range(n):
            for k in range(1, N_DEV):
                peer, peer_slot = _peer(k)
                pltpu.make_async_remote_copy(src_ref=ins[a], dst_ref=outs[a].at[peer_slot], send_sem=send_sems.at[a, k - 1],
                                             recv_sem=recv_sems.at[a, k - 1], device_id=peer, device_id_type=MESH).wait_recv()
        for cp in sends:
            cp.wait_send()
        for loc in locals_:
            loc.wait()

    any_spec = pl.BlockSpec(memory_space=pl.ANY)
    return pl.pallas_call(
        body, name=name,
        out_shape=tuple(jax.ShapeDtypeStruct((N_DEV,) + a.shape, a.dtype) for a in arrs),
        in_specs=[any_spec] * n, out_specs=tuple([any_spec] * n),
        scratch_shapes=[pltpu.SemaphoreType.DMA((n, N_DEV - 1)), pltpu.SemaphoreType.DMA((n, N_DEV - 1)),
                        pltpu.SemaphoreType.DMA((n,))],
    )(*arrs)


def _all_gather_two_level(arrs, name):
    n = len(arrs)

    def body(*refs):
        ins, outs = refs[:n], refs[n:2 * n]
        send_sems, recv_sems, local_sems = refs[2 * n:]
        x, y, c = lax.axis_index("x"), lax.axis_index("y"), lax.axis_index("c")
        me, sibling = (x, y, c), (x, y, 1 - c)
        chips = [(1 - x, y), (x, 1 - y), (1 - x, 1 - y)]

        def copy(a, k, block, to, from_input=False):
            slot = 4 * block[0] + 2 * block[1] + block[2]
            return pltpu.make_async_remote_copy(src_ref=ins[a] if from_input else outs[a].at[slot], dst_ref=outs[a].at[slot],
                                                send_sem=send_sems.at[a, k], recv_sem=recv_sems.at[a, k], device_id=to,
                                                device_id_type=MESH)

        started = []
        locals_ = []
        for a in range(n):
            loc = pltpu.make_async_copy(ins[a], outs[a].at[4 * x + 2 * y + c], local_sems.at[a])
            loc.start()
            locals_.append(loc)
            first = [copy(a, 0, me, sibling, True)] + [copy(a, 1 + j, me, (*chip, c), True) for j, chip in enumerate(chips)]
            for cp in first:
                cp.start()
            started += first
        for a in range(n):
            for j, chip in enumerate(chips):
                copy(a, 1 + j, (*chip, c), me).wait_recv()
                onward = copy(a, 4 + j, (*chip, c), sibling)
                onward.start()
                started.append(onward)
        for a in range(n):
            copy(a, 0, sibling, me).wait_recv()
            for j, chip in enumerate(chips):
                copy(a, 4 + j, (*chip, 1 - c), me).wait_recv()
        for cp in started:
            cp.wait_send()
        for loc in locals_:
            loc.wait()

    any_spec = pl.BlockSpec(memory_space=pl.ANY)
    return pl.pallas_call(
        body, name=name,
        out_shape=tuple(jax.ShapeDtypeStruct((N_DEV,) + a.shape, a.dtype) for a in arrs),
        in_specs=[any_spec] * n, out_specs=tuple([any_spec] * n),
        scratch_shapes=[pltpu.SemaphoreType.DMA((n, N_DEV - 1)), pltpu.SemaphoreType.DMA((n, N_DEV - 1)),
                        pltpu.SemaphoreType.DMA((n,))],
    )(*arrs)


def _all_to_all(arrs, name):
    n = len(arrs)

    def body(*refs):
        ins, outs = refs[:n], refs[n:2 * n]
        send_sems, recv_sems, local_sems = refs[2 * n:]
        _, my_slot = _me()
        sends = []
        locals_ = []
        for a in range(n):
            loc = pltpu.make_async_copy(ins[a].at[my_slot], outs[a].at[my_slot], local_sems.at[a])
            loc.start()
            locals_.append(loc)
            for k in range(1, N_DEV):
                peer, peer_slot = _peer(k)
                cp = pltpu.make_async_remote_copy(src_ref=ins[a].at[peer_slot], dst_ref=outs[a].at[my_slot],
                                                  send_sem=send_sems.at[a, k - 1], recv_sem=recv_sems.at[a, k - 1],
                                                  device_id=peer, device_id_type=MESH)
                cp.start()
                sends.append(cp)
        for a in range(n):
            for k in range(1, N_DEV):
                peer, peer_slot = _peer(k)
                pltpu.make_async_remote_copy(src_ref=ins[a].at[peer_slot], dst_ref=outs[a].at[peer_slot],
                                             send_sem=send_sems.at[a, k - 1], recv_sem=recv_sems.at[a, k - 1],
                                             device_id=peer, device_id_type=MESH).wait_recv()
        for cp in sends:
            cp.wait_send()
        for loc in locals_:
            loc.wait()

    any_spec = pl.BlockSpec(memory_space=pl.ANY)
    return pl.pallas_call(
        body, name=name,
        out_shape=tuple(jax.ShapeDtypeStruct(a.shape, a.dtype) for a in arrs),
        in_specs=[any_spec] * n, out_specs=tuple([any_spec] * n),
        scratch_shapes=[pltpu.SemaphoreType.DMA((n, N_DEV - 1)), pltpu.SemaphoreType.DMA((n, N_DEV - 1)),
                        pltpu.SemaphoreType.DMA((n,))],
    )(*arrs)


def _mod_exchange(c_row, w_ada, b_ada):
    n_col = w_ada.shape[1]

    def body(c_ref, w_ref, b_ref, mod_ref, call_ref, part, modp, send_sems, recv_sems):
        _, my_slot = _me()
        call_ref[my_slot] = c_ref[...]
        sends = []
        for k in range(1, N_DEV):
            peer, _ = _peer(k)
            cp = pltpu.make_async_remote_copy(src_ref=c_ref, dst_ref=call_ref.at[my_slot], send_sem=send_sems.at[0, k - 1],
                                              recv_sem=recv_sems.at[0, k - 1], device_id=peer, device_id_type=MESH)
            cp.start()
            sends.append(cp)
        for k in range(1, N_DEV):
            peer, peer_slot = _peer(k)
            pltpu.make_async_remote_copy(src_ref=c_ref, dst_ref=call_ref.at[peer_slot], send_sem=send_sems.at[0, k - 1],
                                         recv_sem=recv_sems.at[0, k - 1], device_id=peer, device_id_type=MESH).wait_recv()
        for cp in sends:
            cp.wait_send()
        w = w_ref[...]
        for b in range(N_DEV):
            cb = call_ref[b]
            part[b] = _nn(cb * _sigmoid(cb), w, HI)
        modp[my_slot] = part[my_slot]
        sends = []
        for k in range(1, N_DEV):
            peer, peer_slot = _peer(k)
            cp = pltpu.make_async_remote_copy(src_ref=part.at[peer_slot], dst_ref=modp.at[my_slot], send_sem=send_sems.at[1, k - 1],
                                              recv_sem=recv_sems.at[1, k - 1], device_id=peer, device_id_type=MESH)
            cp.start()
            sends.append(cp)
        for k in range(1, N_DEV):
            peer, peer_slot = _peer(k)
            pltpu.make_async_remote_copy(src_ref=part.at[peer_slot], dst_ref=modp.at[peer_slot], send_sem=send_sems.at[1, k - 1],
                                         recv_sem=recv_sems.at[1, k - 1], device_id=peer, device_id_type=MESH).wait_recv()
        for cp in sends:
            cp.wait_send()
        for j in range(N_DEV):
            mod_ref[:, j * n_col:(j + 1) * n_col] = modp[j] + b_ref[:, j * n_col:(j + 1) * n_col]

    vmem = pl.BlockSpec(memory_space=pltpu.VMEM)
    return pl.pallas_call(
        body, name="mod_exchange",
        out_shape=(jax.ShapeDtypeStruct((1, N_DEV * n_col), f32), jax.ShapeDtypeStruct((N_DEV, 1, D_MODEL), f32)),
        in_specs=[vmem, vmem, vmem], out_specs=(vmem, vmem),
        scratch_shapes=[pltpu.VMEM((N_DEV, 1, n_col), f32), pltpu.VMEM((N_DEV, 1, n_col), f32),
                        pltpu.SemaphoreType.DMA((2, N_DEV - 1)), pltpu.SemaphoreType.DMA((2, N_DEV - 1))],
        compiler_params=_params(),
    )(c_row, w_ada, b_ada)


def _cast_bf16(a, rows):
    r, c = a.shape

    def body(a_ref, o_ref):
        o_ref[...] = a_ref[...].astype(bf16)

    return pl.pallas_call(
        body, name="cast_bf16", grid=(r // rows,),
        in_specs=[pl.BlockSpec((rows, c), lambda i: (i, 0))], out_specs=pl.BlockSpec((rows, c), lambda i: (i, 0)),
        out_shape=jax.ShapeDtypeStruct((r, c), bf16), compiler_params=_params(("parallel",)),
    )(a)


def _proj(x, gain, scale, shift, wp, seq):
    ts = 256

    def body(x_ref, g_ref, sc_ref, sh_ref, w_ref, h_ref, qkv_ref, za_ref, zs_ref, xbc_ref, dt_ref):
        xv = x_ref[...]
        r = lax.rsqrt(jnp.mean(xv * xv, axis=-1, keepdims=True) + EPS)
        hb = ((xv * r * g_ref[...]) * (1.0 + sc_ref[...]) + sh_ref[...]).astype(bf16)
        h_ref[...] = hb
        for cb in range(3 * D_ATTN // 256):
            res = jnp.dot(hb, w_ref[:, cb * 256:(cb + 1) * 256], preferred_element_type=f32)
            for u in range(4):
                qkv_ref[cb * 4 + u] = res[:, u * HEAD_DIM:(u + 1) * HEAD_DIM].astype(bf16)
        for out_ref, off, width in ((za_ref, OFF_ZA, D_ATTN), (zs_ref, OFF_ZS, D_SSM), (xbc_ref, OFF_XBC, D_XBC), (dt_ref, OFF_DT, LANES)):
            for cc in range(0, width, 512):
                wd = min(512, width - cc)
                out_ref[:, cc:cc + wd] = jnp.dot(hb, w_ref[:, off + cc:off + cc + wd], preferred_element_type=f32)

    row = lambda i: (i, 0)
    fixed = lambda i: (0, 0)
    return pl.pallas_call(
        body, name="proj", grid=(seq // ts,),
        in_specs=[pl.BlockSpec((ts, D_MODEL), row), pl.BlockSpec((1, D_MODEL), fixed), pl.BlockSpec((1, D_MODEL), fixed),
                  pl.BlockSpec((1, D_MODEL), fixed), pl.BlockSpec((D_MODEL, D_PROJ_P), fixed)],
        out_specs=(pl.BlockSpec((ts, D_MODEL), row), pl.BlockSpec((3 * N_HEADS, ts, HEAD_DIM), lambda i: (0, i, 0)),
                   pl.BlockSpec((ts, D_ATTN), row), pl.BlockSpec((ts, D_SSM), row), pl.BlockSpec((ts, D_XBC), row),
                   pl.BlockSpec((ts, LANES), row)),
        out_shape=(jax.ShapeDtypeStruct((seq, D_MODEL), bf16), jax.ShapeDtypeStruct((3 * N_HEADS, seq, HEAD_DIM), bf16),
                   jax.ShapeDtypeStruct((seq, D_ATTN), f32), jax.ShapeDtypeStruct((seq, D_SSM), f32),
                   jax.ShapeDtypeStruct((seq, D_XBC), f32), jax.ShapeDtypeStruct((seq, LANES), f32)),
        compiler_params=_params(("arbitrary",)),
    )(x, gain, scale, shift, wp)


def _log_sigmoids(z):
    l1p = jnp.log(1.0 + jnp.exp(-jnp.abs(z)))
    return jnp.minimum(z, 0.0) - l1p, -jnp.maximum(z, 0.0) - l1p


def _split_bf16(v):
    hi = v.astype(bf16)
    return hi, (v - hi.astype(f32)).astype(bf16)


def _attn_fwd(qkv, seq):
    t = CHUNK
    tq = min(ATTN_Q_ROWS, seq)
    nd = tq // t

    def body(q_ref, k_ref, v_ref, o_ref):
        i = pl.program_id(1)
        q = q_ref[0] * 0.125
        ur = lax.broadcasted_iota(jnp.int32, (2 * t, t), 0)
        upper = ((ur & (t - 1)) > lax.broadcasted_iota(jnp.int32, (2 * t, t), 1)).astype(bf16)

        def tile(j, q_s, acc, run, masked):
            n = q_s.shape[0]
            start = pl.multiple_of(j * t, t)
            k = k_ref[0, pl.ds(start, t), :]
            v = v_ref[0, pl.ds(start, t), :]
            z = _nt(q_s, k)
            lb, lom = _log_sigmoids(z)
            if masked:
                keep = lax.broadcasted_iota(jnp.int32, (n, t), 1) < lax.broadcasted_iota(jnp.int32, (n, t), 0)
                lom = jnp.where(keep, lom, 0.0)
            tail = jnp.dot(jnp.concatenate(_split_bf16(lom), axis=1), upper, preferred_element_type=f32)
            a = lb + tail + run
            if masked:
                a = jnp.where(keep, a, -jnp.inf)
            w = jnp.exp(a)
            acc = acc + jnp.dot(w.astype(bf16), v, preferred_element_type=f32)
            run = run + tail[:, 0:1] + lom[:, 0:1]
            return acc, run

        acc, run = jnp.zeros((tq, HEAD_DIM), f32), jnp.zeros((tq, 1), f32)
        for jj in reversed(range(nd)):
            r0 = jj * t
            acc_s, run_s = tile(i * nd + jj, q[r0:], acc[r0:], run[r0:], True)
            acc = acc_s if r0 == 0 else jnp.concatenate([acc[:r0], acc_s], axis=0)
            run = run_s if r0 == 0 else jnp.concatenate([run[:r0], run_s], axis=0)
        acc, run = lax.fori_loop(0, i * nd, lambda n, cr: tile(i * nd - 1 - n, q, cr[0], cr[1], False), (acc, run))
        o_ref[0] = jnp.concatenate([acc, jnp.broadcast_to(run, (tq, HEAD_DIM))], axis=1)

    return pl.pallas_call(
        body, name="attn_fwd", grid=(N_HEADS, seq // tq),
        in_specs=[pl.BlockSpec((1, tq, HEAD_DIM), lambda h, i: (h, i, 0)),
                  pl.BlockSpec((1, seq, HEAD_DIM), lambda h, i: (N_HEADS + h, 0, 0)),
                  pl.BlockSpec((1, seq, HEAD_DIM), lambda h, i: (2 * N_HEADS + h, 0, 0))],
        out_specs=pl.BlockSpec((1, tq, 2 * HEAD_DIM), lambda h, i: (h, i, 0)),
        out_shape=jax.ShapeDtypeStruct((N_HEADS, seq, 2 * HEAD_DIM), f32),
        compiler_params=_params(("parallel", "arbitrary")),
    )(qkv, qkv, qkv)


def _attn_bwd(qkv, o_tot, d_o, seq):
    t = CHUNK
    tq = min(ATTN_Q_ROWS, seq)
    nd = tq // t
    nk = seq // t

    def body(q_ref, k_ref, v_ref, ot_ref, do_ref, dq_ref, dkt_ref, dvt_ref):
        i = pl.program_id(1)

        @pl.when(i == 0)
        def _():
            dkt_ref[...] = jnp.zeros_like(dkt_ref)
            dvt_ref[...] = jnp.zeros_like(dvt_ref)

        q = q_ref[0] * 0.125
        d_out = do_ref[0]
        total = ot_ref[0][:, HEAD_DIM:HEAD_DIM + 1]
        eye = (lax.broadcasted_iota(jnp.int32, (HEAD_DIM, HEAD_DIM), 0)
               == lax.broadcasted_iota(jnp.int32, (HEAD_DIM, HEAD_DIM), 1)).astype(bf16)
        q_t = _nt(eye, q).astype(bf16)
        do_t = _nt(eye, d_out).astype(bf16)
        ur = lax.broadcasted_iota(jnp.int32, (t, t), 0)
        uc = lax.broadcasted_iota(jnp.int32, (t, t), 1)
        ur2 = lax.broadcasted_iota(jnp.int32, (2 * t, t), 0) & (t - 1)
        incl = (ur2 <= lax.broadcasted_iota(jnp.int32, (2 * t, t), 1)).astype(bf16)
        before = (ur < uc).astype(bf16)

        def tile(j, r0, r1, dq, pre, dpre, masked):
            q_s, do_s, tot_s = q[r0:r1], d_out[r0:r1], total[r0:r1]
            n = q_s.shape[0]
            start = pl.multiple_of(j * t, t)
            k = k_ref[0, pl.ds(start, t), :]
            v = v_ref[0, pl.ds(start, t), :]
            z = _nt(q_s, k)
            lb, lom = _log_sigmoids(z)
            if masked:
                keep = lax.broadcasted_iota(jnp.int32, (n, t), 1) < lax.broadcasted_iota(jnp.int32, (n, t), 0)
                lom = jnp.where(keep, lom, 0.0)
            pin = jnp.dot(jnp.concatenate(_split_bf16(lom), axis=1), incl, preferred_element_type=f32)
            a = lb + ((tot_s - pre) - pin)
            if masked:
                a = jnp.where(keep, a, -jnp.inf)
            w = jnp.exp(a)
            d_a = _nt(do_s, v) * w
            d_lom_local = jnp.dot(d_a.astype(bf16), before, preferred_element_type=f32)
            d_lom = d_lom_local + dpre
            sig = jnp.exp(lb)
            dz = d_a * (1.0 - sig) - d_lom * sig
            if masked:
                dz = jnp.where(keep, dz, 0.0)
            dzb = dz.astype(bf16)
            dq = dq + jnp.dot(dzb, k, preferred_element_type=f32)
            dkt_ref[0, j] += jnp.dot(q_t[:, r0:r1], dzb, preferred_element_type=f32)
            dvt_ref[0, j] += jnp.dot(do_t[:, r0:r1], w.astype(bf16), preferred_element_type=f32)
            pre = pre + pin[:, t - 1:t]
            dpre = dpre + d_lom_local[:, t - 1:t] + d_a[:, t - 1:t]
            return dq, pre, dpre

        carry = (jnp.zeros((tq, HEAD_DIM), f32), jnp.zeros((tq, 1), f32), jnp.zeros((tq, 1), f32))
        carry = lax.fori_loop(0, i * nd, lambda n, cr: tile(n, 0, tq, cr[0], cr[1], cr[2], False), carry)
        for jj in range(nd):
            r0 = jj * t
            part = tile(i * nd + jj, r0, tq, *(c[r0:] for c in carry), True)
            carry = part if r0 == 0 else tuple(jnp.concatenate([c[:r0], p], axis=0) for c, p in zip(carry, part))
        dq_ref[0] = carry[0] * 0.125

    blk = pl.BlockSpec((1, tq, HEAD_DIM), lambda h, i: (h, i, 0))
    full_t = pl.BlockSpec((1, nk, HEAD_DIM, t), lambda h, i: (h, 0, 0, 0))
    return pl.pallas_call(
        body, name="attn_bwd", grid=(N_HEADS, seq // tq),
        in_specs=[blk, pl.BlockSpec((1, seq, HEAD_DIM), lambda h, i: (N_HEADS + h, 0, 0)),
                  pl.BlockSpec((1, seq, HEAD_DIM), lambda h, i: (2 * N_HEADS + h, 0, 0)),
                  pl.BlockSpec((1, tq, 2 * HEAD_DIM), lambda h, i: (h, i, 0)), blk],
        out_specs=(blk, full_t, full_t),
        out_shape=(jax.ShapeDtypeStruct((N_HEADS, seq, HEAD_DIM), f32),
                   jax.ShapeDtypeStruct((N_HEADS, nk, HEAD_DIM, t), f32), jax.ShapeDtypeStruct((N_HEADS, nk, HEAD_DIM, t), f32)),
        compiler_params=_params(("parallel", "arbitrary")),
    )(qkv, qkv, qkv, o_tot, d_o)


def _ssd_common(conv, dt_raw, dtb, alog):
    t = CHUNK
    sg = _sigmoid(conv)
    act = conv * sg
    dt_pre = dt_raw + dtb
    dt = _softplus(dt_pre)
    a = -jnp.exp(alog)
    row = lax.broadcasted_iota(jnp.int32, (t, t), 0)
    col = lax.broadcasted_iota(jnp.int32, (t, t), 1)
    causal = row >= col
    ac = _nn(causal.astype(f32), dt * a, HI)
    ac_t = _nt((row == col).astype(f32), ac, HI)
    ac_last = ac[t - 1:t, :]
    return sg, act, dt_pre, dt, a, causal, ac, ac_t, ac_last, jnp.exp(ac), jnp.exp(ac_last - ac), jnp.exp(ac_last)


def _ssd_fwd(xbc, dt_raw, conv_w, conv_b, dtb, alog, dsk, seq):
    t = CHUNK
    n_chunks = seq // t

    def body(x_ref, dt_ref, cw_ref, cb_ref, dtb_ref, al_ref, dsk_ref, conv_ref, y_ref, st_ref, prev, state):
        c = pl.program_id(0)

        @pl.when(c == 0)
        def _():
            prev[...] = jnp.zeros_like(prev)
            state[...] = jnp.zeros_like(state)

        cur = x_ref[...]
        pv = prev[...]
        rows = lax.broadcasted_iota(jnp.int32, (t, D_XBC), 0)
        conv = cur * cw_ref[CONV_K - 1:CONV_K, :] + cb_ref[...]
        for m in range(1, CONV_K):
            shifted = jnp.where(rows < m, pltpu.roll(pv, m, 0), pltpu.roll(cur, m, 0))
            conv = conv + shifted * cw_ref[CONV_K - 1 - m:CONV_K - m, :]
        prev[...] = cur
        conv_ref[...] = conv
        _, act, _, dt, _, causal, ac, ac_t, _, e_ac, dte, cdec = _ssd_common(conv, dt_ref[...], dtb_ref[...], al_ref[...])
        dskv = dsk_ref[...]
        ys = []
        for g in range(N_GROUPS):
            bg = act[:, D_SSM + g * N_STATE:D_SSM + (g + 1) * N_STATE].astype(bf16)
            cg = act[:, D_SSM + (N_GROUPS + g) * N_STATE:D_SSM + (N_GROUPS + g + 1) * N_STATE].astype(bf16)
            gm = _nt(cg, bg)
            for r in range(HEADS_PER_GROUP):
                h = g * HEADS_PER_GROUP + r
                xh = act[:, h * HEAD_DIM:(h + 1) * HEAD_DIM]
                xd = xh * dt[:, h:h + 1]
                seg = ac[:, h:h + 1] - ac_t[h:h + 1, :]
                lm = jnp.exp(jnp.where(causal, seg, -jnp.inf))
                y_diag = jnp.dot((gm * lm).astype(bf16), xd.astype(bf16), preferred_element_type=f32)
                hp = state[h]
                st_ref[0, h] = hp
                zo = _nt(cg, hp.astype(bf16))
                ys.append(y_diag + zo * e_ac[:, h:h + 1] + xh * dskv[:, h:h + 1])
                sc = _tn((xd * dte[:, h:h + 1]).astype(bf16), bg)
                state[h] = hp * cdec[:, h:h + 1] + sc
        y_ref[...] = jnp.concatenate(ys, axis=1)

    row = lambda c: (c, 0)
    fixed = lambda c: (0, 0)
    return pl.pallas_call(
        body, name="ssd_fwd", grid=(n_chunks,),
        in_specs=[pl.BlockSpec((t, D_XBC), row), pl.BlockSpec((t, LANES), row), pl.BlockSpec((CONV_K, D_XBC), fixed),
                  pl.BlockSpec((1, D_XBC), fixed), pl.BlockSpec((1, LANES), fixed), pl.BlockSpec((1, LANES), fixed),
                  pl.BlockSpec((1, LANES), fixed)],
        out_specs=(pl.BlockSpec((t, D_XBC), row), pl.BlockSpec((t, D_SSM), row),
                   pl.BlockSpec((1, N_HEADS, HEAD_DIM, N_STATE), lambda c: (c, 0, 0, 0))),
        out_shape=(jax.ShapeDtypeStruct((seq, D_XBC), f32), jax.ShapeDtypeStruct((seq, D_SSM), f32),
                   jax.ShapeDtypeStruct((n_chunks, N_HEADS, HEAD_DIM, N_STATE), f32)),
        scratch_shapes=[pltpu.VMEM((t, D_XBC), f32), pltpu.VMEM((N_HEADS, HEAD_DIM, N_STATE), f32)],
        compiler_params=_params(("arbitrary",)),
    )(xbc, dt_raw, conv_w, conv_b, dtb, alog, dsk)


def _ssd_bwd(dy, conv, xbc, dt_raw, states, conv_w, dtb, alog, dsk, seq):
    t = CHUNK
    n_chunks = seq // t

    def body(dy_ref, conv_ref, x_ref, dt_ref, st_ref, cw_ref, dtb_ref, al_ref, dsk_ref,
             dx_ref, ddt_ref, gcw_ref, gcb_ref, gdtb_ref, gal_ref, gdsk_ref, d_state, d_conv_next):
        c = pl.program_id(0)

        @pl.when(c == 0)
        def _():
            d_state[...] = jnp.zeros_like(d_state)
            d_conv_next[...] = jnp.zeros_like(d_conv_next)
            gcw_ref[...] = jnp.zeros_like(gcw_ref)
            gcb_ref[...] = jnp.zeros_like(gcb_ref)
            gdtb_ref[...] = jnp.zeros_like(gdtb_ref)
            gal_ref[...] = jnp.zeros_like(gal_ref)
            gdsk_ref[...] = jnp.zeros_like(gdsk_ref)

        conv = conv_ref[...]
        sg, act, dt_pre, dt, a, causal, ac, ac_t, _, e_ac, dte, cdec = _ssd_common(conv, dt_ref[...], dtb_ref[...], al_ref[...])
        dskv = dsk_ref[...]
        dyv = dy_ref[...]
        lane = lax.broadcasted_iota(jnp.int32, (1, LANES), 1)
        last_row = (lax.broadcasted_iota(jnp.int32, (t, 1), 0) == t - 1).astype(f32)
        sub = lax.broadcasted_iota(jnp.int32, (t, 1), 0)
        col_sums = jnp.zeros((t, t), f32)
        d_ac = jnp.zeros((t, LANES), f32)
        d_dt = jnp.zeros((t, LANES), f32)
        g_dsk = jnp.zeros((1, LANES), f32)
        dxs = []
        dbs = []
        dcs = []
        for g in range(N_GROUPS):
            bg = act[:, D_SSM + g * N_STATE:D_SSM + (g + 1) * N_STATE].astype(bf16)
            cg = act[:, D_SSM + (N_GROUPS + g) * N_STATE:D_SSM + (N_GROUPS + g + 1) * N_STATE].astype(bf16)
            gm = _nt(cg, bg)
            d_gm = jnp.zeros((t, t), f32)
            d_b = jnp.zeros((t, N_STATE), f32)
            d_c = jnp.zeros((t, N_STATE), f32)
            for r in range(HEADS_PER_GROUP):
                h = g * HEADS_PER_GROUP + r
                onehot = (lane == h).astype(f32)
                xh = act[:, h * HEAD_DIM:(h + 1) * HEAD_DIM]
                dth = dt[:, h:h + 1]
                xd = xh * dth
                xdb = xd.astype(bf16)
                seg = ac[:, h:h + 1] - ac_t[h:h + 1, :]
                lm = jnp.exp(jnp.where(causal, seg, -jnp.inf))
                mm = gm * lm
                hp = st_ref[0, h]
                hpb = hp.astype(bf16)
                d_hn = d_state[h]
                d_hnb = d_hn.astype(bf16)
                d_yh = dyv[:, h * HEAD_DIM:(h + 1) * HEAD_DIM]
                d_yb = d_yh.astype(bf16)
                g_dsk = g_dsk + jnp.sum(d_yh * xh) * onehot
                d_mm = _nt(d_yb, xdb)
                d_xd = _tn(mm.astype(bf16), d_yb)
                d_gm = d_gm + d_mm * lm
                d_seg = d_mm * mm
                d_ac_h = jnp.sum(d_seg, axis=1, keepdims=True)
                col_sums = col_sums + (sub == h).astype(f32) * jnp.sum(d_seg, axis=0, keepdims=True)
                e_h = e_ac[:, h:h + 1]
                zo = _nt(cg, hpb)
                d_zo = d_yh * e_h
                d_zob = d_zo.astype(bf16)
                d_ac_h = d_ac_h + jnp.sum(d_yh * zo, axis=1, keepdims=True) * e_h
                d_c = d_c + jnp.dot(d_zob, hpb, preferred_element_type=f32)
                cd = cdec[:, h:h + 1]
                d_hp = _tn(d_zob, cg) + d_hn * cd
                d_last = jnp.sum(d_hn * hp) * cd
                dte_h = dte[:, h:h + 1]
                d_w = _nt(bg, d_hnb)
                d_b = d_b + jnp.dot((xd * dte_h).astype(bf16), d_hnb, preferred_element_type=f32)
                d_xd = d_xd + d_w * dte_h
                d_dte = jnp.sum(d_w * xd, axis=1, keepdims=True) * dte_h
                d_last = d_last + jnp.sum(d_dte)
                d_ac_h = d_ac_h - d_dte + d_last * last_row
                d_state[h] = d_hp
                dxs.append(d_yh * dskv[:, h:h + 1] + d_xd * dth)
                d_dt = d_dt + jnp.sum(d_xd * xh, axis=1, keepdims=True) * onehot
                d_ac = d_ac + d_ac_h * onehot
            d_gmb = d_gm.astype(bf16)
            dcs.append(d_c + jnp.dot(d_gmb, bg, preferred_element_type=f32))
            dbs.append(d_b + _tn(d_gmb, cg))
        sq_row = lax.broadcasted_iota(jnp.int32, (t, t), 0)
        sq_col = lax.broadcasted_iota(jnp.int32, (t, t), 1)
        d_ac = d_ac - _nt((sq_row == sq_col).astype(f32), col_sums, HI)
        d_ld = _nn((sq_col >= sq_row).astype(f32), d_ac, HI)
        d_dt = d_dt + d_ld * a
        gal_ref[...] += jnp.sum(d_ld * dt, axis=0, keepdims=True) * a
        gdsk_ref[...] += g_dsk
        d_dt_raw = d_dt * _sigmoid(dt_pre)
        ddt_ref[...] = d_dt_raw.astype(bf16)
        gdtb_ref[...] += jnp.sum(d_dt_raw, axis=0, keepdims=True)
        d_act = jnp.concatenate(dxs + dbs + dcs, axis=1)
        d_conv = d_act * (sg * (1.0 + conv * (1.0 - sg)))
        gcb_ref[...] += jnp.sum(d_conv, axis=0, keepdims=True)
        nxt = d_conv_next[...]
        rows = lax.broadcasted_iota(jnp.int32, (t, D_XBC), 0)
        xraw = x_ref[...]
        d_x = d_conv * cw_ref[CONV_K - 1:CONV_K, :]
        gcw_ref[pl.ds(CONV_K - 1, 1), :] += jnp.sum(xraw * d_conv, axis=0, keepdims=True)
        for m in range(1, CONV_K):
            ahead = jnp.where(rows >= t - m, pltpu.roll(nxt, t - m, 0), pltpu.roll(d_conv, t - m, 0))
            d_x = d_x + ahead * cw_ref[CONV_K - 1 - m:CONV_K - m, :]
            gcw_ref[pl.ds(CONV_K - 1 - m, 1), :] += jnp.sum(xraw * ahead, axis=0, keepdims=True)
        d_conv_next[...] = d_conv
        dx_ref[...] = d_x.astype(bf16)

    rev = lambda c: (n_chunks - 1 - c, 0)
    fixed = lambda c: (0, 0)
    return pl.pallas_call(
        body, name="ssd_bwd", grid=(n_chunks,),
        in_specs=[pl.BlockSpec((t, D_SSM), rev), pl.BlockSpec((t, D_XBC), rev), pl.BlockSpec((t, D_XBC), rev),
                  pl.BlockSpec((t, LANES), rev), pl.BlockSpec((1, N_HEADS, HEAD_DIM, N_STATE), lambda c: (n_chunks - 1 - c, 0, 0, 0)),
                  pl.BlockSpec((CONV_K, D_XBC), fixed), pl.BlockSpec((1, LANES), fixed), pl.BlockSpec((1, LANES), fixed),
                  pl.BlockSpec((1, LANES), fixed)],
        out_specs=(pl.BlockSpec((t, D_XBC), rev), pl.BlockSpec((t, LANES), rev), pl.BlockSpec((CONV_K, D_XBC), fixed),
                   pl.BlockSpec((1, D_XBC), fixed), pl.BlockSpec((1, LANES), fixed), pl.BlockSpec((1, LANES), fixed),
                   pl.BlockSpec((1, LANES), fixed)),
        out_shape=(jax.ShapeDtypeStruct((seq, D_XBC), bf16), jax.ShapeDtypeStruct((seq, LANES), bf16),
                   jax.ShapeDtypeStruct((CONV_K, D_XBC), f32), jax.ShapeDtypeStruct((1, D_XBC), f32),
                   jax.ShapeDtypeStruct((1, LANES), f32), jax.ShapeDtypeStruct((1, LANES), f32), jax.ShapeDtypeStruct((1, LANES), f32)),
        scratch_shapes=[pltpu.VMEM((N_HEADS, HEAD_DIM, N_STATE), f32), pltpu.VMEM((t, D_XBC), f32)],
        compiler_params=_params(("arbitrary",)),
    )(dy, conv, xbc, dt_raw, states, conv_w, dtb, alog, dsk)


def _heads_to_cols(ref, width=HEAD_DIM):
    return jnp.concatenate([ref[h][:, :width] for h in range(N_HEADS)], axis=1)


def _silu_and_grad(z):
    sg = _sigmoid(z)
    return z * sg, sg * (1.0 + z * (1.0 - sg))


def _rms(v):
    return lax.rsqrt(jnp.mean(v * v, axis=-1, keepdims=True) + EPS)


def _rms_bwd(d_hat, hat, r):
    return r * (d_hat - hat * jnp.mean(d_hat * hat, axis=-1, keepdims=True))


def _post(x, target, o_tot, y, za, zs, w_out, gate, g_sb, g_ssm, g_f, seq):
    ts = 256

    def body(x_ref, t_ref, o_ref, y_ref, za_ref, zs_ref, w_ref, gate_ref, gsb_ref, gss_ref, gf_ref,
             ycat_ref, dmix_ref, dx2_ref, loss_ref, gnf_ref, dgate_ref):
        @pl.when(pl.program_id(0) == 0)
        def _():
            loss_ref[...] = jnp.zeros_like(loss_ref)
            gnf_ref[...] = jnp.zeros_like(gnf_ref)
            dgate_ref[...] = jnp.zeros_like(dgate_ref)

        o = _heads_to_cols(o_ref)
        zav = za_ref[...]
        ya = (o * _rms(o) * gsb_ref[...]) * (zav * _sigmoid(zav))
        zsv = zs_ref[...]
        u = y_ref[...] * (zsv * _sigmoid(zsv))
        ys = u * _rms(u) * gss_ref[...]
        yab, ysb = ya.astype(bf16), ys.astype(bf16)
        ycat_ref[:, :D_ATTN] = yab
        ycat_ref[:, D_ATTN:] = ysb
        mixed = (jnp.dot(yab, w_ref[:D_ATTN, :], preferred_element_type=f32)
                 + jnp.dot(ysb, w_ref[D_ATTN:, :], preferred_element_type=f32))
        gate_v = gate_ref[...]
        x2 = x_ref[...] + gate_v * mixed
        r2 = _rms(x2)
        xh = x2 * r2
        gf = gf_ref[...]
        diff = xh * gf - t_ref[...]
        loss_ref[...] += jnp.sum(diff * diff) * (0.5 / D_MODEL)
        d_out = diff * (1.0 / D_MODEL)
        gnf_ref[...] += jnp.sum(d_out * xh, axis=0, keepdims=True)
        dx2 = _rms_bwd(d_out * gf, xh, r2)
        dx2_ref[...] = dx2
        dgate_ref[...] += jnp.sum(dx2 * mixed, axis=0, keepdims=True)
        dmix_ref[...] = (dx2 * gate_v).astype(bf16)

    row = lambda i: (i, 0)
    fixed = lambda i: (0, 0)
    vec = pl.BlockSpec((1, D_MODEL), fixed)
    return pl.pallas_call(
        body, name="post", grid=(seq // ts,),
        in_specs=[pl.BlockSpec((ts, D_MODEL), row), pl.BlockSpec((ts, D_MODEL), row),
                  pl.BlockSpec((N_HEADS, ts, 2 * HEAD_DIM), lambda i: (0, i, 0)), pl.BlockSpec((ts, D_SSM), row),
                  pl.BlockSpec((ts, D_ATTN), row), pl.BlockSpec((ts, D_SSM), row), pl.BlockSpec((D_ATTN + D_SSM, D_MODEL), fixed),
                  vec, vec, vec, vec],
        out_specs=(pl.BlockSpec((ts, D_ATTN + D_SSM), row), pl.BlockSpec((ts, D_MODEL), row), pl.BlockSpec((ts, D_MODEL), row),
                   pl.BlockSpec((1, LANES), fixed), vec, vec),
        out_shape=(jax.ShapeDtypeStruct((seq, D_ATTN + D_SSM), bf16), jax.ShapeDtypeStruct((seq, D_MODEL), bf16),
                   jax.ShapeDtypeStruct((seq, D_MODEL), f32), jax.ShapeDtypeStruct((1, LANES), f32),
                   jax.ShapeDtypeStruct((1, D_MODEL), f32), jax.ShapeDtypeStruct((1, D_MODEL), f32)),
        compiler_params=_params(("arbitrary",)),
    )(x, target, o_tot, y, za, zs, w_out, gate, g_sb, g_ssm, g_f)


def _bwd_out(dmix, w_out, o_tot, y, za, zs, g_sb, g_ssm, seq):
    ts = 256

    def body(dm_ref, w_ref, o_ref, y_ref, za_ref, zs_ref, gsb_ref, gss_ref, do_ref, dza_ref, dzs_ref, dy_ref, ggsb_ref, ggss_ref):
        @pl.when(pl.program_id(0) == 0)
        def _():
            ggsb_ref[...] = jnp.zeros_like(ggsb_ref)
            ggss_ref[...] = jnp.zeros_like(ggss_ref)

        dm = dm_ref[...]
        d_ya = _nt(dm, w_ref[:D_ATTN, :])
        d_ys = _nt(dm, w_ref[D_ATTN:, :])
        o = _heads_to_cols(o_ref)
        ro = _rms(o)
        oh = o * ro
        sa, dsa = _silu_and_grad(za_ref[...])
        gsb = gsb_ref[...]
        dza_ref[...] = (d_ya * oh * gsb * dsa).astype(bf16)
        ggsb_ref[...] += jnp.sum(d_ya * oh * sa, axis=0, keepdims=True)
        d_o = _rms_bwd(d_ya * gsb * sa, oh, ro)
        for h in range(N_HEADS):
            do_ref[h] = d_o[:, h * HEAD_DIM:(h + 1) * HEAD_DIM].astype(bf16)
        yv = y_ref[...]
        sz, dsz = _silu_and_grad(zs_ref[...])
        u = yv * sz
        ru = _rms(u)
        uh = u * ru
        ggss_ref[...] += jnp.sum(d_ys * uh, axis=0, keepdims=True)
        du = _rms_bwd(d_ys * gss_ref[...], uh, ru)
        dy_ref[...] = du * sz
        dzs_ref[...] = (du * yv * dsz).astype(bf16)

    row = lambda i: (i, 0)
    fixed = lambda i: (0, 0)
    vec = pl.BlockSpec((1, D_MODEL), fixed)
    return pl.pallas_call(
        body, name="bwd_out", grid=(seq // ts,),
        in_specs=[pl.BlockSpec((ts, D_MODEL), row), pl.BlockSpec((D_ATTN + D_SSM, D_MODEL), fixed),
                  pl.BlockSpec((N_HEADS, ts, 2 * HEAD_DIM), lambda i: (0, i, 0)), pl.BlockSpec((ts, D_SSM), row),
                  pl.BlockSpec((ts, D_ATTN), row), pl.BlockSpec((ts, D_SSM), row), vec, vec],
        out_specs=(pl.BlockSpec((N_HEADS, ts, HEAD_DIM), lambda i: (0, i, 0)), pl.BlockSpec((ts, D_ATTN), row),
                   pl.BlockSpec((ts, D_SSM), row), pl.BlockSpec((ts, D_SSM), row), vec, vec),
        out_shape=(jax.ShapeDtypeStruct((N_HEADS, seq, HEAD_DIM), bf16), jax.ShapeDtypeStruct((seq, D_ATTN), bf16),
                   jax.ShapeDtypeStruct((seq, D_SSM), bf16), jax.ShapeDtypeStruct((seq, D_SSM), f32),
                   jax.ShapeDtypeStruct((1, D_MODEL), f32), jax.ShapeDtypeStruct((1, D_MODEL), f32)),
        compiler_params=_params(("arbitrary",)),
    )(dmix, w_out, o_tot, y, za, zs, g_sb, g_ssm)


def _qkv_grads_to_cols(dq, dkt, dvt, seq):
    ts = 256
    nb = ts // CHUNK

    def body(dq_ref, dkt_ref, dvt_ref, out_ref):
        out_ref[:, :D_ATTN] = _heads_to_cols(dq_ref).astype(bf16)
        eye = (lax.broadcasted_iota(jnp.int32, (CHUNK, CHUNK), 0) == lax.broadcasted_iota(jnp.int32, (CHUNK, CHUNK), 1)).astype(bf16)
        for p, ref in ((1, dkt_ref), (2, dvt_ref)):
            for b in range(nb):
                cols = [_nt(eye, ref[h, b].astype(bf16)) for h in range(N_HEADS)]
                out_ref[b * CHUNK:(b + 1) * CHUNK, p * D_ATTN:(p + 1) * D_ATTN] = jnp.concatenate(cols, axis=1).astype(bf16)

    blk = pl.BlockSpec((N_HEADS, ts, HEAD_DIM), lambda i: (0, i, 0))
    blk_t = pl.BlockSpec((N_HEADS, nb, HEAD_DIM, CHUNK), lambda i: (0, i, 0, 0))
    return pl.pallas_call(
        body, name="qkv_grads_to_cols", grid=(seq // ts,), in_specs=[blk, blk_t, blk_t],
        out_specs=pl.BlockSpec((ts, 3 * D_ATTN), lambda i: (i, 0)),
        out_shape=jax.ShapeDtypeStruct((seq, 3 * D_ATTN), bf16), compiler_params=_params(("parallel",)),
    )(dq, dkt, dvt)


def _bwd_in(dqkv, dza, dzs, dxbc, ddt, wp, x, dx2, gain, scale, seq):
    ts = 256
    pieces = ((0, 0, 3 * D_ATTN), (1, OFF_ZA, D_ATTN), (2, OFF_ZS, D_SSM), (3, OFF_XBC, D_XBC), (4, OFF_DT, LANES))

    def body(dqkv_ref, dza_ref, dzs_ref, dxbc_ref, ddt_ref, w_ref, x_ref, dx2_ref, g_ref, sc_ref,
             gx_ref, dshift_ref, dscale_ref, ggain_ref):
        @pl.when(pl.program_id(0) == 0)
        def _():
            dshift_ref[...] = jnp.zeros_like(dshift_ref)
            dscale_ref[...] = jnp.zeros_like(dscale_ref)
            ggain_ref[...] = jnp.zeros_like(ggain_ref)

        refs = (dqkv_ref, dza_ref, dzs_ref, dxbc_ref, ddt_ref)
        dh = jnp.zeros((ts, D_MODEL), f32)
        for idx, off, width in pieces:
            for cc in range(0, width, 512):
                wd = min(512, width - cc)
                dh = dh + _nt(refs[idx][:, cc:cc + wd], w_ref[:, off + cc:off + cc + wd])
        xv = x_ref[...]
        r = _rms(xv)
        xh = xv * r
        g = g_ref[...]
        dshift_ref[...] += jnp.sum(dh, axis=0, keepdims=True)
        dscale_ref[...] += jnp.sum(dh * xh * g, axis=0, keepdims=True)
        tt = dh * (1.0 + sc_ref[...])
        ggain_ref[...] += jnp.sum(tt * xh, axis=0, keepdims=True)
        gx_ref[...] = dx2_ref[...] + _rms_bwd(tt * g, xh, r)

    row = lambda i: (i, 0)
    fixed = lambda i: (0, 0)
    vec = pl.BlockSpec((1, D_MODEL), fixed)
    return pl.pallas_call(
        body, name="bwd_in", grid=(seq // ts,),
        in_specs=[pl.BlockSpec((ts, 3 * D_ATTN), row), pl.BlockSpec((ts, D_ATTN), row), pl.BlockSpec((ts, D_SSM), row),
                  pl.BlockSpec((ts, D_XBC), row), pl.BlockSpec((ts, LANES), row), pl.BlockSpec((D_MODEL, D_PROJ_P), fixed),
                  pl.BlockSpec((ts, D_MODEL), row), pl.BlockSpec((ts, D_MODEL), row), vec, vec],
        out_specs=(pl.BlockSpec((ts, D_MODEL), row), vec, vec, vec),
        out_shape=(jax.ShapeDtypeStruct((seq, D_MODEL), f32), jax.ShapeDtypeStruct((1, D_MODEL), f32),
                   jax.ShapeDtypeStruct((1, D_MODEL), f32), jax.ShapeDtypeStruct((1, D_MODEL), f32)),
        compiler_params=_params(("arbitrary",)),
    )(dqkv, dza, dzs, dxbc, ddt, wp, x, dx2, gain, scale)


def _grad_w(a, b, tn, name):
    seq, m = a.shape
    n = b.shape[1]
    tk = min(512, seq)
    n_k = seq // tk

    def body(a_ref, b_ref, o_ref, acc):
        @pl.when(pl.program_id(1) == 0)
        def _():
            acc[...] = jnp.zeros_like(acc)

        acc[...] += _tn(a_ref[...], b_ref[...])

        @pl.when(pl.program_id(1) == n_k - 1)
        def _():
            o_ref[...] = acc[...].astype(bf16)

    return pl.pallas_call(
        body, name=name, grid=(n // tn, n_k),
        in_specs=[pl.BlockSpec((tk, m), lambda j, k: (k, 0)), pl.BlockSpec((tk, tn), lambda j, k: (k, j))],
        out_specs=pl.BlockSpec((m, tn), lambda j, k: (0, j)),
        out_shape=jax.ShapeDtypeStruct((m, n), bf16), scratch_shapes=[pltpu.VMEM((m, tn), f32)],
        compiler_params=_params(("parallel", "arbitrary")),
    )(a, b)


def _small_finish(g_all, c_all, dmod_mine):
    def body(g_ref, c_ref, dm_ref, tot_ref, gwada_ref):
        tot = g_ref[0:1, :]
        for j in range(1, N_DEV):
            tot = tot + g_ref[j:j + 1, :]
        tot_ref[...] = tot
        cv = c_ref[...]
        gwada_ref[...] = _tn(cv * _sigmoid(cv), dm_ref[...], HI)

    vmem = pl.BlockSpec(memory_space=pltpu.VMEM)
    return pl.pallas_call(
        body, name="small_finish", in_specs=[vmem, vmem, vmem], out_specs=(vmem, vmem),
        out_shape=(jax.ShapeDtypeStruct((1, N_PACK), f32), jax.ShapeDtypeStruct((D_MODEL, dmod_mine.shape[1]), f32)),
        compiler_params=_params(),
    )(g_all, c_all, dmod_mine)


def _adamw(w, g_parts, m, v, rows, name):
    r, c = w.shape
    n_parts = g_parts.shape[0]
    bc1 = 1.0 - ADAM_B1 ** ADAM_STEP
    bc2 = 1.0 - ADAM_B2 ** ADAM_STEP

    def body(w_ref, g_ref, m_ref, v_ref, go_ref, d_ref, mo_ref, vo_ref):
        g = g_ref[0].astype(f32)
        for j in range(1, n_parts):
            g = g + g_ref[j].astype(f32)
        go_ref[...] = g
        mn = ADAM_B1 * m_ref[...] + (1.0 - ADAM_B1) * g
        vn = ADAM_B2 * v_ref[...] + (1.0 - ADAM_B2) * (g * g)
        mo_ref[...] = mn
        vo_ref[...] = vn
        d_ref[...] = -ADAM_LR * ((mn / bc1) / (jnp.sqrt(vn / bc2) + ADAM_EPS) + ADAM_WD * w_ref[...])

    blk = pl.BlockSpec((rows, c), lambda i: (i, 0))
    return pl.pallas_call(
        body, name=name, grid=(r // rows,),
        in_specs=[blk, pl.BlockSpec((n_parts, rows, c), lambda i: (0, i, 0)), blk, blk],
        out_specs=(blk, blk, blk, blk), out_shape=(jax.ShapeDtypeStruct((r, c), f32),) * 4,
        compiler_params=_params(("parallel",)),
    )(w, g_parts, m, v)


def _pad_lanes(v):
    return jnp.pad(v, ((0, 0), (0, LANES - v.shape[1])))


def kernel(x, c, w_ada, b_ada, norm_in_gain, w_in, conv_w, conv_b, dt_bias, a_log, d_skip, sb_norm_gain, ssm_norm_gain, w_out, norm_f_gain, loss_target, m_w_ada, m_b_ada, m_norm_in_gain, m_w_in, m_conv_w, m_conv_b, m_dt_bias, m_a_log, m_d_skip, m_sb_norm_gain, m_ssm_norm_gain, m_w_out, m_norm_f_gain, v_w_ada, v_b_ada, v_norm_in_gain, v_w_in, v_conv_w, v_conv_b, v_dt_bias, v_a_log, v_d_skip, v_sb_norm_gain, v_ssm_norm_gain, v_w_out, v_norm_f_gain):
    seq = x.shape[1]
    xs = x[0]
    tgt = loss_target[0]
    _, my_slot = _me()

    mod, c_all = _mod_exchange(c, w_ada[0], b_ada)
    shift, scale, gate = mod[:, :D_MODEL], mod[:, D_MODEL:2 * D_MODEL], mod[:, 2 * D_MODEL:]
    w_in_g, w_out_g, conv_w_g = _all_gather_two_level(
        [_cast_bf16(w_in[0], 128), _cast_bf16(w_out[0], 128), conv_w[0]], "gather_weights")
    w_full = jnp.transpose(w_in_g, (1, 0, 2)).reshape(D_MODEL, D_PROJ)
    wp = jnp.concatenate([w_full[:, :4 * D_ATTN], w_full[:, D_PROJ - D_SSM:], w_full[:, 4 * D_ATTN:4 * D_ATTN + D_XBC],
                          _pad_lanes(w_full[:, 4 * D_ATTN + D_XBC:4 * D_ATTN + D_XBC + N_HEADS])], axis=1)
    w_out_full = w_out_g.reshape(D_ATTN + D_SSM, D_MODEL)
    conv_w_full = jnp.transpose(conv_w_g, (1, 0, 2)).reshape(CONV_K, D_XBC)
    dtb, alog, dsk = _pad_lanes(dt_bias), _pad_lanes(a_log), _pad_lanes(d_skip)

    h, qkv, za, zs, xbc, dt_raw = _proj(xs, norm_in_gain, scale, shift, wp, seq)
    o_tot = _attn_fwd(qkv, seq)
    conv, y, states = _ssd_fwd(xbc, dt_raw, conv_w_full, conv_b, dtb, alog, dsk, seq)
    ycat, dmix, dx2, loss_p, g_nf, d_gate = _post(xs, tgt, o_tot, y, za, zs, w_out_full, gate, sb_norm_gain, ssm_norm_gain,
                                                  norm_f_gain.reshape(1, D_MODEL), seq)

    d_o, dza, dzs, dy, g_sb, g_ss = _bwd_out(dmix, w_out_full, o_tot, y, za, zs, sb_norm_gain, ssm_norm_gain, seq)
    dq, dk, dv = _attn_bwd(qkv, o_tot, d_o, seq)
    dxbc, ddt, g_cw, g_cb, g_dtb, g_al, g_dsk = _ssd_bwd(dy, conv, xbc, dt_raw, states, conv_w_full, dtb, alog, dsk, seq)
    dqkv = _qkv_grads_to_cols(dq, dk, dv, seq)
    grad_x, d_shift, d_scale, g_in = _bwd_in(dqkv, dza, dzs, dxbc, ddt, wp, xs, dx2, norm_in_gain, scale, seq)
    gw_qkv = _grad_w(h, dqkv, 512, "grad_w_qkv")
    gw_za = _grad_w(h, dza, 512, "grad_w_za")
    gw_zs = _grad_w(h, dzs, 512, "grad_w_zs")
    gw_xbc = _grad_w(h, dxbc, 512, "grad_w_xbc")
    gw_dt = _grad_w(h, ddt, LANES, "grad_w_dt")
    gw_out = _grad_w(ycat, dmix, 512, "grad_w_out")
    gw_in = jnp.concatenate([gw_qkv, gw_za, gw_xbc, gw_dt[:, :N_HEADS], gw_zs], axis=1)

    gw_in_parts, gw_out_parts = _all_to_all(
        [jnp.transpose(gw_in.reshape(D_MODEL, N_DEV, W_IN_SHARD), (1, 0, 2)),
         gw_out.reshape(N_DEV, (D_ATTN + D_SSM) // N_DEV, D_MODEL)], "scatter_grads")
    packed = jnp.concatenate([loss_p, d_shift, d_scale, d_gate, g_in, g_cb, g_dtb, g_al, g_dsk, g_sb, g_ss, g_nf,
                              g_cw.reshape(1, CONV_K * D_XBC)], axis=1)
    (packed_all,) = _all_gather([packed], "gather_small")
    packed_all = packed_all.reshape(N_DEV, N_PACK)
    n_ada = w_ada.shape[2]
    dmod_mine = lax.dynamic_slice(packed_all, (0, P_DMOD + my_slot * n_ada), (N_DEV, n_ada))
    tot, g_w_ada = _small_finish(packed_all, c_all.reshape(N_DEV, D_MODEL), dmod_mine)

    def small(w, g, m, v):
        shape = w.shape
        w2, g2, m2, v2 = (t.reshape(1, -1) for t in (w, g, m, v))
        outs = _adamw(w2, g2[None], m2, v2, 1, "adamw_small")
        return tuple(t.reshape(shape) for t in outs)

    n_cw = conv_w.shape[2]
    g_cw_tot = tot[:, P_CW:].reshape(CONV_K, D_XBC)
    g_cw_mine = lax.dynamic_slice(g_cw_tot, (0, my_slot * n_cw), (CONV_K, n_cw))
    res = {
        "w_ada": tuple(t[None] for t in _adamw(w_ada[0], g_w_ada[None], m_w_ada[0], v_w_ada[0], 128, "adamw_w_ada")),
        "b_ada": small(b_ada, tot[:, P_DMOD:P_DMOD + 3 * D_MODEL], m_b_ada, v_b_ada),
        "norm_in_gain": small(norm_in_gain, tot[:, P_GIN:P_GIN + D_MODEL], m_norm_in_gain, v_norm_in_gain),
        "w_in": tuple(t[None] for t in _adamw(w_in[0], gw_in_parts, m_w_in[0], v_w_in[0], 128, "adamw_w_in")),
        "conv_w": small(conv_w, g_cw_mine[None], m_conv_w, v_conv_w),
        "conv_b": small(conv_b, tot[:, P_CB:P_CB + D_XBC], m_conv_b, v_conv_b),
        "dt_bias": small(dt_bias, tot[:, P_DTB:P_DTB + N_HEADS], m_dt_bias, v_dt_bias),
        "a_log": small(a_log, tot[:, P_ALOG:P_ALOG + N_HEADS], m_a_log, v_a_log),
        "d_skip": small(d_skip, tot[:, P_DSK:P_DSK + N_HEADS], m_d_skip, v_d_skip),
        "sb_norm_gain": small(sb_norm_gain, tot[:, P_GSB:P_GSB + D_MODEL], m_sb_norm_gain, v_sb_norm_gain),
        "ssm_norm_gain": small(ssm_norm_gain, tot[:, P_GSS:P_GSS + D_MODEL], m_ssm_norm_gain, v_ssm_norm_gain),
        "w_out": tuple(t[None] for t in _adamw(w_out[0], gw_out_parts, m_w_out[0], v_w_out[0], 64, "adamw_w_out")),
        "norm_f_gain": small(norm_f_gain, tot[0, P_GNF:P_GNF + D_MODEL], m_norm_f_gain, v_norm_f_gain),
    }
    names = ["w_ada", "b_ada", "norm_in_gain", "w_in", "conv_w", "conv_b", "dt_bias", "a_log", "d_skip", "sb_norm_gain",
             "ssm_norm_gain", "w_out", "norm_f_gain"]
    loss = tot[0, P_LOSS]
    return (loss, grad_x[None], *[res[n][0] for n in names], *[res[n][1] for n in names],
            *[res[n][2] for n in names], *[res[n][3] for n in names])
```

```python
import functools

import jax
import jax.numpy as jnp
from jax import lax
from jax.experimental import pallas as pl
from jax.experimental.pallas import tpu as pltpu

f32 = jnp.float32
bf16 = jnp.bfloat16
MESH = pl.DeviceIdType.MESH
HI = lax.Precision.HIGHEST

N_DEV = 8
D_MODEL = 1024
D_ATTN = 1024
D_SSM = 1024
N_HEADS = 16
HEAD_DIM = 64
N_GROUPS = 2
HEADS_PER_GROUP = 8
N_STATE = 128
D_XBC = D_SSM + 2 * N_GROUPS * N_STATE
D_PROJ = 4 * D_ATTN + D_XBC + N_HEADS + D_SSM
W_IN_SHARD = D_PROJ // N_DEV
CONV_K = 4
CHUNK = 128
ATTN_Q_ROWS = 2048
LANES = 128
EPS = 1e-6
OFF_ZA = 3072
OFF_ZS = 4096
OFF_XBC = 5120
OFF_DT = 6656
D_PROJ_P = 6784
VMEM_LIMIT_BYTES = 56 * 1024 * 1024

ADAM_LR = 0.001
ADAM_B1 = 0.9
ADAM_B2 = 0.999
ADAM_EPS = 1e-08
ADAM_WD = 0.01
ADAM_STEP = 10

P_LOSS = 0
P_DMOD = 128
P_GIN = 3200
P_CB = 4224
P_DTB = 5760
P_ALOG = 5888
P_DSK = 6016
P_GSB = 6144
P_GSS = 7168
P_GNF = 8192
P_CW = 9216
N_PACK = 15360


def _params(sem=None):
    return pltpu.CompilerParams(dimension_semantics=sem, vmem_limit_bytes=VMEM_LIMIT_BYTES)


def _sigmoid(v):
    return 1.0 / (1.0 + jnp.exp(-v))


def _softplus(v):
    return jnp.maximum(v, 0.0) + jnp.log(1.0 + jnp.exp(-jnp.abs(v)))


def _nt(a, b, precision=None):
    return lax.dot_general(a, b, (((1,), (1,)), ((), ())), preferred_element_type=f32, precision=precision)


def _tn(a, b, precision=None):
    return lax.dot_general(a, b, (((0,), (0,)), ((), ())), preferred_element_type=f32, precision=precision)


def _nn(a, b, precision=None):
    return lax.dot_general(a, b, (((1,), (0,)), ((), ())), preferred_element_type=f32, precision=precision)


def _me():
    x, y, c = lax.axis_index("x"), lax.axis_index("y"), lax.axis_index("c")
    return (x, y, c), 4 * x + 2 * y + c


def _peer(k):
    x, y, c = lax.axis_index("x"), lax.axis_index("y"), lax.axis_index("c")
    px = 1 - x if (k >> 2) & 1 else x
    py = 1 - y if (k >> 1) & 1 else y
    pc = 1 - c if k & 1 else c
    return (px, py, pc), 4 * px + 2 * py + pc


def _all_gather(arrs, name):
    n = len(arrs)

    def body(*refs):
        ins, outs = refs[:n], refs[n:2 * n]
        send_sems, recv_sems, local_sems = refs[2 * n:]
        _, my_slot = _me()
        sends = []
        locals_ = []
        for a in range(n):
            loc = pltpu.make_async_copy(ins[a], outs[a].at[my_slot], local_sems.at[a])
            loc.start()
            locals_.append(loc)
            for k in range(1, N_DEV):
                peer, _ = _peer(k)
                cp = pltpu.make_async_remote_copy(src_ref=ins[a], dst_ref=outs[a].at[my_slot], send_sem=send_sems.at[a, k - 1],
                                                  recv_sem=recv_sems.at[a, k - 1], device_id=peer, device_id_type=MESH)
                cp.start()
                sends.append(cp)
        for a in range(n):
            for k in range(1, N_DEV):
                peer, peer_slot = _peer(k)
                pltpu.make_async_remote_copy(src_ref=ins[a], dst_ref=outs[a].at[peer_slot], send_sem=send_sems.at[a, k - 1],
                                             recv_sem=recv_sems.at[a, k - 1], device_id=peer, device_id_type=MESH).wait_recv()
        for cp in sends:
            cp.wait_send()
        for loc in locals_:
            loc.wait()

    any_spec = pl.BlockSpec(memory_space=pl.ANY)
    return pl.pallas_call(
        body, name=name,
        out_shape=tuple(jax.ShapeDtypeStruct((N_DEV,) + a.shape, a.dtype) for a in arrs),
        in_specs=[any_spec] * n, out_specs=tuple([any_spec] * n),
        scratch_shapes=[pltpu.SemaphoreType.DMA((n, N_DEV - 1)), pltpu.SemaphoreType.DMA((n, N_DEV - 1)),
                        pltpu.SemaphoreType.DMA((n,))],
    )(*arrs)


def _all_gather_two_level(arrs, name):
    n = len(arrs)

    def body(*refs):
        ins, outs = refs[:n], refs[n:2 * n]
        send_sems, recv_sems, local_sems = refs[2 * n:]
        x, y, c = lax.axis_index("x"), lax.axis_index("y"), lax.axis_index("c")
        me, sibling = (x, y, c), (x, y, 1 - c)
        chips = [(1 - x, y), (x, 1 - y), (1 - x, 1 - y)]

        def copy(a, k, block, to, from_input=False):
            slot = 4 * block[0] + 2 * block[1] + block[2]
            return pltpu.make_async_remote_copy(src_ref=ins[a] if from_input else outs[a].at[slot], dst_ref=outs[a].at[slot],
                                                send_sem=send_sems.at[a, k], recv_sem=recv_sems.at[a, k], device_id=to,
                                                device_id_type=MESH)

        started = []
        locals_ = []
        for a in range(n):
            loc = pltpu.make_async_copy(ins[a], outs[a].at[4 * x + 2 * y + c], local_sems.at[a])
            loc.start()
            locals_.append(loc)
            first = [copy(a, 0, me, sibling, True)] + [copy(a, 1 + j, me, (*chip, c), True) for j, chip in enumerate(chips)]
            for cp in first:
                cp.start()
            started += first
        for a in range(n):
            for j, chip in enumerate(chips):
                copy(a, 1 + j, (*chip, c), me).wait_recv()
                onward = copy(a, 4 + j, (*chip, c), sibling)
                onward.start()
                started.append(onward)
        for a in range(n):
            copy(a, 0, sibling, me).wait_recv()
            for j, chip in enumerate(chips):
                copy(a, 4 + j, (*chip, 1 - c), me).wait_recv()
        for cp in started:
            cp.wait_send()
        for loc in locals_:
            loc.wait()

    any_spec = pl.BlockSpec(memory_space=pl.ANY)
    return pl.pallas_call(
        body, name=name,
        out_shape=tuple(jax.ShapeDtypeStruct((N_DEV,) + a.shape, a.dtype) for a in arrs),
        in_specs=[any_spec] * n, out_specs=tuple([any_spec] * n),
        scratch_shapes=[pltpu.SemaphoreType.DMA((n, N_DEV - 1)), pltpu.SemaphoreType.DMA((n, N_DEV - 1)),
                        pltpu.SemaphoreType.DMA((n,))],
    )(*arrs)


def _all_to_all(arrs, name):
    n = len(arrs)

    def body(*refs):
        ins, outs = refs[:n], refs[n:2 * n]
        send_sems, recv_sems, local_sems = refs[2 * n:]
        _, my_slot = _me()
        sends = []
        locals_ = []
        for a in range(n):
            loc = pltpu.make_async_copy(ins[a].at[my_slot], outs[a].at[my_slot], local_sems.at[a])
            loc.start()
            locals_.append(loc)
            for k in range(1, N_DEV):
                peer, peer_slot = _peer(k)
                cp = pltpu.make_async_remote_copy(src_ref=ins[a].at[peer_slot], dst_ref=outs[a].at[my_slot],
                                                  send_sem=send_sems.at[a, k - 1], recv_sem=recv_sems.at[a, k - 1],
                                                  device_id=peer, device_id_type=MESH)
                cp.start()
                sends.append(cp)
        for a in range(n):
            for k in range(1, N_DEV):
                peer, peer_slot = _peer(k)
                pltpu.make_async_remote_copy(src_ref=ins[a].at[peer_slot], dst_ref=outs[a].at[peer_slot],
                                             send_sem=send_sems.at[a, k - 1], recv_sem=recv_sems.at[a, k - 1],
                                             device_id=peer, device_id_type=MESH).wait_recv()
        for cp in sends:
            cp.wait_send()
        for loc in locals_:
            loc.wait()

    any_spec = pl.BlockSpec(memory_space=pl.ANY)
    return pl.pallas_call(
        body, name=name,
        out_shape=tuple(jax.ShapeDtypeStruct(a.shape, a.dtype) for a in arrs),
        in_specs=[any_spec] * n, out_specs=tuple([any_spec] * n),
        scratch_shapes=[pltpu.SemaphoreType.DMA((n, N_DEV - 1)), pltpu.SemaphoreType.DMA((n, N_DEV - 1)),
                        pltpu.SemaphoreType.DMA((n,))],
    )(*arrs)


def _reduce_scatter_two_level(arrs, name):
    n = len(arrs)
    n_chip = N_DEV // 2

    def body(*refs):
        ins, outs = refs[:n], refs[n:2 * n]
        mine_bufs, sib_bufs = refs[2 * n:3 * n], refs[3 * n:4 * n]
        send_sems, recv_sems, local_sems = refs[4 * n:]
        x, y, c = lax.axis_index("x"), lax.axis_index("y"), lax.axis_index("c")
        sibling = (x, y, 1 - c)

        def chip(r):
            return (1 - x if r & 2 else x), (1 - y if r & 1 else y)

        def slot(r, core):
            cx, cy = chip(r)
            return 4 * cx + 2 * cy + core

        def to_sibling(a, r):
            return pltpu.make_async_remote_copy(src_ref=ins[a].at[slot(r, 1 - c)], dst_ref=sib_bufs[a].at[r], send_sem=send_sems.at[a, r],
                                                recv_sem=recv_sems.at[a, r], device_id=sibling, device_id_type=MESH)

        def to_chip(a, r):
            return pltpu.make_async_remote_copy(src_ref=mine_bufs[a].at[r], dst_ref=outs[a].at[r], send_sem=send_sems.at[a, n_chip - 1 + r],
                                                recv_sem=recv_sems.at[a, n_chip - 1 + r], device_id=(*chip(r), c), device_id_type=MESH)

        def load_mine(a, r):
            return pltpu.make_async_copy(ins[a].at[slot(r, c)], mine_bufs[a].at[r], local_sems.at[a, r])

        started = []
        order = [1, 2, 3, 0]
        for a in range(n):
            for r in order:
                load_mine(a, r).start()
                cp = to_sibling(a, r)
                cp.start()
                started.append(cp)
        for a in range(n):
            for r in order:
                load_mine(a, r).wait()
                to_sibling(a, r).wait_recv()
                total = (mine_bufs[a][r].astype(f32) + sib_bufs[a][r].astype(f32)).astype(bf16)
                if r:
                    mine_bufs[a][r] = total
                    cp = to_chip(a, r)
                    cp.start()
                    started.append(cp)
                else:
                    outs[a][0] = total
        for a in range(n):
            for r in range(1, n_chip):
                to_chip(a, r).wait_recv()
        for cp in started:
            cp.wait_send()

    any_spec = pl.BlockSpec(memory_space=pl.ANY)
    vmem = pl.BlockSpec(memory_space=pltpu.VMEM)
    part = lambda a: (n_chip,) + a.shape[1:]
    return pl.pallas_call(
        body, name=name,
        out_shape=tuple(jax.ShapeDtypeStruct(part(a), a.dtype) for a in arrs),
        in_specs=[any_spec] * n, out_specs=tuple([vmem] * n),
        scratch_shapes=([pltpu.VMEM(part(a), a.dtype) for a in arrs] + [pltpu.VMEM(part(a), a.dtype) for a in arrs]
                        + [pltpu.SemaphoreType.DMA((n, N_DEV - 1)), pltpu.SemaphoreType.DMA((n, N_DEV - 1)),
                           pltpu.SemaphoreType.DMA((n, n_chip))]),
        compiler_params=_params(),
    )(*arrs)


def _mod_exchange(c_row, w_ada, b_ada):
    n_col = w_ada.shape[1]

    def body(c_ref, w_ref, b_ref, mod_ref, call_ref, part, modp, send_sems, recv_sems):
        _, my_slot = _me()
        call_ref[my_slot] = c_ref[...]
        sends = []
        for k in range(1, N_DEV):
            peer, _ = _peer(k)
            cp = pltpu.make_async_remote_copy(src_ref=c_ref, dst_ref=call_ref.at[my_slot], send_sem=send_sems.at[0, k - 1],
                                              recv_sem=recv_sems.at[0, k - 1], device_id=peer, device_id_type=MESH)
            cp.start()
            sends.append(cp)
        for k in range(1, N_DEV):
            peer, peer_slot = _peer(k)
            pltpu.make_async_remote_copy(src_ref=c_ref, dst_ref=call_ref.at[peer_slot], send_sem=send_sems.at[0, k - 1],
                                         recv_sem=recv_sems.at[0, k - 1], device_id=peer, device_id_type=MESH).wait_recv()
        for cp in sends:
            cp.wait_send()
        w = w_ref[...]
        for b in range(N_DEV):
            cb = call_ref[b]
            part[b] = _nn(cb * _sigmoid(cb), w, HI)
        modp[my_slot] = part[my_slot]
        sends = []
        for k in range(1, N_DEV):
            peer, peer_slot = _peer(k)
            cp = pltpu.make_async_remote_copy(src_ref=part.at[peer_slot], dst_ref=modp.at[my_slot], send_sem=send_sems.at[1, k - 1],
                                              recv_sem=recv_sems.at[1, k - 1], device_id=peer, device_id_type=MESH)
            cp.start()
            sends.append(cp)
        for k in range(1, N_DEV):
            peer, peer_slot = _peer(k)
            pltpu.make_async_remote_copy(src_ref=part.at[peer_slot], dst_ref=modp.at[peer_slot], send_sem=send_sems.at[1, k - 1],
                                         recv_sem=recv_sems.at[1, k - 1], device_id=peer, device_id_type=MESH).wait_recv()
        for cp in sends:
            cp.wait_send()
        for j in range(N_DEV):
            mod_ref[:, j * n_col:(j + 1) * n_col] = modp[j] + b_ref[:, j * n_col:(j + 1) * n_col]

    vmem = pl.BlockSpec(memory_space=pltpu.VMEM)
    return pl.pallas_call(
        body, name="mod_exchange",
        out_shape=(jax.ShapeDtypeStruct((1, N_DEV * n_col), f32), jax.ShapeDtypeStruct((N_DEV, 1, D_MODEL), f32)),
        in_specs=[vmem, vmem, vmem], out_specs=(vmem, vmem),
        scratch_shapes=[pltpu.VMEM((N_DEV, 1, n_col), f32), pltpu.VMEM((N_DEV, 1, n_col), f32),
                        pltpu.SemaphoreType.DMA((2, N_DEV - 1)), pltpu.SemaphoreType.DMA((2, N_DEV - 1))],
        compiler_params=_params(),
    )(c_row, w_ada, b_ada)


def _cast_bf16(a, rows):
    r, c = a.shape

    def body(a_ref, o_ref):
        o_ref[...] = a_ref[...].astype(bf16)

    return pl.pallas_call(
        body, name="cast_bf16", grid=(r // rows,),
        in_specs=[pl.BlockSpec((rows, c), lambda i: (i, 0))], out_specs=pl.BlockSpec((rows, c), lambda i: (i, 0)),
        out_shape=jax.ShapeDtypeStruct((r, c), bf16), compiler_params=_params(("parallel",)),
    )(a)


def _store_transposed(out_ref, v):
    blk = 256
    eye = (lax.broadcasted_iota(jnp.int32, (blk, blk), 0) == lax.broadcasted_iota(jnp.int32, (blk, blk), 1)).astype(bf16)
    for cb in range(0, v.shape[1], blk):
        out_ref[cb:cb + blk, :] = _nt(eye, v[:, cb:cb + blk]).astype(bf16)


def _proj(x, gain, scale, shift, wp, seq):
    ts = 256

    def body(x_ref, g_ref, sc_ref, sh_ref, w_ref, ht_ref, qkv_ref, za_ref, zs_ref, xbc_ref, dt_ref):
        xv = x_ref[...]
        r = lax.rsqrt(jnp.mean(xv * xv, axis=-1, keepdims=True) + EPS)
        hb = ((xv * r * g_ref[...]) * (1.0 + sc_ref[...]) + sh_ref[...]).astype(bf16)
        _store_transposed(ht_ref, hb)
        for cb in range(3 * D_ATTN // 256):
            res = jnp.dot(hb, w_ref[:, cb * 256:(cb + 1) * 256], preferred_element_type=f32)
            for u in range(4):
                qkv_ref[cb * 4 + u] = res[:, u * HEAD_DIM:(u + 1) * HEAD_DIM].astype(bf16)
        for out_ref, off, width in ((za_ref, OFF_ZA, D_ATTN), (zs_ref, OFF_ZS, D_SSM), (xbc_ref, OFF_XBC, D_XBC), (dt_ref, OFF_DT, LANES)):
            for cc in range(0, width, 512):
                wd = min(512, width - cc)
                out_ref[:, cc:cc + wd] = jnp.dot(hb, w_ref[:, off + cc:off + cc + wd], preferred_element_type=f32)

    row = lambda i: (i, 0)
    fixed = lambda i: (0, 0)
    return pl.pallas_call(
        body, name="proj", grid=(seq // ts,),
        in_specs=[pl.BlockSpec((ts, D_MODEL), row), pl.BlockSpec((1, D_MODEL), fixed), pl.BlockSpec((1, D_MODEL), fixed),
                  pl.BlockSpec((1, D_MODEL), fixed), pl.BlockSpec((D_MODEL, D_PROJ_P), fixed)],
        out_specs=(pl.BlockSpec((D_MODEL, ts), lambda i: (0, i)), pl.BlockSpec((3 * N_HEADS, ts, HEAD_DIM), lambda i: (0, i, 0)),
                   pl.BlockSpec((ts, D_ATTN), row), pl.BlockSpec((ts, D_SSM), row), pl.BlockSpec((ts, D_XBC), row),
                   pl.BlockSpec((ts, LANES), row)),
        out_shape=(jax.ShapeDtypeStruct((D_MODEL, seq), bf16), jax.ShapeDtypeStruct((3 * N_HEADS, seq, HEAD_DIM), bf16),
                   jax.ShapeDtypeStruct((seq, D_ATTN), f32), jax.ShapeDtypeStruct((seq, D_SSM), f32),
                   jax.ShapeDtypeStruct((seq, D_XBC), f32), jax.ShapeDtypeStruct((seq, LANES), f32)),
        compiler_params=_params(("arbitrary",)),
    )(x, gain, scale, shift, wp)


def _log_sigmoids(z):
    lb = jnp.minimum(z, 0.0) - jnp.log(1.0 + jnp.exp(-jnp.abs(z)))
    return lb, lb - z


def _split_bf16(v):
    hi = v.astype(bf16)
    return hi, (v - hi.astype(f32)).astype(bf16)


def _attn_fwd(qkv, seq):
    t = CHUNK
    tq = min(ATTN_Q_ROWS, seq)
    nd = tq // t

    def body(q_ref, k_ref, v_ref, o_ref):
        i = pl.program_id(1)
        q = q_ref[0] * 0.125
        ur = lax.broadcasted_iota(jnp.int32, (2 * t, t), 0)
        upper = ((ur & (t - 1)) > lax.broadcasted_iota(jnp.int32, (2 * t, t), 1)).astype(bf16)

        def tile(j, q_s, acc, run, masked):
            n = q_s.shape[0]
            start = pl.multiple_of(j * t, t)
            k = k_ref[0, pl.ds(start, t), :]
            v = v_ref[0, pl.ds(start, t), :]
            z = _nt(q_s, k)
            lb, lom = _log_sigmoids(z)
            if masked:
                keep = lax.broadcasted_iota(jnp.int32, (n, t), 1) < lax.broadcasted_iota(jnp.int32, (n, t), 0)
                lom = jnp.where(keep, lom, 0.0)
            tail = jnp.dot(jnp.concatenate(_split_bf16(lom), axis=1), upper, preferred_element_type=f32)
            a = lb + tail + run
            if masked:
                a = jnp.where(keep, a, -jnp.inf)
            w = jnp.exp(a)
            acc = acc + jnp.dot(w.astype(bf16), v, preferred_element_type=f32)
            run = run + tail[:, 0:1] + lom[:, 0:1]
            return acc, run

        acc, run = jnp.zeros((tq, HEAD_DIM), f32), jnp.zeros((tq, 1), f32)
        for jj in reversed(range(nd)):
            r0 = jj * t
            acc_s, run_s = tile(i * nd + jj, q[r0:], acc[r0:], run[r0:], True)
            acc = acc_s if r0 == 0 else jnp.concatenate([acc[:r0], acc_s], axis=0)
            run = run_s if r0 == 0 else jnp.concatenate([run[:r0], run_s], axis=0)
        acc, run = lax.fori_loop(0, i * nd, lambda n, cr: tile(i * nd - 1 - n, q, cr[0], cr[1], False), (acc, run))
        o_ref[0] = jnp.concatenate([acc, jnp.broadcast_to(run, (tq, HEAD_DIM))], axis=1)

    return pl.pallas_call(
        body, name="attn_fwd", grid=(N_HEADS, seq // tq),
        in_specs=[pl.BlockSpec((1, tq, HEAD_DIM), lambda h, i: (h, i, 0)),
                  pl.BlockSpec((1, seq, HEAD_DIM), lambda h, i: (N_HEADS + h, 0, 0)),
                  pl.BlockSpec((1, seq, HEAD_DIM), lambda h, i: (2 * N_HEADS + h, 0, 0))],
        out_specs=pl.BlockSpec((1, tq, 2 * HEAD_DIM), lambda h, i: (h, i, 0)),
        out_shape=jax.ShapeDtypeStruct((N_HEADS, seq, 2 * HEAD_DIM), f32),
        compiler_params=_params(("parallel", "arbitrary")),
    )(qkv, qkv, qkv)


def _attn_bwd(qkv, o_tot, d_o, seq):
    t = CHUNK
    tq = min(ATTN_Q_ROWS, seq)
    nd = tq // t
    nk = seq // t

    def body(q_ref, k_ref, v_ref, ot_ref, do_ref, dq_ref, dkt_ref, dvt_ref):
        i = pl.program_id(1)

        @pl.when(i == 0)
        def _():
            dkt_ref[...] = jnp.zeros_like(dkt_ref)
            dvt_ref[...] = jnp.zeros_like(dvt_ref)

        q = q_ref[0] * 0.125
        d_out = do_ref[0]
        total = ot_ref[0][:, HEAD_DIM:HEAD_DIM + 1]
        eye = (lax.broadcasted_iota(jnp.int32, (HEAD_DIM, HEAD_DIM), 0)
               == lax.broadcasted_iota(jnp.int32, (HEAD_DIM, HEAD_DIM), 1)).astype(bf16)
        q_t = _nt(eye, q).astype(bf16)
        do_t = _nt(eye, d_out).astype(bf16)
        ur = lax.broadcasted_iota(jnp.int32, (t, t), 0)
        uc = lax.broadcasted_iota(jnp.int32, (t, t), 1)
        ur2 = lax.broadcasted_iota(jnp.int32, (2 * t, t), 0) & (t - 1)
        incl = (ur2 <= lax.broadcasted_iota(jnp.int32, (2 * t, t), 1)).astype(bf16)
        before = (ur < uc).astype(bf16)

        def tile(j, r0, r1, dq, pre, dpre, masked):
            q_s, do_s, tot_s = q[r0:r1], d_out[r0:r1], total[r0:r1]
            n = q_s.shape[0]
            start = pl.multiple_of(j * t, t)
            k = k_ref[0, pl.ds(start, t), :]
            v = v_ref[0, pl.ds(start, t), :]
            z = _nt(q_s, k)
            lb, lom = _log_sigmoids(z)
            if masked:
                keep = lax.broadcasted_iota(jnp.int32, (n, t), 1) < lax.broadcasted_iota(jnp.int32, (n, t), 0)
                lom = jnp.where(keep, lom, 0.0)
            pin = jnp.dot(jnp.concatenate(_split_bf16(lom), axis=1), incl, preferred_element_type=f32)
            a = lb + ((tot_s - pre) - pin)
            if masked:
                a = jnp.where(keep, a, -jnp.inf)
            w = jnp.exp(a)
            d_a = _nt(do_s, v) * w
            d_lom_local = jnp.dot(d_a.astype(bf16), before, preferred_element_type=f32)
            d_lom = d_lom_local + dpre
            sig = jnp.exp(lb)
            dz = d_a * (1.0 - sig) - d_lom * sig
            if masked:
                dz = jnp.where(keep, dz, 0.0)
            dzb = dz.astype(bf16)
            dq = dq + jnp.dot(dzb, k, preferred_element_type=f32)
            dkt_ref[0, j] += jnp.dot(q_t[:, r0:r1], dzb, preferred_element_type=f32)
            dvt_ref[0, j] += jnp.dot(do_t[:, r0:r1], w.astype(bf16), preferred_element_type=f32)
            pre = pre + pin[:, t - 1:t]
            dpre = dpre + d_lom_local[:, t - 1:t] + d_a[:, t - 1:t]
            return dq, pre, dpre

        carry = (jnp.zeros((tq, HEAD_DIM), f32), jnp.zeros((tq, 1), f32), jnp.zeros((tq, 1), f32))
        carry = lax.fori_loop(0, i * nd, lambda n, cr: tile(n, 0, tq, cr[0], cr[1], cr[2], False), carry)
        for jj in range(nd):
            r0 = jj * t
            part = tile(i * nd + jj, r0, tq, *(c[r0:] for c in carry), True)
            carry = part if r0 == 0 else tuple(jnp.concatenate([c[:r0], p], axis=0) for c, p in zip(carry, part))
        dq_ref[0] = carry[0] * 0.125

    blk = pl.BlockSpec((1, tq, HEAD_DIM), lambda h, i: (h, i, 0))
    full_t = pl.BlockSpec((1, nk, HEAD_DIM, t), lambda h, i: (h, 0, 0, 0))
    return pl.pallas_call(
        body, name="attn_bwd", grid=(N_HEADS, seq // tq),
        in_specs=[blk, pl.BlockSpec((1, seq, HEAD_DIM), lambda h, i: (N_HEADS + h, 0, 0)),
                  pl.BlockSpec((1, seq, HEAD_DIM), lambda h, i: (2 * N_HEADS + h, 0, 0)),
                  pl.BlockSpec((1, tq, 2 * HEAD_DIM), lambda h, i: (h, i, 0)), blk],
        out_specs=(blk, full_t, full_t),
        out_shape=(jax.ShapeDtypeStruct((N_HEADS, seq, HEAD_DIM), f32),
                   jax.ShapeDtypeStruct((N_HEADS, nk, HEAD_DIM, t), f32), jax.ShapeDtypeStruct((N_HEADS, nk, HEAD_DIM, t), f32)),
        compiler_params=_params(("parallel", "arbitrary")),
    )(qkv, qkv, qkv, o_tot, d_o)


def _ssd_common(conv, dt_raw, dtb, alog):
    t = CHUNK
    sg = _sigmoid(conv)
    act = conv * sg
    dt_pre = dt_raw + dtb
    dt = _softplus(dt_pre)
    a = -jnp.exp(alog)
    row = lax.broadcasted_iota(jnp.int32, (t, t), 0)
    col = lax.broadcasted_iota(jnp.int32, (t, t), 1)
    causal = row >= col
    ac = _nn(causal.astype(f32), dt * a, HI)
    ac_t = _nt((row == col).astype(f32), ac, HI)
    ac_last = ac[t - 1:t, :]
    return sg, act, dt_pre, dt, a, causal, ac, ac_t, ac_last, jnp.exp(ac), jnp.exp(ac_last - ac), jnp.exp(ac_last)


def _ssd_fwd(xbc, dt_raw, conv_w, conv_b, dtb, alog, dsk, seq):
    t = CHUNK
    n_chunks = seq // t

    def body(x_ref, dt_ref, cw_ref, cb_ref, dtb_ref, al_ref, dsk_ref, conv_ref, y_ref, st_ref, prev, state):
        c = pl.program_id(0)

        @pl.when(c == 0)
        def _():
            prev[...] = jnp.zeros_like(prev)
            state[...] = jnp.zeros_like(state)

        cur = x_ref[...]
        pv = prev[...]
        rows = lax.broadcasted_iota(jnp.int32, (t, D_XBC), 0)
        conv = cur * cw_ref[CONV_K - 1:CONV_K, :] + cb_ref[...]
        for m in range(1, CONV_K):
            shifted = jnp.where(rows < m, pltpu.roll(pv, m, 0), pltpu.roll(cur, m, 0))
            conv = conv + shifted * cw_ref[CONV_K - 1 - m:CONV_K - m, :]
        prev[...] = cur
        conv_ref[...] = conv
        _, act, _, dt, _, causal, ac, ac_t, _, e_ac, dte, cdec = _ssd_common(conv, dt_ref[...], dtb_ref[...], al_ref[...])
        dskv = dsk_ref[...]
        ys = []
        for g in range(N_GROUPS):
            bg = act[:, D_SSM + g * N_STATE:D_SSM + (g + 1) * N_STATE].astype(bf16)
            cg = act[:, D_SSM + (N_GROUPS + g) * N_STATE:D_SSM + (N_GROUPS + g + 1) * N_STATE].astype(bf16)
            gm = _nt(cg, bg)
            for r in range(HEADS_PER_GROUP):
                h = g * HEADS_PER_GROUP + r
                xh = act[:, h * HEAD_DIM:(h + 1) * HEAD_DIM]
                xd = xh * dt[:, h:h + 1]
                seg = ac[:, h:h + 1] - ac_t[h:h + 1, :]
                lm = jnp.exp(jnp.where(causal, seg, -jnp.inf))
                y_diag = jnp.dot((gm * lm).astype(bf16), xd.astype(bf16), preferred_element_type=f32)
                hp = state[h]
                st_ref[0, h] = hp
                zo = _nt(cg, hp.astype(bf16))
                ys.append(y_diag + zo * e_ac[:, h:h + 1] + xh * dskv[:, h:h + 1])
                sc = _tn((xd * dte[:, h:h + 1]).astype(bf16), bg)
                state[h] = hp * cdec[:, h:h + 1] + sc
        y_ref[...] = jnp.concatenate(ys, axis=1)

    row = lambda c: (c, 0)
    fixed = lambda c: (0, 0)
    return pl.pallas_call(
        body, name="ssd_fwd", grid=(n_chunks,),
        in_specs=[pl.BlockSpec((t, D_XBC), row), pl.BlockSpec((t, LANES), row), pl.BlockSpec((CONV_K, D_XBC), fixed),
                  pl.BlockSpec((1, D_XBC), fixed), pl.BlockSpec((1, LANES), fixed), pl.BlockSpec((1, LANES), fixed),
                  pl.BlockSpec((1, LANES), fixed)],
        out_specs=(pl.BlockSpec((t, D_XBC), row), pl.BlockSpec((t, D_SSM), row),
                   pl.BlockSpec((1, N_HEADS, HEAD_DIM, N_STATE), lambda c: (c, 0, 0, 0))),
        out_shape=(jax.ShapeDtypeStruct((seq, D_XBC), f32), jax.ShapeDtypeStruct((seq, D_SSM), f32),
                   jax.ShapeDtypeStruct((n_chunks, N_HEADS, HEAD_DIM, N_STATE), f32)),
        scratch_shapes=[pltpu.VMEM((t, D_XBC), f32), pltpu.VMEM((N_HEADS, HEAD_DIM, N_STATE), f32)],
        compiler_params=_params(("arbitrary",)),
    )(xbc, dt_raw, conv_w, conv_b, dtb, alog, dsk)


def _ssd_bwd(dy, conv, xbc, dt_raw, states, conv_w, dtb, alog, dsk, seq):
    t = CHUNK
    n_chunks = seq // t

    def body(dy_ref, conv_ref, x_ref, dt_ref, st_ref, cw_ref, dtb_ref, al_ref, dsk_ref,
             dx_ref, ddt_ref, gcw_ref, gcb_ref, gdtb_ref, gal_ref, gdsk_ref, d_state, d_conv_next):
        c = pl.program_id(0)

        @pl.when(c == 0)
        def _():
            d_state[...] = jnp.zeros_like(d_state)
            d_conv_next[...] = jnp.zeros_like(d_conv_next)
            gcw_ref[...] = jnp.zeros_like(gcw_ref)
            gcb_ref[...] = jnp.zeros_like(gcb_ref)
            gdtb_ref[...] = jnp.zeros_like(gdtb_ref)
            gal_ref[...] = jnp.zeros_like(gal_ref)
            gdsk_ref[...] = jnp.zeros_like(gdsk_ref)

        conv = conv_ref[...]
        sg, act, dt_pre, dt, a, causal, ac, ac_t, _, e_ac, dte, cdec = _ssd_common(conv, dt_ref[...], dtb_ref[...], al_ref[...])
        dskv = dsk_ref[...]
        dyv = dy_ref[...]
        lane = lax.broadcasted_iota(jnp.int32, (1, LANES), 1)
        last_row = (lax.broadcasted_iota(jnp.int32, (t, 1), 0) == t - 1).astype(f32)
        sub = lax.broadcasted_iota(jnp.int32, (t, 1), 0)
        col_sums = jnp.zeros((t, t), f32)
        d_ac = jnp.zeros((t, LANES), f32)
        d_dt = jnp.zeros((t, LANES), f32)
        g_dsk = jnp.zeros((1, LANES), f32)
        dxs = []
        dbs = []
        dcs = []
        for g in range(N_GROUPS):
            bg = act[:, D_SSM + g * N_STATE:D_SSM + (g + 1) * N_STATE].astype(bf16)
            cg = act[:, D_SSM + (N_GROUPS + g) * N_STATE:D_SSM + (N_GROUPS + g + 1) * N_STATE].astype(bf16)
            gm = _nt(cg, bg)
            d_gm = jnp.zeros((t, t), f32)
            d_b = jnp.zeros((t, N_STATE), f32)
            d_c = jnp.zeros((t, N_STATE), f32)
            for r in range(HEADS_PER_GROUP):
                h = g * HEADS_PER_GROUP + r
                onehot = (lane == h).astype(f32)
                xh = act[:, h * HEAD_DIM:(h + 1) * HEAD_DIM]
                dth = dt[:, h:h + 1]
                xd = xh * dth
                xdb = xd.astype(bf16)
                seg = ac[:, h:h + 1] - ac_t[h:h + 1, :]
                lm = jnp.exp(jnp.where(causal, seg, -jnp.inf))
                mm = gm * lm
                hp = st_ref[0, h]
                hpb = hp.astype(bf16)
                d_hn = d_state[h]
                d_hnb = d_hn.astype(bf16)
                d_yh = dyv[:, h * HEAD_DIM:(h + 1) * HEAD_DIM]
                d_yb = d_yh.astype(bf16)
                g_dsk = g_dsk + jnp.sum(d_yh * xh) * onehot
                d_mm = _nt(d_yb, xdb)
                d_xd = _tn(mm.astype(bf16), d_yb)
                d_gm = d_gm + d_mm * lm
                d_seg = d_mm * mm
                d_ac_h = jnp.sum(d_seg, axis=1, keepdims=True)
                col_sums = col_sums + (sub == h).astype(f32) * jnp.sum(d_seg, axis=0, keepdims=True)
                e_h = e_ac[:, h:h + 1]
                zo = _nt(cg, hpb)
                d_zo = d_yh * e_h
                d_zob = d_zo.astype(bf16)
                d_ac_h = d_ac_h + jnp.sum(d_yh * zo, axis=1, keepdims=True) * e_h
                d_c = d_c + jnp.dot(d_zob, hpb, preferred_element_type=f32)
                cd = cdec[:, h:h + 1]
                d_hp = _tn(d_zob, cg) + d_hn * cd
                d_last = jnp.sum(d_hn * hp) * cd
                dte_h = dte[:, h:h + 1]
                d_w = _nt(bg, d_hnb)
                d_b = d_b + jnp.dot((xd * dte_h).astype(bf16), d_hnb, preferred_element_type=f32)
                d_xd = d_xd + d_w * dte_h
                d_dte = jnp.sum(d_w * xd, axis=1, keepdims=True) * dte_h
                d_last = d_last + jnp.sum(d_dte)
                d_ac_h = d_ac_h - d_dte + d_last * last_row
                d_state[h] = d_hp
                dxs.append(d_yh * dskv[:, h:h + 1] + d_xd * dth)
                d_dt = d_dt + jnp.sum(d_xd * xh, axis=1, keepdims=True) * onehot
                d_ac = d_ac + d_ac_h * onehot
            d_gmb = d_gm.astype(bf16)
            dcs.append(d_c + jnp.dot(d_gmb, bg, preferred_element_type=f32))
            dbs.append(d_b + _tn(d_gmb, cg))
        sq_row = lax.broadcasted_iota(jnp.int32, (t, t), 0)
        sq_col = lax.broadcasted_iota(jnp.int32, (t, t), 1)
        d_ac = d_ac - _nt((sq_row == sq_col).astype(f32), col_sums, HI)
        d_ld = _nn((sq_col >= sq_row).astype(f32), d_ac, HI)
        d_dt = d_dt + d_ld * a
        gal_ref[...] += jnp.sum(d_ld * dt, axis=0, keepdims=True) * a
        gdsk_ref[...] += g_dsk
        d_dt_raw = d_dt * _sigmoid(dt_pre)
        ddt_ref[...] = d_dt_raw.astype(bf16)
        gdtb_ref[...] += jnp.sum(d_dt_raw, axis=0, keepdims=True)
        d_act = jnp.concatenate(dxs + dbs + dcs, axis=1)
        d_conv = d_act * (sg * (1.0 + conv * (1.0 - sg)))
        gcb_ref[...] += jnp.sum(d_conv, axis=0, keepdims=True)
        nxt = d_conv_next[...]
        rows = lax.broadcasted_iota(jnp.int32, (t, D_XBC), 0)
        xraw = x_ref[...]
        d_x = d_conv * cw_ref[CONV_K - 1:CONV_K, :]
        gcw_ref[pl.ds(CONV_K - 1, 1), :] += jnp.sum(xraw * d_conv, axis=0, keepdims=True)
        for m in range(1, CONV_K):
            ahead = jnp.where(rows >= t - m, pltpu.roll(nxt, t - m, 0), pltpu.roll(d_conv, t - m, 0))
            d_x = d_x + ahead * cw_ref[CONV_K - 1 - m:CONV_K - m, :]
            gcw_ref[pl.ds(CONV_K - 1 - m, 1), :] += jnp.sum(xraw * ahead, axis=0, keepdims=True)
        d_conv_next[...] = d_conv
        dx_ref[...] = d_x.astype(bf16)

    rev = lambda c: (n_chunks - 1 - c, 0)
    fixed = lambda c: (0, 0)
    return pl.pallas_call(
        body, name="ssd_bwd", grid=(n_chunks,),
        in_specs=[pl.BlockSpec((t, D_SSM), rev), pl.BlockSpec((t, D_XBC), rev), pl.BlockSpec((t, D_XBC), rev),
                  pl.BlockSpec((t, LANES), rev), pl.BlockSpec((1, N_HEADS, HEAD_DIM, N_STATE), lambda c: (n_chunks - 1 - c, 0, 0, 0)),
                  pl.BlockSpec((CONV_K, D_XBC), fixed), pl.BlockSpec((1, LANES), fixed), pl.BlockSpec((1, LANES), fixed),
                  pl.BlockSpec((1, LANES), fixed)],
        out_specs=(pl.BlockSpec((t, D_XBC), rev), pl.BlockSpec((t, LANES), rev), pl.BlockSpec((CONV_K, D_XBC), fixed),
                   pl.BlockSpec((1, D_XBC), fixed), pl.BlockSpec((1, LANES), fixed), pl.BlockSpec((1, LANES), fixed),
                   pl.BlockSpec((1, LANES), fixed)),
        out_shape=(jax.ShapeDtypeStruct((seq, D_XBC), bf16), jax.ShapeDtypeStruct((seq, LANES), bf16),
                   jax.ShapeDtypeStruct((CONV_K, D_XBC), f32), jax.ShapeDtypeStruct((1, D_XBC), f32),
                   jax.ShapeDtypeStruct((1, LANES), f32), jax.ShapeDtypeStruct((1, LANES), f32), jax.ShapeDtypeStruct((1, LANES), f32)),
        scratch_shapes=[pltpu.VMEM((N_HEADS, HEAD_DIM, N_STATE), f32), pltpu.VMEM((t, D_XBC), f32)],
        compiler_params=_params(("arbitrary",)),
    )(dy, conv, xbc, dt_raw, states, conv_w, dtb, alog, dsk)


def _heads_to_cols(ref, width=HEAD_DIM):
    return jnp.concatenate([ref[h][:, :width] for h in range(N_HEADS)], axis=1)


def _silu_and_grad(z):
    sg = _sigmoid(z)
    return z * sg, sg * (1.0 + z * (1.0 - sg))


def _rms(v):
    return lax.rsqrt(jnp.mean(v * v, axis=-1, keepdims=True) + EPS)


def _rms_bwd(d_hat, hat, r):
    return r * (d_hat - hat * jnp.mean(d_hat * hat, axis=-1, keepdims=True))


def _post(x, target, o_tot, y, za, zs, w_out, gate, g_sb, g_ssm, g_f, seq):
    ts = 256

    def body(x_ref, t_ref, o_ref, y_ref, za_ref, zs_ref, w_ref, gate_ref, gsb_ref, gss_ref, gf_ref,
             ycat_t_ref, dmix_ref, dx2_ref, loss_ref, gnf_ref, dgate_ref):
        @pl.when(pl.program_id(0) == 0)
        def _():
            loss_ref[...] = jnp.zeros_like(loss_ref)
            gnf_ref[...] = jnp.zeros_like(gnf_ref)
            dgate_ref[...] = jnp.zeros_like(dgate_ref)

        o = _heads_to_cols(o_ref)
        zav = za_ref[...]
        ya = (o * _rms(o) * gsb_ref[...]) * (zav * _sigmoid(zav))
        zsv = zs_ref[...]
        u = y_ref[...] * (zsv * _sigmoid(zsv))
        ys = u * _rms(u) * gss_ref[...]
        yab, ysb = ya.astype(bf16), ys.astype(bf16)
        _store_transposed(ycat_t_ref.at[:D_ATTN], yab)
        _store_transposed(ycat_t_ref.at[D_ATTN:], ysb)
        mixed = (jnp.dot(yab, w_ref[:D_ATTN, :], preferred_element_type=f32)
                 + jnp.dot(ysb, w_ref[D_ATTN:, :], preferred_element_type=f32))
        gate_v = gate_ref[...]
        x2 = x_ref[...] + gate_v * mixed
        r2 = _rms(x2)
        xh = x2 * r2
        gf = gf_ref[...]
        diff = xh * gf - t_ref[...]
        loss_ref[...] += jnp.sum(diff * diff) * (0.5 / D_MODEL)
        d_out = diff * (1.0 / D_MODEL)
        gnf_ref[...] += jnp.sum(d_out * xh, axis=0, keepdims=True)
        dx2 = _rms_bwd(d_out * gf, xh, r2)
        dx2_ref[...] = dx2
        dgate_ref[...] += jnp.sum(dx2 * mixed, axis=0, keepdims=True)
        dmix_ref[...] = (dx2 * gate_v).astype(bf16)

    row = lambda i: (i, 0)
    fixed = lambda i: (0, 0)
    vec = pl.BlockSpec((1, D_MODEL), fixed)
    return pl.pallas_call(
        body, name="post", grid=(seq // ts,),
        in_specs=[pl.BlockSpec((ts, D_MODEL), row), pl.BlockSpec((ts, D_MODEL), row),
                  pl.BlockSpec((N_HEADS, ts, 2 * HEAD_DIM), lambda i: (0, i, 0)), pl.BlockSpec((ts, D_SSM), row),
                  pl.BlockSpec((ts, D_ATTN), row), pl.BlockSpec((ts, D_SSM), row), pl.BlockSpec((D_ATTN + D_SSM, D_MODEL), fixed),
                  vec, vec, vec, vec],
        out_specs=(pl.BlockSpec((D_ATTN + D_SSM, ts), lambda i: (0, i)), pl.BlockSpec((ts, D_MODEL), row), pl.BlockSpec((ts, D_MODEL), row),
                   pl.BlockSpec((1, LANES), fixed), vec, vec),
        out_shape=(jax.ShapeDtypeStruct((D_ATTN + D_SSM, seq), bf16), jax.ShapeDtypeStruct((seq, D_MODEL), bf16),
                   jax.ShapeDtypeStruct((seq, D_MODEL), f32), jax.ShapeDtypeStruct((1, LANES), f32),
                   jax.ShapeDtypeStruct((1, D_MODEL), f32), jax.ShapeDtypeStruct((1, D_MODEL), f32)),
        compiler_params=_params(("arbitrary",)),
    )(x, target, o_tot, y, za, zs, w_out, gate, g_sb, g_ssm, g_f)


def _bwd_out(dmix, w_out, o_tot, y, za, zs, g_sb, g_ssm, seq):
    ts = 256

    def body(dm_ref, w_ref, o_ref, y_ref, za_ref, zs_ref, gsb_ref, gss_ref, do_ref, dza_ref, dzs_ref, dy_ref, ggsb_ref, ggss_ref):
        @pl.when(pl.program_id(0) == 0)
        def _():
            ggsb_ref[...] = jnp.zeros_like(ggsb_ref)
            ggss_ref[...] = jnp.zeros_like(ggss_ref)

        dm = dm_ref[...]
        d_ya = _nt(dm, w_ref[:D_ATTN, :])
        d_ys = _nt(dm, w_ref[D_ATTN:, :])
        o = _heads_to_cols(o_ref)
        ro = _rms(o)
        oh = o * ro
        sa, dsa = _silu_and_grad(za_ref[...])
        gsb = gsb_ref[...]
        dza_ref[...] = (d_ya * oh * gsb * dsa).astype(bf16)
        ggsb_ref[...] += jnp.sum(d_ya * oh * sa, axis=0, keepdims=True)
        d_o = _rms_bwd(d_ya * gsb * sa, oh, ro)
        for h in range(N_HEADS):
            do_ref[h] = d_o[:, h * HEAD_DIM:(h + 1) * HEAD_DIM].astype(bf16)
        yv = y_ref[...]
        sz, dsz = _silu_and_grad(zs_ref[...])
        u = yv * sz
        ru = _rms(u)
        uh = u * ru
        ggss_ref[...] += jnp.sum(d_ys * uh, axis=0, keepdims=True)
        du = _rms_bwd(d_ys * gss_ref[...], uh, ru)
        dy_ref[...] = du * sz
        dzs_ref[...] = (du * yv * dsz).astype(bf16)

    row = lambda i: (i, 0)
    fixed = lambda i: (0, 0)
    vec = pl.BlockSpec((1, D_MODEL), fixed)
    return pl.pallas_call(
        body, name="bwd_out", grid=(seq // ts,),
        in_specs=[pl.BlockSpec((ts, D_MODEL), row), pl.BlockSpec((D_ATTN + D_SSM, D_MODEL), fixed),
                  pl.BlockSpec((N_HEADS, ts, 2 * HEAD_DIM), lambda i: (0, i, 0)), pl.BlockSpec((ts, D_SSM), row),
                  pl.BlockSpec((ts, D_ATTN), row), pl.BlockSpec((ts, D_SSM), row), vec, vec],
        out_specs=(pl.BlockSpec((N_HEADS, ts, HEAD_DIM), lambda i: (0, i, 0)), pl.BlockSpec((ts, D_ATTN), row),
                   pl.BlockSpec((ts, D_SSM), row), pl.BlockSpec((ts, D_SSM), row), vec, vec),
        out_shape=(jax.ShapeDtypeStruct((N_HEADS, seq, HEAD_DIM), bf16), jax.ShapeDtypeStruct((seq, D_ATTN), bf16),
                   jax.ShapeDtypeStruct((seq, D_SSM), bf16), jax.ShapeDtypeStruct((seq, D_SSM), f32),
                   jax.ShapeDtypeStruct((1, D_MODEL), f32), jax.ShapeDtypeStruct((1, D_MODEL), f32)),
        compiler_params=_params(("arbitrary",)),
    )(dmix, w_out, o_tot, y, za, zs, g_sb, g_ssm)


def _qkv_grads_to_cols(dq, dkt, dvt, seq):
    ts = 256
    nb = ts // CHUNK

    def body(dq_ref, dkt_ref, dvt_ref, out_ref):
        out_ref[:, :D_ATTN] = _heads_to_cols(dq_ref).astype(bf16)
        eye = (lax.broadcasted_iota(jnp.int32, (CHUNK, CHUNK), 0) == lax.broadcasted_iota(jnp.int32, (CHUNK, CHUNK), 1)).astype(bf16)
        for p, ref in ((1, dkt_ref), (2, dvt_ref)):
            for b in range(nb):
                cols = [_nt(eye, ref[h, b].astype(bf16)) for h in range(N_HEADS)]
                out_ref[b * CHUNK:(b + 1) * CHUNK, p * D_ATTN:(p + 1) * D_ATTN] = jnp.concatenate(cols, axis=1).astype(bf16)

    blk = pl.BlockSpec((N_HEADS, ts, HEAD_DIM), lambda i: (0, i, 0))
    blk_t = pl.BlockSpec((N_HEADS, nb, HEAD_DIM, CHUNK), lambda i: (0, i, 0, 0))
    return pl.pallas_call(
        body, name="qkv_grads_to_cols", grid=(seq // ts,), in_specs=[blk, blk_t, blk_t],
        out_specs=pl.BlockSpec((ts, 3 * D_ATTN), lambda i: (i, 0)),
        out_shape=jax.ShapeDtypeStruct((seq, 3 * D_ATTN), bf16), compiler_params=_params(("parallel",)),
    )(dq, dkt, dvt)


def _bwd_in(dqkv, dza, dzs, dxbc, ddt, wp, x, dx2, gain, scale, seq):
    ts = 256
    pieces = ((0, 0, 3 * D_ATTN), (1, OFF_ZA, D_ATTN), (2, OFF_ZS, D_SSM), (3, OFF_XBC, D_XBC), (4, OFF_DT, LANES))

    def body(dqkv_ref, dza_ref, dzs_ref, dxbc_ref, ddt_ref, w_ref, x_ref, dx2_ref, g_ref, sc_ref,
             gx_ref, dshift_ref, dscale_ref, ggain_ref):
        @pl.when(pl.program_id(0) == 0)
        def _():
            dshift_ref[...] = jnp.zeros_like(dshift_ref)
            dscale_ref[...] = jnp.zeros_like(dscale_ref)
            ggain_ref[...] = jnp.zeros_like(ggain_ref)

        refs = (dqkv_ref, dza_ref, dzs_ref, dxbc_ref, ddt_ref)
        dh = jnp.zeros((ts, D_MODEL), f32)
        for idx, off, width in pieces:
            for cc in range(0, width, 512):
                wd = min(512, width - cc)
                dh = dh + _nt(refs[idx][:, cc:cc + wd], w_ref[:, off + cc:off + cc + wd])
        xv = x_ref[...]
        r = _rms(xv)
        xh = xv * r
        g = g_ref[...]
        dshift_ref[...] += jnp.sum(dh, axis=0, keepdims=True)
        dscale_ref[...] += jnp.sum(dh * xh * g, axis=0, keepdims=True)
        tt = dh * (1.0 + sc_ref[...])
        ggain_ref[...] += jnp.sum(tt * xh, axis=0, keepdims=True)
        gx_ref[...] = dx2_ref[...] + _rms_bwd(tt * g, xh, r)

    row = lambda i: (i, 0)
    fixed = lambda i: (0, 0)
    vec = pl.BlockSpec((1, D_MODEL), fixed)
    return pl.pallas_call(
        body, name="bwd_in", grid=(seq // ts,),
        in_specs=[pl.BlockSpec((ts, 3 * D_ATTN), row), pl.BlockSpec((ts, D_ATTN), row), pl.BlockSpec((ts, D_SSM), row),
                  pl.BlockSpec((ts, D_XBC), row), pl.BlockSpec((ts, LANES), row), pl.BlockSpec((D_MODEL, D_PROJ_P), fixed),
                  pl.BlockSpec((ts, D_MODEL), row), pl.BlockSpec((ts, D_MODEL), row), vec, vec],
        out_specs=(pl.BlockSpec((ts, D_MODEL), row), vec, vec, vec),
        out_shape=(jax.ShapeDtypeStruct((seq, D_MODEL), f32), jax.ShapeDtypeStruct((1, D_MODEL), f32),
                   jax.ShapeDtypeStruct((1, D_MODEL), f32), jax.ShapeDtypeStruct((1, D_MODEL), f32)),
        compiler_params=_params(("arbitrary",)),
    )(dqkv, dza, dzs, dxbc, ddt, wp, x, dx2, gain, scale)


def _grad_w(a_t, b, tn, name):
    m, seq = a_t.shape
    n = b.shape[1]
    tk = min(512, seq)
    n_k = seq // tk

    def body(a_ref, b_ref, o_ref, acc):
        @pl.when(pl.program_id(1) == 0)
        def _():
            acc[...] = jnp.zeros_like(acc)

        acc[...] += jnp.dot(a_ref[...], b_ref[...], preferred_element_type=f32)

        @pl.when(pl.program_id(1) == n_k - 1)
        def _():
            o_ref[...] = acc[...].astype(bf16)

    return pl.pallas_call(
        body, name=name, grid=(n // tn, n_k),
        in_specs=[pl.BlockSpec((m, tk), lambda j, k: (0, k)), pl.BlockSpec((tk, tn), lambda j, k: (k, j))],
        out_specs=pl.BlockSpec((m, tn), lambda j, k: (0, j)),
        out_shape=jax.ShapeDtypeStruct((m, n), bf16), scratch_shapes=[pltpu.VMEM((m, tn), f32)],
        compiler_params=_params(("parallel", "arbitrary")),
    )(a_t, b)


def _small_finish(g_all, c_all, dmod_mine):
    def body(g_ref, c_ref, dm_ref, tot_ref, gwada_ref):
        tot = g_ref[0:1, :]
        for j in range(1, N_DEV):
            tot = tot + g_ref[j:j + 1, :]
        tot_ref[...] = tot
        cv = c_ref[...]
        gwada_ref[...] = _tn(cv * _sigmoid(cv), dm_ref[...], HI)

    vmem = pl.BlockSpec(memory_space=pltpu.VMEM)
    return pl.pallas_call(
        body, name="small_finish", in_specs=[vmem, vmem, vmem], out_specs=(vmem, vmem),
        out_shape=(jax.ShapeDtypeStruct((1, N_PACK), f32), jax.ShapeDtypeStruct((D_MODEL, dmod_mine.shape[1]), f32)),
        compiler_params=_params(),
    )(g_all, c_all, dmod_mine)


def _adamw(w, g_parts, m, v, rows, name):
    r, c = w.shape
    n_parts = g_parts.shape[0]
    bc1 = 1.0 - ADAM_B1 ** ADAM_STEP
    bc2 = 1.0 - ADAM_B2 ** ADAM_STEP

    def body(w_ref, g_ref, m_ref, v_ref, go_ref, d_ref, mo_ref, vo_ref):
        g = g_ref[0].astype(f32)
        for j in range(1, n_parts):
            g = g + g_ref[j].astype(f32)
        go_ref[...] = g
        mn = ADAM_B1 * m_ref[...] + (1.0 - ADAM_B1) * g
        vn = ADAM_B2 * v_ref[...] + (1.0 - ADAM_B2) * (g * g)
        mo_ref[...] = mn
        vo_ref[...] = vn
        d_ref[...] = -ADAM_LR * ((mn / bc1) / (jnp.sqrt(vn / bc2) + ADAM_EPS) + ADAM_WD * w_ref[...])

    blk = pl.BlockSpec((rows, c), lambda i: (i, 0))
    return pl.pallas_call(
        body, name=name, grid=(r // rows,),
        in_specs=[blk, pl.BlockSpec((n_parts, rows, c), lambda i: (0, i, 0)), blk, blk],
        out_specs=(blk, blk, blk, blk), out_shape=(jax.ShapeDtypeStruct((r, c), f32),) * 4,
        compiler_params=_params(("parallel",)),
    )(w, g_parts, m, v)


def _pad_lanes(v):
    return jnp.pad(v, ((0, 0), (0, LANES - v.shape[1])))


def kernel(x, c, w_ada, b_ada, norm_in_gain, w_in, conv_w, conv_b, dt_bias, a_log, d_skip, sb_norm_gain, ssm_norm_gain, w_out, norm_f_gain, loss_target, m_w_ada, m_b_ada, m_norm_in_gain, m_w_in, m_conv_w, m_conv_b, m_dt_bias, m_a_log, m_d_skip, m_sb_norm_gain, m_ssm_norm_gain, m_w_out, m_norm_f_gain, v_w_ada, v_b_ada, v_norm_in_gain, v_w_in, v_conv_w, v_conv_b, v_dt_bias, v_a_log, v_d_skip, v_sb_norm_gain, v_ssm_norm_gain, v_w_out, v_norm_f_gain):
    seq = x.shape[1]
    xs = x[0]
    tgt = loss_target[0]
    _, my_slot = _me()

    mod, c_all = _mod_exchange(c, w_ada[0], b_ada)
    shift, scale, gate = mod[:, :D_MODEL], mod[:, D_MODEL:2 * D_MODEL], mod[:, 2 * D_MODEL:]
    w_in_g, w_out_g, conv_w_g = _all_gather_two_level(
        [_cast_bf16(w_in[0], 128), _cast_bf16(w_out[0], 128), conv_w[0]], "gather_weights")
    w_full = jnp.transpose(w_in_g, (1, 0, 2)).reshape(D_MODEL, D_PROJ)
    wp = jnp.concatenate([w_full[:, :4 * D_ATTN], w_full[:, D_PROJ - D_SSM:], w_full[:, 4 * D_ATTN:4 * D_ATTN + D_XBC],
                          _pad_lanes(w_full[:, 4 * D_ATTN + D_XBC:4 * D_ATTN + D_XBC + N_HEADS])], axis=1)
    w_out_full = w_out_g.reshape(D_ATTN + D_SSM, D_MODEL)
    conv_w_full = jnp.transpose(conv_w_g, (1, 0, 2)).reshape(CONV_K, D_XBC)
    dtb, alog, dsk = _pad_lanes(dt_bias), _pad_lanes(a_log), _pad_lanes(d_skip)

    h_t, qkv, za, zs, xbc, dt_raw = _proj(xs, norm_in_gain, scale, shift, wp, seq)
    o_tot = _attn_fwd(qkv, seq)
    conv, y, states = _ssd_fwd(xbc, dt_raw, conv_w_full, conv_b, dtb, alog, dsk, seq)
    ycat_t, dmix, dx2, loss_p, g_nf, d_gate = _post(xs, tgt, o_tot, y, za, zs, w_out_full, gate, sb_norm_gain, ssm_norm_gain,
                                                  norm_f_gain.reshape(1, D_MODEL), seq)

    d_o, dza, dzs, dy, g_sb, g_ss = _bwd_out(dmix, w_out_full, o_tot, y, za, zs, sb_norm_gain, ssm_norm_gain, seq)
    dq, dk, dv = _attn_bwd(qkv, o_tot, d_o, seq)
    dxbc, ddt, g_cw, g_cb, g_dtb, g_al, g_dsk = _ssd_bwd(dy, conv, xbc, dt_raw, states, conv_w_full, dtb, alog, dsk, seq)
    dqkv = _qkv_grads_to_cols(dq, dk, dv, seq)
    grad_x, d_shift, d_scale, g_in = _bwd_in(dqkv, dza, dzs, dxbc, ddt, wp, xs, dx2, norm_in_gain, scale, seq)
    gw_qkv = _grad_w(h_t, dqkv, 512, "grad_w_qkv")
    gw_za = _grad_w(h_t, dza, 512, "grad_w_za")
    gw_zs = _grad_w(h_t, dzs, 512, "grad_w_zs")
    gw_xbc = _grad_w(h_t, dxbc, 512, "grad_w_xbc")
    gw_dt = _grad_w(h_t, ddt, LANES, "grad_w_dt")
    gw_out = _grad_w(ycat_t, dmix, 512, "grad_w_out")
    gw_in = jnp.concatenate([gw_qkv, gw_za, gw_xbc, gw_dt[:, :N_HEADS], gw_zs], axis=1)

    gw_in_parts, gw_out_parts = _reduce_scatter_two_level(
        [jnp.transpose(gw_in.reshape(D_MODEL, N_DEV, W_IN_SHARD), (1, 0, 2)),
         gw_out.reshape(N_DEV, (D_ATTN + D_SSM) // N_DEV, D_MODEL)], "scatter_grads")
    packed = jnp.concatenate([loss_p, d_shift, d_scale, d_gate, g_in, g_cb, g_dtb, g_al, g_dsk, g_sb, g_ss, g_nf,
                              g_cw.reshape(1, CONV_K * D_XBC)], axis=1)
    (packed_all,) = _all_gather([packed], "gather_small")
    packed_all = packed_all.reshape(N_DEV, N_PACK)
    n_ada = w_ada.shape[2]
    dmod_mine = lax.dynamic_slice(packed_all, (0, P_DMOD + my_slot * n_ada), (N_DEV, n_ada))
    tot, g_w_ada = _small_finish(packed_all, c_all.reshape(N_DEV, D_MODEL), dmod_mine)

    def big(w, parts, m, v, rows, name):
        return tuple(t[None] for t in _adamw(w[0], parts, m[0], v[0], rows, name))

    small_names = ["b_ada", "norm_in_gain", "conv_b", "dt_bias", "a_log", "d_skip", "sb_norm_gain", "ssm_norm_gain", "norm_f_gain"]
    given = {"b_ada": (b_ada, m_b_ada, v_b_ada), "norm_in_gain": (norm_in_gain, m_norm_in_gain, v_norm_in_gain),
             "conv_b": (conv_b, m_conv_b, v_conv_b), "dt_bias": (dt_bias, m_dt_bias, v_dt_bias), "a_log": (a_log, m_a_log, v_a_log),
             "d_skip": (d_skip, m_d_skip, v_d_skip), "sb_norm_gain": (sb_norm_gain, m_sb_norm_gain, v_sb_norm_gain),
             "ssm_norm_gain": (ssm_norm_gain, m_ssm_norm_gain, v_ssm_norm_gain), "norm_f_gain": (norm_f_gain, m_norm_f_gain, v_norm_f_gain)}

    def pack(which):
        cols = []
        for nm in small_names:
            t = given[nm][which].reshape(1, -1)
            cols.append(_pad_lanes(t) if t.shape[1] < LANES else t)
        return jnp.concatenate(cols, axis=1)

    packed_out = _adamw(pack(0), tot[:, P_DMOD:P_CW][None], pack(1), pack(2), 1, "adamw_small")
    res = {}
    off = 0
    for nm in small_names:
        shape = given[nm][0].shape
        size = given[nm][0].size
        res[nm] = tuple(t[:, off:off + size].reshape(shape) for t in packed_out)
        off += max(size, LANES)
    n_cw = conv_w.shape[2]
    g_cw_mine = lax.dynamic_slice(tot[:, P_CW:].reshape(CONV_K, D_XBC), (0, my_slot * n_cw), (CONV_K, n_cw))
    res["conv_w"] = tuple(t.reshape(conv_w.shape) for t in _adamw(conv_w.reshape(1, -1), g_cw_mine.reshape(1, 1, -1),
                                                                  m_conv_w.reshape(1, -1), v_conv_w.reshape(1, -1), 1, "adamw_conv_w"))
    res["w_ada"] = big(w_ada, g_w_ada[None], m_w_ada, v_w_ada, 128, "adamw_w_ada")
    res["w_in"] = big(w_in, gw_in_parts, m_w_in, v_w_in, 128, "adamw_w_in")
    res["w_out"] = big(w_out, gw_out_parts, m_w_out, v_w_out, 64, "adamw_w_out")
    names = ["w_ada", "b_ada", "norm_in_gain", "w_in", "conv_w", "conv_b", "dt_bias", "a_log", "d_skip", "sb_norm_gain",
             "ssm_norm_gain", "w_out", "norm_f_gain"]
    loss = tot[0, P_LOSS]
    return (loss, grad_x[None], *[res[n][0] for n in names], *[res[n][1] for n in names],
            *[res[n][2] for n in names], *[res[n][3] for n in names])
```

```python
import functools

import jax
import jax.numpy as jnp
from jax import lax
from jax.experimental import pallas as pl
from jax.experimental.pallas import tpu as pltpu

f32 = jnp.float32
bf16 = jnp.bfloat16
MESH = pl.DeviceIdType.MESH
HI = lax.Precision.HIGHEST

N_DEV = 8
D_MODEL = 1024
D_ATTN = 1024
D_SSM = 1024
N_HEADS = 16
HEAD_DIM = 64
N_GROUPS = 2
HEADS_PER_GROUP = 8
N_STATE = 128
D_XBC = D_SSM + 2 * N_GROUPS * N_STATE
D_PROJ = 4 * D_ATTN + D_XBC + N_HEADS + D_SSM
W_IN_SHARD = D_PROJ // N_DEV
CONV_K = 4
CHUNK = 128
ATTN_Q_ROWS = 2048
ATTN_UNROLL = 8
LANES = 128
EPS = 1e-6
OFF_ZA = 3072
OFF_ZS = 4096
OFF_XBC = 5120
OFF_DT = 6656
D_PROJ_P = 6784
VMEM_LIMIT_BYTES = 56 * 1024 * 1024

ADAM_LR = 0.001
ADAM_B1 = 0.9
ADAM_B2 = 0.999
ADAM_EPS = 1e-08
ADAM_WD = 0.01
ADAM_STEP = 10

P_LOSS = 0
P_DMOD = 128
P_GIN = 3200
P_CB = 4224
P_DTB = 5760
P_ALOG = 5888
P_DSK = 6016
P_GSB = 6144
P_GSS = 7168
P_GNF = 8192
P_CW = 9216
N_PACK = 15360


def _params(sem=None):
    return pltpu.CompilerParams(dimension_semantics=sem, vmem_limit_bytes=VMEM_LIMIT_BYTES)


def _sigmoid(v):
    return 1.0 / (1.0 + jnp.exp(-v))


def _softplus(v):
    return jnp.maximum(v, 0.0) + jnp.log(1.0 + jnp.exp(-jnp.abs(v)))


def _nt(a, b, precision=None):
    return lax.dot_general(a, b, (((1,), (1,)), ((), ())), preferred_element_type=f32, precision=precision)


def _tn(a, b, precision=None):
    return lax.dot_general(a, b, (((0,), (0,)), ((), ())), preferred_element_type=f32, precision=precision)


def _nn(a, b, precision=None):
    return lax.dot_general(a, b, (((1,), (0,)), ((), ())), preferred_element_type=f32, precision=precision)


def _me():
    x, y, c = lax.axis_index("x"), lax.axis_index("y"), lax.axis_index("c")
    return (x, y, c), 4 * x + 2 * y + c


def _peer(k):
    x, y, c = lax.axis_index("x"), lax.axis_index("y"), lax.axis_index("c")
    px = 1 - x if (k >> 2) & 1 else x
    py = 1 - y if (k >> 1) & 1 else y
    pc = 1 - c if k & 1 else c
    return (px, py, pc), 4 * px + 2 * py + pc


def _all_gather(arrs, name):
    n = len(arrs)

    def body(*refs):
        ins, outs = refs[:n], refs[n:2 * n]
        send_sems, recv_sems, local_sems = refs[2 * n:]
        _, my_slot = _me()
        sends = []
        locals_ = []
        for a in range(n):
            loc = pltpu.make_async_copy(ins[a], outs[a].at[my_slot], local_sems.at[a])
            loc.start()
            locals_.append(loc)
            for k in range(1, N_DEV):
                peer, _ = _peer(k)
                cp = pltpu.make_async_remote_copy(src_ref=ins[a], dst_ref=outs[a].at[my_slot], send_sem=send_sems.at[a, k - 1],
                                                  recv_sem=recv_sems.at[a, k - 1], device_id=peer, device_id_type=MESH)
                cp.start()
                sends.append(cp)
        for a in range(n):
            for k in range(1, N_DEV):
                peer, peer_slot = _peer(k)
                pltpu.make_async_remote_copy(src_ref=ins[a], dst_ref=outs[a].at[peer_slot], send_sem=send_sems.at[a, k - 1],
                                             recv_sem=recv_sems.at[a, k - 1], device_id=peer, device_id_type=MESH).wait_recv()
        for cp in sends:
            cp.wait_send()
        for loc in locals_:
            loc.wait()

    any_spec = pl.BlockSpec(memory_space=pl.ANY)
    return pl.pallas_call(
        body, name=name,
        out_shape=tuple(jax.ShapeDtypeStruct((N_DEV,) + a.shape, a.dtype) for a in arrs),
        in_specs=[any_spec] * n, out_specs=tuple([any_spec] * n),
        scratch_shapes=[pltpu.SemaphoreType.DMA((n, N_DEV - 1)), pltpu.SemaphoreType.DMA((n, N_DEV - 1)),
                        pltpu.SemaphoreType.DMA((n,))],
    )(*arrs)


def _all_gather_two_level(arrs, name):
    n = len(arrs)

    def body(*refs):
        ins, outs = refs[:n], refs[n:2 * n]
        send_sems, recv_sems, local_sems = refs[2 * n:]
        x, y, c = lax.axis_index("x"), lax.axis_index("y"), lax.axis_index("c")
        me, sibling = (x, y, c), (x, y, 1 - c)
        chips = [(1 - x, y), (x, 1 - y), (1 - x, 1 - y)]

        def copy(a, k, block, to, from_input=False):
            slot = 4 * block[0] + 2 * block[1] + block[2]
            return pltpu.make_async_remote_copy(src_ref=ins[a] if from_input else outs[a].at[slot], dst_ref=outs[a].at[slot],
                                                send_sem=send_sems.at[a, k], recv_sem=recv_sems.at[a, k], device_id=to,
                                                device_id_type=MESH)

        started = []
        locals_ = []
        for a in range(n):
            loc = pltpu.make_async_copy(ins[a], outs[a].at[4 * x + 2 * y + c], local_sems.at[a])
            loc.start()
            locals_.append(loc)
            first = [copy(a, 0, me, sibling, True)] + [copy(a, 1 + j, me, (*chip, c), True) for j, chip in enumerate(chips)]
            for cp in first:
                cp.start()
            started += first
        for a in range(n):
            for j, chip in enumerate(chips):
                copy(a, 1 + j, (*chip, c), me).wait_recv()
                onward = copy(a, 4 + j, (*chip, c), sibling)
                onward.start()
                started.append(onward)
        for a in range(n):
            copy(a, 0, sibling, me).wait_recv()
            for j, chip in enumerate(chips):
                copy(a, 4 + j, (*chip, 1 - c), me).wait_recv()
        for cp in started:
            cp.wait_send()
        for loc in locals_:
            loc.wait()

    any_spec = pl.BlockSpec(memory_space=pl.ANY)
    return pl.pallas_call(
        body, name=name,
        out_shape=tuple(jax.ShapeDtypeStruct((N_DEV,) + a.shape, a.dtype) for a in arrs),
        in_specs=[any_spec] * n, out_specs=tuple([any_spec] * n),
        scratch_shapes=[pltpu.SemaphoreType.DMA((n, N_DEV - 1)), pltpu.SemaphoreType.DMA((n, N_DEV - 1)),
                        pltpu.SemaphoreType.DMA((n,))],
    )(*arrs)


def _all_to_all(arrs, name):
    n = len(arrs)

    def body(*refs):
        ins, outs = refs[:n], refs[n:2 * n]
        send_sems, recv_sems, local_sems = refs[2 * n:]
        _, my_slot = _me()
        sends = []
        locals_ = []
        for a in range(n):
            loc = pltpu.make_async_copy(ins[a].at[my_slot], outs[a].at[my_slot], local_sems.at[a])
            loc.start()
            locals_.append(loc)
            for k in range(1, N_DEV):
                peer, peer_slot = _peer(k)
                cp = pltpu.make_async_remote_copy(src_ref=ins[a].at[peer_slot], dst_ref=outs[a].at[my_slot],
                                                  send_sem=send_sems.at[a, k - 1], recv_sem=recv_sems.at[a, k - 1],
                                                  device_id=peer, device_id_type=MESH)
                cp.start()
                sends.append(cp)
        for a in range(n):
            for k in range(1, N_DEV):
                peer, peer_slot = _peer(k)
                pltpu.make_async_remote_copy(src_ref=ins[a].at[peer_slot], dst_ref=outs[a].at[peer_slot],
                                             send_sem=send_sems.at[a, k - 1], recv_sem=recv_sems.at[a, k - 1],
                                             device_id=peer, device_id_type=MESH).wait_recv()
        for cp in sends:
            cp.wait_send()
        for loc in locals_:
            loc.wait()

    any_spec = pl.BlockSpec(memory_space=pl.ANY)
    return pl.pallas_call(
        body, name=name,
        out_shape=tuple(jax.ShapeDtypeStruct(a.shape, a.dtype) for a in arrs),
        in_specs=[any_spec] * n, out_specs=tuple([any_spec] * n),
        scratch_shapes=[pltpu.SemaphoreType.DMA((n, N_DEV - 1)), pltpu.SemaphoreType.DMA((n, N_DEV - 1)),
                        pltpu.SemaphoreType.DMA((n,))],
    )(*arrs)


def _reduce_scatter_two_level(arrs, name):
    n = len(arrs)
    n_chip = N_DEV // 2

    def body(*refs):
        ins, outs = refs[:n], refs[n:2 * n]
        mine_bufs, sib_bufs = refs[2 * n:3 * n], refs[3 * n:4 * n]
        send_sems, recv_sems, local_sems = refs[4 * n:]
        x, y, c = lax.axis_index("x"), lax.axis_index("y"), lax.axis_index("c")
        sibling = (x, y, 1 - c)

        def chip(r):
            return (1 - x if r & 2 else x), (1 - y if r & 1 else y)

        def slot(r, core):
            cx, cy = chip(r)
            return 4 * cx + 2 * cy + core

        def to_sibling(a, r):
            return pltpu.make_async_remote_copy(src_ref=ins[a].at[slot(r, 1 - c)], dst_ref=sib_bufs[a].at[r], send_sem=send_sems.at[a, r],
                                                recv_sem=recv_sems.at[a, r], device_id=sibling, device_id_type=MESH)

        def to_chip(a, r):
            return pltpu.make_async_remote_copy(src_ref=mine_bufs[a].at[r], dst_ref=outs[a].at[r], send_sem=send_sems.at[a, n_chip - 1 + r],
                                                recv_sem=recv_sems.at[a, n_chip - 1 + r], device_id=(*chip(r), c), device_id_type=MESH)

        def load_mine(a, r):
            return pltpu.make_async_copy(ins[a].at[slot(r, c)], mine_bufs[a].at[r], local_sems.at[a, r])

        started = []
        order = [1, 2, 3, 0]
        for a in range(n):
            for r in order:
                load_mine(a, r).start()
                cp = to_sibling(a, r)
                cp.start()
                started.append(cp)
        for a in range(n):
            for r in order:
                load_mine(a, r).wait()
                to_sibling(a, r).wait_recv()
                total = (mine_bufs[a][r].astype(f32) + sib_bufs[a][r].astype(f32)).astype(bf16)
                if r:
                    mine_bufs[a][r] = total
                    cp = to_chip(a, r)
                    cp.start()
                    started.append(cp)
                else:
                    outs[a][0] = total
        for a in range(n):
            for r in range(1, n_chip):
                to_chip(a, r).wait_recv()
        for cp in started:
            cp.wait_send()

    any_spec = pl.BlockSpec(memory_space=pl.ANY)
    vmem = pl.BlockSpec(memory_space=pltpu.VMEM)
    part = lambda a: (n_chip,) + a.shape[1:]
    return pl.pallas_call(
        body, name=name,
        out_shape=tuple(jax.ShapeDtypeStruct(part(a), a.dtype) for a in arrs),
        in_specs=[any_spec] * n, out_specs=tuple([vmem] * n),
        scratch_shapes=([pltpu.VMEM(part(a), a.dtype) for a in arrs] + [pltpu.VMEM(part(a), a.dtype) for a in arrs]
                        + [pltpu.SemaphoreType.DMA((n, N_DEV - 1)), pltpu.SemaphoreType.DMA((n, N_DEV - 1)),
                           pltpu.SemaphoreType.DMA((n, n_chip))]),
        compiler_params=_params(),
    )(*arrs)


def _mod_exchange(c_row, w_ada, b_ada):
    n_col = w_ada.shape[1]

    def body(c_ref, w_ref, b_ref, mod_ref, call_ref, part, modp, send_sems, recv_sems):
        _, my_slot = _me()
        call_ref[my_slot] = c_ref[...]
        sends = []
        for k in range(1, N_DEV):
            peer, _ = _peer(k)
            cp = pltpu.make_async_remote_copy(src_ref=c_ref, dst_ref=call_ref.at[my_slot], send_sem=send_sems.at[0, k - 1],
                                              recv_sem=recv_sems.at[0, k - 1], device_id=peer, device_id_type=MESH)
            cp.start()
            sends.append(cp)
        for k in range(1, N_DEV):
            peer, peer_slot = _peer(k)
            pltpu.make_async_remote_copy(src_ref=c_ref, dst_ref=call_ref.at[peer_slot], send_sem=send_sems.at[0, k - 1],
                                         recv_sem=recv_sems.at[0, k - 1], device_id=peer, device_id_type=MESH).wait_recv()
        for cp in sends:
            cp.wait_send()
        w = w_ref[...]
        for b in range(N_DEV):
            cb = call_ref[b]
            part[b] = _nn(cb * _sigmoid(cb), w, HI)
        modp[my_slot] = part[my_slot]
        sends = []
        for k in range(1, N_DEV):
            peer, peer_slot = _peer(k)
            cp = pltpu.make_async_remote_copy(src_ref=part.at[peer_slot], dst_ref=modp.at[my_slot], send_sem=send_sems.at[1, k - 1],
                                              recv_sem=recv_sems.at[1, k - 1], device_id=peer, device_id_type=MESH)
            cp.start()
            sends.append(cp)
        for k in range(1, N_DEV):
            peer, peer_slot = _peer(k)
            pltpu.make_async_remote_copy(src_ref=part.at[peer_slot], dst_ref=modp.at[peer_slot], send_sem=send_sems.at[1, k - 1],
                                         recv_sem=recv_sems.at[1, k - 1], device_id=peer, device_id_type=MESH).wait_recv()
        for cp in sends:
            cp.wait_send()
        for j in range(N_DEV):
            mod_ref[:, j * n_col:(j + 1) * n_col] = modp[j] + b_ref[:, j * n_col:(j + 1) * n_col]

    vmem = pl.BlockSpec(memory_space=pltpu.VMEM)
    return pl.pallas_call(
        body, name="mod_exchange",
        out_shape=(jax.ShapeDtypeStruct((1, N_DEV * n_col), f32), jax.ShapeDtypeStruct((N_DEV, 1, D_MODEL), f32)),
        in_specs=[vmem, vmem, vmem], out_specs=(vmem, vmem),
        scratch_shapes=[pltpu.VMEM((N_DEV, 1, n_col), f32), pltpu.VMEM((N_DEV, 1, n_col), f32),
                        pltpu.SemaphoreType.DMA((2, N_DEV - 1)), pltpu.SemaphoreType.DMA((2, N_DEV - 1))],
        compiler_params=_params(),
    )(c_row, w_ada, b_ada)


def _cast_bf16(a, rows):
    r, c = a.shape

    def body(a_ref, o_ref):
        o_ref[...] = a_ref[...].astype(bf16)

    return pl.pallas_call(
        body, name="cast_bf16", grid=(r // rows,),
        in_specs=[pl.BlockSpec((rows, c), lambda i: (i, 0))], out_specs=pl.BlockSpec((rows, c), lambda i: (i, 0)),
        out_shape=jax.ShapeDtypeStruct((r, c), bf16), compiler_params=_params(("parallel",)),
    )(a)


def _store_transposed(out_ref, v):
    blk = 256
    eye = (lax.broadcasted_iota(jnp.int32, (blk, blk), 0) == lax.broadcasted_iota(jnp.int32, (blk, blk), 1)).astype(bf16)
    for cb in range(0, v.shape[1], blk):
        out_ref[cb:cb + blk, :] = _nt(eye, v[:, cb:cb + blk]).astype(bf16)


def _proj(x, gain, scale, shift, wp, seq):
    ts = 256

    def body(x_ref, g_ref, sc_ref, sh_ref, w_ref, ht_ref, qkv_ref, za_ref, zs_ref, xbc_ref, dt_ref):
        xv = x_ref[...]
        r = lax.rsqrt(jnp.mean(xv * xv, axis=-1, keepdims=True) + EPS)
        hb = ((xv * r * g_ref[...]) * (1.0 + sc_ref[...]) + sh_ref[...]).astype(bf16)
        _store_transposed(ht_ref, hb)
        for cb in range(3 * D_ATTN // 256):
            res = jnp.dot(hb, w_ref[:, cb * 256:(cb + 1) * 256], preferred_element_type=f32)
            for u in range(4):
                qkv_ref[cb * 4 + u] = res[:, u * HEAD_DIM:(u + 1) * HEAD_DIM].astype(bf16)
        for out_ref, off, width in ((za_ref, OFF_ZA, D_ATTN), (zs_ref, OFF_ZS, D_SSM), (xbc_ref, OFF_XBC, D_XBC), (dt_ref, OFF_DT, LANES)):
            for cc in range(0, width, 512):
                wd = min(512, width - cc)
                out_ref[:, cc:cc + wd] = jnp.dot(hb, w_ref[:, off + cc:off + cc + wd], preferred_element_type=f32)

    row = lambda i: (i, 0)
    fixed = lambda i: (0, 0)
    return pl.pallas_call(
        body, name="proj", grid=(seq // ts,),
        in_specs=[pl.BlockSpec((ts, D_MODEL), row), pl.BlockSpec((1, D_MODEL), fixed), pl.BlockSpec((1, D_MODEL), fixed),
                  pl.BlockSpec((1, D_MODEL), fixed), pl.BlockSpec((D_MODEL, D_PROJ_P), fixed)],
        out_specs=(pl.BlockSpec((D_MODEL, ts), lambda i: (0, i)), pl.BlockSpec((3 * N_HEADS, ts, HEAD_DIM), lambda i: (0, i, 0)),
                   pl.BlockSpec((ts, D_ATTN), row), pl.BlockSpec((ts, D_SSM), row), pl.BlockSpec((ts, D_XBC), row),
                   pl.BlockSpec((ts, LANES), row)),
        out_shape=(jax.ShapeDtypeStruct((D_MODEL, seq), bf16), jax.ShapeDtypeStruct((3 * N_HEADS, seq, HEAD_DIM), bf16),
                   jax.ShapeDtypeStruct((seq, D_ATTN), f32), jax.ShapeDtypeStruct((seq, D_SSM), f32),
                   jax.ShapeDtypeStruct((seq, D_XBC), f32), jax.ShapeDtypeStruct((seq, LANES), f32)),
        compiler_params=_params(("arbitrary",)),
    )(x, gain, scale, shift, wp)


def _log_sigmoids(z):
    lb = jnp.minimum(z, 0.0) - jnp.log(1.0 + jnp.exp(-jnp.abs(z)))
    return lb, lb - z


def _split_bf16(v):
    hi = v.astype(bf16)
    return hi, (v - hi.astype(f32)).astype(bf16)


def _attn_fwd(qkv, seq):
    t = CHUNK
    tq = min(ATTN_Q_ROWS, seq)
    nd = tq // t
    unroll = min(ATTN_UNROLL, nd)
    assert nd % unroll == 0

    def body(q_ref, k_ref, v_ref, o_ref):
        i = pl.program_id(1)
        q = q_ref[0] * 0.125
        ur = lax.broadcasted_iota(jnp.int32, (2 * t, t), 0)
        upper = ((ur & (t - 1)) > lax.broadcasted_iota(jnp.int32, (2 * t, t), 1)).astype(bf16)

        def tile(j, q_s, acc, run, masked):
            n = q_s.shape[0]
            start = pl.multiple_of(j * t, t)
            k = k_ref[0, pl.ds(start, t), :]
            v = v_ref[0, pl.ds(start, t), :]
            z = _nt(q_s, k)
            lb, lom = _log_sigmoids(z)
            if masked:
                keep = lax.broadcasted_iota(jnp.int32, (n, t), 1) < lax.broadcasted_iota(jnp.int32, (n, t), 0)
                lom = jnp.where(keep, lom, 0.0)
            tail = jnp.dot(jnp.concatenate(_split_bf16(lom), axis=1), upper, preferred_element_type=f32)
            a = lb + tail + run
            if masked:
                a = jnp.where(keep, a, -jnp.inf)
            w = jnp.exp(a)
            acc = acc + jnp.dot(w.astype(bf16), v, preferred_element_type=f32)
            run = run + tail[:, 0:1] + lom[:, 0:1]
            return acc, run

        acc, run = jnp.zeros((tq, HEAD_DIM), f32), jnp.zeros((tq, 1), f32)
        for jj in reversed(range(nd)):
            r0 = jj * t
            acc_s, run_s = tile(i * nd + jj, q[r0:], acc[r0:], run[r0:], True)
            acc = acc_s if r0 == 0 else jnp.concatenate([acc[:r0], acc_s], axis=0)
            run = run_s if r0 == 0 else jnp.concatenate([run[:r0], run_s], axis=0)
        def group(n, cr):
            for u in range(unroll):
                cr = tile(i * nd - 1 - unroll * n - u, q, cr[0], cr[1], False)
            return cr

        acc, run = lax.fori_loop(0, i * (nd // unroll), group, (acc, run))
        o_ref[0] = jnp.concatenate([acc, jnp.broadcast_to(run, (tq, HEAD_DIM))], axis=1)

    return pl.pallas_call(
        body, name="attn_fwd", grid=(N_HEADS, seq // tq),
        in_specs=[pl.BlockSpec((1, tq, HEAD_DIM), lambda h, i: (h, i, 0)),
                  pl.BlockSpec((1, seq, HEAD_DIM), lambda h, i: (N_HEADS + h, 0, 0)),
                  pl.BlockSpec((1, seq, HEAD_DIM), lambda h, i: (2 * N_HEADS + h, 0, 0))],
        out_specs=pl.BlockSpec((1, tq, 2 * HEAD_DIM), lambda h, i: (h, i, 0)),
        out_shape=jax.ShapeDtypeStruct((N_HEADS, seq, 2 * HEAD_DIM), f32),
        compiler_params=_params(("parallel", "arbitrary")),
    )(qkv, qkv, qkv)


def _attn_bwd(qkv, o_tot, d_o, seq):
    t = CHUNK
    tq = min(ATTN_Q_ROWS, seq)
    nd = tq // t
    unroll = min(ATTN_UNROLL, nd)
    assert nd % unroll == 0
    nk = seq // t

    def body(q_ref, k_ref, v_ref, ot_ref, do_ref, dq_ref, dkt_ref, dvt_ref):
        i = pl.program_id(1)

        @pl.when(i == 0)
        def _():
            dkt_ref[...] = jnp.zeros_like(dkt_ref)
            dvt_ref[...] = jnp.zeros_like(dvt_ref)

        q = q_ref[0] * 0.125
        d_out = do_ref[0]
        total = ot_ref[0][:, HEAD_DIM:HEAD_DIM + 1]
        eye = (lax.broadcasted_iota(jnp.int32, (HEAD_DIM, HEAD_DIM), 0)
               == lax.broadcasted_iota(jnp.int32, (HEAD_DIM, HEAD_DIM), 1)).astype(bf16)
        q_t = _nt(eye, q).astype(bf16)
        do_t = _nt(eye, d_out).astype(bf16)
        ur = lax.broadcasted_iota(jnp.int32, (t, t), 0)
        uc = lax.broadcasted_iota(jnp.int32, (t, t), 1)
        ur2 = lax.broadcasted_iota(jnp.int32, (2 * t, t), 0) & (t - 1)
        incl = (ur2 <= lax.broadcasted_iota(jnp.int32, (2 * t, t), 1)).astype(bf16)
        before = (ur < uc).astype(bf16)

        def tile(j, r0, r1, dq, pre, dpre, masked):
            q_s, do_s, tot_s = q[r0:r1], d_out[r0:r1], total[r0:r1]
            n = q_s.shape[0]
            start = pl.multiple_of(j * t, t)
            k = k_ref[0, pl.ds(start, t), :]
            v = v_ref[0, pl.ds(start, t), :]
            z = _nt(q_s, k)
            lb, lom = _log_sigmoids(z)
            if masked:
                keep = lax.broadcasted_iota(jnp.int32, (n, t), 1) < lax.broadcasted_iota(jnp.int32, (n, t), 0)
                lom = jnp.where(keep, lom, 0.0)
            pin = jnp.dot(jnp.concatenate(_split_bf16(lom), axis=1), incl, preferred_element_type=f32)
            a = lb + ((tot_s - pre) - pin)
            if masked:
                a = jnp.where(keep, a, -jnp.inf)
            w = jnp.exp(a)
            d_a = _nt(do_s, v) * w
            d_lom_local = jnp.dot(d_a.astype(bf16), before, preferred_element_type=f32)
            d_lom = d_lom_local + dpre
            sig = jnp.exp(lb)
            dz = d_a * (1.0 - sig) - d_lom * sig
            if masked:
                dz = jnp.where(keep, dz, 0.0)
            dzb = dz.astype(bf16)
            dq = dq + jnp.dot(dzb, k, preferred_element_type=f32)
            dkt_ref[0, j] += jnp.dot(q_t[:, r0:r1], dzb, preferred_element_type=f32)
            dvt_ref[0, j] += jnp.dot(do_t[:, r0:r1], w.astype(bf16), preferred_element_type=f32)
            pre = pre + pin[:, t - 1:t]
            dpre = dpre + d_lom_local[:, t - 1:t] + d_a[:, t - 1:t]
            return dq, pre, dpre

        carry = (jnp.zeros((tq, HEAD_DIM), f32), jnp.zeros((tq, 1), f32), jnp.zeros((tq, 1), f32))
        def group(n, cr):
            for u in range(unroll):
                cr = tile(unroll * n + u, 0, tq, cr[0], cr[1], cr[2], False)
            return cr

        carry = lax.fori_loop(0, i * (nd // unroll), group, carry)
        for jj in range(nd):
            r0 = jj * t
            part = tile(i * nd + jj, r0, tq, *(c[r0:] for c in carry), True)
            carry = part if r0 == 0 else tuple(jnp.concatenate([c[:r0], p], axis=0) for c, p in zip(carry, part))
        dq_ref[0] = carry[0] * 0.125

    blk = pl.BlockSpec((1, tq, HEAD_DIM), lambda h, i: (h, i, 0))
    full_t = pl.BlockSpec((1, nk, HEAD_DIM, t), lambda h, i: (h, 0, 0, 0))
    return pl.pallas_call(
        body, name="attn_bwd", grid=(N_HEADS, seq // tq),
        in_specs=[blk, pl.BlockSpec((1, seq, HEAD_DIM), lambda h, i: (N_HEADS + h, 0, 0)),
                  pl.BlockSpec((1, seq, HEAD_DIM), lambda h, i: (2 * N_HEADS + h, 0, 0)),
                  pl.BlockSpec((1, tq, 2 * HEAD_DIM), lambda h, i: (h, i, 0)), blk],
        out_specs=(blk, full_t, full_t),
        out_shape=(jax.ShapeDtypeStruct((N_HEADS, seq, HEAD_DIM), f32),
                   jax.ShapeDtypeStruct((N_HEADS, nk, HEAD_DIM, t), f32), jax.ShapeDtypeStruct((N_HEADS, nk, HEAD_DIM, t), f32)),
        compiler_params=_params(("parallel", "arbitrary")),
    )(qkv, qkv, qkv, o_tot, d_o)


def _ssd_common(conv, dt_raw, dtb, alog):
    t = CHUNK
    sg = _sigmoid(conv)
    act = conv * sg
    dt_pre = dt_raw + dtb
    dt = _softplus(dt_pre)
    a = -jnp.exp(alog)
    row = lax.broadcasted_iota(jnp.int32, (t, t), 0)
    col = lax.broadcasted_iota(jnp.int32, (t, t), 1)
    causal = row >= col
    ac = _nn(causal.astype(f32), dt * a, HI)
    ac_t = _nt((row == col).astype(f32), ac, HI)
    ac_last = ac[t - 1:t, :]
    return sg, act, dt_pre, dt, a, causal, ac, ac_t, ac_last, jnp.exp(ac), jnp.exp(ac_last - ac), jnp.exp(ac_last)


def _ssd_fwd(xbc, dt_raw, conv_w, conv_b, dtb, alog, dsk, seq):
    t = CHUNK
    n_chunks = seq // t

    def body(x_ref, dt_ref, cw_ref, cb_ref, dtb_ref, al_ref, dsk_ref, conv_ref, y_ref, st_ref, prev, state):
        c = pl.program_id(0)

        @pl.when(c == 0)
        def _():
            prev[...] = jnp.zeros_like(prev)
            state[...] = jnp.zeros_like(state)

        cur = x_ref[...]
        pv = prev[...]
        rows = lax.broadcasted_iota(jnp.int32, (t, D_XBC), 0)
        conv = cur * cw_ref[CONV_K - 1:CONV_K, :] + cb_ref[...]
        for m in range(1, CONV_K):
            shifted = jnp.where(rows < m, pltpu.roll(pv, m, 0), pltpu.roll(cur, m, 0))
            conv = conv + shifted * cw_ref[CONV_K - 1 - m:CONV_K - m, :]
        prev[...] = cur
        conv_ref[...] = conv
        _, act, _, dt, _, causal, ac, ac_t, _, e_ac, dte, cdec = _ssd_common(conv, dt_ref[...], dtb_ref[...], al_ref[...])
        dskv = dsk_ref[...]
        ys = []
        for g in range(N_GROUPS):
            bg = act[:, D_SSM + g * N_STATE:D_SSM + (g + 1) * N_STATE].astype(bf16)
            cg = act[:, D_SSM + (N_GROUPS + g) * N_STATE:D_SSM + (N_GROUPS + g + 1) * N_STATE].astype(bf16)
            gm = _nt(cg, bg)
            for r in range(HEADS_PER_GROUP):
                h = g * HEADS_PER_GROUP + r
                xh = act[:, h * HEAD_DIM:(h + 1) * HEAD_DIM]
                xd = xh * dt[:, h:h + 1]
                seg = ac[:, h:h + 1] - ac_t[h:h + 1, :]
                lm = jnp.exp(jnp.where(causal, seg, -jnp.inf))
                y_diag = jnp.dot((gm * lm).astype(bf16), xd.astype(bf16), preferred_element_type=f32)
                hp = state[h]
                st_ref[0, h] = hp
                zo = _nt(cg, hp.astype(bf16))
                ys.append(y_diag + zo * e_ac[:, h:h + 1] + xh * dskv[:, h:h + 1])
                sc = _tn((xd * dte[:, h:h + 1]).astype(bf16), bg)
                state[h] = hp * cdec[:, h:h + 1] + sc
        y_ref[...] = jnp.concatenate(ys, axis=1)

    row = lambda c: (c, 0)
    fixed = lambda c: (0, 0)
    return pl.pallas_call(
        body, name="ssd_fwd", grid=(n_chunks,),
        in_specs=[pl.BlockSpec((t, D_XBC), row), pl.BlockSpec((t, LANES), row), pl.BlockSpec((CONV_K, D_XBC), fixed),
                  pl.BlockSpec((1, D_XBC), fixed), pl.BlockSpec((1, LANES), fixed), pl.BlockSpec((1, LANES), fixed),
                  pl.BlockSpec((1, LANES), fixed)],
        out_specs=(pl.BlockSpec((t, D_XBC), row), pl.BlockSpec((t, D_SSM), row),
                   pl.BlockSpec((1, N_HEADS, HEAD_DIM, N_STATE), lambda c: (c, 0, 0, 0))),
        out_shape=(jax.ShapeDtypeStruct((seq, D_XBC), f32), jax.ShapeDtypeStruct((seq, D_SSM), f32),
                   jax.ShapeDtypeStruct((n_chunks, N_HEADS, HEAD_DIM, N_STATE), f32)),
        scratch_shapes=[pltpu.VMEM((t, D_XBC), f32), pltpu.VMEM((N_HEADS, HEAD_DIM, N_STATE), f32)],
        compiler_params=_params(("arbitrary",)),
    )(xbc, dt_raw, conv_w, conv_b, dtb, alog, dsk)


def _ssd_bwd(dy, conv, xbc, dt_raw, states, conv_w, dtb, alog, dsk, seq):
    t = CHUNK
    n_chunks = seq // t

    def body(dy_ref, conv_ref, x_ref, dt_ref, st_ref, cw_ref, dtb_ref, al_ref, dsk_ref,
             dx_ref, ddt_ref, gcw_ref, gcb_ref, gdtb_ref, gal_ref, gdsk_ref, d_state, d_conv_next):
        c = pl.program_id(0)

        @pl.when(c == 0)
        def _():
            d_state[...] = jnp.zeros_like(d_state)
            d_conv_next[...] = jnp.zeros_like(d_conv_next)
            gcw_ref[...] = jnp.zeros_like(gcw_ref)
            gcb_ref[...] = jnp.zeros_like(gcb_ref)
            gdtb_ref[...] = jnp.zeros_like(gdtb_ref)
            gal_ref[...] = jnp.zeros_like(gal_ref)
            gdsk_ref[...] = jnp.zeros_like(gdsk_ref)

        conv = conv_ref[...]
        sg, act, dt_pre, dt, a, causal, ac, ac_t, _, e_ac, dte, cdec = _ssd_common(conv, dt_ref[...], dtb_ref[...], al_ref[...])
        dskv = dsk_ref[...]
        dyv = dy_ref[...]
        lane = lax.broadcasted_iota(jnp.int32, (1, LANES), 1)
        last_row = (lax.broadcasted_iota(jnp.int32, (t, 1), 0) == t - 1).astype(f32)
        sub = lax.broadcasted_iota(jnp.int32, (t, 1), 0)
        col_sums = jnp.zeros((t, t), f32)
        d_ac = jnp.zeros((t, LANES), f32)
        d_dt = jnp.zeros((t, LANES), f32)
        g_dsk = jnp.zeros((1, LANES), f32)
        dxs = []
        dbs = []
        dcs = []
        for g in range(N_GROUPS):
            bg = act[:, D_SSM + g * N_STATE:D_SSM + (g + 1) * N_STATE].astype(bf16)
            cg = act[:, D_SSM + (N_GROUPS + g) * N_STATE:D_SSM + (N_GROUPS + g + 1) * N_STATE].astype(bf16)
            gm = _nt(cg, bg)
            d_gm = jnp.zeros((t, t), f32)
            d_b = jnp.zeros((t, N_STATE), f32)
            d_c = jnp.zeros((t, N_STATE), f32)
            for r in range(HEADS_PER_GROUP):
                h = g * HEADS_PER_GROUP + r
                onehot = (lane == h).astype(f32)
                xh = act[:, h * HEAD_DIM:(h + 1) * HEAD_DIM]
                dth = dt[:, h:h + 1]
                xd = xh * dth
                xdb = xd.astype(bf16)
                seg = ac[:, h:h + 1] - ac_t[h:h + 1, :]
                lm = jnp.exp(jnp.where(causal, seg, -jnp.inf))
                mm = gm * lm
                hp = st_ref[0, h]
                hpb = hp.astype(bf16)
                d_hn = d_state[h]
                d_hnb = d_hn.astype(bf16)
                d_yh = dyv[:, h * HEAD_DIM:(h + 1) * HEAD_DIM]
                d_yb = d_yh.astype(bf16)
                g_dsk = g_dsk + jnp.sum(d_yh * xh) * onehot
                d_mm = _nt(d_yb, xdb)
                d_xd = _tn(mm.astype(bf16), d_yb)
                d_gm = d_gm + d_mm * lm
                d_seg = d_mm * mm
                d_ac_h = jnp.sum(d_seg, axis=1, keepdims=True)
                col_sums = col_sums + (sub == h).astype(f32) * jnp.sum(d_seg, axis=0, keepdims=True)
                e_h = e_ac[:, h:h + 1]
                zo = _nt(cg, hpb)
                d_zo = d_yh * e_h
                d_zob = d_zo.astype(bf16)
                d_ac_h = d_ac_h + jnp.sum(d_yh * zo, axis=1, keepdims=True) * e_h
                d_c = d_c + jnp.dot(d_zob, hpb, preferred_element_type=f32)
                cd = cdec[:, h:h + 1]
                d_hp = _tn(d_zob, cg) + d_hn * cd
                d_last = jnp.sum(d_hn * hp) * cd
                dte_h = dte[:, h:h + 1]
                d_w = _nt(bg, d_hnb)
                d_b = d_b + jnp.dot((xd * dte_h).astype(bf16), d_hnb, preferred_element_type=f32)
                d_xd = d_xd + d_w * dte_h
                d_dte = jnp.sum(d_w * xd, axis=1, keepdims=True) * dte_h
                d_last = d_last + jnp.sum(d_dte)
                d_ac_h = d_ac_h - d_dte + d_last * last_row
                d_state[h] = d_hp
                dxs.append(d_yh * dskv[:, h:h + 1] + d_xd * dth)
                d_dt = d_dt + jnp.sum(d_xd * xh, axis=1, keepdims=True) * onehot
                d_ac = d_ac + d_ac_h * onehot
            d_gmb = d_gm.astype(bf16)
            dcs.append(d_c + jnp.dot(d_gmb, bg, preferred_element_type=f32))
            dbs.append(d_b + _tn(d_gmb, cg))
        sq_row = lax.broadcasted_iota(jnp.int32, (t, t), 0)
        sq_col = lax.broadcasted_iota(jnp.int32, (t, t), 1)
        d_ac = d_ac - _nt((sq_row == sq_col).astype(f32), col_sums, HI)
        d_ld = _nn((sq_col >= sq_row).astype(f32), d_ac, HI)
        d_dt = d_dt + d_ld * a
        gal_ref[...] += jnp.sum(d_ld * dt, axis=0, keepdims=True) * a
        gdsk_ref[...] += g_dsk
        d_dt_raw = d_dt * _sigmoid(dt_pre)
        ddt_ref[...] = d_dt_raw.astype(bf16)
        gdtb_ref[...] += jnp.sum(d_dt_raw, axis=0, keepdims=True)
        d_act = jnp.concatenate(dxs + dbs + dcs, axis=1)
        d_conv = d_act * (sg * (1.0 + conv * (1.0 - sg)))
        gcb_ref[...] += jnp.sum(d_conv, axis=0, keepdims=True)
        nxt = d_conv_next[...]
        rows = lax.broadcasted_iota(jnp.int32, (t, D_XBC), 0)
        xraw = x_ref[...]
        d_x = d_conv * cw_ref[CONV_K - 1:CONV_K, :]
        gcw_ref[pl.ds(CONV_K - 1, 1), :] += jnp.sum(xraw * d_conv, axis=0, keepdims=True)
        for m in range(1, CONV_K):
            ahead = jnp.where(rows >= t - m, pltpu.roll(nxt, t - m, 0), pltpu.roll(d_conv, t - m, 0))
            d_x = d_x + ahead * cw_ref[CONV_K - 1 - m:CONV_K - m, :]
            gcw_ref[pl.ds(CONV_K - 1 - m, 1), :] += jnp.sum(xraw * ahead, axis=0, keepdims=True)
        d_conv_next[...] = d_conv
        dx_ref[...] = d_x.astype(bf16)

    rev = lambda c: (n_chunks - 1 - c, 0)
    fixed = lambda c: (0, 0)
    return pl.pallas_call(
        body, name="ssd_bwd", grid=(n_chunks,),
        in_specs=[pl.BlockSpec((t, D_SSM), rev), pl.BlockSpec((t, D_XBC), rev), pl.BlockSpec((t, D_XBC), rev),
                  pl.BlockSpec((t, LANES), rev), pl.BlockSpec((1, N_HEADS, HEAD_DIM, N_STATE), lambda c: (n_chunks - 1 - c, 0, 0, 0)),
                  pl.BlockSpec((CONV_K, D_XBC), fixed), pl.BlockSpec((1, LANES), fixed), pl.BlockSpec((1, LANES), fixed),
                  pl.BlockSpec((1, LANES), fixed)],
        out_specs=(pl.BlockSpec((t, D_XBC), rev), pl.BlockSpec((t, LANES), rev), pl.BlockSpec((CONV_K, D_XBC), fixed),
                   pl.BlockSpec((1, D_XBC), fixed), pl.BlockSpec((1, LANES), fixed), pl.BlockSpec((1, LANES), fixed),
                   pl.BlockSpec((1, LANES), fixed)),
        out_shape=(jax.ShapeDtypeStruct((seq, D_XBC), bf16), jax.ShapeDtypeStruct((seq, LANES), bf16),
                   jax.ShapeDtypeStruct((CONV_K, D_XBC), f32), jax.ShapeDtypeStruct((1, D_XBC), f32),
                   jax.ShapeDtypeStruct((1, LANES), f32), jax.ShapeDtypeStruct((1, LANES), f32), jax.ShapeDtypeStruct((1, LANES), f32)),
        scratch_shapes=[pltpu.VMEM((N_HEADS, HEAD_DIM, N_STATE), f32), pltpu.VMEM((t, D_XBC), f32)],
        compiler_params=_params(("arbitrary",)),
    )(dy, conv, xbc, dt_raw, states, conv_w, dtb, alog, dsk)


def _heads_to_cols(ref, width=HEAD_DIM):
    return jnp.concatenate([ref[h][:, :width] for h in range(N_HEADS)], axis=1)


def _silu_and_grad(z):
    sg = _sigmoid(z)
    return z * sg, sg * (1.0 + z * (1.0 - sg))


def _rms(v):
    return lax.rsqrt(jnp.mean(v * v, axis=-1, keepdims=True) + EPS)


def _rms_bwd(d_hat, hat, r):
    return r * (d_hat - hat * jnp.mean(d_hat * hat, axis=-1, keepdims=True))


def _post(x, target, o_tot, y, za, zs, w_out, gate, g_sb, g_ssm, g_f, seq):
    ts = 256

    def body(x_ref, t_ref, o_ref, y_ref, za_ref, zs_ref, w_ref, gate_ref, gsb_ref, gss_ref, gf_ref,
             ycat_t_ref, dmix_ref, dx2_ref, loss_ref, gnf_ref, dgate_ref):
        @pl.when(pl.program_id(0) == 0)
        def _():
            loss_ref[...] = jnp.zeros_like(loss_ref)
            gnf_ref[...] = jnp.zeros_like(gnf_ref)
            dgate_ref[...] = jnp.zeros_like(dgate_ref)

        o = _heads_to_cols(o_ref)
        zav = za_ref[...]
        ya = (o * _rms(o) * gsb_ref[...]) * (zav * _sigmoid(zav))
        zsv = zs_ref[...]
        u = y_ref[...] * (zsv * _sigmoid(zsv))
        ys = u * _rms(u) * gss_ref[...]
        yab, ysb = ya.astype(bf16), ys.astype(bf16)
        _store_transposed(ycat_t_ref.at[:D_ATTN], yab)
        _store_transposed(ycat_t_ref.at[D_ATTN:], ysb)
        mixed = (jnp.dot(yab, w_ref[:D_ATTN, :], preferred_element_type=f32)
                 + jnp.dot(ysb, w_ref[D_ATTN:, :], preferred_element_type=f32))
        gate_v = gate_ref[...]
        x2 = x_ref[...] + gate_v * mixed
        r2 = _rms(x2)
        xh = x2 * r2
        gf = gf_ref[...]
        diff = xh * gf - t_ref[...]
        loss_ref[...] += jnp.sum(diff * diff) * (0.5 / D_MODEL)
        d_out = diff * (1.0 / D_MODEL)
        gnf_ref[...] += jnp.sum(d_out * xh, axis=0, keepdims=True)
        dx2 = _rms_bwd(d_out * gf, xh, r2)
        dx2_ref[...] = dx2
        dgate_ref[...] += jnp.sum(dx2 * mixed, axis=0, keepdims=True)
        dmix_ref[...] = (dx2 * gate_v).astype(bf16)

    row = lambda i: (i, 0)
    fixed = lambda i: (0, 0)
    vec = pl.BlockSpec((1, D_MODEL), fixed)
    return pl.pallas_call(
        body, name="post", grid=(seq // ts,),
        in_specs=[pl.BlockSpec((ts, D_MODEL), row), pl.BlockSpec((ts, D_MODEL), row),
                  pl.BlockSpec((N_HEADS, ts, 2 * HEAD_DIM), lambda i: (0, i, 0)), pl.BlockSpec((ts, D_SSM), row),
                  pl.BlockSpec((ts, D_ATTN), row), pl.BlockSpec((ts, D_SSM), row), pl.BlockSpec((D_ATTN + D_SSM, D_MODEL), fixed),
                  vec, vec, vec, vec],
        out_specs=(pl.BlockSpec((D_ATTN + D_SSM, ts), lambda i: (0, i)), pl.BlockSpec((ts, D_MODEL), row), pl.BlockSpec((ts, D_MODEL), row),
                   pl.BlockSpec((1, LANES), fixed), vec, vec),
        out_shape=(jax.ShapeDtypeStruct((D_ATTN + D_SSM, seq), bf16), jax.ShapeDtypeStruct((seq, D_MODEL), bf16),
                   jax.ShapeDtypeStruct((seq, D_MODEL), f32), jax.ShapeDtypeStruct((1, LANES), f32),
                   jax.ShapeDtypeStruct((1, D_MODEL), f32), jax.ShapeDtypeStruct((1, D_MODEL), f32)),
        compiler_params=_params(("arbitrary",)),
    )(x, target, o_tot, y, za, zs, w_out, gate, g_sb, g_ssm, g_f)


def _bwd_out(dmix, w_out, o_tot, y, za, zs, g_sb, g_ssm, seq):
    ts = 256

    def body(dm_ref, w_ref, o_ref, y_ref, za_ref, zs_ref, gsb_ref, gss_ref, do_ref, dza_ref, dzs_ref, dy_ref, ggsb_ref, ggss_ref):
        @pl.when(pl.program_id(0) == 0)
        def _():
            ggsb_ref[...] = jnp.zeros_like(ggsb_ref)
            ggss_ref[...] = jnp.zeros_like(ggss_ref)

        dm = dm_ref[...]
        d_ya = _nt(dm, w_ref[:D_ATTN, :])
        d_ys = _nt(dm, w_ref[D_ATTN:, :])
        o = _heads_to_cols(o_ref)
        ro = _rms(o)
        oh = o * ro
        sa, dsa = _silu_and_grad(za_ref[...])
        gsb = gsb_ref[...]
        dza_ref[...] = (d_ya * oh * gsb * dsa).astype(bf16)
        ggsb_ref[...] += jnp.sum(d_ya * oh * sa, axis=0, keepdims=True)
        d_o = _rms_bwd(d_ya * gsb * sa, oh, ro)
        for h in range(N_HEADS):
            do_ref[h] = d_o[:, h * HEAD_DIM:(h + 1) * HEAD_DIM].astype(bf16)
        yv = y_ref[...]
        sz, dsz = _silu_and_grad(zs_ref[...])
        u = yv * sz
        ru = _rms(u)
        uh = u * ru
        ggss_ref[...] += jnp.sum(d_ys * uh, axis=0, keepdims=True)
        du = _rms_bwd(d_ys * gss_ref[...], uh, ru)
        dy_ref[...] = du * sz
        dzs_ref[...] = (du * yv * dsz).astype(bf16)

    row = lambda i: (i, 0)
    fixed = lambda i: (0, 0)
    vec = pl.BlockSpec((1, D_MODEL), fixed)
    return pl.pallas_call(
        body, name="bwd_out", grid=(seq // ts,),
        in_specs=[pl.BlockSpec((ts, D_MODEL), row), pl.BlockSpec((D_ATTN + D_SSM, D_MODEL), fixed),
                  pl.BlockSpec((N_HEADS, ts, 2 * HEAD_DIM), lambda i: (0, i, 0)), pl.BlockSpec((ts, D_SSM), row),
                  pl.BlockSpec((ts, D_ATTN), row), pl.BlockSpec((ts, D_SSM), row), vec, vec],
        out_specs=(pl.BlockSpec((N_HEADS, ts, HEAD_DIM), lambda i: (0, i, 0)), pl.BlockSpec((ts, D_ATTN), row),
                   pl.BlockSpec((ts, D_SSM), row), pl.BlockSpec((ts, D_SSM), row), vec, vec),
        out_shape=(jax.ShapeDtypeStruct((N_HEADS, seq, HEAD_DIM), bf16), jax.ShapeDtypeStruct((seq, D_ATTN), bf16),
                   jax.ShapeDtypeStruct((seq, D_SSM), bf16), jax.ShapeDtypeStruct((seq, D_SSM), f32),
                   jax.ShapeDtypeStruct((1, D_MODEL), f32), jax.ShapeDtypeStruct((1, D_MODEL), f32)),
        compiler_params=_params(("arbitrary",)),
    )(dmix, w_out, o_tot, y, za, zs, g_sb, g_ssm)


def _qkv_grads_to_cols(dq, dkt, dvt, seq):
    ts = 256
    nb = ts // CHUNK

    def body(dq_ref, dkt_ref, dvt_ref, out_ref):
        out_ref[:, :D_ATTN] = _heads_to_cols(dq_ref).astype(bf16)
        eye = (lax.broadcasted_iota(jnp.int32, (CHUNK, CHUNK), 0) == lax.broadcasted_iota(jnp.int32, (CHUNK, CHUNK), 1)).astype(bf16)
        for p, ref in ((1, dkt_ref), (2, dvt_ref)):
            for b in range(nb):
                cols = [_nt(eye, ref[h, b].astype(bf16)) for h in range(N_HEADS)]
                out_ref[b * CHUNK:(b + 1) * CHUNK, p * D_ATTN:(p + 1) * D_ATTN] = jnp.concatenate(cols, axis=1).astype(bf16)

    blk = pl.BlockSpec((N_HEADS, ts, HEAD_DIM), lambda i: (0, i, 0))
    blk_t = pl.BlockSpec((N_HEADS, nb, HEAD_DIM, CHUNK), lambda i: (0, i, 0, 0))
    return pl.pallas_call(
        body, name="qkv_grads_to_cols", grid=(seq // ts,), in_specs=[blk, blk_t, blk_t],
        out_specs=pl.BlockSpec((ts, 3 * D_ATTN), lambda i: (i, 0)),
        out_shape=jax.ShapeDtypeStruct((seq, 3 * D_ATTN), bf16), compiler_params=_params(("parallel",)),
    )(dq, dkt, dvt)


def _bwd_in(dqkv, dza, dzs, dxbc, ddt, wp, x, dx2, gain, scale, seq):
    ts = 256
    pieces = ((0, 0, 3 * D_ATTN), (1, OFF_ZA, D_ATTN), (2, OFF_ZS, D_SSM), (3, OFF_XBC, D_XBC), (4, OFF_DT, LANES))

    def body(dqkv_ref, dza_ref, dzs_ref, dxbc_ref, ddt_ref, w_ref, x_ref, dx2_ref, g_ref, sc_ref,
             gx_ref, dshift_ref, dscale_ref, ggain_ref):
        @pl.when(pl.program_id(0) == 0)
        def _():
            dshift_ref[...] = jnp.zeros_like(dshift_ref)
            dscale_ref[...] = jnp.zeros_like(dscale_ref)
            ggain_ref[...] = jnp.zeros_like(ggain_ref)

        refs = (dqkv_ref, dza_ref, dzs_ref, dxbc_ref, ddt_ref)
        dh = jnp.zeros((ts, D_MODEL), f32)
        for idx, off, width in pieces:
            for cc in range(0, width, 512):
                wd = min(512, width - cc)
                dh = dh + _nt(refs[idx][:, cc:cc + wd], w_ref[:, off + cc:off + cc + wd])
        xv = x_ref[...]
        r = _rms(xv)
        xh = xv * r
        g = g_ref[...]
        dshift_ref[...] += jnp.sum(dh, axis=0, keepdims=True)
        dscale_ref[...] += jnp.sum(dh * xh * g, axis=0, keepdims=True)
        tt = dh * (1.0 + sc_ref[...])
        ggain_ref[...] += jnp.sum(tt * xh, axis=0, keepdims=True)
        gx_ref[...] = dx2_ref[...] + _rms_bwd(tt * g, xh, r)

    row = lambda i: (i, 0)
    fixed = lambda i: (0, 0)
    vec = pl.BlockSpec((1, D_MODEL), fixed)
    return pl.pallas_call(
        body, name="bwd_in", grid=(seq // ts,),
        in_specs=[pl.BlockSpec((ts, 3 * D_ATTN), row), pl.BlockSpec((ts, D_ATTN), row), pl.BlockSpec((ts, D_SSM), row),
                  pl.BlockSpec((ts, D_XBC), row), pl.BlockSpec((ts, LANES), row), pl.BlockSpec((D_MODEL, D_PROJ_P), fixed),
                  pl.BlockSpec((ts, D_MODEL), row), pl.BlockSpec((ts, D_MODEL), row), vec, vec],
        out_specs=(pl.BlockSpec((ts, D_MODEL), row), vec, vec, vec),
        out_shape=(jax.ShapeDtypeStruct((seq, D_MODEL), f32), jax.ShapeDtypeStruct((1, D_MODEL), f32),
                   jax.ShapeDtypeStruct((1, D_MODEL), f32), jax.ShapeDtypeStruct((1, D_MODEL), f32)),
        compiler_params=_params(("arbitrary",)),
    )(dqkv, dza, dzs, dxbc, ddt, wp, x, dx2, gain, scale)


def _grad_w(a_t, b, tn, name):
    m, seq = a_t.shape
    n = b.shape[1]
    tk = min(512, seq)
    n_k = seq // tk

    def body(a_ref, b_ref, o_ref, acc):
        @pl.when(pl.program_id(1) == 0)
        def _():
            acc[...] = jnp.zeros_like(acc)

        acc[...] += jnp.dot(a_ref[...], b_ref[...], preferred_element_type=f32)

        @pl.when(pl.program_id(1) == n_k - 1)
        def _():
            o_ref[...] = acc[...].astype(bf16)

    return pl.pallas_call(
        body, name=name, grid=(n // tn, n_k),
        in_specs=[pl.BlockSpec((m, tk), lambda j, k: (0, k)), pl.BlockSpec((tk, tn), lambda j, k: (k, j))],
        out_specs=pl.BlockSpec((m, tn), lambda j, k: (0, j)),
        out_shape=jax.ShapeDtypeStruct((m, n), bf16), scratch_shapes=[pltpu.VMEM((m, tn), f32)],
        compiler_params=_params(("parallel", "arbitrary")),
    )(a_t, b)


def _small_finish(g_all, c_all, dmod_mine):
    def body(g_ref, c_ref, dm_ref, tot_ref, gwada_ref):
        tot = g_ref[0:1, :]
        for j in range(1, N_DEV):
            tot = tot + g_ref[j:j + 1, :]
        tot_ref[...] = tot
        cv = c_ref[...]
        gwada_ref[...] = _tn(cv * _sigmoid(cv), dm_ref[...], HI)

    vmem = pl.BlockSpec(memory_space=pltpu.VMEM)
    return pl.pallas_call(
        body, name="small_finish", in_specs=[vmem, vmem, vmem], out_specs=(vmem, vmem),
        out_shape=(jax.ShapeDtypeStruct((1, N_PACK), f32), jax.ShapeDtypeStruct((D_MODEL, dmod_mine.shape[1]), f32)),
        compiler_params=_params(),
    )(g_all, c_all, dmod_mine)


def _adamw(w, g_parts, m, v, rows, name):
    r, c = w.shape
    n_parts = g_parts.shape[0]
    bc1 = 1.0 - ADAM_B1 ** ADAM_STEP
    bc2 = 1.0 - ADAM_B2 ** ADAM_STEP

    def body(w_ref, g_ref, m_ref, v_ref, go_ref, d_ref, mo_ref, vo_ref):
        g = g_ref[0].astype(f32)
        for j in range(1, n_parts):
            g = g + g_ref[j].astype(f32)
        go_ref[...] = g
        mn = ADAM_B1 * m_ref[...] + (1.0 - ADAM_B1) * g
        vn = ADAM_B2 * v_ref[...] + (1.0 - ADAM_B2) * (g * g)
        mo_ref[...] = mn
        vo_ref[...] = vn
        d_ref[...] = -ADAM_LR * ((mn / bc1) / (jnp.sqrt(vn / bc2) + ADAM_EPS) + ADAM_WD * w_ref[...])

    blk = pl.BlockSpec((rows, c), lambda i: (i, 0))
    return pl.pallas_call(
        body, name=name, grid=(r // rows,),
        in_specs=[blk, pl.BlockSpec((n_parts, rows, c), lambda i: (0, i, 0)), blk, blk],
        out_specs=(blk, blk, blk, blk), out_shape=(jax.ShapeDtypeStruct((r, c), f32),) * 4,
        compiler_params=_params(("parallel",)),
    )(w, g_parts, m, v)


def _pad_lanes(v):
    return jnp.pad(v, ((0, 0), (0, LANES - v.shape[1])))


def kernel(x, c, w_ada, b_ada, norm_in_gain, w_in, conv_w, conv_b, dt_bias, a_log, d_skip, sb_norm_gain, ssm_norm_gain, w_out, norm_f_gain, loss_target, m_w_ada, m_b_ada, m_norm_in_gain, m_w_in, m_conv_w, m_conv_b, m_dt_bias, m_a_log, m_d_skip, m_sb_norm_gain, m_ssm_norm_gain, m_w_out, m_norm_f_gain, v_w_ada, v_b_ada, v_norm_in_gain, v_w_in, v_conv_w, v_conv_b, v_dt_bias, v_a_log, v_d_skip, v_sb_norm_gain, v_ssm_norm_gain, v_w_out, v_norm_f_gain):
    seq = x.shape[1]
    xs = x[0]
    tgt = loss_target[0]
    _, my_slot = _me()

    mod, c_all = _mod_exchange(c, w_ada[0], b_ada)
    shift, scale, gate = mod[:, :D_MODEL], mod[:, D_MODEL:2 * D_MODEL], mod[:, 2 * D_MODEL:]
    w_in_g, w_out_g, conv_w_g = _all_gather_two_level(
        [_cast_bf16(w_in[0], 128), _cast_bf16(w_out[0], 128), conv_w[0]], "gather_weights")
    w_full = jnp.transpose(w_in_g, (1, 0, 2)).reshape(D_MODEL, D_PROJ)
    wp = jnp.concatenate([w_full[:, :4 * D_ATTN], w_full[:, D_PROJ - D_SSM:], w_full[:, 4 * D_ATTN:4 * D_ATTN + D_XBC],
                          _pad_lanes(w_full[:, 4 * D_ATTN + D_XBC:4 * D_ATTN + D_XBC + N_HEADS])], axis=1)
    w_out_full = w_out_g.reshape(D_ATTN + D_SSM, D_MODEL)
    conv_w_full = jnp.transpose(conv_w_g, (1, 0, 2)).reshape(CONV_K, D_XBC)
    dtb, alog, dsk = _pad_lanes(dt_bias), _pad_lanes(a_log), _pad_lanes(d_skip)

    h_t, qkv, za, zs, xbc, dt_raw = _proj(xs, norm_in_gain, scale, shift, wp, seq)
    o_tot = _attn_fwd(qkv, seq)
    conv, y, states = _ssd_fwd(xbc, dt_raw, conv_w_full, conv_b, dtb, alog, dsk, seq)
    ycat_t, dmix, dx2, loss_p, g_nf, d_gate = _post(xs, tgt, o_tot, y, za, zs, w_out_full, gate, sb_norm_gain, ssm_norm_gain,
                                                  norm_f_gain.reshape(1, D_MODEL), seq)

    d_o, dza, dzs, dy, g_sb, g_ss = _bwd_out(dmix, w_out_full, o_tot, y, za, zs, sb_norm_gain, ssm_norm_gain, seq)
    dq, dk, dv = _attn_bwd(qkv, o_tot, d_o, seq)
    dxbc, ddt, g_cw, g_cb, g_dtb, g_al, g_dsk = _ssd_bwd(dy, conv, xbc, dt_raw, states, conv_w_full, dtb, alog, dsk, seq)
    dqkv = _qkv_grads_to_cols(dq, dk, dv, seq)
    grad_x, d_shift, d_scale, g_in = _bwd_in(dqkv, dza, dzs, dxbc, ddt, wp, xs, dx2, norm_in_gain, scale, seq)
    gw_qkv = _grad_w(h_t, dqkv, 512, "grad_w_qkv")
    gw_za = _grad_w(h_t, dza, 512, "grad_w_za")
    gw_zs = _grad_w(h_t, dzs, 512, "grad_w_zs")
    gw_xbc = _grad_w(h_t, dxbc, 512, "grad_w_xbc")
    gw_dt = _grad_w(h_t, ddt, LANES, "grad_w_dt")
    gw_out = _grad_w(ycat_t, dmix, 512, "grad_w_out")
    gw_in = jnp.concatenate([gw_qkv, gw_za, gw_xbc, gw_dt[:, :N_HEADS], gw_zs], axis=1)

    gw_in_parts, gw_out_parts = _reduce_scatter_two_level(
        [jnp.transpose(gw_in.reshape(D_MODEL, N_DEV, W_IN_SHARD), (1, 0, 2)),
         gw_out.reshape(N_DEV, (D_ATTN + D_SSM) // N_DEV, D_MODEL)], "scatter_grads")
    packed = jnp.concatenate([loss_p, d_shift, d_scale, d_gate, g_in, g_cb, g_dtb, g_al, g_dsk, g_sb, g_ss, g_nf,
                              g_cw.reshape(1, CONV_K * D_XBC)], axis=1)
    (packed_all,) = _all_gather([packed], "gather_small")
    packed_all = packed_all.reshape(N_DEV, N_PACK)
    n_ada = w_ada.shape[2]
    dmod_mine = lax.dynamic_slice(packed_all, (0, P_DMOD + my_slot * n_ada), (N_DEV, n_ada))
    tot, g_w_ada = _small_finish(packed_all, c_all.reshape(N_DEV, D_MODEL), dmod_mine)

    def big(w, parts, m, v, rows, name):
        return tuple(t[None] for t in _adamw(w[0], parts, m[0], v[0], rows, name))

    small_names = ["b_ada", "norm_in_gain", "conv_b", "dt_bias", "a_log", "d_skip", "sb_norm_gain", "ssm_norm_gain", "norm_f_gain"]
    given = {"b_ada": (b_ada, m_b_ada, v_b_ada), "norm_in_gain": (norm_in_gain, m_norm_in_gain, v_norm_in_gain),
             "conv_b": (conv_b, m_conv_b, v_conv_b), "dt_bias": (dt_bias, m_dt_bias, v_dt_bias), "a_log": (a_log, m_a_log, v_a_log),
             "d_skip": (d_skip, m_d_skip, v_d_skip), "sb_norm_gain": (sb_norm_gain, m_sb_norm_gain, v_sb_norm_gain),
             "ssm_norm_gain": (ssm_norm_gain, m_ssm_norm_gain, v_ssm_norm_gain), "norm_f_gain": (norm_f_gain, m_norm_f_gain, v_norm_f_gain)}

    def pack(which):
        cols = []
        for nm in small_names:
            t = given[nm][which].reshape(1, -1)
            cols.append(_pad_lanes(t) if t.shape[1] < LANES else t)
        return jnp.concatenate(cols, axis=1)

    packed_out = _adamw(pack(0), tot[:, P_DMOD:P_CW][None], pack(1), pack(2), 1, "adamw_small")
    res = {}
    off = 0
    for nm in small_names:
        shape = given[nm][0].shape
        size = given[nm][0].size
        res[nm] = tuple(t[:, off:off + size].reshape(shape) for t in packed_out)
        off += max(size, LANES)
    n_cw = conv_w.shape[2]
    g_cw_mine = lax.dynamic_slice(tot[:, P_CW:].reshape(CONV_K, D_XBC), (0, my_slot * n_cw), (CONV_K, n_cw))
    res["conv_w"] = tuple(t.reshape(conv_w.shape) for t in _adamw(conv_w.reshape(1, -1), g_cw_mine.reshape(1, 1, -1),
                                                                  m_conv_w.reshape(1, -1), v_conv_w.reshape(1, -1), 1, "adamw_conv_w"))
    res["w_ada"] = big(w_ada, g_w_ada[None], m_w_ada, v_w_ada, 128, "adamw_w_ada")
    res["w_in"] = big(w_in, gw_in_parts, m_w_in, v_w_in, 128, "adamw_w_in")
    res["w_out"] = big(w_out, gw_out_parts, m_w_out, v_w_out, 64, "adamw_w_out")
    names = ["w_ada", "b_ada", "norm_in_gain", "w_in", "conv_w", "conv_b", "dt_bias", "a_log", "d_skip", "sb_norm_gain",
             "ssm_norm_gain", "w_out", "norm_f_gain"]
    loss = tot[0, P_LOSS]
    return (loss, grad_x[None], *[res[n][0] for n in names], *[res[n][1] for n in names],
            *[res[n][2] for n in names], *[res[n][3] for n in names])
```

```python
import functools

import jax
import jax.numpy as jnp
from jax import lax
from jax.experimental import pallas as pl
from jax.experimental.pallas import tpu as pltpu

f32 = jnp.float32
bf16 = jnp.bfloat16
MESH = pl.DeviceIdType.MESH
HI = lax.Precision.HIGHEST

N_DEV = 8
D_MODEL = 1024
D_ATTN = 1024
D_SSM = 1024
N_HEADS = 16
HEAD_DIM = 64
N_GROUPS = 2
HEADS_PER_GROUP = 8
N_STATE = 128
D_XBC = D_SSM + 2 * N_GROUPS * N_STATE
D_PROJ = 4 * D_ATTN + D_XBC + N_HEADS + D_SSM
W_IN_SHARD = D_PROJ // N_DEV
CONV_K = 4
CHUNK = 128
ATTN_Q_ROWS = 2048
ATTN_UNROLL = 8
LANES = 128
EPS = 1e-6
OFF_ZA = 3072
OFF_ZS = 4096
OFF_XBC = 5120
OFF_DT = 6656
D_PROJ_P = 6784
VMEM_LIMIT_BYTES = 56 * 1024 * 1024

ADAM_LR = 0.001
ADAM_B1 = 0.9
ADAM_B2 = 0.999
ADAM_EPS = 1e-08
ADAM_WD = 0.01
ADAM_STEP = 10

P_LOSS = 0
P_DMOD = 128
P_GIN = 3200
P_CB = 4224
P_DTB = 5760
P_ALOG = 5888
P_DSK = 6016
P_GSB = 6144
P_GSS = 7168
P_GNF = 8192
P_CW = 9216
N_PACK = 15360


def _params(sem=None):
    return pltpu.CompilerParams(dimension_semantics=sem, vmem_limit_bytes=VMEM_LIMIT_BYTES)


def _sigmoid(v):
    return 1.0 / (1.0 + jnp.exp(-v))


def _softplus(v):
    return jnp.maximum(v, 0.0) + jnp.log(1.0 + jnp.exp(-jnp.abs(v)))


def _nt(a, b, precision=None):
    return lax.dot_general(a, b, (((1,), (1,)), ((), ())), preferred_element_type=f32, precision=precision)


def _tn(a, b, precision=None):
    return lax.dot_general(a, b, (((0,), (0,)), ((), ())), preferred_element_type=f32, precision=precision)


def _nn(a, b, precision=None):
    return lax.dot_general(a, b, (((1,), (0,)), ((), ())), preferred_element_type=f32, precision=precision)


def _me():
    x, y, c = lax.axis_index("x"), lax.axis_index("y"), lax.axis_index("c")
    return (x, y, c), 4 * x + 2 * y + c


def _peer(k):
    x, y, c = lax.axis_index("x"), lax.axis_index("y"), lax.axis_index("c")
    px = 1 - x if (k >> 2) & 1 else x
    py = 1 - y if (k >> 1) & 1 else y
    pc = 1 - c if k & 1 else c
    return (px, py, pc), 4 * px + 2 * py + pc


def _all_gather(arrs, name):
    n = len(arrs)

    def body(*refs):
        ins, outs = refs[:n], refs[n:2 * n]
        send_sems, recv_sems, local_sems = refs[2 * n:]
        _, my_slot = _me()
        sends = []
        locals_ = []
        for a in range(n):
            loc = pltpu.make_async_copy(ins[a], outs[a].at[my_slot], local_sems.at[a])
            loc.start()
            locals_.append(loc)
            for k in range(1, N_DEV):
                peer, _ = _peer(k)
                cp = pltpu.make_async_remote_copy(src_ref=ins[a], dst_ref=outs[a].at[my_slot], send_sem=send_sems.at[a, k - 1],
                                                  recv_sem=recv_sems.at[a, k - 1], device_id=peer, device_id_type=MESH)
                cp.start()
                sends.append(cp)
        for a in range(n):
            for k in range(1, N_DEV):
                peer, peer_slot = _peer(k)
                pltpu.make_async_remote_copy(src_ref=ins[a], dst_ref=outs[a].at[peer_slot], send_sem=send_sems.at[a, k - 1],
                                             recv_sem=recv_sems.at[a, k - 1], device_id=peer, device_id_type=MESH).wait_recv()
        for cp in sends:
            cp.wait_send()
        for loc in locals_:
            loc.wait()

    any_spec = pl.BlockSpec(memory_space=pl.ANY)
    return pl.pallas_call(
        body, name=name,
        out_shape=tuple(jax.ShapeDtypeStruct((N_DEV,) + a.shape, a.dtype) for a in arrs),
        in_specs=[any_spec] * n, out_specs=tuple([any_spec] * n),
        scratch_shapes=[pltpu.SemaphoreType.DMA((n, N_DEV - 1)), pltpu.SemaphoreType.DMA((n, N_DEV - 1)),
                        pltpu.SemaphoreType.DMA((n,))],
    )(*arrs)


def _all_gather_two_level(arrs, name):
    n = len(arrs)

    def body(*refs):
        ins, outs = refs[:n], refs[n:2 * n]
        send_sems, recv_sems, local_sems = refs[2 * n:]
        x, y, c = lax.axis_index("x"), lax.axis_index("y"), lax.axis_index("c")
        me, sibling = (x, y, c), (x, y, 1 - c)
        chips = [(1 - x, y), (x, 1 - y), (1 - x, 1 - y)]

        def copy(a, k, block, to, from_input=False):
            slot = 4 * block[0] + 2 * block[1] + block[2]
            return pltpu.make_async_remote_copy(src_ref=ins[a] if from_input else outs[a].at[slot], dst_ref=outs[a].at[slot],
                                                send_sem=send_sems.at[a, k], recv_sem=recv_sems.at[a, k], device_id=to,
                                                device_id_type=MESH)

        started = []
        locals_ = []
        for a in range(n):
            loc = pltpu.make_async_copy(ins[a], outs[a].at[4 * x + 2 * y + c], local_sems.at[a])
            loc.start()
            locals_.append(loc)
            first = [copy(a, 0, me, sibling, True)] + [copy(a, 1 + j, me, (*chip, c), True) for j, chip in enumerate(chips)]
            for cp in first:
                cp.start()
            started += first
        for a in range(n):
            for j, chip in enumerate(chips):
                copy(a, 1 + j, (*chip, c), me).wait_recv()
                onward = copy(a, 4 + j, (*chip, c), sibling)
                onward.start()
                started.append(onward)
        for a in range(n):
            copy(a, 0, sibling, me).wait_recv()
            for j, chip in enumerate(chips):
                copy(a, 4 + j, (*chip, 1 - c), me).wait_recv()
        for cp in started:
            cp.wait_send()
        for loc in locals_:
            loc.wait()

    any_spec = pl.BlockSpec(memory_space=pl.ANY)
    return pl.pallas_call(
        body, name=name,
        out_shape=tuple(jax.ShapeDtypeStruct((N_DEV,) + a.shape, a.dtype) for a in arrs),
        in_specs=[any_spec] * n, out_specs=tuple([any_spec] * n),
        scratch_shapes=[pltpu.SemaphoreType.DMA((n, N_DEV - 1)), pltpu.SemaphoreType.DMA((n, N_DEV - 1)),
                        pltpu.SemaphoreType.DMA((n,))],
    )(*arrs)


def _all_to_all(arrs, name):
    n = len(arrs)

    def body(*refs):
        ins, outs = refs[:n], refs[n:2 * n]
        send_sems, recv_sems, local_sems = refs[2 * n:]
        _, my_slot = _me()
        sends = []
        locals_ = []
        for a in range(n):
            loc = pltpu.make_async_copy(ins[a].at[my_slot], outs[a].at[my_slot], local_sems.at[a])
            loc.start()
            locals_.append(loc)
            for k in range(1, N_DEV):
                peer, peer_slot = _peer(k)
                cp = pltpu.make_async_remote_copy(src_ref=ins[a].at[peer_slot], dst_ref=outs[a].at[my_slot],
                                                  send_sem=send_sems.at[a, k - 1], recv_sem=recv_sems.at[a, k - 1],
                                                  device_id=peer, device_id_type=MESH)
                cp.start()
                sends.append(cp)
        for a in range(n):
            for k in range(1, N_DEV):
                peer, peer_slot = _peer(k)
                pltpu.make_async_remote_copy(src_ref=ins[a].at[peer_slot], dst_ref=outs[a].at[peer_slot],
                                             send_sem=send_sems.at[a, k - 1], recv_sem=recv_sems.at[a, k - 1],
                                             device_id=peer, device_id_type=MESH).wait_recv()
        for cp in sends:
            cp.wait_send()
        for loc in locals_:
            loc.wait()

    any_spec = pl.BlockSpec(memory_space=pl.ANY)
    return pl.pallas_call(
        body, name=name,
        out_shape=tuple(jax.ShapeDtypeStruct(a.shape, a.dtype) for a in arrs),
        in_specs=[any_spec] * n, out_specs=tuple([any_spec] * n),
        scratch_shapes=[pltpu.SemaphoreType.DMA((n, N_DEV - 1)), pltpu.SemaphoreType.DMA((n, N_DEV - 1)),
                        pltpu.SemaphoreType.DMA((n,))],
    )(*arrs)


def _reduce_scatter_two_level(arrs, name):
    n = len(arrs)
    n_chip = N_DEV // 2

    def body(*refs):
        ins, outs = refs[:n], refs[n:2 * n]
        mine_bufs, sib_bufs = refs[2 * n:3 * n], refs[3 * n:4 * n]
        send_sems, recv_sems, local_sems = refs[4 * n:]
        x, y, c = lax.axis_index("x"), lax.axis_index("y"), lax.axis_index("c")
        sibling = (x, y, 1 - c)

        def chip(r):
            return (1 - x if r & 2 else x), (1 - y if r & 1 else y)

        def slot(r, core):
            cx, cy = chip(r)
            return 4 * cx + 2 * cy + core

        def to_sibling(a, r):
            return pltpu.make_async_remote_copy(src_ref=ins[a].at[slot(r, 1 - c)], dst_ref=sib_bufs[a].at[r], send_sem=send_sems.at[a, r],
                                                recv_sem=recv_sems.at[a, r], device_id=sibling, device_id_type=MESH)

        def to_chip(a, r):
            return pltpu.make_async_remote_copy(src_ref=mine_bufs[a].at[r], dst_ref=outs[a].at[r], send_sem=send_sems.at[a, n_chip - 1 + r],
                                                recv_sem=recv_sems.at[a, n_chip - 1 + r], device_id=(*chip(r), c), device_id_type=MESH)

        def load_mine(a, r):
            return pltpu.make_async_copy(ins[a].at[slot(r, c)], mine_bufs[a].at[r], local_sems.at[a, r])

        started = []
        order = [1, 2, 3, 0]
        for a in range(n):
            for r in order:
                load_mine(a, r).start()
                cp = to_sibling(a, r)
                cp.start()
                started.append(cp)
        for a in range(n):
            for r in order:
                load_mine(a, r).wait()
                to_sibling(a, r).wait_recv()
                total = (mine_bufs[a][r].astype(f32) + sib_bufs[a][r].astype(f32)).astype(bf16)
                if r:
                    mine_bufs[a][r] = total
                    cp = to_chip(a, r)
                    cp.start()
                    started.append(cp)
                else:
                    outs[a][0] = total
        for a in range(n):
            for r in range(1, n_chip):
                to_chip(a, r).wait_recv()
        for cp in started:
            cp.wait_send()

    any_spec = pl.BlockSpec(memory_space=pl.ANY)
    vmem = pl.BlockSpec(memory_space=pltpu.VMEM)
    part = lambda a: (n_chip,) + a.shape[1:]
    return pl.pallas_call(
        body, name=name,
        out_shape=tuple(jax.ShapeDtypeStruct(part(a), a.dtype) for a in arrs),
        in_specs=[any_spec] * n, out_specs=tuple([vmem] * n),
        scratch_shapes=([pltpu.VMEM(part(a), a.dtype) for a in arrs] + [pltpu.VMEM(part(a), a.dtype) for a in arrs]
                        + [pltpu.SemaphoreType.DMA((n, N_DEV - 1)), pltpu.SemaphoreType.DMA((n, N_DEV - 1)),
                           pltpu.SemaphoreType.DMA((n, n_chip))]),
        compiler_params=_params(),
    )(*arrs)


def _mod_exchange(c_row, w_ada, b_ada):
    n_col = w_ada.shape[1]

    def body(c_ref, w_ref, b_ref, mod_ref, call_ref, part, modp, send_sems, recv_sems):
        _, my_slot = _me()
        call_ref[my_slot] = c_ref[...]
        sends = []
        for k in range(1, N_DEV):
            peer, _ = _peer(k)
            cp = pltpu.make_async_remote_copy(src_ref=c_ref, dst_ref=call_ref.at[my_slot], send_sem=send_sems.at[0, k - 1],
                                              recv_sem=recv_sems.at[0, k - 1], device_id=peer, device_id_type=MESH)
            cp.start()
            sends.append(cp)
        for k in range(1, N_DEV):
            peer, peer_slot = _peer(k)
            pltpu.make_async_remote_copy(src_ref=c_ref, dst_ref=call_ref.at[peer_slot], send_sem=send_sems.at[0, k - 1],
                                         recv_sem=recv_sems.at[0, k - 1], device_id=peer, device_id_type=MESH).wait_recv()
        for cp in sends:
            cp.wait_send()
        w = w_ref[...]
        for b in range(N_DEV):
            cb = call_ref[b]
            part[b] = _nn(cb * _sigmoid(cb), w, HI)
        modp[my_slot] = part[my_slot]
        sends = []
        for k in range(1, N_DEV):
            peer, peer_slot = _peer(k)
            cp = pltpu.make_async_remote_copy(src_ref=part.at[peer_slot], dst_ref=modp.at[my_slot], send_sem=send_sems.at[1, k - 1],
                                              recv_sem=recv_sems.at[1, k - 1], device_id=peer, device_id_type=MESH)
            cp.start()
            sends.append(cp)
        for k in range(1, N_DEV):
            peer, peer_slot = _peer(k)
            pltpu.make_async_remote_copy(src_ref=part.at[peer_slot], dst_ref=modp.at[peer_slot], send_sem=send_sems.at[1, k - 1],
                                         recv_sem=recv_sems.at[1, k - 1], device_id=peer, device_id_type=MESH).wait_recv()
        for cp in sends:
            cp.wait_send()
        for j in range(N_DEV):
            mod_ref[:, j * n_col:(j + 1) * n_col] = modp[j] + b_ref[:, j * n_col:(j + 1) * n_col]

    vmem = pl.BlockSpec(memory_space=pltpu.VMEM)
    return pl.pallas_call(
        body, name="mod_exchange",
        out_shape=(jax.ShapeDtypeStruct((1, N_DEV * n_col), f32), jax.ShapeDtypeStruct((N_DEV, 1, D_MODEL), f32)),
        in_specs=[vmem, vmem, vmem], out_specs=(vmem, vmem),
        scratch_shapes=[pltpu.VMEM((N_DEV, 1, n_col), f32), pltpu.VMEM((N_DEV, 1, n_col), f32),
                        pltpu.SemaphoreType.DMA((2, N_DEV - 1)), pltpu.SemaphoreType.DMA((2, N_DEV - 1))],
        compiler_params=_params(),
    )(c_row, w_ada, b_ada)


def _cast_bf16(a, rows):
    r, c = a.shape

    def body(a_ref, o_ref):
        o_ref[...] = a_ref[...].astype(bf16)

    return pl.pallas_call(
        body, name="cast_bf16", grid=(r // rows,),
        in_specs=[pl.BlockSpec((rows, c), lambda i: (i, 0))], out_specs=pl.BlockSpec((rows, c), lambda i: (i, 0)),
        out_shape=jax.ShapeDtypeStruct((r, c), bf16), compiler_params=_params(("parallel",)),
    )(a)


def _store_transposed(out_ref, v):
    blk = 256
    eye = (lax.broadcasted_iota(jnp.int32, (blk, blk), 0) == lax.broadcasted_iota(jnp.int32, (blk, blk), 1)).astype(bf16)
    for cb in range(0, v.shape[1], blk):
        out_ref[cb:cb + blk, :] = _nt(eye, v[:, cb:cb + blk]).astype(bf16)


def _proj(x, gain, scale, shift, wp, seq):
    ts = 256

    def body(x_ref, g_ref, sc_ref, sh_ref, w_ref, ht_ref, qkv_ref, za_ref, zs_ref, xbc_ref, dt_ref):
        xv = x_ref[...]
        r = lax.rsqrt(jnp.mean(xv * xv, axis=-1, keepdims=True) + EPS)
        hb = ((xv * r * g_ref[...]) * (1.0 + sc_ref[...]) + sh_ref[...]).astype(bf16)
        _store_transposed(ht_ref, hb)
        for cb in range(3 * D_ATTN // 256):
            res = jnp.dot(hb, w_ref[:, cb * 256:(cb + 1) * 256], preferred_element_type=f32)
            for u in range(4):
                qkv_ref[cb * 4 + u] = res[:, u * HEAD_DIM:(u + 1) * HEAD_DIM].astype(bf16)
        for out_ref, off, width in ((za_ref, OFF_ZA, D_ATTN), (zs_ref, OFF_ZS, D_SSM), (xbc_ref, OFF_XBC, D_XBC), (dt_ref, OFF_DT, LANES)):
            for cc in range(0, width, 512):
                wd = min(512, width - cc)
                out_ref[:, cc:cc + wd] = jnp.dot(hb, w_ref[:, off + cc:off + cc + wd], preferred_element_type=f32)

    row = lambda i: (i, 0)
    fixed = lambda i: (0, 0)
    return pl.pallas_call(
        body, name="proj", grid=(seq // ts,),
        in_specs=[pl.BlockSpec((ts, D_MODEL), row), pl.BlockSpec((1, D_MODEL), fixed), pl.BlockSpec((1, D_MODEL), fixed),
                  pl.BlockSpec((1, D_MODEL), fixed), pl.BlockSpec((D_MODEL, D_PROJ_P), fixed)],
        out_specs=(pl.BlockSpec((D_MODEL, ts), lambda i: (0, i)), pl.BlockSpec((3 * N_HEADS, ts, HEAD_DIM), lambda i: (0, i, 0)),
                   pl.BlockSpec((ts, D_ATTN), row), pl.BlockSpec((ts, D_SSM), row), pl.BlockSpec((ts, D_XBC), row),
                   pl.BlockSpec((ts, LANES), row)),
        out_shape=(jax.ShapeDtypeStruct((D_MODEL, seq), bf16), jax.ShapeDtypeStruct((3 * N_HEADS, seq, HEAD_DIM), bf16),
                   jax.ShapeDtypeStruct((seq, D_ATTN), f32), jax.ShapeDtypeStruct((seq, D_SSM), f32),
                   jax.ShapeDtypeStruct((seq, D_XBC), f32), jax.ShapeDtypeStruct((seq, LANES), f32)),
        compiler_params=_params(("arbitrary",)),
    )(x, gain, scale, shift, wp)


def _log_sigmoids(z):
    lb = jnp.minimum(z, 0.0) - jnp.log(1.0 + jnp.exp(-jnp.abs(z)))
    return lb, lb - z


def _split_bf16(v):
    hi = v.astype(bf16)
    return hi, (v - hi.astype(f32)).astype(bf16)


def _attn_fwd(qkv, seq):
    t = CHUNK
    tq = min(ATTN_Q_ROWS, seq)
    nd = tq // t
    unroll = min(ATTN_UNROLL, nd)
    assert nd % unroll == 0

    def body(q_ref, k_ref, v_ref, o_ref):
        i = pl.program_id(1)
        q = q_ref[0] * 0.125
        ur = lax.broadcasted_iota(jnp.int32, (2 * t, t), 0)
        upper = ((ur & (t - 1)) > lax.broadcasted_iota(jnp.int32, (2 * t, t), 1)).astype(bf16)

        def tile(j, q_s, acc, run, masked):
            n = q_s.shape[0]
            start = pl.multiple_of(j * t, t)
            k = k_ref[0, pl.ds(start, t), :]
            v = v_ref[0, pl.ds(start, t), :]
            z = _nt(q_s, k)
            lb, lom = _log_sigmoids(z)
            if masked:
                keep = lax.broadcasted_iota(jnp.int32, (n, t), 1) < lax.broadcasted_iota(jnp.int32, (n, t), 0)
                lom = jnp.where(keep, lom, 0.0)
            tail = jnp.dot(jnp.concatenate(_split_bf16(lom), axis=1), upper, preferred_element_type=f32)
            a = lb + tail + run
            if masked:
                a = jnp.where(keep, a, -jnp.inf)
            w = jnp.exp(a)
            acc = acc + jnp.dot(w.astype(bf16), v, preferred_element_type=f32)
            run = run + tail[:, 0:1] + lom[:, 0:1]
            return acc, run

        acc, run = jnp.zeros((tq, HEAD_DIM), f32), jnp.zeros((tq, 1), f32)
        for jj in reversed(range(nd)):
            r0 = jj * t
            acc_s, run_s = tile(i * nd + jj, q[r0:], acc[r0:], run[r0:], True)
            acc = acc_s if r0 == 0 else jnp.concatenate([acc[:r0], acc_s], axis=0)
            run = run_s if r0 == 0 else jnp.concatenate([run[:r0], run_s], axis=0)
        def group(n, cr):
            for u in range(unroll):
                cr = tile(i * nd - 1 - unroll * n - u, q, cr[0], cr[1], False)
            return cr

        acc, run = lax.fori_loop(0, i * (nd // unroll), group, (acc, run))
        o_ref[0] = jnp.concatenate([acc, jnp.broadcast_to(run, (tq, HEAD_DIM))], axis=1)

    return pl.pallas_call(
        body, name="attn_fwd", grid=(N_HEADS, seq // tq),
        in_specs=[pl.BlockSpec((1, tq, HEAD_DIM), lambda h, i: (h, i, 0)),
                  pl.BlockSpec((1, seq, HEAD_DIM), lambda h, i: (N_HEADS + h, 0, 0)),
                  pl.BlockSpec((1, seq, HEAD_DIM), lambda h, i: (2 * N_HEADS + h, 0, 0))],
        out_specs=pl.BlockSpec((1, tq, 2 * HEAD_DIM), lambda h, i: (h, i, 0)),
        out_shape=jax.ShapeDtypeStruct((N_HEADS, seq, 2 * HEAD_DIM), f32),
        compiler_params=_params(("parallel", "arbitrary")),
    )(qkv, qkv, qkv)


def _attn_bwd(qkv, o_tot, d_o, seq):
    t = CHUNK
    tq = min(ATTN_Q_ROWS, seq)
    nd = tq // t
    unroll = min(ATTN_UNROLL, nd)
    assert nd % unroll == 0
    nk = seq // t

    def body(q_ref, k_ref, v_ref, ot_ref, do_ref, dq_ref, dkt_ref, dvt_ref):
        i = pl.program_id(1)

        @pl.when(i == 0)
        def _():
            dkt_ref[...] = jnp.zeros_like(dkt_ref)
            dvt_ref[...] = jnp.zeros_like(dvt_ref)

        q = q_ref[0] * 0.125
        d_out = do_ref[0]
        total = ot_ref[0][:, HEAD_DIM:HEAD_DIM + 1]
        eye = (lax.broadcasted_iota(jnp.int32, (HEAD_DIM, HEAD_DIM), 0)
               == lax.broadcasted_iota(jnp.int32, (HEAD_DIM, HEAD_DIM), 1)).astype(bf16)
        q_t = _nt(eye, q).astype(bf16)
        do_t = _nt(eye, d_out).astype(bf16)
        ur = lax.broadcasted_iota(jnp.int32, (t, t), 0)
        uc = lax.broadcasted_iota(jnp.int32, (t, t), 1)
        ur2 = lax.broadcasted_iota(jnp.int32, (2 * t, t), 0) & (t - 1)
        incl = (ur2 <= lax.broadcasted_iota(jnp.int32, (2 * t, t), 1)).astype(bf16)
        before = (ur < uc).astype(bf16)

        def tile(j, r0, r1, dq, pre, dpre, masked):
            q_s, do_s, tot_s = q[r0:r1], d_out[r0:r1], total[r0:r1]
            n = q_s.shape[0]
            start = pl.multiple_of(j * t, t)
            k = k_ref[0, pl.ds(start, t), :]
            v = v_ref[0, pl.ds(start, t), :]
            z = _nt(q_s, k)
            lb, lom = _log_sigmoids(z)
            if masked:
                keep = lax.broadcasted_iota(jnp.int32, (n, t), 1) < lax.broadcasted_iota(jnp.int32, (n, t), 0)
                lom = jnp.where(keep, lom, 0.0)
            pin = jnp.dot(jnp.concatenate(_split_bf16(lom), axis=1), incl, preferred_element_type=f32)
            a = lb + ((tot_s - pre) - pin)
            if masked:
                a = jnp.where(keep, a, -jnp.inf)
            w = jnp.exp(a)
            d_a = _nt(do_s, v) * w
            d_lom_local = jnp.dot(d_a.astype(bf16), before, preferred_element_type=f32)
            d_lom = d_lom_local + dpre
            sig = jnp.exp(lb)
            dz = d_a * (1.0 - sig) - d_lom * sig
            if masked:
                dz = jnp.where(keep, dz, 0.0)
            dzb = dz.astype(bf16)
            dq = dq + jnp.dot(dzb, k, preferred_element_type=f32)
            dkt_ref[0, j] += jnp.dot(q_t[:, r0:r1], dzb, preferred_element_type=f32)
            dvt_ref[0, j] += jnp.dot(do_t[:, r0:r1], w.astype(bf16), preferred_element_type=f32)
            pre = pre + pin[:, t - 1:t]
            dpre = dpre + d_lom_local[:, t - 1:t] + d_a[:, t - 1:t]
            return dq, pre, dpre

        carry = (jnp.zeros((tq, HEAD_DIM), f32), jnp.zeros((tq, 1), f32), jnp.zeros((tq, 1), f32))
        def group(n, cr):
            for u in range(unroll):
                cr = tile(unroll * n + u, 0, tq, cr[0], cr[1], cr[2], False)
            return cr

        carry = lax.fori_loop(0, i * (nd // unroll), group, carry)
        for jj in range(nd):
            r0 = jj * t
            part = tile(i * nd + jj, r0, tq, *(c[r0:] for c in carry), True)
            carry = part if r0 == 0 else tuple(jnp.concatenate([c[:r0], p], axis=0) for c, p in zip(carry, part))
        dq_ref[0] = carry[0] * 0.125

    blk = pl.BlockSpec((1, tq, HEAD_DIM), lambda h, i: (h, i, 0))
    full_t = pl.BlockSpec((1, nk, HEAD_DIM, t), lambda h, i: (h, 0, 0, 0))
    return pl.pallas_call(
        body, name="attn_bwd", grid=(N_HEADS, seq // tq),
        in_specs=[blk, pl.BlockSpec((1, seq, HEAD_DIM), lambda h, i: (N_HEADS + h, 0, 0)),
                  pl.BlockSpec((1, seq, HEAD_DIM), lambda h, i: (2 * N_HEADS + h, 0, 0)),
                  pl.BlockSpec((1, tq, 2 * HEAD_DIM), lambda h, i: (h, i, 0)), blk],
        out_specs=(blk, full_t, full_t),
        out_shape=(jax.ShapeDtypeStruct((N_HEADS, seq, HEAD_DIM), f32),
                   jax.ShapeDtypeStruct((N_HEADS, nk, HEAD_DIM, t), f32), jax.ShapeDtypeStruct((N_HEADS, nk, HEAD_DIM, t), f32)),
        compiler_params=_params(("parallel", "arbitrary")),
    )(qkv, qkv, qkv, o_tot, d_o)


def _ssd_common(conv, dt_raw, dtb, alog):
    t = CHUNK
    sg = _sigmoid(conv)
    act = conv * sg
    dt_pre = dt_raw + dtb
    dt = _softplus(dt_pre)
    a = -jnp.exp(alog)
    row = lax.broadcasted_iota(jnp.int32, (t, t), 0)
    col = lax.broadcasted_iota(jnp.int32, (t, t), 1)
    causal = row >= col
    ac = _nn(causal.astype(f32), dt * a, HI)
    ac_t = _nt((row == col).astype(f32), ac, HI)
    ac_last = ac[t - 1:t, :]
    return sg, act, dt_pre, dt, a, causal, ac, ac_t, ac_last, jnp.exp(ac), jnp.exp(ac_last - ac), jnp.exp(ac_last)


def _ssd_fwd(xbc, dt_raw, conv_w, conv_b, dtb, alog, dsk, seq):
    t = CHUNK
    n_chunks = seq // t

    def body(x_ref, dt_ref, cw_ref, cb_ref, dtb_ref, al_ref, dsk_ref, conv_ref, y_ref, st_ref, prev, state):
        c = pl.program_id(0)

        @pl.when(c == 0)
        def _():
            prev[...] = jnp.zeros_like(prev)
            state[...] = jnp.zeros_like(state)

        cur = x_ref[...]
        pv = prev[...]
        rows = lax.broadcasted_iota(jnp.int32, (t, D_XBC), 0)
        conv = cur * cw_ref[CONV_K - 1:CONV_K, :] + cb_ref[...]
        for m in range(1, CONV_K):
            shifted = jnp.where(rows < m, pltpu.roll(pv, m, 0), pltpu.roll(cur, m, 0))
            conv = conv + shifted * cw_ref[CONV_K - 1 - m:CONV_K - m, :]
        prev[...] = cur
        conv_ref[...] = conv
        _, act, _, dt, _, _, ac, ac_t, _, e_ac, dte, cdec = _ssd_common(conv, dt_ref[...], dtb_ref[...], al_ref[...])
        xs = act[:, :D_SSM]
        wide = N_HEADS * t
        sel64 = (lax.broadcasted_iota(jnp.int32, (LANES, D_SSM), 1) // HEAD_DIM
                 == lax.broadcasted_iota(jnp.int32, (LANES, D_SSM), 0)).astype(f32)
        sel128 = (lax.broadcasted_iota(jnp.int32, (LANES, wide), 1) // t
                  == lax.broadcasted_iota(jnp.int32, (LANES, wide), 0)).astype(f32)
        xd_all = xs * _nn(dt, sel64, HI)
        xdb = xd_all.astype(bf16)
        xdte_b = (xd_all * _nn(dte, sel64, HI)).astype(bf16)
        seg_all = _nn(ac, sel128, HI) - jnp.concatenate([jnp.broadcast_to(ac_t[h:h + 1, :], (t, t)) for h in range(N_HEADS)], axis=1)
        causal_all = (lax.broadcasted_iota(jnp.int32, (t, wide), 0) >= (lax.broadcasted_iota(jnp.int32, (t, wide), 1) & (t - 1)))
        lm_all = jnp.exp(jnp.where(causal_all, seg_all, -jnp.inf))
        bgs = [act[:, D_SSM + g * N_STATE:D_SSM + (g + 1) * N_STATE].astype(bf16) for g in range(N_GROUPS)]
        cgs = [act[:, D_SSM + (N_GROUPS + g) * N_STATE:D_SSM + (N_GROUPS + g + 1) * N_STATE].astype(bf16) for g in range(N_GROUPS)]
        gms = [_nt(cgs[g], bgs[g]) for g in range(N_GROUPS)]
        mm_b = (jnp.concatenate([gms[h // HEADS_PER_GROUP] for h in range(N_HEADS)], axis=1) * lm_all).astype(bf16)
        y_diags, zos = [], []
        for h in range(N_HEADS):
            g = h // HEADS_PER_GROUP
            hs = slice(h * HEAD_DIM, (h + 1) * HEAD_DIM)
            hp = state[h]
            st_ref[0, h] = hp
            y_diags.append(jnp.dot(mm_b[:, h * t:(h + 1) * t], xdb[:, hs], preferred_element_type=f32))
            zos.append(_nt(cgs[g], hp.astype(bf16)))
            state[h] = hp * cdec[:, h:h + 1] + _tn(xdte_b[:, hs], bgs[g])
        y_ref[...] = (jnp.concatenate(y_diags, axis=1) + jnp.concatenate(zos, axis=1) * _nn(e_ac, sel64, HI)
                      + xs * _nn(dsk_ref[...], sel64, HI))

    row = lambda c: (c, 0)
    fixed = lambda c: (0, 0)
    return pl.pallas_call(
        body, name="ssd_fwd", grid=(n_chunks,),
        in_specs=[pl.BlockSpec((t, D_XBC), row), pl.BlockSpec((t, LANES), row), pl.BlockSpec((CONV_K, D_XBC), fixed),
                  pl.BlockSpec((1, D_XBC), fixed), pl.BlockSpec((1, LANES), fixed), pl.BlockSpec((1, LANES), fixed),
                  pl.BlockSpec((1, LANES), fixed)],
        out_specs=(pl.BlockSpec((t, D_XBC), row), pl.BlockSpec((t, D_SSM), row),
                   pl.BlockSpec((1, N_HEADS, HEAD_DIM, N_STATE), lambda c: (c, 0, 0, 0))),
        out_shape=(jax.ShapeDtypeStruct((seq, D_XBC), f32), jax.ShapeDtypeStruct((seq, D_SSM), f32),
                   jax.ShapeDtypeStruct((n_chunks, N_HEADS, HEAD_DIM, N_STATE), f32)),
        scratch_shapes=[pltpu.VMEM((t, D_XBC), f32), pltpu.VMEM((N_HEADS, HEAD_DIM, N_STATE), f32)],
        compiler_params=_params(("arbitrary",)),
    )(xbc, dt_raw, conv_w, conv_b, dtb, alog, dsk)


def _ssd_bwd(dy, conv, xbc, dt_raw, states, conv_w, dtb, alog, dsk, seq):
    t = CHUNK
    n_chunks = seq // t

    def body(dy_ref, conv_ref, x_ref, dt_ref, st_ref, cw_ref, dtb_ref, al_ref, dsk_ref,
             dx_ref, ddt_ref, gcw_ref, gcb_ref, gdtb_ref, gal_ref, gdsk_ref, d_state, d_conv_next):
        c = pl.program_id(0)

        @pl.when(c == 0)
        def _():
            d_state[...] = jnp.zeros_like(d_state)
            d_conv_next[...] = jnp.zeros_like(d_conv_next)
            gcw_ref[...] = jnp.zeros_like(gcw_ref)
            gcb_ref[...] = jnp.zeros_like(gcb_ref)
            gdtb_ref[...] = jnp.zeros_like(gdtb_ref)
            gal_ref[...] = jnp.zeros_like(gal_ref)
            gdsk_ref[...] = jnp.zeros_like(gdsk_ref)

        conv = conv_ref[...]
        sg, act, dt_pre, dt, a, causal, ac, ac_t, _, e_ac, dte, cdec = _ssd_common(conv, dt_ref[...], dtb_ref[...], al_ref[...])
        dyv = dy_ref[...]
        xs = act[:, :D_SSM]
        sel64 = lax.broadcasted_iota(jnp.int32, (LANES, D_SSM), 1) // HEAD_DIM == lax.broadcasted_iota(jnp.int32, (LANES, D_SSM), 0)
        sel64_t = lax.broadcasted_iota(jnp.int32, (D_SSM, LANES), 0) // HEAD_DIM == lax.broadcasted_iota(jnp.int32, (D_SSM, LANES), 1)
        wide = N_HEADS * t
        sel128 = lax.broadcasted_iota(jnp.int32, (LANES, wide), 1) // t == lax.broadcasted_iota(jnp.int32, (LANES, wide), 0)
        sel128_t = lax.broadcasted_iota(jnp.int32, (wide, LANES), 0) // t == lax.broadcasted_iota(jnp.int32, (wide, LANES), 1)

        def spread(v, sel):
            return _nn(v, sel.astype(f32), HI)

        def lane_sums(v, sel_t):
            hi, lo = _split_bf16(v)
            sel_b = sel_t.astype(bf16)
            return jnp.dot(hi, sel_b, preferred_element_type=f32) + jnp.dot(lo, sel_b, preferred_element_type=f32)

        dt_x = spread(dt, sel64)
        e_x = spread(e_ac, sel64)
        dte_x = spread(dte, sel64)
        seg_all = spread(ac, sel128) - jnp.concatenate([jnp.broadcast_to(ac_t[h:h + 1, :], (t, t)) for h in range(N_HEADS)], axis=1)
        causal_all = (lax.broadcasted_iota(jnp.int32, (t, wide), 0) >= (lax.broadcasted_iota(jnp.int32, (t, wide), 1) & (t - 1)))
        lm_all = jnp.exp(jnp.where(causal_all, seg_all, -jnp.inf))
        xd_all = xs * dt_x
        xdb = xd_all.astype(bf16)
        xdte_b = (xd_all * dte_x).astype(bf16)
        d_yb = dyv.astype(bf16)
        d_zo_all = dyv * e_x
        d_zob = d_zo_all.astype(bf16)
        bgs = [act[:, D_SSM + g * N_STATE:D_SSM + (g + 1) * N_STATE].astype(bf16) for g in range(N_GROUPS)]
        cgs = [act[:, D_SSM + (N_GROUPS + g) * N_STATE:D_SSM + (N_GROUPS + g + 1) * N_STATE].astype(bf16) for g in range(N_GROUPS)]
        gms = [_nt(cgs[g], bgs[g]) for g in range(N_GROUPS)]
        mm_all = jnp.concatenate([gms[h // HEADS_PER_GROUP] for h in range(N_HEADS)], axis=1) * lm_all
        mm_b = mm_all.astype(bf16)
        d_hn_all = d_state[...].reshape(N_HEADS * HEAD_DIM, N_STATE)
        hp_all = st_ref[0].reshape(N_HEADS * HEAD_DIM, N_STATE)
        state_dot = lane_sums(d_hn_all * hp_all, jnp.ones((N_STATE, LANES), jnp.bool_))
        state_dot = jnp.sum(jnp.where(sel64_t, state_dot, 0.0), axis=0, keepdims=True)
        d_mms, d_xds, zos, d_ws = [], [], [], []
        d_bs = [jnp.zeros((t, N_STATE), f32) for _ in range(N_GROUPS)]
        d_cs = [jnp.zeros((t, N_STATE), f32) for _ in range(N_GROUPS)]
        for h in range(N_HEADS):
            g = h // HEADS_PER_GROUP
            hs = slice(h * HEAD_DIM, (h + 1) * HEAD_DIM)
            hpb = st_ref[0, h].astype(bf16)
            d_hn = d_state[h]
            d_hnb = d_hn.astype(bf16)
            d_mms.append(_nt(d_yb[:, hs], xdb[:, hs]))
            d_xds.append(_tn(mm_b[:, h * t:(h + 1) * t], d_yb[:, hs]))
            zos.append(_nt(cgs[g], hpb))
            d_cs[g] = d_cs[g] + jnp.dot(d_zob[:, hs], hpb, preferred_element_type=f32)
            d_state[h] = _tn(d_zob[:, hs], cgs[g]) + d_hn * cdec[:, h:h + 1]
            d_ws.append(_nt(bgs[g], d_hnb))
            d_bs[g] = d_bs[g] + jnp.dot(xdte_b[:, hs], d_hnb, preferred_element_type=f32)
        d_mm_all = jnp.concatenate(d_mms, axis=1)
        d_seg_all = d_mm_all * mm_all
        d_gm_all = d_mm_all * lm_all
        dwd_all = jnp.concatenate(d_ws, axis=1) * dte_x
        d_xd_all = jnp.concatenate(d_xds, axis=1) + dwd_all
        decay_part = dwd_all * xd_all
        decay_sums = lane_sums(decay_part, sel64_t)
        d_ac = lane_sums(d_seg_all, sel128_t) + lane_sums(d_zo_all * jnp.concatenate(zos, axis=1) - decay_part, sel64_t)
        d_dt = lane_sums(d_xd_all * xs, sel64_t)
        g_dsk = jnp.sum(lane_sums(dyv * xs, sel64_t), axis=0, keepdims=True)
        stacked = jnp.concatenate([d_seg_all[:, h * t:(h + 1) * t] for h in range(N_HEADS)], axis=0)
        s_hi, s_lo = _split_bf16(stacked)
        sel_b = sel128.astype(bf16)
        col_sums = jnp.dot(sel_b, s_hi, preferred_element_type=f32) + jnp.dot(sel_b, s_lo, preferred_element_type=f32)
        d_last = state_dot * cdec + jnp.sum(decay_sums, axis=0, keepdims=True)
        last_row = (lax.broadcasted_iota(jnp.int32, (t, 1), 0) == t - 1).astype(f32)
        sq_row = lax.broadcasted_iota(jnp.int32, (t, t), 0)
        sq_col = lax.broadcasted_iota(jnp.int32, (t, t), 1)
        d_ac = d_ac - _nt((sq_row == sq_col).astype(f32), col_sums, HI) + last_row * d_last
        d_ld = _nn((sq_col >= sq_row).astype(f32), d_ac, HI)
        dxs = [dyv * spread(dsk_ref[...], sel64) + d_xd_all * dt_x]
        dbs, dcs = [], []
        for g in range(N_GROUPS):
            d_gm = d_gm_all[:, g * HEADS_PER_GROUP * t:g * HEADS_PER_GROUP * t + t]
            for r in range(1, HEADS_PER_GROUP):
                d_gm = d_gm + d_gm_all[:, (g * HEADS_PER_GROUP + r) * t:(g * HEADS_PER_GROUP + r + 1) * t]
            d_gmb = d_gm.astype(bf16)
            dcs.append(d_cs[g] + jnp.dot(d_gmb, bgs[g], preferred_element_type=f32))
            dbs.append(d_bs[g] + _tn(d_gmb, cgs[g]))
        d_dt = d_dt + d_ld * a
        gal_ref[...] += jnp.sum(d_ld * dt, axis=0, keepdims=True) * a
        gdsk_ref[...] += g_dsk
        d_dt_raw = d_dt * _sigmoid(dt_pre)
        ddt_ref[...] = d_dt_raw.astype(bf16)
        gdtb_ref[...] += jnp.sum(d_dt_raw, axis=0, keepdims=True)
        d_act = jnp.concatenate(dxs + dbs + dcs, axis=1)
        d_conv = d_act * (sg * (1.0 + conv * (1.0 - sg)))
        gcb_ref[...] += jnp.sum(d_conv, axis=0, keepdims=True)
        nxt = d_conv_next[...]
        rows = lax.broadcasted_iota(jnp.int32, (t, D_XBC), 0)
        xraw = x_ref[...]
        d_x = d_conv * cw_ref[CONV_K - 1:CONV_K, :]
        gcw_ref[pl.ds(CONV_K - 1, 1), :] += jnp.sum(xraw * d_conv, axis=0, keepdims=True)
        for m in range(1, CONV_K):
            ahead = jnp.where(rows >= t - m, pltpu.roll(nxt, t - m, 0), pltpu.roll(d_conv, t - m, 0))
            d_x = d_x + ahead * cw_ref[CONV_K - 1 - m:CONV_K - m, :]
            gcw_ref[pl.ds(CONV_K - 1 - m, 1), :] += jnp.sum(xraw * ahead, axis=0, keepdims=True)
        d_conv_next[...] = d_conv
        dx_ref[...] = d_x.astype(bf16)

    rev = lambda c: (n_chunks - 1 - c, 0)
    fixed = lambda c: (0, 0)
    return pl.pallas_call(
        body, name="ssd_bwd", grid=(n_chunks,),
        in_specs=[pl.BlockSpec((t, D_SSM), rev), pl.BlockSpec((t, D_XBC), rev), pl.BlockSpec((t, D_XBC), rev),
                  pl.BlockSpec((t, LANES), rev), pl.BlockSpec((1, N_HEADS, HEAD_DIM, N_STATE), lambda c: (n_chunks - 1 - c, 0, 0, 0)),
                  pl.BlockSpec((CONV_K, D_XBC), fixed), pl.BlockSpec((1, LANES), fixed), pl.BlockSpec((1, LANES), fixed),
                  pl.BlockSpec((1, LANES), fixed)],
        out_specs=(pl.BlockSpec((t, D_XBC), rev), pl.BlockSpec((t, LANES), rev), pl.BlockSpec((CONV_K, D_XBC), fixed),
                   pl.BlockSpec((1, D_XBC), fixed), pl.BlockSpec((1, LANES), fixed), pl.BlockSpec((1, LANES), fixed),
                   pl.BlockSpec((1, LANES), fixed)),
        out_shape=(jax.ShapeDtypeStruct((seq, D_XBC), bf16), jax.ShapeDtypeStruct((seq, LANES), bf16),
                   jax.ShapeDtypeStruct((CONV_K, D_XBC), f32), jax.ShapeDtypeStruct((1, D_XBC), f32),
                   jax.ShapeDtypeStruct((1, LANES), f32), jax.ShapeDtypeStruct((1, LANES), f32), jax.ShapeDtypeStruct((1, LANES), f32)),
        scratch_shapes=[pltpu.VMEM((N_HEADS, HEAD_DIM, N_STATE), f32), pltpu.VMEM((t, D_XBC), f32)],
        compiler_params=_params(("arbitrary",)),
    )(dy, conv, xbc, dt_raw, states, conv_w, dtb, alog, dsk)


def _heads_to_cols(ref, width=HEAD_DIM):
    return jnp.concatenate([ref[h][:, :width] for h in range(N_HEADS)], axis=1)


def _silu_and_grad(z):
    sg = _sigmoid(z)
    return z * sg, sg * (1.0 + z * (1.0 - sg))


def _rms(v):
    return lax.rsqrt(jnp.mean(v * v, axis=-1, keepdims=True) + EPS)


def _rms_bwd(d_hat, hat, r):
    return r * (d_hat - hat * jnp.mean(d_hat * hat, axis=-1, keepdims=True))


def _post(x, target, o_tot, y, za, zs, w_out, gate, g_sb, g_ssm, g_f, seq):
    ts = 256

    def body(x_ref, t_ref, o_ref, y_ref, za_ref, zs_ref, w_ref, gate_ref, gsb_ref, gss_ref, gf_ref,
             ycat_t_ref, dmix_ref, dx2_ref, loss_ref, gnf_ref, dgate_ref):
        @pl.when(pl.program_id(0) == 0)
        def _():
            loss_ref[...] = jnp.zeros_like(loss_ref)
            gnf_ref[...] = jnp.zeros_like(gnf_ref)
            dgate_ref[...] = jnp.zeros_like(dgate_ref)

        o = _heads_to_cols(o_ref)
        zav = za_ref[...]
        ya = (o * _rms(o) * gsb_ref[...]) * (zav * _sigmoid(zav))
        zsv = zs_ref[...]
        u = y_ref[...] * (zsv * _sigmoid(zsv))
        ys = u * _rms(u) * gss_ref[...]
        yab, ysb = ya.astype(bf16), ys.astype(bf16)
        _store_transposed(ycat_t_ref.at[:D_ATTN], yab)
        _store_transposed(ycat_t_ref.at[D_ATTN:], ysb)
        mixed = (jnp.dot(yab, w_ref[:D_ATTN, :], preferred_element_type=f32)
                 + jnp.dot(ysb, w_ref[D_ATTN:, :], preferred_element_type=f32))
        gate_v = gate_ref[...]
        x2 = x_ref[...] + gate_v * mixed
        r2 = _rms(x2)
        xh = x2 * r2
        gf = gf_ref[...]
        diff = xh * gf - t_ref[...]
        loss_ref[...] += jnp.sum(diff * diff) * (0.5 / D_MODEL)
        d_out = diff * (1.0 / D_MODEL)
        gnf_ref[...] += jnp.sum(d_out * xh, axis=0, keepdims=True)
        dx2 = _rms_bwd(d_out * gf, xh, r2)
        dx2_ref[...] = dx2
        dgate_ref[...] += jnp.sum(dx2 * mixed, axis=0, keepdims=True)
        dmix_ref[...] = (dx2 * gate_v).astype(bf16)

    row = lambda i: (i, 0)
    fixed = lambda i: (0, 0)
    vec = pl.BlockSpec((1, D_MODEL), fixed)
    return pl.pallas_call(
        body, name="post", grid=(seq // ts,),
        in_specs=[pl.BlockSpec((ts, D_MODEL), row), pl.BlockSpec((ts, D_MODEL), row),
                  pl.BlockSpec((N_HEADS, ts, 2 * HEAD_DIM), lambda i: (0, i, 0)), pl.BlockSpec((ts, D_SSM), row),
                  pl.BlockSpec((ts, D_ATTN), row), pl.BlockSpec((ts, D_SSM), row), pl.BlockSpec((D_ATTN + D_SSM, D_MODEL), fixed),
                  vec, vec, vec, vec],
        out_specs=(pl.BlockSpec((D_ATTN + D_SSM, ts), lambda i: (0, i)), pl.BlockSpec((ts, D_MODEL), row), pl.BlockSpec((ts, D_MODEL), row),
                   pl.BlockSpec((1, LANES), fixed), vec, vec),
        out_shape=(jax.ShapeDtypeStruct((D_ATTN + D_SSM, seq), bf16), jax.ShapeDtypeStruct((seq, D_MODEL), bf16),
                   jax.ShapeDtypeStruct((seq, D_MODEL), f32), jax.ShapeDtypeStruct((1, LANES), f32),
                   jax.ShapeDtypeStruct((1, D_MODEL), f32), jax.ShapeDtypeStruct((1, D_MODEL), f32)),
        compiler_params=_params(("arbitrary",)),
    )(x, target, o_tot, y, za, zs, w_out, gate, g_sb, g_ssm, g_f)


def _bwd_out(dmix, w_out, o_tot, y, za, zs, g_sb, g_ssm, seq):
    ts = 256

    def body(dm_ref, w_ref, o_ref, y_ref, za_ref, zs_ref, gsb_ref, gss_ref, do_ref, dza_ref, dzs_ref, dy_ref, ggsb_ref, ggss_ref):
        @pl.when(pl.program_id(0) == 0)
        def _():
            ggsb_ref[...] = jnp.zeros_like(ggsb_ref)
            ggss_ref[...] = jnp.zeros_like(ggss_ref)

        dm = dm_ref[...]
        d_ya = _nt(dm, w_ref[:D_ATTN, :])
        d_ys = _nt(dm, w_ref[D_ATTN:, :])
        o = _heads_to_cols(o_ref)
        ro = _rms(o)
        oh = o * ro
        sa, dsa = _silu_and_grad(za_ref[...])
        gsb = gsb_ref[...]
        dza_ref[...] = (d_ya * oh * gsb * dsa).astype(bf16)
        ggsb_ref[...] += jnp.sum(d_ya * oh * sa, axis=0, keepdims=True)
        d_o = _rms_bwd(d_ya * gsb * sa, oh, ro)
        for h in range(N_HEADS):
            do_ref[h] = d_o[:, h * HEAD_DIM:(h + 1) * HEAD_DIM].astype(bf16)
        yv = y_ref[...]
        sz, dsz = _silu_and_grad(zs_ref[...])
        u = yv * sz
        ru = _rms(u)
        uh = u * ru
        ggss_ref[...] += jnp.sum(d_ys * uh, axis=0, keepdims=True)
        du = _rms_bwd(d_ys * gss_ref[...], uh, ru)
        dy_ref[...] = du * sz
        dzs_ref[...] = (du * yv * dsz).astype(bf16)

    row = lambda i: (i, 0)
    fixed = lambda i: (0, 0)
    vec = pl.BlockSpec((1, D_MODEL), fixed)
    return pl.pallas_call(
        body, name="bwd_out", grid=(seq // ts,),
        in_specs=[pl.BlockSpec((ts, D_MODEL), row), pl.BlockSpec((D_ATTN + D_SSM, D_MODEL), fixed),
                  pl.BlockSpec((N_HEADS, ts, 2 * HEAD_DIM), lambda i: (0, i, 0)), pl.BlockSpec((ts, D_SSM), row),
                  pl.BlockSpec((ts, D_ATTN), row), pl.BlockSpec((ts, D_SSM), row), vec, vec],
        out_specs=(pl.BlockSpec((N_HEADS, ts, HEAD_DIM), lambda i: (0, i, 0)), pl.BlockSpec((ts, D_ATTN), row),
                   pl.BlockSpec((ts, D_SSM), row), pl.BlockSpec((ts, D_SSM), row), vec, vec),
        out_shape=(jax.ShapeDtypeStruct((N_HEADS, seq, HEAD_DIM), bf16), jax.ShapeDtypeStruct((seq, D_ATTN), bf16),
                   jax.ShapeDtypeStruct((seq, D_SSM), bf16), jax.ShapeDtypeStruct((seq, D_SSM), f32),
                   jax.ShapeDtypeStruct((1, D_MODEL), f32), jax.ShapeDtypeStruct((1, D_MODEL), f32)),
        compiler_params=_params(("arbitrary",)),
    )(dmix, w_out, o_tot, y, za, zs, g_sb, g_ssm)


def _qkv_grads_to_cols(dq, dkt, dvt, seq):
    ts = 256
    nb = ts // CHUNK

    def body(dq_ref, dkt_ref, dvt_ref, out_ref):
        out_ref[:, :D_ATTN] = _heads_to_cols(dq_ref).astype(bf16)
        eye = (lax.broadcasted_iota(jnp.int32, (CHUNK, CHUNK), 0) == lax.broadcasted_iota(jnp.int32, (CHUNK, CHUNK), 1)).astype(bf16)
        for p, ref in ((1, dkt_ref), (2, dvt_ref)):
            for b in range(nb):
                cols = [_nt(eye, ref[h, b].astype(bf16)) for h in range(N_HEADS)]
                out_ref[b * CHUNK:(b + 1) * CHUNK, p * D_ATTN:(p + 1) * D_ATTN] = jnp.concatenate(cols, axis=1).astype(bf16)

    blk = pl.BlockSpec((N_HEADS, ts, HEAD_DIM), lambda i: (0, i, 0))
    blk_t = pl.BlockSpec((N_HEADS, nb, HEAD_DIM, CHUNK), lambda i: (0, i, 0, 0))
    return pl.pallas_call(
        body, name="qkv_grads_to_cols", grid=(seq // ts,), in_specs=[blk, blk_t, blk_t],
        out_specs=pl.BlockSpec((ts, 3 * D_ATTN), lambda i: (i, 0)),
        out_shape=jax.ShapeDtypeStruct((seq, 3 * D_ATTN), bf16), compiler_params=_params(("parallel",)),
    )(dq, dkt, dvt)


def _bwd_in(dqkv, dza, dzs, dxbc, ddt, wp, x, dx2, gain, scale, seq):
    ts = 256
    pieces = ((0, 0, 3 * D_ATTN), (1, OFF_ZA, D_ATTN), (2, OFF_ZS, D_SSM), (3, OFF_XBC, D_XBC), (4, OFF_DT, LANES))

    def body(dqkv_ref, dza_ref, dzs_ref, dxbc_ref, ddt_ref, w_ref, x_ref, dx2_ref, g_ref, sc_ref,
             gx_ref, dshift_ref, dscale_ref, ggain_ref):
        @pl.when(pl.program_id(0) == 0)
        def _():
            dshift_ref[...] = jnp.zeros_like(dshift_ref)
            dscale_ref[...] = jnp.zeros_like(dscale_ref)
            ggain_ref[...] = jnp.zeros_like(ggain_ref)

        refs = (dqkv_ref, dza_ref, dzs_ref, dxbc_ref, ddt_ref)
        dh = jnp.zeros((ts, D_MODEL), f32)
        for idx, off, width in pieces:
            for cc in range(0, width, 512):
                wd = min(512, width - cc)
                dh = dh + _nt(refs[idx][:, cc:cc + wd], w_ref[:, off + cc:off + cc + wd])
        xv = x_ref[...]
        r = _rms(xv)
        xh = xv * r
        g = g_ref[...]
        dshift_ref[...] += jnp.sum(dh, axis=0, keepdims=True)
        dscale_ref[...] += jnp.sum(dh * xh * g, axis=0, keepdims=True)
        tt = dh * (1.0 + sc_ref[...])
        ggain_ref[...] += jnp.sum(tt * xh, axis=0, keepdims=True)
        gx_ref[...] = dx2_ref[...] + _rms_bwd(tt * g, xh, r)

    row = lambda i: (i, 0)
    fixed = lambda i: (0, 0)
    vec = pl.BlockSpec((1, D_MODEL), fixed)
    return pl.pallas_call(
        body, name="bwd_in", grid=(seq // ts,),
        in_specs=[pl.BlockSpec((ts, 3 * D_ATTN), row), pl.BlockSpec((ts, D_ATTN), row), pl.BlockSpec((ts, D_SSM), row),
                  pl.BlockSpec((ts, D_XBC), row), pl.BlockSpec((ts, LANES), row), pl.BlockSpec((D_MODEL, D_PROJ_P), fixed),
                  pl.BlockSpec((ts, D_MODEL), row), pl.BlockSpec((ts, D_MODEL), row), vec, vec],
        out_specs=(pl.BlockSpec((ts, D_MODEL), row), vec, vec, vec),
        out_shape=(jax.ShapeDtypeStruct((seq, D_MODEL), f32), jax.ShapeDtypeStruct((1, D_MODEL), f32),
                   jax.ShapeDtypeStruct((1, D_MODEL), f32), jax.ShapeDtypeStruct((1, D_MODEL), f32)),
        compiler_params=_params(("arbitrary",)),
    )(dqkv, dza, dzs, dxbc, ddt, wp, x, dx2, gain, scale)


def _grad_w(a_t, b, tn, name):
    m, seq = a_t.shape
    n = b.shape[1]
    tk = min(512, seq)
    n_k = seq // tk

    def body(a_ref, b_ref, o_ref, acc):
        @pl.when(pl.program_id(1) == 0)
        def _():
            acc[...] = jnp.zeros_like(acc)

        acc[...] += jnp.dot(a_ref[...], b_ref[...], preferred_element_type=f32)

        @pl.when(pl.program_id(1) == n_k - 1)
        def _():
            o_ref[...] = acc[...].astype(bf16)

    return pl.pallas_call(
        body, name=name, grid=(n // tn, n_k),
        in_specs=[pl.BlockSpec((m, tk), lambda j, k: (0, k)), pl.BlockSpec((tk, tn), lambda j, k: (k, j))],
        out_specs=pl.BlockSpec((m, tn), lambda j, k: (0, j)),
        out_shape=jax.ShapeDtypeStruct((m, n), bf16), scratch_shapes=[pltpu.VMEM((m, tn), f32)],
        compiler_params=_params(("parallel", "arbitrary")),
    )(a_t, b)


def _small_finish(g_all, c_all, dmod_mine):
    def body(g_ref, c_ref, dm_ref, tot_ref, gwada_ref):
        tot = g_ref[0:1, :]
        for j in range(1, N_DEV):
            tot = tot + g_ref[j:j + 1, :]
        tot_ref[...] = tot
        cv = c_ref[...]
        gwada_ref[...] = _tn(cv * _sigmoid(cv), dm_ref[...], HI)

    vmem = pl.BlockSpec(memory_space=pltpu.VMEM)
    return pl.pallas_call(
        body, name="small_finish", in_specs=[vmem, vmem, vmem], out_specs=(vmem, vmem),
        out_shape=(jax.ShapeDtypeStruct((1, N_PACK), f32), jax.ShapeDtypeStruct((D_MODEL, dmod_mine.shape[1]), f32)),
        compiler_params=_params(),
    )(g_all, c_all, dmod_mine)


def _adamw(w, g_parts, m, v, rows, name):
    r, c = w.shape
    n_parts = g_parts.shape[0]
    bc1 = 1.0 - ADAM_B1 ** ADAM_STEP
    bc2 = 1.0 - ADAM_B2 ** ADAM_STEP

    def body(w_ref, g_ref, m_ref, v_ref, go_ref, d_ref, mo_ref, vo_ref):
        g = g_ref[0].astype(f32)
        for j in range(1, n_parts):
            g = g + g_ref[j].astype(f32)
        go_ref[...] = g
        mn = ADAM_B1 * m_ref[...] + (1.0 - ADAM_B1) * g
        vn = ADAM_B2 * v_ref[...] + (1.0 - ADAM_B2) * (g * g)
        mo_ref[...] = mn
        vo_ref[...] = vn
        d_ref[...] = -ADAM_LR * ((mn / bc1) / (jnp.sqrt(vn / bc2) + ADAM_EPS) + ADAM_WD * w_ref[...])

    blk = pl.BlockSpec((rows, c), lambda i: (i, 0))
    return pl.pallas_call(
        body, name=name, grid=(r // rows,),
        in_specs=[blk, pl.BlockSpec((n_parts, rows, c), lambda i: (0, i, 0)), blk, blk],
        out_specs=(blk, blk, blk, blk), out_shape=(jax.ShapeDtypeStruct((r, c), f32),) * 4,
        compiler_params=_params(("parallel",)),
    )(w, g_parts, m, v)


def _pad_lanes(v):
    return jnp.pad(v, ((0, 0), (0, LANES - v.shape[1])))


def kernel(x, c, w_ada, b_ada, norm_in_gain, w_in, conv_w, conv_b, dt_bias, a_log, d_skip, sb_norm_gain, ssm_norm_gain, w_out, norm_f_gain, loss_target, m_w_ada, m_b_ada, m_norm_in_gain, m_w_in, m_conv_w, m_conv_b, m_dt_bias, m_a_log, m_d_skip, m_sb_norm_gain, m_ssm_norm_gain, m_w_out, m_norm_f_gain, v_w_ada, v_b_ada, v_norm_in_gain, v_w_in, v_conv_w, v_conv_b, v_dt_bias, v_a_log, v_d_skip, v_sb_norm_gain, v_ssm_norm_gain, v_w_out, v_norm_f_gain):
    seq = x.shape[1]
    xs = x[0]
    tgt = loss_target[0]
    _, my_slot = _me()

    mod, c_all = _mod_exchange(c, w_ada[0], b_ada)
    shift, scale, gate = mod[:, :D_MODEL], mod[:, D_MODEL:2 * D_MODEL], mod[:, 2 * D_MODEL:]
    w_in_g, w_out_g, conv_w_g = _all_gather_two_level(
        [_cast_bf16(w_in[0], 128), _cast_bf16(w_out[0], 128), conv_w[0]], "gather_weights")
    w_full = jnp.transpose(w_in_g, (1, 0, 2)).reshape(D_MODEL, D_PROJ)
    wp = jnp.concatenate([w_full[:, :4 * D_ATTN], w_full[:, D_PROJ - D_SSM:], w_full[:, 4 * D_ATTN:4 * D_ATTN + D_XBC],
                          _pad_lanes(w_full[:, 4 * D_ATTN + D_XBC:4 * D_ATTN + D_XBC + N_HEADS])], axis=1)
    w_out_full = w_out_g.reshape(D_ATTN + D_SSM, D_MODEL)
    conv_w_full = jnp.transpose(conv_w_g, (1, 0, 2)).reshape(CONV_K, D_XBC)
    dtb, alog, dsk = _pad_lanes(dt_bias), _pad_lanes(a_log), _pad_lanes(d_skip)

    h_t, qkv, za, zs, xbc, dt_raw = _proj(xs, norm_in_gain, scale, shift, wp, seq)
    o_tot = _attn_fwd(qkv, seq)
    conv, y, states = _ssd_fwd(xbc, dt_raw, conv_w_full, conv_b, dtb, alog, dsk, seq)
    ycat_t, dmix, dx2, loss_p, g_nf, d_gate = _post(xs, tgt, o_tot, y, za, zs, w_out_full, gate, sb_norm_gain, ssm_norm_gain,
                                                  norm_f_gain.reshape(1, D_MODEL), seq)

    d_o, dza, dzs, dy, g_sb, g_ss = _bwd_out(dmix, w_out_full, o_tot, y, za, zs, sb_norm_gain, ssm_norm_gain, seq)
    dq, dk, dv = _attn_bwd(qkv, o_tot, d_o, seq)
    dxbc, ddt, g_cw, g_cb, g_dtb, g_al, g_dsk = _ssd_bwd(dy, conv, xbc, dt_raw, states, conv_w_full, dtb, alog, dsk, seq)
    dqkv = _qkv_grads_to_cols(dq, dk, dv, seq)
    grad_x, d_shift, d_scale, g_in = _bwd_in(dqkv, dza, dzs, dxbc, ddt, wp, xs, dx2, norm_in_gain, scale, seq)
    gw_qkv = _grad_w(h_t, dqkv, 512, "grad_w_qkv")
    gw_za = _grad_w(h_t, dza, 512, "grad_w_za")
    gw_zs = _grad_w(h_t, dzs, 512, "grad_w_zs")
    gw_xbc = _grad_w(h_t, dxbc, 512, "grad_w_xbc")
    gw_dt = _grad_w(h_t, ddt, LANES, "grad_w_dt")
    gw_out = _grad_w(ycat_t, dmix, 512, "grad_w_out")
    gw_in = jnp.concatenate([gw_qkv, gw_za, gw_xbc, gw_dt[:, :N_HEADS], gw_zs], axis=1)

    gw_in_parts, gw_out_parts = _reduce_scatter_two_level(
        [jnp.transpose(gw_in.reshape(D_MODEL, N_DEV, W_IN_SHARD), (1, 0, 2)),
         gw_out.reshape(N_DEV, (D_ATTN + D_SSM) // N_DEV, D_MODEL)], "scatter_grads")
    packed = jnp.concatenate([loss_p, d_shift, d_scale, d_gate, g_in, g_cb, g_dtb, g_al, g_dsk, g_sb, g_ss, g_nf,
                              g_cw.reshape(1, CONV_K * D_XBC)], axis=1)
    (packed_all,) = _all_gather([packed], "gather_small")
    packed_all = packed_all.reshape(N_DEV, N_PACK)
    n_ada = w_ada.shape[2]
    dmod_mine = lax.dynamic_slice(packed_all, (0, P_DMOD + my_slot * n_ada), (N_DEV, n_ada))
    tot, g_w_ada = _small_finish(packed_all, c_all.reshape(N_DEV, D_MODEL), dmod_mine)

    def big(w, parts, m, v, rows, name):
        return tuple(t[None] for t in _adamw(w[0], parts, m[0], v[0], rows, name))

    small_names = ["b_ada", "norm_in_gain", "conv_b", "dt_bias", "a_log", "d_skip", "sb_norm_gain", "ssm_norm_gain", "norm_f_gain"]
    given = {"b_ada": (b_ada, m_b_ada, v_b_ada), "norm_in_gain": (norm_in_gain, m_norm_in_gain, v_norm_in_gain),
             "conv_b": (conv_b, m_conv_b, v_conv_b), "dt_bias": (dt_bias, m_dt_bias, v_dt_bias), "a_log": (a_log, m_a_log, v_a_log),
             "d_skip": (d_skip, m_d_skip, v_d_skip), "sb_norm_gain": (sb_norm_gain, m_sb_norm_gain, v_sb_norm_gain),
             "ssm_norm_gain": (ssm_norm_gain, m_ssm_norm_gain, v_ssm_norm_gain), "norm_f_gain": (norm_f_gain, m_norm_f_gain, v_norm_f_gain)}

    def pack(which):
        cols = []
        for nm in small_names:
            t = given[nm][which].reshape(1, -1)
            cols.append(_pad_lanes(t) if t.shape[1] < LANES else t)
        return jnp.concatenate(cols, axis=1)

    packed_out = _adamw(pack(0), tot[:, P_DMOD:P_CW][None], pack(1), pack(2), 1, "adamw_small")
    res = {}
    off = 0
    for nm in small_names:
        shape = given[nm][0].shape
        size = given[nm][0].size
        res[nm] = tuple(t[:, off:off + size].reshape(shape) for t in packed_out)
        off += max(size, LANES)
    n_cw = conv_w.shape[2]
    g_cw_mine = lax.dynamic_slice(tot[:, P_CW:].reshape(CONV_K, D_XBC), (0, my_slot * n_cw), (CONV_K, n_cw))
    res["conv_w"] = tuple(t.reshape(conv_w.shape) for t in _adamw(conv_w.reshape(1, -1), g_cw_mine.reshape(1, 1, -1),
                                                                  m_conv_w.reshape(1, -1), v_conv_w.reshape(1, -1), 1, "adamw_conv_w"))
    res["w_ada"] = big(w_ada, g_w_ada[None], m_w_ada, v_w_ada, 128, "adamw_w_ada")
    res["w_in"] = big(w_in, gw_in_parts, m_w_in, v_w_in, 128, "adamw_w_in")
    res["w_out"] = big(w_out, gw_out_parts, m_w_out, v_w_out, 64, "adamw_w_out")
    names = ["w_ada", "b_ada", "norm_in_gain", "w_in", "conv_w", "conv_b", "dt_bias", "a_log", "d_skip", "sb_norm_gain",
             "ssm_norm_gain", "w_out", "norm_f_gain"]
    loss = tot[0, P_LOSS]
    return (loss, grad_x[None], *[res[n][0] for n in names], *[res[n][1] for n in names],
            *[res[n][2] for n in names], *[res[n][3] for n in names])
```

```python
import functools

import jax
import jax.numpy as jnp
from jax import lax
from jax.experimental import pallas as pl
from jax.experimental.pallas import tpu as pltpu

f32 = jnp.float32
bf16 = jnp.bfloat16
MESH = pl.DeviceIdType.MESH
HI = lax.Precision.HIGHEST

N_DEV = 8
D_MODEL = 1024
D_ATTN = 1024
D_SSM = 1024
N_HEADS = 16
HEAD_DIM = 64
N_GROUPS = 2
HEADS_PER_GROUP = 8
N_STATE = 128
D_XBC = D_SSM + 2 * N_GROUPS * N_STATE
D_PROJ = 4 * D_ATTN + D_XBC + N_HEADS + D_SSM
W_IN_SHARD = D_PROJ // N_DEV
CONV_K = 4
CHUNK = 128
ATTN_Q_ROWS = 2048
ATTN_UNROLL = 8
LANES = 128
EPS = 1e-6
OFF_ZA = 3072
OFF_ZS = 4096
OFF_XBC = 5120
OFF_DT = 6656
D_PROJ_P = 6784
VMEM_LIMIT_BYTES = 56 * 1024 * 1024

ADAM_LR = 0.001
ADAM_B1 = 0.9
ADAM_B2 = 0.999
ADAM_EPS = 1e-08
ADAM_WD = 0.01
ADAM_STEP = 10

P_LOSS = 0
P_DMOD = 128
P_GIN = 3200
P_CB = 4224
P_DTB = 5760
P_ALOG = 5888
P_DSK = 6016
P_GSB = 6144
P_GSS = 7168
P_GNF = 8192
P_CW = 9216
N_PACK = 15360


def _params(sem=None):
    return pltpu.CompilerParams(dimension_semantics=sem, vmem_limit_bytes=VMEM_LIMIT_BYTES)


def _sigmoid(v):
    return 1.0 / (1.0 + jnp.exp(-v))


def _softplus(v):
    return jnp.maximum(v, 0.0) + jnp.log(1.0 + jnp.exp(-jnp.abs(v)))


def _nt(a, b, precision=None):
    return lax.dot_general(a, b, (((1,), (1,)), ((), ())), preferred_element_type=f32, precision=precision)


def _tn(a, b, precision=None):
    return lax.dot_general(a, b, (((0,), (0,)), ((), ())), preferred_element_type=f32, precision=precision)


def _nn(a, b, precision=None):
    return lax.dot_general(a, b, (((1,), (0,)), ((), ())), preferred_element_type=f32, precision=precision)


def _me():
    x, y, c = lax.axis_index("x"), lax.axis_index("y"), lax.axis_index("c")
    return (x, y, c), 4 * x + 2 * y + c


def _peer(k):
    x, y, c = lax.axis_index("x"), lax.axis_index("y"), lax.axis_index("c")
    px = 1 - x if (k >> 2) & 1 else x
    py = 1 - y if (k >> 1) & 1 else y
    pc = 1 - c if k & 1 else c
    return (px, py, pc), 4 * px + 2 * py + pc


def _all_gather(arrs, name):
    n = len(arrs)

    def body(*refs):
        ins, outs = refs[:n], refs[n:2 * n]
        send_sems, recv_sems, local_sems = refs[2 * n:]
        _, my_slot = _me()
        sends = []
        locals_ = []
        for a in range(n):
            loc = pltpu.make_async_copy(ins[a], outs[a].at[my_slot], local_sems.at[a])
            loc.start()
            locals_.append(loc)
            for k in range(1, N_DEV):
                peer, _ = _peer(k)
                cp = pltpu.make_async_remote_copy(src_ref=ins[a], dst_ref=outs[a].at[my_slot], send_sem=send_sems.at[a, k - 1],
                                                  recv_sem=recv_sems.at[a, k - 1], device_id=peer, device_id_type=MESH)
                cp.start()
                sends.append(cp)
        for a in range(n):
            for k in range(1, N_DEV):
                peer, peer_slot = _peer(k)
                pltpu.make_async_remote_copy(src_ref=ins[a], dst_ref=outs[a].at[peer_slot], send_sem=send_sems.at[a, k - 1],
                                             recv_sem=recv_sems.at[a, k - 1], device_id=peer, device_id_type=MESH).wait_recv()
        for cp in sends:
            cp.wait_send()
        for loc in locals_:
            loc.wait()

    any_spec = pl.BlockSpec(memory_space=pl.ANY)
    return pl.pallas_call(
        body, name=name,
        out_shape=tuple(jax.ShapeDtypeStruct((N_DEV,) + a.shape, a.dtype) for a in arrs),
        in_specs=[any_spec] * n, out_specs=tuple([any_spec] * n),
        scratch_shapes=[pltpu.SemaphoreType.DMA((n, N_DEV - 1)), pltpu.SemaphoreType.DMA((n, N_DEV - 1)),
                        pltpu.SemaphoreType.DMA((n,))],
    )(*arrs)


def _all_gather_two_level(arrs, name):
    n = len(arrs)

    def body(*refs):
        ins, outs = refs[:n], refs[n:2 * n]
        send_sems, recv_sems, local_sems = refs[2 * n:]
        x, y, c = lax.axis_index("x"), lax.axis_index("y"), lax.axis_index("c")
        me, sibling = (x, y, c), (x, y, 1 - c)
        chips = [(1 - x, y), (x, 1 - y), (1 - x, 1 - y)]

        def copy(a, k, block, to, from_input=False):
            slot = 4 * block[0] + 2 * block[1] + block[2]
            return pltpu.make_async_remote_copy(src_ref=ins[a] if from_input else outs[a].at[slot], dst_ref=outs[a].at[slot],
                                                send_sem=send_sems.at[a, k], recv_sem=recv_sems.at[a, k], device_id=to,
                                                device_id_type=MESH)

        started = []
        locals_ = []
        for a in range(n):
            loc = pltpu.make_async_copy(ins[a], outs[a].at[4 * x + 2 * y + c], local_sems.at[a])
            loc.start()
            locals_.append(loc)
            first = [copy(a, 0, me, sibling, True)] + [copy(a, 1 + j, me, (*chip, c), True) for j, chip in enumerate(chips)]
            for cp in first:
                cp.start()
            started += first
        for a in range(n):
            for j, chip in enumerate(chips):
                copy(a, 1 + j, (*chip, c), me).wait_recv()
                onward = copy(a, 4 + j, (*chip, c), sibling)
                onward.start()
                started.append(onward)
        for a in range(n):
            copy(a, 0, sibling, me).wait_recv()
            for j, chip in enumerate(chips):
                copy(a, 4 + j, (*chip, 1 - c), me).wait_recv()
        for cp in started:
            cp.wait_send()
        for loc in locals_:
            loc.wait()

    any_spec = pl.BlockSpec(memory_space=pl.ANY)
    return pl.pallas_call(
        body, name=name,
        out_shape=tuple(jax.ShapeDtypeStruct((N_DEV,) + a.shape, a.dtype) for a in arrs),
        in_specs=[any_spec] * n, out_specs=tuple([any_spec] * n),
        scratch_shapes=[pltpu.SemaphoreType.DMA((n, N_DEV - 1)), pltpu.SemaphoreType.DMA((n, N_DEV - 1)),
                        pltpu.SemaphoreType.DMA((n,))],
    )(*arrs)


def _all_to_all(arrs, name):
    n = len(arrs)

    def body(*refs):
        ins, outs = refs[:n], refs[n:2 * n]
        send_sems, recv_sems, local_sems = refs[2 * n:]
        _, my_slot = _me()
        sends = []
        locals_ = []
        for a in range(n):
            loc = pltpu.make_async_copy(ins[a].at[my_slot], outs[a].at[my_slot], local_sems.at[a])
            loc.start()
            locals_.append(loc)
            for k in range(1, N_DEV):
                peer, peer_slot = _peer(k)
                cp = pltpu.make_async_remote_copy(src_ref=ins[a].at[peer_slot], dst_ref=outs[a].at[my_slot],
                                                  send_sem=send_sems.at[a, k - 1], recv_sem=recv_sems.at[a, k - 1],
                                                  device_id=peer, device_id_type=MESH)
                cp.start()
                sends.append(cp)
        for a in range(n):
            for k in range(1, N_DEV):
                peer, peer_slot = _peer(k)
                pltpu.make_async_remote_copy(src_ref=ins[a].at[peer_slot], dst_ref=outs[a].at[peer_slot],
                                             send_sem=send_sems.at[a, k - 1], recv_sem=recv_sems.at[a, k - 1],
                                             device_id=peer, device_id_type=MESH).wait_recv()
        for cp in sends:
            cp.wait_send()
        for loc in locals_:
            loc.wait()

    any_spec = pl.BlockSpec(memory_space=pl.ANY)
    return pl.pallas_call(
        body, name=name,
        out_shape=tuple(jax.ShapeDtypeStruct(a.shape, a.dtype) for a in arrs),
        in_specs=[any_spec] * n, out_specs=tuple([any_spec] * n),
        scratch_shapes=[pltpu.SemaphoreType.DMA((n, N_DEV - 1)), pltpu.SemaphoreType.DMA((n, N_DEV - 1)),
                        pltpu.SemaphoreType.DMA((n,))],
    )(*arrs)


def _reduce_scatter_two_level(arrs, name):
    n = len(arrs)
    n_chip = N_DEV // 2

    def body(*refs):
        ins, outs = refs[:n], refs[n:2 * n]
        mine_bufs, sib_bufs = refs[2 * n:3 * n], refs[3 * n:4 * n]
        send_sems, recv_sems, local_sems = refs[4 * n:]
        x, y, c = lax.axis_index("x"), lax.axis_index("y"), lax.axis_index("c")
        sibling = (x, y, 1 - c)

        def chip(r):
            return (1 - x if r & 2 else x), (1 - y if r & 1 else y)

        def slot(r, core):
            cx, cy = chip(r)
            return 4 * cx + 2 * cy + core

        def to_sibling(a, r):
            return pltpu.make_async_remote_copy(src_ref=ins[a].at[slot(r, 1 - c)], dst_ref=sib_bufs[a].at[r], send_sem=send_sems.at[a, r],
                                                recv_sem=recv_sems.at[a, r], device_id=sibling, device_id_type=MESH)

        def to_chip(a, r):
            return pltpu.make_async_remote_copy(src_ref=mine_bufs[a].at[r], dst_ref=outs[a].at[r], send_sem=send_sems.at[a, n_chip - 1 + r],
                                                recv_sem=recv_sems.at[a, n_chip - 1 + r], device_id=(*chip(r), c), device_id_type=MESH)

        def load_mine(a, r):
            return pltpu.make_async_copy(ins[a].at[slot(r, c)], mine_bufs[a].at[r], local_sems.at[a, r])

        started = []
        order = [1, 2, 3, 0]
        for a in range(n):
            for r in order:
                load_mine(a, r).start()
                cp = to_sibling(a, r)
                cp.start()
                started.append(cp)
        for a in range(n):
            for r in order:
                load_mine(a, r).wait()
                to_sibling(a, r).wait_recv()
                total = (mine_bufs[a][r].astype(f32) + sib_bufs[a][r].astype(f32)).astype(bf16)
                if r:
                    mine_bufs[a][r] = total
                    cp = to_chip(a, r)
                    cp.start()
                    started.append(cp)
                else:
                    outs[a][0] = total
        for a in range(n):
            for r in range(1, n_chip):
                to_chip(a, r).wait_recv()
        for cp in started:
            cp.wait_send()

    any_spec = pl.BlockSpec(memory_space=pl.ANY)
    vmem = pl.BlockSpec(memory_space=pltpu.VMEM)
    part = lambda a: (n_chip,) + a.shape[1:]
    return pl.pallas_call(
        body, name=name,
        out_shape=tuple(jax.ShapeDtypeStruct(part(a), a.dtype) for a in arrs),
        in_specs=[any_spec] * n, out_specs=tuple([vmem] * n),
        scratch_shapes=([pltpu.VMEM(part(a), a.dtype) for a in arrs] + [pltpu.VMEM(part(a), a.dtype) for a in arrs]
                        + [pltpu.SemaphoreType.DMA((n, N_DEV - 1)), pltpu.SemaphoreType.DMA((n, N_DEV - 1)),
                           pltpu.SemaphoreType.DMA((n, n_chip))]),
        compiler_params=_params(),
    )(*arrs)


def _mod_exchange(c_row, w_ada, b_ada):
    n_col = w_ada.shape[1]

    def body(c_ref, w_ref, b_ref, mod_ref, call_ref, part, modp, send_sems, recv_sems):
        _, my_slot = _me()
        call_ref[my_slot] = c_ref[...]
        sends = []
        for k in range(1, N_DEV):
            peer, _ = _peer(k)
            cp = pltpu.make_async_remote_copy(src_ref=c_ref, dst_ref=call_ref.at[my_slot], send_sem=send_sems.at[0, k - 1],
                                              recv_sem=recv_sems.at[0, k - 1], device_id=peer, device_id_type=MESH)
            cp.start()
            sends.append(cp)
        for k in range(1, N_DEV):
            peer, peer_slot = _peer(k)
            pltpu.make_async_remote_copy(src_ref=c_ref, dst_ref=call_ref.at[peer_slot], send_sem=send_sems.at[0, k - 1],
                                         recv_sem=recv_sems.at[0, k - 1], device_id=peer, device_id_type=MESH).wait_recv()
        for cp in sends:
            cp.wait_send()
        w = w_ref[...]
        for b in range(N_DEV):
            cb = call_ref[b]
            part[b] = _nn(cb * _sigmoid(cb), w, HI)
        modp[my_slot] = part[my_slot]
        sends = []
        for k in range(1, N_DEV):
            peer, peer_slot = _peer(k)
            cp = pltpu.make_async_remote_copy(src_ref=part.at[peer_slot], dst_ref=modp.at[my_slot], send_sem=send_sems.at[1, k - 1],
                                              recv_sem=recv_sems.at[1, k - 1], device_id=peer, device_id_type=MESH)
            cp.start()
            sends.append(cp)
        for k in range(1, N_DEV):
            peer, peer_slot = _peer(k)
            pltpu.make_async_remote_copy(src_ref=part.at[peer_slot], dst_ref=modp.at[peer_slot], send_sem=send_sems.at[1, k - 1],
                                         recv_sem=recv_sems.at[1, k - 1], device_id=peer, device_id_type=MESH).wait_recv()
        for cp in sends:
            cp.wait_send()
        for j in range(N_DEV):
            mod_ref[:, j * n_col:(j + 1) * n_col] = modp[j] + b_ref[:, j * n_col:(j + 1) * n_col]

    vmem = pl.BlockSpec(memory_space=pltpu.VMEM)
    return pl.pallas_call(
        body, name="mod_exchange",
        out_shape=(jax.ShapeDtypeStruct((1, N_DEV * n_col), f32), jax.ShapeDtypeStruct((N_DEV, 1, D_MODEL), f32)),
        in_specs=[vmem, vmem, vmem], out_specs=(vmem, vmem),
        scratch_shapes=[pltpu.VMEM((N_DEV, 1, n_col), f32), pltpu.VMEM((N_DEV, 1, n_col), f32),
                        pltpu.SemaphoreType.DMA((2, N_DEV - 1)), pltpu.SemaphoreType.DMA((2, N_DEV - 1))],
        compiler_params=_params(),
    )(c_row, w_ada, b_ada)


def _cast_bf16(a, rows):
    r, c = a.shape

    def body(a_ref, o_ref):
        o_ref[...] = a_ref[...].astype(bf16)

    return pl.pallas_call(
        body, name="cast_bf16", grid=(r // rows,),
        in_specs=[pl.BlockSpec((rows, c), lambda i: (i, 0))], out_specs=pl.BlockSpec((rows, c), lambda i: (i, 0)),
        out_shape=jax.ShapeDtypeStruct((r, c), bf16), compiler_params=_params(("parallel",)),
    )(a)


def _store_transposed(out_ref, v):
    blk = 256
    eye = (lax.broadcasted_iota(jnp.int32, (blk, blk), 0) == lax.broadcasted_iota(jnp.int32, (blk, blk), 1)).astype(bf16)
    for cb in range(0, v.shape[1], blk):
        out_ref[cb:cb + blk, :] = _nt(eye, v[:, cb:cb + blk]).astype(bf16)


def _proj(x, gain, scale, shift, wp, seq):
    ts = 256

    def body(x_ref, g_ref, sc_ref, sh_ref, w_ref, ht_ref, qkv_ref, za_ref, zs_ref, xbc_ref, dt_ref):
        xv = x_ref[...]
        r = lax.rsqrt(jnp.mean(xv * xv, axis=-1, keepdims=True) + EPS)
        hb = ((xv * r * g_ref[...]) * (1.0 + sc_ref[...]) + sh_ref[...]).astype(bf16)
        _store_transposed(ht_ref, hb)
        for cb in range(3 * D_ATTN // 256):
            res = jnp.dot(hb, w_ref[:, cb * 256:(cb + 1) * 256], preferred_element_type=f32)
            for u in range(4):
                qkv_ref[cb * 4 + u] = res[:, u * HEAD_DIM:(u + 1) * HEAD_DIM].astype(bf16)
        for out_ref, off, width in ((za_ref, OFF_ZA, D_ATTN), (zs_ref, OFF_ZS, D_SSM), (xbc_ref, OFF_XBC, D_XBC), (dt_ref, OFF_DT, LANES)):
            for cc in range(0, width, 512):
                wd = min(512, width - cc)
                out_ref[:, cc:cc + wd] = jnp.dot(hb, w_ref[:, off + cc:off + cc + wd], preferred_element_type=f32)

    row = lambda i: (i, 0)
    fixed = lambda i: (0, 0)
    return pl.pallas_call(
        body, name="proj", grid=(seq // ts,),
        in_specs=[pl.BlockSpec((ts, D_MODEL), row), pl.BlockSpec((1, D_MODEL), fixed), pl.BlockSpec((1, D_MODEL), fixed),
                  pl.BlockSpec((1, D_MODEL), fixed), pl.BlockSpec((D_MODEL, D_PROJ_P), fixed)],
        out_specs=(pl.BlockSpec((D_MODEL, ts), lambda i: (0, i)), pl.BlockSpec((3 * N_HEADS, ts, HEAD_DIM), lambda i: (0, i, 0)),
                   pl.BlockSpec((ts, D_ATTN), row), pl.BlockSpec((ts, D_SSM), row), pl.BlockSpec((ts, D_XBC), row),
                   pl.BlockSpec((ts, LANES), row)),
        out_shape=(jax.ShapeDtypeStruct((D_MODEL, seq), bf16), jax.ShapeDtypeStruct((3 * N_HEADS, seq, HEAD_DIM), bf16),
                   jax.ShapeDtypeStruct((seq, D_ATTN), f32), jax.ShapeDtypeStruct((seq, D_SSM), f32),
                   jax.ShapeDtypeStruct((seq, D_XBC), f32), jax.ShapeDtypeStruct((seq, LANES), f32)),
        compiler_params=_params(("arbitrary",)),
    )(x, gain, scale, shift, wp)


def _log_sigmoids(z):
    lb = jnp.minimum(z, 0.0) - jnp.log(1.0 + jnp.exp(-jnp.abs(z)))
    return lb, lb - z


def _split_bf16(v):
    hi = v.astype(bf16)
    return hi, (v - hi.astype(f32)).astype(bf16)


def _spread(v, sel):
    hi = v.astype(bf16)
    mid, lo = _split_bf16(v - hi.astype(f32))
    sel_b = sel.astype(bf16)
    return (jnp.dot(hi, sel_b, preferred_element_type=f32) + jnp.dot(mid, sel_b, preferred_element_type=f32)
            + jnp.dot(lo, sel_b, preferred_element_type=f32))


def _attn_fwd(qkv, seq):
    t = CHUNK
    tq = min(ATTN_Q_ROWS, seq)
    nd = tq // t
    unroll = min(ATTN_UNROLL, nd)
    assert nd % unroll == 0

    def body(q_ref, k_ref, v_ref, o_ref):
        i = pl.program_id(1)
        q = q_ref[0] * 0.125
        ur = lax.broadcasted_iota(jnp.int32, (2 * t, t), 0)
        upper = ((ur & (t - 1)) > lax.broadcasted_iota(jnp.int32, (2 * t, t), 1)).astype(bf16)

        def tile(j, q_s, acc, run, masked):
            n = q_s.shape[0]
            start = pl.multiple_of(j * t, t)
            k = k_ref[0, pl.ds(start, t), :]
            v = v_ref[0, pl.ds(start, t), :]
            z = _nt(q_s, k)
            lb, lom = _log_sigmoids(z)
            if masked:
                keep = lax.broadcasted_iota(jnp.int32, (n, t), 1) < lax.broadcasted_iota(jnp.int32, (n, t), 0)
                lom = jnp.where(keep, lom, 0.0)
            tail = jnp.dot(jnp.concatenate(_split_bf16(lom), axis=1), upper, preferred_element_type=f32)
            a = lb + tail + run
            if masked:
                a = jnp.where(keep, a, -jnp.inf)
            w = jnp.exp(a)
            acc = acc + jnp.dot(w.astype(bf16), v, preferred_element_type=f32)
            run = run + tail[:, 0:1] + lom[:, 0:1]
            return acc, run

        acc, run = jnp.zeros((tq, HEAD_DIM), f32), jnp.zeros((tq, 1), f32)
        for jj in reversed(range(nd)):
            r0 = jj * t
            acc_s, run_s = tile(i * nd + jj, q[r0:], acc[r0:], run[r0:], True)
            acc = acc_s if r0 == 0 else jnp.concatenate([acc[:r0], acc_s], axis=0)
            run = run_s if r0 == 0 else jnp.concatenate([run[:r0], run_s], axis=0)
        def group(n, cr):
            for u in range(unroll):
                cr = tile(i * nd - 1 - unroll * n - u, q, cr[0], cr[1], False)
            return cr

        acc, run = lax.fori_loop(0, i * (nd // unroll), group, (acc, run))
        o_ref[0] = jnp.concatenate([acc, jnp.broadcast_to(run, (tq, HEAD_DIM))], axis=1)

    return pl.pallas_call(
        body, name="attn_fwd", grid=(N_HEADS, seq // tq),
        in_specs=[pl.BlockSpec((1, tq, HEAD_DIM), lambda h, i: (h, i, 0)),
                  pl.BlockSpec((1, seq, HEAD_DIM), lambda h, i: (N_HEADS + h, 0, 0)),
                  pl.BlockSpec((1, seq, HEAD_DIM), lambda h, i: (2 * N_HEADS + h, 0, 0))],
        out_specs=pl.BlockSpec((1, tq, 2 * HEAD_DIM), lambda h, i: (h, i, 0)),
        out_shape=jax.ShapeDtypeStruct((N_HEADS, seq, 2 * HEAD_DIM), f32),
        compiler_params=_params(("parallel", "arbitrary")),
    )(qkv, qkv, qkv)


def _attn_bwd(qkv, o_tot, d_o, seq):
    t = CHUNK
    tq = min(ATTN_Q_ROWS, seq)
    nd = tq // t
    unroll = min(ATTN_UNROLL, nd)
    assert nd % unroll == 0
    nk = seq // t

    def body(q_ref, k_ref, v_ref, ot_ref, do_ref, dq_ref, dkt_ref, dvt_ref):
        i = pl.program_id(1)

        @pl.when(i == 0)
        def _():
            dkt_ref[...] = jnp.zeros_like(dkt_ref)
            dvt_ref[...] = jnp.zeros_like(dvt_ref)

        q = q_ref[0] * 0.125
        d_out = do_ref[0]
        total = ot_ref[0][:, HEAD_DIM:HEAD_DIM + 1]
        eye = (lax.broadcasted_iota(jnp.int32, (HEAD_DIM, HEAD_DIM), 0)
               == lax.broadcasted_iota(jnp.int32, (HEAD_DIM, HEAD_DIM), 1)).astype(bf16)
        q_t = _nt(eye, q).astype(bf16)
        do_t = _nt(eye, d_out).astype(bf16)
        ur = lax.broadcasted_iota(jnp.int32, (t, t), 0)
        uc = lax.broadcasted_iota(jnp.int32, (t, t), 1)
        ur2 = lax.broadcasted_iota(jnp.int32, (2 * t, t), 0) & (t - 1)
        incl = (ur2 <= lax.broadcasted_iota(jnp.int32, (2 * t, t), 1)).astype(bf16)
        before = (ur < uc).astype(bf16)

        def tile(j, r0, r1, dq, pre, dpre, masked):
            q_s, do_s, tot_s = q[r0:r1], d_out[r0:r1], total[r0:r1]
            n = q_s.shape[0]
            start = pl.multiple_of(j * t, t)
            k = k_ref[0, pl.ds(start, t), :]
            v = v_ref[0, pl.ds(start, t), :]
            z = _nt(q_s, k)
            lb, lom = _log_sigmoids(z)
            if masked:
                keep = lax.broadcasted_iota(jnp.int32, (n, t), 1) < lax.broadcasted_iota(jnp.int32, (n, t), 0)
                lom = jnp.where(keep, lom, 0.0)
            pin = jnp.dot(jnp.concatenate(_split_bf16(lom), axis=1), incl, preferred_element_type=f32)
            a = lb + ((tot_s - pre) - pin)
            if masked:
                a = jnp.where(keep, a, -jnp.inf)
            w = jnp.exp(a)
            d_a = _nt(do_s, v) * w
            d_lom_local = jnp.dot(d_a.astype(bf16), before, preferred_element_type=f32)
            d_lom = d_lom_local + dpre
            sig = jnp.exp(lb)
            dz = d_a * (1.0 - sig) - d_lom * sig
            if masked:
                dz = jnp.where(keep, dz, 0.0)
            dzb = dz.astype(bf16)
            dq = dq + jnp.dot(dzb, k, preferred_element_type=f32)
            dkt_ref[0, j] += jnp.dot(q_t[:, r0:r1], dzb, preferred_element_type=f32)
            dvt_ref[0, j] += jnp.dot(do_t[:, r0:r1], w.astype(bf16), preferred_element_type=f32)
            pre = pre + pin[:, t - 1:t]
            dpre = dpre + d_lom_local[:, t - 1:t] + d_a[:, t - 1:t]
            return dq, pre, dpre

        carry = (jnp.zeros((tq, HEAD_DIM), f32), jnp.zeros((tq, 1), f32), jnp.zeros((tq, 1), f32))
        def group(n, cr):
            for u in range(unroll):
                cr = tile(unroll * n + u, 0, tq, cr[0], cr[1], cr[2], False)
            return cr

        carry = lax.fori_loop(0, i * (nd // unroll), group, carry)
        for jj in range(nd):
            r0 = jj * t
            part = tile(i * nd + jj, r0, tq, *(c[r0:] for c in carry), True)
            carry = part if r0 == 0 else tuple(jnp.concatenate([c[:r0], p], axis=0) for c, p in zip(carry, part))
        dq_ref[0] = carry[0] * 0.125

    blk = pl.BlockSpec((1, tq, HEAD_DIM), lambda h, i: (h, i, 0))
    full_t = pl.BlockSpec((1, nk, HEAD_DIM, t), lambda h, i: (h, 0, 0, 0))
    return pl.pallas_call(
        body, name="attn_bwd", grid=(N_HEADS, seq // tq),
        in_specs=[blk, pl.BlockSpec((1, seq, HEAD_DIM), lambda h, i: (N_HEADS + h, 0, 0)),
                  pl.BlockSpec((1, seq, HEAD_DIM), lambda h, i: (2 * N_HEADS + h, 0, 0)),
                  pl.BlockSpec((1, tq, 2 * HEAD_DIM), lambda h, i: (h, i, 0)), blk],
        out_specs=(blk, full_t, full_t),
        out_shape=(jax.ShapeDtypeStruct((N_HEADS, seq, HEAD_DIM), f32),
                   jax.ShapeDtypeStruct((N_HEADS, nk, HEAD_DIM, t), f32), jax.ShapeDtypeStruct((N_HEADS, nk, HEAD_DIM, t), f32)),
        compiler_params=_params(("parallel", "arbitrary")),
    )(qkv, qkv, qkv, o_tot, d_o)


def _ssd_common(conv, dt_raw, dtb, alog):
    t = CHUNK
    sg = _sigmoid(conv)
    act = conv * sg
    dt_pre = dt_raw + dtb
    dt = _softplus(dt_pre)
    a = -jnp.exp(alog)
    row = lax.broadcasted_iota(jnp.int32, (t, t), 0)
    col = lax.broadcasted_iota(jnp.int32, (t, t), 1)
    causal = row >= col
    ac = _nn(causal.astype(f32), dt * a, HI)
    ac_t = _nt((row == col).astype(f32), ac, HI)
    ac_last = ac[t - 1:t, :]
    return sg, act, dt_pre, dt, a, causal, ac, ac_t, ac_last, jnp.exp(ac), jnp.exp(ac_last - ac), jnp.exp(ac_last)


def _ssd_fwd(xbc, dt_raw, conv_w, conv_b, dtb, alog, dsk, seq):
    t = CHUNK
    n_chunks = seq // t

    def body(x_ref, dt_ref, cw_ref, cb_ref, dtb_ref, al_ref, dsk_ref, conv_ref, y_ref, st_ref, prev, state):
        c = pl.program_id(0)

        @pl.when(c == 0)
        def _():
            prev[...] = jnp.zeros_like(prev)
            state[...] = jnp.zeros_like(state)

        cur = x_ref[...]
        pv = prev[...]
        rows = lax.broadcasted_iota(jnp.int32, (t, D_XBC), 0)
        conv = cur * cw_ref[CONV_K - 1:CONV_K, :] + cb_ref[...]
        for m in range(1, CONV_K):
            shifted = jnp.where(rows < m, pltpu.roll(pv, m, 0), pltpu.roll(cur, m, 0))
            conv = conv + shifted * cw_ref[CONV_K - 1 - m:CONV_K - m, :]
        prev[...] = cur
        conv_ref[...] = conv
        _, act, _, dt, _, _, ac, ac_t, _, e_ac, dte, cdec = _ssd_common(conv, dt_ref[...], dtb_ref[...], al_ref[...])
        xs = act[:, :D_SSM]
        wide = N_HEADS * t
        sel64 = (lax.broadcasted_iota(jnp.int32, (LANES, D_SSM), 1) // HEAD_DIM
                 == lax.broadcasted_iota(jnp.int32, (LANES, D_SSM), 0)).astype(f32)
        sel128 = (lax.broadcasted_iota(jnp.int32, (LANES, wide), 1) // t
                  == lax.broadcasted_iota(jnp.int32, (LANES, wide), 0)).astype(f32)
        xd_all = xs * _spread(dt, sel64)
        xdb = xd_all.astype(bf16)
        xdte_b = (xd_all * _spread(dte, sel64)).astype(bf16)
        seg_all = _spread(ac, sel128) - jnp.concatenate([jnp.broadcast_to(ac_t[h:h + 1, :], (t, t)) for h in range(N_HEADS)], axis=1)
        causal_all = (lax.broadcasted_iota(jnp.int32, (t, wide), 0) >= (lax.broadcasted_iota(jnp.int32, (t, wide), 1) & (t - 1)))
        lm_all = jnp.exp(jnp.where(causal_all, seg_all, -jnp.inf))
        bgs = [act[:, D_SSM + g * N_STATE:D_SSM + (g + 1) * N_STATE].astype(bf16) for g in range(N_GROUPS)]
        cgs = [act[:, D_SSM + (N_GROUPS + g) * N_STATE:D_SSM + (N_GROUPS + g + 1) * N_STATE].astype(bf16) for g in range(N_GROUPS)]
        gms = [_nt(cgs[g], bgs[g]) for g in range(N_GROUPS)]
        mm_b = (jnp.concatenate([gms[h // HEADS_PER_GROUP] for h in range(N_HEADS)], axis=1) * lm_all).astype(bf16)
        y_diags, zos = [], []
        for h in range(N_HEADS):
            g = h // HEADS_PER_GROUP
            hs = slice(h * HEAD_DIM, (h + 1) * HEAD_DIM)
            hp = state[h]
            st_ref[0, h] = hp
            y_diags.append(jnp.dot(mm_b[:, h * t:(h + 1) * t], xdb[:, hs], preferred_element_type=f32))
            zos.append(_nt(cgs[g], hp.astype(bf16)))
            state[h] = hp * cdec[:, h:h + 1] + _tn(xdte_b[:, hs], bgs[g])
        y_ref[...] = (jnp.concatenate(y_diags, axis=1) + jnp.concatenate(zos, axis=1) * _spread(e_ac, sel64)
                      + xs * _spread(dsk_ref[...], sel64))

    row = lambda c: (c, 0)
    fixed = lambda c: (0, 0)
    return pl.pallas_call(
        body, name="ssd_fwd", grid=(n_chunks,),
        in_specs=[pl.BlockSpec((t, D_XBC), row), pl.BlockSpec((t, LANES), row), pl.BlockSpec((CONV_K, D_XBC), fixed),
                  pl.BlockSpec((1, D_XBC), fixed), pl.BlockSpec((1, LANES), fixed), pl.BlockSpec((1, LANES), fixed),
                  pl.BlockSpec((1, LANES), fixed)],
        out_specs=(pl.BlockSpec((t, D_XBC), row), pl.BlockSpec((t, D_SSM), row),
                   pl.BlockSpec((1, N_HEADS, HEAD_DIM, N_STATE), lambda c: (c, 0, 0, 0))),
        out_shape=(jax.ShapeDtypeStruct((seq, D_XBC), f32), jax.ShapeDtypeStruct((seq, D_SSM), f32),
                   jax.ShapeDtypeStruct((n_chunks, N_HEADS, HEAD_DIM, N_STATE), f32)),
        scratch_shapes=[pltpu.VMEM((t, D_XBC), f32), pltpu.VMEM((N_HEADS, HEAD_DIM, N_STATE), f32)],
        compiler_params=_params(("arbitrary",)),
    )(xbc, dt_raw, conv_w, conv_b, dtb, alog, dsk)


def _ssd_bwd(dy, conv, xbc, dt_raw, states, conv_w, dtb, alog, dsk, seq):
    t = CHUNK
    n_chunks = seq // t

    def body(dy_ref, conv_ref, x_ref, dt_ref, st_ref, cw_ref, dtb_ref, al_ref, dsk_ref,
             dx_ref, ddt_ref, gcw_ref, gcb_ref, gdtb_ref, gal_ref, gdsk_ref, d_state, d_conv_next):
        c = pl.program_id(0)

        @pl.when(c == 0)
        def _():
            d_state[...] = jnp.zeros_like(d_state)
            d_conv_next[...] = jnp.zeros_like(d_conv_next)
            gcw_ref[...] = jnp.zeros_like(gcw_ref)
            gcb_ref[...] = jnp.zeros_like(gcb_ref)
            gdtb_ref[...] = jnp.zeros_like(gdtb_ref)
            gal_ref[...] = jnp.zeros_like(gal_ref)
            gdsk_ref[...] = jnp.zeros_like(gdsk_ref)

        conv = conv_ref[...]
        sg, act, dt_pre, dt, a, causal, ac, ac_t, _, e_ac, dte, cdec = _ssd_common(conv, dt_ref[...], dtb_ref[...], al_ref[...])
        dyv = dy_ref[...]
        xs = act[:, :D_SSM]
        sel64 = lax.broadcasted_iota(jnp.int32, (LANES, D_SSM), 1) // HEAD_DIM == lax.broadcasted_iota(jnp.int32, (LANES, D_SSM), 0)
        sel64_t = lax.broadcasted_iota(jnp.int32, (D_SSM, LANES), 0) // HEAD_DIM == lax.broadcasted_iota(jnp.int32, (D_SSM, LANES), 1)
        wide = N_HEADS * t
        sel128 = lax.broadcasted_iota(jnp.int32, (LANES, wide), 1) // t == lax.broadcasted_iota(jnp.int32, (LANES, wide), 0)
        sel128_t = lax.broadcasted_iota(jnp.int32, (wide, LANES), 0) // t == lax.broadcasted_iota(jnp.int32, (wide, LANES), 1)

        spread = _spread

        def lane_sums(v, sel_t):
            hi, lo = _split_bf16(v)
            sel_b = sel_t.astype(bf16)
            return jnp.dot(hi, sel_b, preferred_element_type=f32) + jnp.dot(lo, sel_b, preferred_element_type=f32)

        dt_x = spread(dt, sel64)
        e_x = spread(e_ac, sel64)
        dte_x = spread(dte, sel64)
        seg_all = spread(ac, sel128) - jnp.concatenate([jnp.broadcast_to(ac_t[h:h + 1, :], (t, t)) for h in range(N_HEADS)], axis=1)
        causal_all = (lax.broadcasted_iota(jnp.int32, (t, wide), 0) >= (lax.broadcasted_iota(jnp.int32, (t, wide), 1) & (t - 1)))
        lm_all = jnp.exp(jnp.where(causal_all, seg_all, -jnp.inf))
        xd_all = xs * dt_x
        xdb = xd_all.astype(bf16)
        xdte_b = (xd_all * dte_x).astype(bf16)
        d_yb = dyv.astype(bf16)
        d_zo_all = dyv * e_x
        d_zob = d_zo_all.astype(bf16)
        bgs = [act[:, D_SSM + g * N_STATE:D_SSM + (g + 1) * N_STATE].astype(bf16) for g in range(N_GROUPS)]
        cgs = [act[:, D_SSM + (N_GROUPS + g) * N_STATE:D_SSM + (N_GROUPS + g + 1) * N_STATE].astype(bf16) for g in range(N_GROUPS)]
        gms = [_nt(cgs[g], bgs[g]) for g in range(N_GROUPS)]
        mm_all = jnp.concatenate([gms[h // HEADS_PER_GROUP] for h in range(N_HEADS)], axis=1) * lm_all
        mm_b = mm_all.astype(bf16)
        d_hn_all = d_state[...].reshape(N_HEADS * HEAD_DIM, N_STATE)
        hp_all = st_ref[0].reshape(N_HEADS * HEAD_DIM, N_STATE)
        state_dot = lane_sums(d_hn_all * hp_all, jnp.ones((N_STATE, LANES), jnp.bool_))
        state_dot = jnp.sum(jnp.where(sel64_t, state_dot, 0.0), axis=0, keepdims=True)
        d_mms, d_xds, zos, d_ws = [], [], [], []
        d_bs = [jnp.zeros((t, N_STATE), f32) for _ in range(N_GROUPS)]
        d_cs = [jnp.zeros((t, N_STATE), f32) for _ in range(N_GROUPS)]
        for h in range(N_HEADS):
            g = h // HEADS_PER_GROUP
            hs = slice(h * HEAD_DIM, (h + 1) * HEAD_DIM)
            hpb = st_ref[0, h].astype(bf16)
            d_hn = d_state[h]
            d_hnb = d_hn.astype(bf16)
            d_mms.append(_nt(d_yb[:, hs], xdb[:, hs]))
            d_xds.append(_tn(mm_b[:, h * t:(h + 1) * t], d_yb[:, hs]))
            zos.append(_nt(cgs[g], hpb))
            d_cs[g] = d_cs[g] + jnp.dot(d_zob[:, hs], hpb, preferred_element_type=f32)
            d_state[h] = _tn(d_zob[:, hs], cgs[g]) + d_hn * cdec[:, h:h + 1]
            d_ws.append(_nt(bgs[g], d_hnb))
            d_bs[g] = d_bs[g] + jnp.dot(xdte_b[:, hs], d_hnb, preferred_element_type=f32)
        d_mm_all = jnp.concatenate(d_mms, axis=1)
        d_seg_all = d_mm_all * mm_all
        d_gm_all = d_mm_all * lm_all
        dwd_all = jnp.concatenate(d_ws, axis=1) * dte_x
        d_xd_all = jnp.concatenate(d_xds, axis=1) + dwd_all
        decay_part = dwd_all * xd_all
        decay_sums = lane_sums(decay_part, sel64_t)
        d_ac = lane_sums(d_seg_all, sel128_t) + lane_sums(d_zo_all * jnp.concatenate(zos, axis=1) - decay_part, sel64_t)
        d_dt = lane_sums(d_xd_all * xs, sel64_t)
        g_dsk = jnp.sum(lane_sums(dyv * xs, sel64_t), axis=0, keepdims=True)
        stacked = jnp.concatenate([d_seg_all[:, h * t:(h + 1) * t] for h in range(N_HEADS)], axis=0)
        s_hi, s_lo = _split_bf16(stacked)
        sel_b = sel128.astype(bf16)
        col_sums = jnp.dot(sel_b, s_hi, preferred_element_type=f32) + jnp.dot(sel_b, s_lo, preferred_element_type=f32)
        d_last = state_dot * cdec + jnp.sum(decay_sums, axis=0, keepdims=True)
        last_row = (lax.broadcasted_iota(jnp.int32, (t, 1), 0) == t - 1).astype(f32)
        sq_row = lax.broadcasted_iota(jnp.int32, (t, t), 0)
        sq_col = lax.broadcasted_iota(jnp.int32, (t, t), 1)
        d_ac = d_ac - _nt((sq_row == sq_col).astype(f32), col_sums, HI) + last_row * d_last
        d_ld = _nn((sq_col >= sq_row).astype(f32), d_ac, HI)
        dxs = [dyv * spread(dsk_ref[...], sel64) + d_xd_all * dt_x]
        dbs, dcs = [], []
        for g in range(N_GROUPS):
            d_gm = d_gm_all[:, g * HEADS_PER_GROUP * t:g * HEADS_PER_GROUP * t + t]
            for r in range(1, HEADS_PER_GROUP):
                d_gm = d_gm + d_gm_all[:, (g * HEADS_PER_GROUP + r) * t:(g * HEADS_PER_GROUP + r + 1) * t]
            d_gmb = d_gm.astype(bf16)
            dcs.append(d_cs[g] + jnp.dot(d_gmb, bgs[g], preferred_element_type=f32))
            dbs.append(d_bs[g] + _tn(d_gmb, cgs[g]))
        d_dt = d_dt + d_ld * a
        gal_ref[...] += jnp.sum(d_ld * dt, axis=0, keepdims=True) * a
        gdsk_ref[...] += g_dsk
        d_dt_raw = d_dt * _sigmoid(dt_pre)
        ddt_ref[...] = d_dt_raw.astype(bf16)
        gdtb_ref[...] += jnp.sum(d_dt_raw, axis=0, keepdims=True)
        d_act = jnp.concatenate(dxs + dbs + dcs, axis=1)
        d_conv = d_act * (sg * (1.0 + conv * (1.0 - sg)))
        gcb_ref[...] += jnp.sum(d_conv, axis=0, keepdims=True)
        nxt = d_conv_next[...]
        rows = lax.broadcasted_iota(jnp.int32, (t, D_XBC), 0)
        xraw = x_ref[...]
        d_x = d_conv * cw_ref[CONV_K - 1:CONV_K, :]
        gcw_ref[pl.ds(CONV_K - 1, 1), :] += jnp.sum(xraw * d_conv, axis=0, keepdims=True)
        for m in range(1, CONV_K):
            ahead = jnp.where(rows >= t - m, pltpu.roll(nxt, t - m, 0), pltpu.roll(d_conv, t - m, 0))
            d_x = d_x + ahead * cw_ref[CONV_K - 1 - m:CONV_K - m, :]
            gcw_ref[pl.ds(CONV_K - 1 - m, 1), :] += jnp.sum(xraw * ahead, axis=0, keepdims=True)
        d_conv_next[...] = d_conv
        dx_ref[...] = d_x.astype(bf16)

    rev = lambda c: (n_chunks - 1 - c, 0)
    fixed = lambda c: (0, 0)
    return pl.pallas_call(
        body, name="ssd_bwd", grid=(n_chunks,),
        in_specs=[pl.BlockSpec((t, D_SSM), rev), pl.BlockSpec((t, D_XBC), rev), pl.BlockSpec((t, D_XBC), rev),
                  pl.BlockSpec((t, LANES), rev), pl.BlockSpec((1, N_HEADS, HEAD_DIM, N_STATE), lambda c: (n_chunks - 1 - c, 0, 0, 0)),
                  pl.BlockSpec((CONV_K, D_XBC), fixed), pl.BlockSpec((1, LANES), fixed), pl.BlockSpec((1, LANES), fixed),
                  pl.BlockSpec((1, LANES), fixed)],
        out_specs=(pl.BlockSpec((t, D_XBC), rev), pl.BlockSpec((t, LANES), rev), pl.BlockSpec((CONV_K, D_XBC), fixed),
                   pl.BlockSpec((1, D_XBC), fixed), pl.BlockSpec((1, LANES), fixed), pl.BlockSpec((1, LANES), fixed),
                   pl.BlockSpec((1, LANES), fixed)),
        out_shape=(jax.ShapeDtypeStruct((seq, D_XBC), bf16), jax.ShapeDtypeStruct((seq, LANES), bf16),
                   jax.ShapeDtypeStruct((CONV_K, D_XBC), f32), jax.ShapeDtypeStruct((1, D_XBC), f32),
                   jax.ShapeDtypeStruct((1, LANES), f32), jax.ShapeDtypeStruct((1, LANES), f32), jax.ShapeDtypeStruct((1, LANES), f32)),
        scratch_shapes=[pltpu.VMEM((N_HEADS, HEAD_DIM, N_STATE), f32), pltpu.VMEM((t, D_XBC), f32)],
        compiler_params=_params(("arbitrary",)),
    )(dy, conv, xbc, dt_raw, states, conv_w, dtb, alog, dsk)


def _heads_to_cols(ref, width=HEAD_DIM):
    return jnp.concatenate([ref[h][:, :width] for h in range(N_HEADS)], axis=1)


def _silu_and_grad(z):
    sg = _sigmoid(z)
    return z * sg, sg * (1.0 + z * (1.0 - sg))


def _rms(v):
    return lax.rsqrt(jnp.mean(v * v, axis=-1, keepdims=True) + EPS)


def _rms_bwd(d_hat, hat, r):
    return r * (d_hat - hat * jnp.mean(d_hat * hat, axis=-1, keepdims=True))


def _post(x, target, o_tot, y, za, zs, w_out, gate, g_sb, g_ssm, g_f, seq):
    ts = 256

    def body(x_ref, t_ref, o_ref, y_ref, za_ref, zs_ref, w_ref, gate_ref, gsb_ref, gss_ref, gf_ref,
             ycat_t_ref, dmix_ref, dx2_ref, loss_ref, gnf_ref, dgate_ref):
        @pl.when(pl.program_id(0) == 0)
        def _():
            loss_ref[...] = jnp.zeros_like(loss_ref)
            gnf_ref[...] = jnp.zeros_like(gnf_ref)
            dgate_ref[...] = jnp.zeros_like(dgate_ref)

        o = _heads_to_cols(o_ref)
        zav = za_ref[...]
        ya = (o * _rms(o) * gsb_ref[...]) * (zav * _sigmoid(zav))
        zsv = zs_ref[...]
        u = y_ref[...] * (zsv * _sigmoid(zsv))
        ys = u * _rms(u) * gss_ref[...]
        yab, ysb = ya.astype(bf16), ys.astype(bf16)
        _store_transposed(ycat_t_ref.at[:D_ATTN], yab)
        _store_transposed(ycat_t_ref.at[D_ATTN:], ysb)
        mixed = (jnp.dot(yab, w_ref[:D_ATTN, :], preferred_element_type=f32)
                 + jnp.dot(ysb, w_ref[D_ATTN:, :], preferred_element_type=f32))
        gate_v = gate_ref[...]
        x2 = x_ref[...] + gate_v * mixed
        r2 = _rms(x2)
        xh = x2 * r2
        gf = gf_ref[...]
        diff = xh * gf - t_ref[...]
        loss_ref[...] += jnp.sum(diff * diff) * (0.5 / D_MODEL)
        d_out = diff * (1.0 / D_MODEL)
        gnf_ref[...] += jnp.sum(d_out * xh, axis=0, keepdims=True)
        dx2 = _rms_bwd(d_out * gf, xh, r2)
        dx2_ref[...] = dx2
        dgate_ref[...] += jnp.sum(dx2 * mixed, axis=0, keepdims=True)
        dmix_ref[...] = (dx2 * gate_v).astype(bf16)

    row = lambda i: (i, 0)
    fixed = lambda i: (0, 0)
    vec = pl.BlockSpec((1, D_MODEL), fixed)
    return pl.pallas_call(
        body, name="post", grid=(seq // ts,),
        in_specs=[pl.BlockSpec((ts, D_MODEL), row), pl.BlockSpec((ts, D_MODEL), row),
                  pl.BlockSpec((N_HEADS, ts, 2 * HEAD_DIM), lambda i: (0, i, 0)), pl.BlockSpec((ts, D_SSM), row),
                  pl.BlockSpec((ts, D_ATTN), row), pl.BlockSpec((ts, D_SSM), row), pl.BlockSpec((D_ATTN + D_SSM, D_MODEL), fixed),
                  vec, vec, vec, vec],
        out_specs=(pl.BlockSpec((D_ATTN + D_SSM, ts), lambda i: (0, i)), pl.BlockSpec((ts, D_MODEL), row), pl.BlockSpec((ts, D_MODEL), row),
                   pl.BlockSpec((1, LANES), fixed), vec, vec),
        out_shape=(jax.ShapeDtypeStruct((D_ATTN + D_SSM, seq), bf16), jax.ShapeDtypeStruct((seq, D_MODEL), bf16),
                   jax.ShapeDtypeStruct((seq, D_MODEL), f32), jax.ShapeDtypeStruct((1, LANES), f32),
                   jax.ShapeDtypeStruct((1, D_MODEL), f32), jax.ShapeDtypeStruct((1, D_MODEL), f32)),
        compiler_params=_params(("arbitrary",)),
    )(x, target, o_tot, y, za, zs, w_out, gate, g_sb, g_ssm, g_f)


def _bwd_out(dmix, w_out, o_tot, y, za, zs, g_sb, g_ssm, seq):
    ts = 256

    def body(dm_ref, w_ref, o_ref, y_ref, za_ref, zs_ref, gsb_ref, gss_ref, do_ref, dza_ref, dzs_ref, dy_ref, ggsb_ref, ggss_ref):
        @pl.when(pl.program_id(0) == 0)
        def _():
            ggsb_ref[...] = jnp.zeros_like(ggsb_ref)
            ggss_ref[...] = jnp.zeros_like(ggss_ref)

        dm = dm_ref[...]
        d_ya = _nt(dm, w_ref[:D_ATTN, :])
        d_ys = _nt(dm, w_ref[D_ATTN:, :])
        o = _heads_to_cols(o_ref)
        ro = _rms(o)
        oh = o * ro
        sa, dsa = _silu_and_grad(za_ref[...])
        gsb = gsb_ref[...]
        dza_ref[...] = (d_ya * oh * gsb * dsa).astype(bf16)
        ggsb_ref[...] += jnp.sum(d_ya * oh * sa, axis=0, keepdims=True)
        d_o = _rms_bwd(d_ya * gsb * sa, oh, ro)
        for h in range(N_HEADS):
            do_ref[h] = d_o[:, h * HEAD_DIM:(h + 1) * HEAD_DIM].astype(bf16)
        yv = y_ref[...]
        sz, dsz = _silu_and_grad(zs_ref[...])
        u = yv * sz
        ru = _rms(u)
        uh = u * ru
        ggss_ref[...] += jnp.sum(d_ys * uh, axis=0, keepdims=True)
        du = _rms_bwd(d_ys * gss_ref[...], uh, ru)
        dy_ref[...] = du * sz
        dzs_ref[...] = (du * yv * dsz).astype(bf16)

    row = lambda i: (i, 0)
    fixed = lambda i: (0, 0)
    vec = pl.BlockSpec((1, D_MODEL), fixed)
    return pl.pallas_call(
        body, name="bwd_out", grid=(seq // ts,),
        in_specs=[pl.BlockSpec((ts, D_MODEL), row), pl.BlockSpec((D_ATTN + D_SSM, D_MODEL), fixed),
                  pl.BlockSpec((N_HEADS, ts, 2 * HEAD_DIM), lambda i: (0, i, 0)), pl.BlockSpec((ts, D_SSM), row),
                  pl.BlockSpec((ts, D_ATTN), row), pl.BlockSpec((ts, D_SSM), row), vec, vec],
        out_specs=(pl.BlockSpec((N_HEADS, ts, HEAD_DIM), lambda i: (0, i, 0)), pl.BlockSpec((ts, D_ATTN), row),
                   pl.BlockSpec((ts, D_SSM), row), pl.BlockSpec((ts, D_SSM), row), vec, vec),
        out_shape=(jax.ShapeDtypeStruct((N_HEADS, seq, HEAD_DIM), bf16), jax.ShapeDtypeStruct((seq, D_ATTN), bf16),
                   jax.ShapeDtypeStruct((seq, D_SSM), bf16), jax.ShapeDtypeStruct((seq, D_SSM), f32),
                   jax.ShapeDtypeStruct((1, D_MODEL), f32), jax.ShapeDtypeStruct((1, D_MODEL), f32)),
        compiler_params=_params(("arbitrary",)),
    )(dmix, w_out, o_tot, y, za, zs, g_sb, g_ssm)


def _qkv_grads_to_cols(dq, dkt, dvt, seq):
    ts = 256
    nb = ts // CHUNK

    def body(dq_ref, dkt_ref, dvt_ref, out_ref):
        out_ref[:, :D_ATTN] = _heads_to_cols(dq_ref).astype(bf16)
        eye = (lax.broadcasted_iota(jnp.int32, (CHUNK, CHUNK), 0) == lax.broadcasted_iota(jnp.int32, (CHUNK, CHUNK), 1)).astype(bf16)
        for p, ref in ((1, dkt_ref), (2, dvt_ref)):
            for b in range(nb):
                cols = [_nt(eye, ref[h, b].astype(bf16)) for h in range(N_HEADS)]
                out_ref[b * CHUNK:(b + 1) * CHUNK, p * D_ATTN:(p + 1) * D_ATTN] = jnp.concatenate(cols, axis=1).astype(bf16)

    blk = pl.BlockSpec((N_HEADS, ts, HEAD_DIM), lambda i: (0, i, 0))
    blk_t = pl.BlockSpec((N_HEADS, nb, HEAD_DIM, CHUNK), lambda i: (0, i, 0, 0))
    return pl.pallas_call(
        body, name="qkv_grads_to_cols", grid=(seq // ts,), in_specs=[blk, blk_t, blk_t],
        out_specs=pl.BlockSpec((ts, 3 * D_ATTN), lambda i: (i, 0)),
        out_shape=jax.ShapeDtypeStruct((seq, 3 * D_ATTN), bf16), compiler_params=_params(("parallel",)),
    )(dq, dkt, dvt)


def _bwd_in(dqkv, dza, dzs, dxbc, ddt, wp, x, dx2, gain, scale, seq):
    ts = 256
    pieces = ((0, 0, 3 * D_ATTN), (1, OFF_ZA, D_ATTN), (2, OFF_ZS, D_SSM), (3, OFF_XBC, D_XBC), (4, OFF_DT, LANES))

    def body(dqkv_ref, dza_ref, dzs_ref, dxbc_ref, ddt_ref, w_ref, x_ref, dx2_ref, g_ref, sc_ref,
             gx_ref, dshift_ref, dscale_ref, ggain_ref):
        @pl.when(pl.program_id(0) == 0)
        def _():
            dshift_ref[...] = jnp.zeros_like(dshift_ref)
            dscale_ref[...] = jnp.zeros_like(dscale_ref)
            ggain_ref[...] = jnp.zeros_like(ggain_ref)

        refs = (dqkv_ref, dza_ref, dzs_ref, dxbc_ref, ddt_ref)
        dh = jnp.zeros((ts, D_MODEL), f32)
        for idx, off, width in pieces:
            for cc in range(0, width, 512):
                wd = min(512, width - cc)
                dh = dh + _nt(refs[idx][:, cc:cc + wd], w_ref[:, off + cc:off + cc + wd])
        xv = x_ref[...]
        r = _rms(xv)
        xh = xv * r
        g = g_ref[...]
        dshift_ref[...] += jnp.sum(dh, axis=0, keepdims=True)
        dscale_ref[...] += jnp.sum(dh * xh * g, axis=0, keepdims=True)
        tt = dh * (1.0 + sc_ref[...])
        ggain_ref[...] += jnp.sum(tt * xh, axis=0, keepdims=True)
        gx_ref[...] = dx2_ref[...] + _rms_bwd(tt * g, xh, r)

    row = lambda i: (i, 0)
    fixed = lambda i: (0, 0)
    vec = pl.BlockSpec((1, D_MODEL), fixed)
    return pl.pallas_call(
        body, name="bwd_in", grid=(seq // ts,),
        in_specs=[pl.BlockSpec((ts, 3 * D_ATTN), row), pl.BlockSpec((ts, D_ATTN), row), pl.BlockSpec((ts, D_SSM), row),
                  pl.BlockSpec((ts, D_XBC), row), pl.BlockSpec((ts, LANES), row), pl.BlockSpec((D_MODEL, D_PROJ_P), fixed),
                  pl.BlockSpec((ts, D_MODEL), row), pl.BlockSpec((ts, D_MODEL), row), vec, vec],
        out_specs=(pl.BlockSpec((ts, D_MODEL), row), vec, vec, vec),
        out_shape=(jax.ShapeDtypeStruct((seq, D_MODEL), f32), jax.ShapeDtypeStruct((1, D_MODEL), f32),
                   jax.ShapeDtypeStruct((1, D_MODEL), f32), jax.ShapeDtypeStruct((1, D_MODEL), f32)),
        compiler_params=_params(("arbitrary",)),
    )(dqkv, dza, dzs, dxbc, ddt, wp, x, dx2, gain, scale)


def _grad_w(a_t, b, tn, name):
    m, seq = a_t.shape
    n = b.shape[1]
    tk = min(1024, seq)
    n_k = seq // tk

    def body(a_ref, b_ref, o_ref, acc):
        @pl.when(pl.program_id(1) == 0)
        def _():
            acc[...] = jnp.zeros_like(acc)

        acc[...] += jnp.dot(a_ref[...], b_ref[...], preferred_element_type=f32)

        @pl.when(pl.program_id(1) == n_k - 1)
        def _():
            o_ref[...] = acc[...].astype(bf16)

    return pl.pallas_call(
        body, name=name, grid=(n // tn, n_k),
        in_specs=[pl.BlockSpec((m, tk), lambda j, k: (0, k)), pl.BlockSpec((tk, tn), lambda j, k: (k, j))],
        out_specs=pl.BlockSpec((m, tn), lambda j, k: (0, j)),
        out_shape=jax.ShapeDtypeStruct((m, n), bf16), scratch_shapes=[pltpu.VMEM((m, tn), f32)],
        compiler_params=_params(("parallel", "arbitrary")),
    )(a_t, b)


def _small_finish(g_all, c_all, dmod_mine):
    def body(g_ref, c_ref, dm_ref, tot_ref, gwada_ref):
        tot = g_ref[0:1, :]
        for j in range(1, N_DEV):
            tot = tot + g_ref[j:j + 1, :]
        tot_ref[...] = tot
        cv = c_ref[...]
        gwada_ref[...] = _tn(cv * _sigmoid(cv), dm_ref[...], HI)

    vmem = pl.BlockSpec(memory_space=pltpu.VMEM)
    return pl.pallas_call(
        body, name="small_finish", in_specs=[vmem, vmem, vmem], out_specs=(vmem, vmem),
        out_shape=(jax.ShapeDtypeStruct((1, N_PACK), f32), jax.ShapeDtypeStruct((D_MODEL, dmod_mine.shape[1]), f32)),
        compiler_params=_params(),
    )(g_all, c_all, dmod_mine)


def _adamw(w, g_parts, m, v, rows, name):
    r, c = w.shape
    n_parts = g_parts.shape[0]
    bc1 = 1.0 - ADAM_B1 ** ADAM_STEP
    bc2 = 1.0 - ADAM_B2 ** ADAM_STEP

    def body(w_ref, g_ref, m_ref, v_ref, go_ref, d_ref, mo_ref, vo_ref):
        g = g_ref[0].astype(f32)
        for j in range(1, n_parts):
            g = g + g_ref[j].astype(f32)
        go_ref[...] = g
        mn = ADAM_B1 * m_ref[...] + (1.0 - ADAM_B1) * g
        vn = ADAM_B2 * v_ref[...] + (1.0 - ADAM_B2) * (g * g)
        mo_ref[...] = mn
        vo_ref[...] = vn
        d_ref[...] = -ADAM_LR * ((mn / bc1) / (jnp.sqrt(vn / bc2) + ADAM_EPS) + ADAM_WD * w_ref[...])

    blk = pl.BlockSpec((rows, c), lambda i: (i, 0))
    return pl.pallas_call(
        body, name=name, grid=(r // rows,),
        in_specs=[blk, pl.BlockSpec((n_parts, rows, c), lambda i: (0, i, 0)), blk, blk],
        out_specs=(blk, blk, blk, blk), out_shape=(jax.ShapeDtypeStruct((r, c), f32),) * 4,
        compiler_params=_params(("parallel",)),
    )(w, g_parts, m, v)


def _pad_lanes(v):
    return jnp.pad(v, ((0, 0), (0, LANES - v.shape[1])))


def kernel(x, c, w_ada, b_ada, norm_in_gain, w_in, conv_w, conv_b, dt_bias, a_log, d_skip, sb_norm_gain, ssm_norm_gain, w_out, norm_f_gain, loss_target, m_w_ada, m_b_ada, m_norm_in_gain, m_w_in, m_conv_w, m_conv_b, m_dt_bias, m_a_log, m_d_skip, m_sb_norm_gain, m_ssm_norm_gain, m_w_out, m_norm_f_gain, v_w_ada, v_b_ada, v_norm_in_gain, v_w_in, v_conv_w, v_conv_b, v_dt_bias, v_a_log, v_d_skip, v_sb_norm_gain, v_ssm_norm_gain, v_w_out, v_norm_f_gain):
    seq = x.shape[1]
    xs = x[0]
    tgt = loss_target[0]
    _, my_slot = _me()

    mod, c_all = _mod_exchange(c, w_ada[0], b_ada)
    shift, scale, gate = mod[:, :D_MODEL], mod[:, D_MODEL:2 * D_MODEL], mod[:, 2 * D_MODEL:]
    w_in_g, w_out_g, conv_w_g = _all_gather_two_level(
        [_cast_bf16(w_in[0], 128), _cast_bf16(w_out[0], 128), conv_w[0]], "gather_weights")
    w_full = jnp.transpose(w_in_g, (1, 0, 2)).reshape(D_MODEL, D_PROJ)
    wp = jnp.concatenate([w_full[:, :4 * D_ATTN], w_full[:, D_PROJ - D_SSM:], w_full[:, 4 * D_ATTN:4 * D_ATTN + D_XBC],
                          _pad_lanes(w_full[:, 4 * D_ATTN + D_XBC:4 * D_ATTN + D_XBC + N_HEADS])], axis=1)
    w_out_full = w_out_g.reshape(D_ATTN + D_SSM, D_MODEL)
    conv_w_full = jnp.transpose(conv_w_g, (1, 0, 2)).reshape(CONV_K, D_XBC)
    dtb, alog, dsk = _pad_lanes(dt_bias), _pad_lanes(a_log), _pad_lanes(d_skip)

    h_t, qkv, za, zs, xbc, dt_raw = _proj(xs, norm_in_gain, scale, shift, wp, seq)
    o_tot = _attn_fwd(qkv, seq)
    conv, y, states = _ssd_fwd(xbc, dt_raw, conv_w_full, conv_b, dtb, alog, dsk, seq)
    ycat_t, dmix, dx2, loss_p, g_nf, d_gate = _post(xs, tgt, o_tot, y, za, zs, w_out_full, gate, sb_norm_gain, ssm_norm_gain,
                                                  norm_f_gain.reshape(1, D_MODEL), seq)

    d_o, dza, dzs, dy, g_sb, g_ss = _bwd_out(dmix, w_out_full, o_tot, y, za, zs, sb_norm_gain, ssm_norm_gain, seq)
    dq, dk, dv = _attn_bwd(qkv, o_tot, d_o, seq)
    dxbc, ddt, g_cw, g_cb, g_dtb, g_al, g_dsk = _ssd_bwd(dy, conv, xbc, dt_raw, states, conv_w_full, dtb, alog, dsk, seq)
    dqkv = _qkv_grads_to_cols(dq, dk, dv, seq)
    grad_x, d_shift, d_scale, g_in = _bwd_in(dqkv, dza, dzs, dxbc, ddt, wp, xs, dx2, norm_in_gain, scale, seq)
    gw_qkv = _grad_w(h_t, dqkv, D_XBC, "grad_w_qkv")
    gw_za = _grad_w(h_t, dza, D_ATTN, "grad_w_za")
    gw_zs = _grad_w(h_t, dzs, D_SSM, "grad_w_zs")
    gw_xbc = _grad_w(h_t, dxbc, D_XBC, "grad_w_xbc")
    gw_dt = _grad_w(h_t, ddt, LANES, "grad_w_dt")
    gw_out = _grad_w(ycat_t, dmix, D_MODEL, "grad_w_out")
    gw_in = jnp.concatenate([gw_qkv, gw_za, gw_xbc, gw_dt[:, :N_HEADS], gw_zs], axis=1)

    gw_in_parts, gw_out_parts = _reduce_scatter_two_level(
        [jnp.transpose(gw_in.reshape(D_MODEL, N_DEV, W_IN_SHARD), (1, 0, 2)),
         gw_out.reshape(N_DEV, (D_ATTN + D_SSM) // N_DEV, D_MODEL)], "scatter_grads")
    packed = jnp.concatenate([loss_p, d_shift, d_scale, d_gate, g_in, g_cb, g_dtb, g_al, g_dsk, g_sb, g_ss, g_nf,
                              g_cw.reshape(1, CONV_K * D_XBC)], axis=1)
    (packed_all,) = _all_gather([packed], "gather_small")
    packed_all = packed_all.reshape(N_DEV, N_PACK)
    n_ada = w_ada.shape[2]
    dmod_mine = lax.dynamic_slice(packed_all, (0, P_DMOD + my_slot * n_ada), (N_DEV, n_ada))
    tot, g_w_ada = _small_finish(packed_all, c_all.reshape(N_DEV, D_MODEL), dmod_mine)

    def big(w, parts, m, v, rows, name):
        return tuple(t[None] for t in _adamw(w[0], parts, m[0], v[0], rows, name))

    small_names = ["b_ada", "norm_in_gain", "conv_b", "dt_bias", "a_log", "d_skip", "sb_norm_gain", "ssm_norm_gain", "norm_f_gain"]
    given = {"b_ada": (b_ada, m_b_ada, v_b_ada), "norm_in_gain": (norm_in_gain, m_norm_in_gain, v_norm_in_gain),
             "conv_b": (conv_b, m_conv_b, v_conv_b), "dt_bias": (dt_bias, m_dt_bias, v_dt_bias), "a_log": (a_log, m_a_log, v_a_log),
             "d_skip": (d_skip, m_d_skip, v_d_skip), "sb_norm_gain": (sb_norm_gain, m_sb_norm_gain, v_sb_norm_gain),
             "ssm_norm_gain": (ssm_norm_gain, m_ssm_norm_gain, v_ssm_norm_gain), "norm_f_gain": (norm_f_gain, m_norm_f_gain, v_norm_f_gain)}

    def pack(which):
        cols = []
        for nm in small_names:
            t = given[nm][which].reshape(1, -1)
            cols.append(_pad_lanes(t) if t.shape[1] < LANES else t)
        return jnp.concatenate(cols, axis=1)

    packed_out = _adamw(pack(0), tot[:, P_DMOD:P_CW][None], pack(1), pack(2), 1, "adamw_small")
    res = {}
    off = 0
    for nm in small_names:
        shape = given[nm][0].shape
        size = given[nm][0].size
        res[nm] = tuple(t[:, off:off + size].reshape(shape) for t in packed_out)
        off += max(size, LANES)
    n_cw = conv_w.shape[2]
    g_cw_mine = lax.dynamic_slice(tot[:, P_CW:].reshape(CONV_K, D_XBC), (0, my_slot * n_cw), (CONV_K, n_cw))
    res["conv_w"] = tuple(t.reshape(conv_w.shape) for t in _adamw(conv_w.reshape(1, -1), g_cw_mine.reshape(1, 1, -1),
                                                                  m_conv_w.reshape(1, -1), v_conv_w.reshape(1, -1), 1, "adamw_conv_w"))
    res["w_ada"] = big(w_ada, g_w_ada[None], m_w_ada, v_w_ada, 128, "adamw_w_ada")
    res["w_in"] = big(w_in, gw_in_parts, m_w_in, v_w_in, 128, "adamw_w_in")
    res["w_out"] = big(w_out, gw_out_parts, m_w_out, v_w_out, 64, "adamw_w_out")
    names = ["w_ada", "b_ada", "norm_in_gain", "w_in", "conv_w", "conv_b", "dt_bias", "a_log", "d_skip", "sb_norm_gain",
             "ssm_norm_gain", "w_out", "norm_f_gain"]
    loss = tot[0, P_LOSS]
    return (loss, grad_x[None], *[res[n][0] for n in names], *[res[n][1] for n in names],
            *[res[n][2] for n in names], *[res[n][3] for n in names])
```

```python
import functools

import jax
import jax.numpy as jnp
from jax import lax
from jax.experimental import pallas as pl
from jax.experimental.pallas import tpu as pltpu

f32 = jnp.float32
bf16 = jnp.bfloat16
MESH = pl.DeviceIdType.MESH
HI = lax.Precision.HIGHEST

N_DEV = 8
D_MODEL = 1024
D_ATTN = 1024
D_SSM = 1024
N_HEADS = 16
HEAD_DIM = 64
N_GROUPS = 2
HEADS_PER_GROUP = 8
N_STATE = 128
D_XBC = D_SSM + 2 * N_GROUPS * N_STATE
D_PROJ = 4 * D_ATTN + D_XBC + N_HEADS + D_SSM
W_IN_SHARD = D_PROJ // N_DEV
CONV_K = 4
CHUNK = 128
ATTN_Q_ROWS = 2048
ATTN_UNROLL = 8
LANES = 128
EPS = 1e-6
OFF_ZA = 3072
OFF_ZS = 4096
OFF_XBC = 5120
OFF_DT = 6656
D_PROJ_P = 6784
VMEM_LIMIT_BYTES = 56 * 1024 * 1024

ADAM_LR = 0.001
ADAM_B1 = 0.9
ADAM_B2 = 0.999
ADAM_EPS = 1e-08
ADAM_WD = 0.01
ADAM_STEP = 10

P_LOSS = 0
P_DMOD = 128
P_GIN = 3200
P_CB = 4224
P_DTB = 5760
P_ALOG = 5888
P_DSK = 6016
P_GSB = 6144
P_GSS = 7168
P_GNF = 8192
P_CW = 9216
N_PACK = 15360


def _params(sem=None):
    return pltpu.CompilerParams(dimension_semantics=sem, vmem_limit_bytes=VMEM_LIMIT_BYTES)


def _sigmoid(v):
    return 1.0 / (1.0 + jnp.exp(-v))


def _softplus(v):
    return jnp.maximum(v, 0.0) + jnp.log(1.0 + jnp.exp(-jnp.abs(v)))


def _nt(a, b, precision=None):
    return lax.dot_general(a, b, (((1,), (1,)), ((), ())), preferred_element_type=f32, precision=precision)


def _tn(a, b, precision=None):
    return lax.dot_general(a, b, (((0,), (0,)), ((), ())), preferred_element_type=f32, precision=precision)


def _nn(a, b, precision=None):
    return lax.dot_general(a, b, (((1,), (0,)), ((), ())), preferred_element_type=f32, precision=precision)


def _me():
    x, y, c = lax.axis_index("x"), lax.axis_index("y"), lax.axis_index("c")
    return (x, y, c), 4 * x + 2 * y + c


def _peer(k):
    x, y, c = lax.axis_index("x"), lax.axis_index("y"), lax.axis_index("c")
    px = 1 - x if (k >> 2) & 1 else x
    py = 1 - y if (k >> 1) & 1 else y
    pc = 1 - c if k & 1 else c
    return (px, py, pc), 4 * px + 2 * py + pc


def _all_gather(arrs, name):
    n = len(arrs)

    def body(*refs):
        ins, outs = refs[:n], refs[n:2 * n]
        send_sems, recv_sems, local_sems = refs[2 * n:]
        _, my_slot = _me()
        sends = []
        locals_ = []
        for a in range(n):
            loc = pltpu.make_async_copy(ins[a], outs[a].at[my_slot], local_sems.at[a])
            loc.start()
            locals_.append(loc)
            for k in range(1, N_DEV):
                peer, _ = _peer(k)
                cp = pltpu.make_async_remote_copy(src_ref=ins[a], dst_ref=outs[a].at[my_slot], send_sem=send_sems.at[a, k - 1],
                                                  recv_sem=recv_sems.at[a, k - 1], device_id=peer, device_id_type=MESH)
                cp.start()
                sends.append(cp)
        for a in range(n):
            for k in range(1, N_DEV):
                peer, peer_slot = _peer(k)
                pltpu.make_async_remote_copy(src_ref=ins[a], dst_ref=outs[a].at[peer_slot], send_sem=send_sems.at[a, k - 1],
                                             recv_sem=recv_sems.at[a, k - 1], device_id=peer, device_id_type=MESH).wait_recv()
        for cp in sends:
            cp.wait_send()
        for loc in locals_:
            loc.wait()

    any_spec = pl.BlockSpec(memory_space=pl.ANY)
    return pl.pallas_call(
        body, name=name,
        out_shape=tuple(jax.ShapeDtypeStruct((N_DEV,) + a.shape, a.dtype) for a in arrs),
        in_specs=[any_spec] * n, out_specs=tuple([any_spec] * n),
        scratch_shapes=[pltpu.SemaphoreType.DMA((n, N_DEV - 1)), pltpu.SemaphoreType.DMA((n, N_DEV - 1)),
                        pltpu.SemaphoreType.DMA((n,))],
    )(*arrs)


def _all_gather_two_level(arrs, name):
    n = len(arrs)

    def body(*refs):
        ins, outs = refs[:n], refs[n:2 * n]
        send_sems, recv_sems, local_sems = refs[2 * n:]
        x, y, c = lax.axis_index("x"), lax.axis_index("y"), lax.axis_index("c")
        me, sibling = (x, y, c), (x, y, 1 - c)
        chips = [(1 - x, y), (x, 1 - y), (1 - x, 1 - y)]

        def copy(a, k, block, to, from_input=False):
            slot = 4 * block[0] + 2 * block[1] + block[2]
            return pltpu.make_async_remote_copy(src_ref=ins[a] if from_input else outs[a].at[slot], dst_ref=outs[a].at[slot],
                                                send_sem=send_sems.at[a, k], recv_sem=recv_sems.at[a, k], device_id=to,
                                                device_id_type=MESH)

        started = []
        locals_ = []
        for a in range(n):
            loc = pltpu.make_async_copy(ins[a], outs[a].at[4 * x + 2 * y + c], local_sems.at[a])
            loc.start()
            locals_.append(loc)
            first = [copy(a, 0, me, sibling, True)] + [copy(a, 1 + j, me, (*chip, c), True) for j, chip in enumerate(chips)]
            for cp in first:
                cp.start()
            started += first
        for a in range(n):
            for j, chip in enumerate(chips):
                copy(a, 1 + j, (*chip, c), me).wait_recv()
                onward = copy(a, 4 + j, (*chip, c), sibling)
                onward.start()
                started.append(onward)
        for a in range(n):
            copy(a, 0, sibling, me).wait_recv()
            for j, chip in enumerate(chips):
                copy(a, 4 + j, (*chip, 1 - c), me).wait_recv()
        for cp in started:
            cp.wait_send()
        for loc in locals_:
            loc.wait()

    any_spec = pl.BlockSpec(memory_space=pl.ANY)
    return pl.pallas_call(
        body, name=name,
        out_shape=tuple(jax.ShapeDtypeStruct((N_DEV,) + a.shape, a.dtype) for a in arrs),
        in_specs=[any_spec] * n, out_specs=tuple([any_spec] * n),
        scratch_shapes=[pltpu.SemaphoreType.DMA((n, N_DEV - 1)), pltpu.SemaphoreType.DMA((n, N_DEV - 1)),
                        pltpu.SemaphoreType.DMA((n,))],
    )(*arrs)


def _all_to_all(arrs, name):
    n = len(arrs)

    def body(*refs):
        ins, outs = refs[:n], refs[n:2 * n]
        send_sems, recv_sems, local_sems = refs[2 * n:]
        _, my_slot = _me()
        sends = []
        locals_ = []
        for a in range(n):
            loc = pltpu.make_async_copy(ins[a].at[my_slot], outs[a].at[my_slot], local_sems.at[a])
            loc.start()
            locals_.append(loc)
            for k in range(1, N_DEV):
                peer, peer_slot = _peer(k)
                cp = pltpu.make_async_remote_copy(src_ref=ins[a].at[peer_slot], dst_ref=outs[a].at[my_slot],
                                                  send_sem=send_sems.at[a, k - 1], recv_sem=recv_sems.at[a, k - 1],
                                                  device_id=peer, device_id_type=MESH)
                cp.start()
                sends.append(cp)
        for a in range(n):
            for k in range(1, N_DEV):
                peer, peer_slot = _peer(k)
                pltpu.make_async_remote_copy(src_ref=ins[a].at[peer_slot], dst_ref=outs[a].at[peer_slot],
                                             send_sem=send_sems.at[a, k - 1], recv_sem=recv_sems.at[a, k - 1],
                                             device_id=peer, device_id_type=MESH).wait_recv()
        for cp in sends:
            cp.wait_send()
        for loc in locals_:
            loc.wait()

    any_spec = pl.BlockSpec(memory_space=pl.ANY)
    return pl.pallas_call(
        body, name=name,
        out_shape=tuple(jax.ShapeDtypeStruct(a.shape, a.dtype) for a in arrs),
        in_specs=[any_spec] * n, out_specs=tuple([any_spec] * n),
        scratch_shapes=[pltpu.SemaphoreType.DMA((n, N_DEV - 1)), pltpu.SemaphoreType.DMA((n, N_DEV - 1)),
                        pltpu.SemaphoreType.DMA((n,))],
    )(*arrs)


def _reduce_scatter_two_level(arrs, name):
    n = len(arrs)
    n_chip = N_DEV // 2

    def body(*refs):
        ins, outs = refs[:n], refs[n:2 * n]
        mine_bufs, sib_bufs = refs[2 * n:3 * n], refs[3 * n:4 * n]
        send_sems, recv_sems, local_sems = refs[4 * n:]
        x, y, c = lax.axis_index("x"), lax.axis_index("y"), lax.axis_index("c")
        sibling = (x, y, 1 - c)

        def chip(r):
            return (1 - x if r & 2 else x), (1 - y if r & 1 else y)

        def slot(r, core):
            cx, cy = chip(r)
            return 4 * cx + 2 * cy + core

        def to_sibling(a, r):
            return pltpu.make_async_remote_copy(src_ref=ins[a].at[slot(r, 1 - c)], dst_ref=sib_bufs[a].at[r], send_sem=send_sems.at[a, r],
                                                recv_sem=recv_sems.at[a, r], device_id=sibling, device_id_type=MESH)

        def to_chip(a, r):
            return pltpu.make_async_remote_copy(src_ref=mine_bufs[a].at[r], dst_ref=outs[a].at[r], send_sem=send_sems.at[a, n_chip - 1 + r],
                                                recv_sem=recv_sems.at[a, n_chip - 1 + r], device_id=(*chip(r), c), device_id_type=MESH)

        def load_mine(a, r):
            return pltpu.make_async_copy(ins[a].at[slot(r, c)], mine_bufs[a].at[r], local_sems.at[a, r])

        started = []
        order = [1, 2, 3, 0]
        for a in range(n):
            for r in order:
                load_mine(a, r).start()
                cp = to_sibling(a, r)
                cp.start()
                started.append(cp)
        for a in range(n):
            for r in order:
                load_mine(a, r).wait()
                to_sibling(a, r).wait_recv()
                total = (mine_bufs[a][r].astype(f32) + sib_bufs[a][r].astype(f32)).astype(bf16)
                if r:
                    mine_bufs[a][r] = total
                    cp = to_chip(a, r)
                    cp.start()
                    started.append(cp)
                else:
                    outs[a][0] = total
        for a in range(n):
            for r in range(1, n_chip):
                to_chip(a, r).wait_recv()
        for cp in started:
            cp.wait_send()

    any_spec = pl.BlockSpec(memory_space=pl.ANY)
    vmem = pl.BlockSpec(memory_space=pltpu.VMEM)
    part = lambda a: (n_chip,) + a.shape[1:]
    return pl.pallas_call(
        body, name=name,
        out_shape=tuple(jax.ShapeDtypeStruct(part(a), a.dtype) for a in arrs),
        in_specs=[any_spec] * n, out_specs=tuple([vmem] * n),
        scratch_shapes=([pltpu.VMEM(part(a), a.dtype) for a in arrs] + [pltpu.VMEM(part(a), a.dtype) for a in arrs]
                        + [pltpu.SemaphoreType.DMA((n, N_DEV - 1)), pltpu.SemaphoreType.DMA((n, N_DEV - 1)),
                           pltpu.SemaphoreType.DMA((n, n_chip))]),
        compiler_params=_params(),
    )(*arrs)


def _mod_exchange(c_row, w_ada, b_ada):
    n_col = w_ada.shape[1]

    def body(c_ref, w_ref, b_ref, mod_ref, call_ref, part, modp, send_sems, recv_sems):
        _, my_slot = _me()
        call_ref[my_slot] = c_ref[...]
        sends = []
        for k in range(1, N_DEV):
            peer, _ = _peer(k)
            cp = pltpu.make_async_remote_copy(src_ref=c_ref, dst_ref=call_ref.at[my_slot], send_sem=send_sems.at[0, k - 1],
                                              recv_sem=recv_sems.at[0, k - 1], device_id=peer, device_id_type=MESH)
            cp.start()
            sends.append(cp)
        for k in range(1, N_DEV):
            peer, peer_slot = _peer(k)
            pltpu.make_async_remote_copy(src_ref=c_ref, dst_ref=call_ref.at[peer_slot], send_sem=send_sems.at[0, k - 1],
                                         recv_sem=recv_sems.at[0, k - 1], device_id=peer, device_id_type=MESH).wait_recv()
        for cp in sends:
            cp.wait_send()
        w = w_ref[...]
        for b in range(N_DEV):
            cb = call_ref[b]
            part[b] = _nn(cb * _sigmoid(cb), w, HI)
        modp[my_slot] = part[my_slot]
        sends = []
        for k in range(1, N_DEV):
            peer, peer_slot = _peer(k)
            cp = pltpu.make_async_remote_copy(src_ref=part.at[peer_slot], dst_ref=modp.at[my_slot], send_sem=send_sems.at[1, k - 1],
                                              recv_sem=recv_sems.at[1, k - 1], device_id=peer, device_id_type=MESH)
            cp.start()
            sends.append(cp)
        for k in range(1, N_DEV):
            peer, peer_slot = _peer(k)
            pltpu.make_async_remote_copy(src_ref=part.at[peer_slot], dst_ref=modp.at[peer_slot], send_sem=send_sems.at[1, k - 1],
                                         recv_sem=recv_sems.at[1, k - 1], device_id=peer, device_id_type=MESH).wait_recv()
        for cp in sends:
            cp.wait_send()
        for j in range(N_DEV):
            mod_ref[:, j * n_col:(j + 1) * n_col] = modp[j] + b_ref[:, j * n_col:(j + 1) * n_col]

    vmem = pl.BlockSpec(memory_space=pltpu.VMEM)
    return pl.pallas_call(
        body, name="mod_exchange",
        out_shape=(jax.ShapeDtypeStruct((1, N_DEV * n_col), f32), jax.ShapeDtypeStruct((N_DEV, 1, D_MODEL), f32)),
        in_specs=[vmem, vmem, vmem], out_specs=(vmem, vmem),
        scratch_shapes=[pltpu.VMEM((N_DEV, 1, n_col), f32), pltpu.VMEM((N_DEV, 1, n_col), f32),
                        pltpu.SemaphoreType.DMA((2, N_DEV - 1)), pltpu.SemaphoreType.DMA((2, N_DEV - 1))],
        compiler_params=_params(),
    )(c_row, w_ada, b_ada)


def _cast_bf16(a, rows):
    r, c = a.shape

    def body(a_ref, o_ref):
        o_ref[...] = a_ref[...].astype(bf16)

    return pl.pallas_call(
        body, name="cast_bf16", grid=(r // rows,),
        in_specs=[pl.BlockSpec((rows, c), lambda i: (i, 0))], out_specs=pl.BlockSpec((rows, c), lambda i: (i, 0)),
        out_shape=jax.ShapeDtypeStruct((r, c), bf16), compiler_params=_params(("parallel",)),
    )(a)


def _store_transposed(out_ref, v):
    blk = 256
    eye = (lax.broadcasted_iota(jnp.int32, (blk, blk), 0) == lax.broadcasted_iota(jnp.int32, (blk, blk), 1)).astype(bf16)
    for cb in range(0, v.shape[1], blk):
        out_ref[cb:cb + blk, :] = _nt(eye, v[:, cb:cb + blk]).astype(bf16)


def _proj(x, gain, scale, shift, wp, seq):
    ts = 256

    def body(x_ref, g_ref, sc_ref, sh_ref, w_ref, ht_ref, qkv_ref, za_ref, zs_ref, xbc_ref, dt_ref):
        xv = x_ref[...]
        r = lax.rsqrt(jnp.mean(xv * xv, axis=-1, keepdims=True) + EPS)
        hb = ((xv * r * g_ref[...]) * (1.0 + sc_ref[...]) + sh_ref[...]).astype(bf16)
        _store_transposed(ht_ref, hb)
        for cb in range(3 * D_ATTN // 256):
            res = jnp.dot(hb, w_ref[:, cb * 256:(cb + 1) * 256], preferred_element_type=f32)
            for u in range(4):
                qkv_ref[cb * 4 + u] = res[:, u * HEAD_DIM:(u + 1) * HEAD_DIM].astype(bf16)
        for out_ref, off, width in ((za_ref, OFF_ZA, D_ATTN), (zs_ref, OFF_ZS, D_SSM), (xbc_ref, OFF_XBC, D_XBC), (dt_ref, OFF_DT, LANES)):
            for cc in range(0, width, 512):
                wd = min(512, width - cc)
                out_ref[:, cc:cc + wd] = jnp.dot(hb, w_ref[:, off + cc:off + cc + wd], preferred_element_type=f32)

    row = lambda i: (i, 0)
    fixed = lambda i: (0, 0)
    return pl.pallas_call(
        body, name="proj", grid=(seq // ts,),
        in_specs=[pl.BlockSpec((ts, D_MODEL), row), pl.BlockSpec((1, D_MODEL), fixed), pl.BlockSpec((1, D_MODEL), fixed),
                  pl.BlockSpec((1, D_MODEL), fixed), pl.BlockSpec((D_MODEL, D_PROJ_P), fixed)],
        out_specs=(pl.BlockSpec((D_MODEL, ts), lambda i: (0, i)), pl.BlockSpec((3 * N_HEADS, ts, HEAD_DIM), lambda i: (0, i, 0)),
                   pl.BlockSpec((ts, D_ATTN), row), pl.BlockSpec((ts, D_SSM), row), pl.BlockSpec((ts, D_XBC), row),
                   pl.BlockSpec((ts, LANES), row)),
        out_shape=(jax.ShapeDtypeStruct((D_MODEL, seq), bf16), jax.ShapeDtypeStruct((3 * N_HEADS, seq, HEAD_DIM), bf16),
                   jax.ShapeDtypeStruct((seq, D_ATTN), f32), jax.ShapeDtypeStruct((seq, D_SSM), f32),
                   jax.ShapeDtypeStruct((seq, D_XBC), f32), jax.ShapeDtypeStruct((seq, LANES), f32)),
        compiler_params=_params(("arbitrary",)),
    )(x, gain, scale, shift, wp)


def _log_sigmoids(z):
    lb = jnp.minimum(z, 0.0) - jnp.log(1.0 + jnp.exp(-jnp.abs(z)))
    return lb, lb - z


def _split_bf16(v):
    hi = v.astype(bf16)
    return hi, (v - hi.astype(f32)).astype(bf16)


def _spread(v, sel):
    hi = v.astype(bf16)
    mid, lo = _split_bf16(v - hi.astype(f32))
    sel_b = sel.astype(bf16)
    return (jnp.dot(hi, sel_b, preferred_element_type=f32) + jnp.dot(mid, sel_b, preferred_element_type=f32)
            + jnp.dot(lo, sel_b, preferred_element_type=f32))


def _attn_fwd(qkv, seq):
    t = CHUNK
    tq = min(ATTN_Q_ROWS, seq)
    nd = tq // t
    unroll = min(ATTN_UNROLL, nd)
    assert nd % unroll == 0

    def body(q_ref, k_ref, v_ref, o_ref):
        i = pl.program_id(1)
        q = q_ref[0] * 0.125
        ur = lax.broadcasted_iota(jnp.int32, (2 * t, t), 0)
        upper = ((ur & (t - 1)) > lax.broadcasted_iota(jnp.int32, (2 * t, t), 1)).astype(bf16)

        def tile(j, q_s, acc, run, masked):
            n = q_s.shape[0]
            start = pl.multiple_of(j * t, t)
            k = k_ref[0, pl.ds(start, t), :]
            v = v_ref[0, pl.ds(start, t), :]
            z = _nt(q_s, k)
            lb, lom = _log_sigmoids(z)
            if masked:
                keep = lax.broadcasted_iota(jnp.int32, (n, t), 1) < lax.broadcasted_iota(jnp.int32, (n, t), 0)
                lom = jnp.where(keep, lom, 0.0)
            tail = jnp.dot(jnp.concatenate(_split_bf16(lom), axis=1), upper, preferred_element_type=f32)
            a = lb + tail + run
            if masked:
                a = jnp.where(keep, a, -jnp.inf)
            w = jnp.exp(a)
            acc = acc + jnp.dot(w.astype(bf16), v, preferred_element_type=f32)
            run = run + tail[:, 0:1] + lom[:, 0:1]
            return acc, run

        acc, run = jnp.zeros((tq, HEAD_DIM), f32), jnp.zeros((tq, 1), f32)
        for jj in reversed(range(nd)):
            r0 = jj * t
            acc_s, run_s = tile(i * nd + jj, q[r0:], acc[r0:], run[r0:], True)
            acc = acc_s if r0 == 0 else jnp.concatenate([acc[:r0], acc_s], axis=0)
            run = run_s if r0 == 0 else jnp.concatenate([run[:r0], run_s], axis=0)
        def group(n, cr):
            for u in range(unroll):
                cr = tile(i * nd - 1 - unroll * n - u, q, cr[0], cr[1], False)
            return cr

        acc, run = lax.fori_loop(0, i * (nd // unroll), group, (acc, run))
        o_ref[0] = jnp.concatenate([acc, jnp.broadcast_to(run, (tq, HEAD_DIM))], axis=1)

    return pl.pallas_call(
        body, name="attn_fwd", grid=(N_HEADS, seq // tq),
        in_specs=[pl.BlockSpec((1, tq, HEAD_DIM), lambda h, i: (h, i, 0)),
                  pl.BlockSpec((1, seq, HEAD_DIM), lambda h, i: (N_HEADS + h, 0, 0)),
                  pl.BlockSpec((1, seq, HEAD_DIM), lambda h, i: (2 * N_HEADS + h, 0, 0))],
        out_specs=pl.BlockSpec((1, tq, 2 * HEAD_DIM), lambda h, i: (h, i, 0)),
        out_shape=jax.ShapeDtypeStruct((N_HEADS, seq, 2 * HEAD_DIM), f32),
        compiler_params=_params(("parallel", "arbitrary")),
    )(qkv, qkv, qkv)


def _attn_bwd(qkv, o_tot, d_o, seq):
    t = CHUNK
    tq = min(ATTN_Q_ROWS, seq)
    nd = tq // t
    unroll = min(ATTN_UNROLL, nd)
    assert nd % unroll == 0
    nk = seq // t

    def body(q_ref, k_ref, v_ref, ot_ref, do_ref, dq_ref, dkt_ref, dvt_ref):
        i = pl.program_id(1)

        @pl.when(i == 0)
        def _():
            dkt_ref[...] = jnp.zeros_like(dkt_ref)
            dvt_ref[...] = jnp.zeros_like(dvt_ref)

        q = q_ref[0] * 0.125
        d_out = do_ref[0]
        total = ot_ref[0][:, HEAD_DIM:HEAD_DIM + 1]
        eye = (lax.broadcasted_iota(jnp.int32, (HEAD_DIM, HEAD_DIM), 0)
               == lax.broadcasted_iota(jnp.int32, (HEAD_DIM, HEAD_DIM), 1)).astype(bf16)
        q_t = _nt(eye, q).astype(bf16)
        do_t = _nt(eye, d_out).astype(bf16)
        ur = lax.broadcasted_iota(jnp.int32, (t, t), 0)
        uc = lax.broadcasted_iota(jnp.int32, (t, t), 1)
        ur2 = lax.broadcasted_iota(jnp.int32, (2 * t, t), 0) & (t - 1)
        incl = (ur2 <= lax.broadcasted_iota(jnp.int32, (2 * t, t), 1)).astype(bf16)
        before = (ur < uc).astype(bf16)

        def tile(j, r0, r1, dq, pre, dpre, masked):
            q_s, do_s, tot_s = q[r0:r1], d_out[r0:r1], total[r0:r1]
            n = q_s.shape[0]
            start = pl.multiple_of(j * t, t)
            k = k_ref[0, pl.ds(start, t), :]
            v = v_ref[0, pl.ds(start, t), :]
            z = _nt(q_s, k)
            lb, lom = _log_sigmoids(z)
            if masked:
                keep = lax.broadcasted_iota(jnp.int32, (n, t), 1) < lax.broadcasted_iota(jnp.int32, (n, t), 0)
                lom = jnp.where(keep, lom, 0.0)
            pin = jnp.dot(jnp.concatenate(_split_bf16(lom), axis=1), incl, preferred_element_type=f32)
            a = lb + ((tot_s - pre) - pin)
            if masked:
                a = jnp.where(keep, a, -jnp.inf)
            w = jnp.exp(a)
            d_a = _nt(do_s, v) * w
            d_lom_local = jnp.dot(d_a.astype(bf16), before, preferred_element_type=f32)
            d_lom = d_lom_local + dpre
            sig = jnp.exp(lb)
            dz = d_a * (1.0 - sig) - d_lom * sig
            if masked:
                dz = jnp.where(keep, dz, 0.0)
            dzb = dz.astype(bf16)
            dq = dq + jnp.dot(dzb, k, preferred_element_type=f32)
            dkt_ref[0, j] += jnp.dot(q_t[:, r0:r1], dzb, preferred_element_type=f32)
            dvt_ref[0, j] += jnp.dot(do_t[:, r0:r1], w.astype(bf16), preferred_element_type=f32)
            pre = pre + pin[:, t - 1:t]
            dpre = dpre + d_lom_local[:, t - 1:t] + d_a[:, t - 1:t]
            return dq, pre, dpre

        carry = (jnp.zeros((tq, HEAD_DIM), f32), jnp.zeros((tq, 1), f32), jnp.zeros((tq, 1), f32))
        def group(n, cr):
            for u in range(unroll):
                cr = tile(unroll * n + u, 0, tq, cr[0], cr[1], cr[2], False)
            return cr

        carry = lax.fori_loop(0, i * (nd // unroll), group, carry)
        for jj in range(nd):
            r0 = jj * t
            part = tile(i * nd + jj, r0, tq, *(c[r0:] for c in carry), True)
            carry = part if r0 == 0 else tuple(jnp.concatenate([c[:r0], p], axis=0) for c, p in zip(carry, part))
        dq_ref[0] = carry[0] * 0.125

    blk = pl.BlockSpec((1, tq, HEAD_DIM), lambda h, i: (h, i, 0))
    full_t = pl.BlockSpec((1, nk, HEAD_DIM, t), lambda h, i: (h, 0, 0, 0))
    return pl.pallas_call(
        body, name="attn_bwd", grid=(N_HEADS, seq // tq),
        in_specs=[blk, pl.BlockSpec((1, seq, HEAD_DIM), lambda h, i: (N_HEADS + h, 0, 0)),
                  pl.BlockSpec((1, seq, HEAD_DIM), lambda h, i: (2 * N_HEADS + h, 0, 0)),
                  pl.BlockSpec((1, tq, 2 * HEAD_DIM), lambda h, i: (h, i, 0)), blk],
        out_specs=(blk, full_t, full_t),
        out_shape=(jax.ShapeDtypeStruct((N_HEADS, seq, HEAD_DIM), f32),
                   jax.ShapeDtypeStruct((N_HEADS, nk, HEAD_DIM, t), f32), jax.ShapeDtypeStruct((N_HEADS, nk, HEAD_DIM, t), f32)),
        compiler_params=_params(("parallel", "arbitrary")),
    )(qkv, qkv, qkv, o_tot, d_o)


def _ssd_common(conv, dt_raw, dtb, alog):
    t = CHUNK
    sg = _sigmoid(conv)
    act = conv * sg
    dt_pre = dt_raw + dtb
    dt = _softplus(dt_pre)
    a = -jnp.exp(alog)
    row = lax.broadcasted_iota(jnp.int32, (t, t), 0)
    col = lax.broadcasted_iota(jnp.int32, (t, t), 1)
    causal = row >= col
    ac = _nn(causal.astype(f32), dt * a, HI)
    ac_t = _nt((row == col).astype(f32), ac, HI)
    ac_last = ac[t - 1:t, :]
    return sg, act, dt_pre, dt, a, causal, ac, ac_t, ac_last, jnp.exp(ac), jnp.exp(ac_last - ac), jnp.exp(ac_last)


def _ssd_fwd(xbc, dt_raw, conv_w, conv_b, dtb, alog, dsk, seq):
    t = CHUNK
    n_chunks = seq // t

    def body(x_ref, dt_ref, cw_ref, cb_ref, dtb_ref, al_ref, dsk_ref, conv_ref, y_ref, st_ref, prev, state):
        c = pl.program_id(0)

        @pl.when(c == 0)
        def _():
            prev[...] = jnp.zeros_like(prev)
            state[...] = jnp.zeros_like(state)

        cur = x_ref[...]
        pv = prev[...]
        rows = lax.broadcasted_iota(jnp.int32, (t, D_XBC), 0)
        conv = cur * cw_ref[CONV_K - 1:CONV_K, :] + cb_ref[...]
        for m in range(1, CONV_K):
            shifted = jnp.where(rows < m, pltpu.roll(pv, m, 0), pltpu.roll(cur, m, 0))
            conv = conv + shifted * cw_ref[CONV_K - 1 - m:CONV_K - m, :]
        prev[...] = cur
        conv_ref[...] = conv
        _, act, _, dt, _, _, ac, ac_t, _, e_ac, dte, cdec = _ssd_common(conv, dt_ref[...], dtb_ref[...], al_ref[...])
        xs = act[:, :D_SSM]
        wide = N_HEADS * t
        sel64 = (lax.broadcasted_iota(jnp.int32, (LANES, D_SSM), 1) // HEAD_DIM
                 == lax.broadcasted_iota(jnp.int32, (LANES, D_SSM), 0)).astype(f32)
        sel128 = (lax.broadcasted_iota(jnp.int32, (LANES, wide), 1) // t
                  == lax.broadcasted_iota(jnp.int32, (LANES, wide), 0)).astype(f32)
        xd_all = xs * _spread(dt, sel64)
        xdb = xd_all.astype(bf16)
        xdte_b = (xd_all * _spread(dte, sel64)).astype(bf16)
        seg_all = _spread(ac, sel128) - jnp.concatenate([jnp.broadcast_to(ac_t[h:h + 1, :], (t, t)) for h in range(N_HEADS)], axis=1)
        causal_all = (lax.broadcasted_iota(jnp.int32, (t, wide), 0) >= (lax.broadcasted_iota(jnp.int32, (t, wide), 1) & (t - 1)))
        lm_all = jnp.exp(jnp.where(causal_all, seg_all, -jnp.inf))
        bgs = [act[:, D_SSM + g * N_STATE:D_SSM + (g + 1) * N_STATE].astype(bf16) for g in range(N_GROUPS)]
        cgs = [act[:, D_SSM + (N_GROUPS + g) * N_STATE:D_SSM + (N_GROUPS + g + 1) * N_STATE].astype(bf16) for g in range(N_GROUPS)]
        gms = [_nt(cgs[g], bgs[g]) for g in range(N_GROUPS)]
        mm_b = (jnp.concatenate([gms[h // HEADS_PER_GROUP] for h in range(N_HEADS)], axis=1) * lm_all).astype(bf16)
        y_diags, zos = [], []
        for h in range(N_HEADS):
            g = h // HEADS_PER_GROUP
            hs = slice(h * HEAD_DIM, (h + 1) * HEAD_DIM)
            hp = state[h]
            st_ref[0, h] = hp
            y_diags.append(jnp.dot(mm_b[:, h * t:(h + 1) * t], xdb[:, hs], preferred_element_type=f32))
            zos.append(_nt(cgs[g], hp.astype(bf16)))
            state[h] = hp * cdec[:, h:h + 1] + _tn(xdte_b[:, hs], bgs[g])
        y_ref[...] = (jnp.concatenate(y_diags, axis=1) + jnp.concatenate(zos, axis=1) * _spread(e_ac, sel64)
                      + xs * _spread(dsk_ref[...], sel64))

    row = lambda c: (c, 0)
    fixed = lambda c: (0, 0)
    return pl.pallas_call(
        body, name="ssd_fwd", grid=(n_chunks,),
        in_specs=[pl.BlockSpec((t, D_XBC), row), pl.BlockSpec((t, LANES), row), pl.BlockSpec((CONV_K, D_XBC), fixed),
                  pl.BlockSpec((1, D_XBC), fixed), pl.BlockSpec((1, LANES), fixed), pl.BlockSpec((1, LANES), fixed),
                  pl.BlockSpec((1, LANES), fixed)],
        out_specs=(pl.BlockSpec((t, D_XBC), row), pl.BlockSpec((t, D_SSM), row),
                   pl.BlockSpec((1, N_HEADS, HEAD_DIM, N_STATE), lambda c: (c, 0, 0, 0))),
        out_shape=(jax.ShapeDtypeStruct((seq, D_XBC), f32), jax.ShapeDtypeStruct((seq, D_SSM), f32),
                   jax.ShapeDtypeStruct((n_chunks, N_HEADS, HEAD_DIM, N_STATE), f32)),
        scratch_shapes=[pltpu.VMEM((t, D_XBC), f32), pltpu.VMEM((N_HEADS, HEAD_DIM, N_STATE), f32)],
        compiler_params=_params(("arbitrary",)),
    )(xbc, dt_raw, conv_w, conv_b, dtb, alog, dsk)


def _ssd_bwd(dy, conv, xbc, dt_raw, states, conv_w, dtb, alog, dsk, seq):
    t = CHUNK
    n_chunks = seq // t

    def body(dy_ref, conv_ref, x_ref, dt_ref, st_ref, cw_ref, dtb_ref, al_ref, dsk_ref,
             dx_ref, ddt_ref, gcw_ref, gcb_ref, gdtb_ref, gal_ref, gdsk_ref, d_state, d_conv_next):
        c = pl.program_id(0)

        @pl.when(c == 0)
        def _():
            d_state[...] = jnp.zeros_like(d_state)
            d_conv_next[...] = jnp.zeros_like(d_conv_next)
            gcw_ref[...] = jnp.zeros_like(gcw_ref)
            gcb_ref[...] = jnp.zeros_like(gcb_ref)
            gdtb_ref[...] = jnp.zeros_like(gdtb_ref)
            gal_ref[...] = jnp.zeros_like(gal_ref)
            gdsk_ref[...] = jnp.zeros_like(gdsk_ref)

        conv = conv_ref[...]
        sg, act, dt_pre, dt, a, causal, ac, ac_t, _, e_ac, dte, cdec = _ssd_common(conv, dt_ref[...], dtb_ref[...], al_ref[...])
        dyv = dy_ref[...]
        xs = act[:, :D_SSM]
        sel64 = lax.broadcasted_iota(jnp.int32, (LANES, D_SSM), 1) // HEAD_DIM == lax.broadcasted_iota(jnp.int32, (LANES, D_SSM), 0)
        sel64_t = lax.broadcasted_iota(jnp.int32, (D_SSM, LANES), 0) // HEAD_DIM == lax.broadcasted_iota(jnp.int32, (D_SSM, LANES), 1)
        wide = N_HEADS * t
        sel128 = lax.broadcasted_iota(jnp.int32, (LANES, wide), 1) // t == lax.broadcasted_iota(jnp.int32, (LANES, wide), 0)
        sel128_t = lax.broadcasted_iota(jnp.int32, (wide, LANES), 0) // t == lax.broadcasted_iota(jnp.int32, (wide, LANES), 1)

        spread = _spread

        def lane_sums(v, sel_t):
            hi, lo = _split_bf16(v)
            sel_b = sel_t.astype(bf16)
            return jnp.dot(hi, sel_b, preferred_element_type=f32) + jnp.dot(lo, sel_b, preferred_element_type=f32)

        dt_x = spread(dt, sel64)
        e_x = spread(e_ac, sel64)
        dte_x = spread(dte, sel64)
        seg_all = spread(ac, sel128) - jnp.concatenate([jnp.broadcast_to(ac_t[h:h + 1, :], (t, t)) for h in range(N_HEADS)], axis=1)
        causal_all = (lax.broadcasted_iota(jnp.int32, (t, wide), 0) >= (lax.broadcasted_iota(jnp.int32, (t, wide), 1) & (t - 1)))
        lm_all = jnp.exp(jnp.where(causal_all, seg_all, -jnp.inf))
        xd_all = xs * dt_x
        xdb = xd_all.astype(bf16)
        xdte_b = (xd_all * dte_x).astype(bf16)
        d_yb = dyv.astype(bf16)
        d_zo_all = dyv * e_x
        d_zob = d_zo_all.astype(bf16)
        bgs = [act[:, D_SSM + g * N_STATE:D_SSM + (g + 1) * N_STATE].astype(bf16) for g in range(N_GROUPS)]
        cgs = [act[:, D_SSM + (N_GROUPS + g) * N_STATE:D_SSM + (N_GROUPS + g + 1) * N_STATE].astype(bf16) for g in range(N_GROUPS)]
        gms = [_nt(cgs[g], bgs[g]) for g in range(N_GROUPS)]
        mm_all = jnp.concatenate([gms[h // HEADS_PER_GROUP] for h in range(N_HEADS)], axis=1) * lm_all
        mm_b = mm_all.astype(bf16)
        d_hn_all = d_state[...].reshape(N_HEADS * HEAD_DIM, N_STATE)
        hp_all = st_ref[0].reshape(N_HEADS * HEAD_DIM, N_STATE)
        state_dot = lane_sums(d_hn_all * hp_all, jnp.ones((N_STATE, LANES), jnp.bool_))
        state_dot = jnp.sum(jnp.where(sel64_t, state_dot, 0.0), axis=0, keepdims=True)
        d_mms, d_xds, zos, d_ws = [], [], [], []
        d_bs = [jnp.zeros((t, N_STATE), f32) for _ in range(N_GROUPS)]
        d_cs = [jnp.zeros((t, N_STATE), f32) for _ in range(N_GROUPS)]
        for h in range(N_HEADS):
            g = h // HEADS_PER_GROUP
            hs = slice(h * HEAD_DIM, (h + 1) * HEAD_DIM)
            hpb = st_ref[0, h].astype(bf16)
            d_hn = d_state[h]
            d_hnb = d_hn.astype(bf16)
            d_mms.append(_nt(d_yb[:, hs], xdb[:, hs]))
            d_xds.append(_tn(mm_b[:, h * t:(h + 1) * t], d_yb[:, hs]))
            zos.append(_nt(cgs[g], hpb))
            d_cs[g] = d_cs[g] + jnp.dot(d_zob[:, hs], hpb, preferred_element_type=f32)
            d_state[h] = _tn(d_zob[:, hs], cgs[g]) + d_hn * cdec[:, h:h + 1]
            d_ws.append(_nt(bgs[g], d_hnb))
            d_bs[g] = d_bs[g] + jnp.dot(xdte_b[:, hs], d_hnb, preferred_element_type=f32)
        d_mm_all = jnp.concatenate(d_mms, axis=1)
        d_seg_all = d_mm_all * mm_all
        d_gm_all = d_mm_all * lm_all
        dwd_all = jnp.concatenate(d_ws, axis=1) * dte_x
        d_xd_all = jnp.concatenate(d_xds, axis=1) + dwd_all
        decay_part = dwd_all * xd_all
        decay_sums = lane_sums(decay_part, sel64_t)
        d_ac = lane_sums(d_seg_all, sel128_t) + lane_sums(d_zo_all * jnp.concatenate(zos, axis=1) - decay_part, sel64_t)
        d_dt = lane_sums(d_xd_all * xs, sel64_t)
        g_dsk = jnp.sum(lane_sums(dyv * xs, sel64_t), axis=0, keepdims=True)
        stacked = jnp.concatenate([d_seg_all[:, h * t:(h + 1) * t] for h in range(N_HEADS)], axis=0)
        s_hi, s_lo = _split_bf16(stacked)
        sel_b = sel128.astype(bf16)
        col_sums = jnp.dot(sel_b, s_hi, preferred_element_type=f32) + jnp.dot(sel_b, s_lo, preferred_element_type=f32)
        d_last = state_dot * cdec + jnp.sum(decay_sums, axis=0, keepdims=True)
        last_row = (lax.broadcasted_iota(jnp.int32, (t, 1), 0) == t - 1).astype(f32)
        sq_row = lax.broadcasted_iota(jnp.int32, (t, t), 0)
        sq_col = lax.broadcasted_iota(jnp.int32, (t, t), 1)
        d_ac = d_ac - _nt((sq_row == sq_col).astype(f32), col_sums, HI) + last_row * d_last
        d_ld = _nn((sq_col >= sq_row).astype(f32), d_ac, HI)
        dxs = [dyv * spread(dsk_ref[...], sel64) + d_xd_all * dt_x]
        dbs, dcs = [], []
        for g in range(N_GROUPS):
            d_gm = d_gm_all[:, g * HEADS_PER_GROUP * t:g * HEADS_PER_GROUP * t + t]
            for r in range(1, HEADS_PER_GROUP):
                d_gm = d_gm + d_gm_all[:, (g * HEADS_PER_GROUP + r) * t:(g * HEADS_PER_GROUP + r + 1) * t]
            d_gmb = d_gm.astype(bf16)
            dcs.append(d_cs[g] + jnp.dot(d_gmb, bgs[g], preferred_element_type=f32))
            dbs.append(d_bs[g] + _tn(d_gmb, cgs[g]))
        d_dt = d_dt + d_ld * a
        gal_ref[...] += jnp.sum(d_ld * dt, axis=0, keepdims=True) * a
        gdsk_ref[...] += g_dsk
        d_dt_raw = d_dt * _sigmoid(dt_pre)
        ddt_ref[...] = d_dt_raw.astype(bf16)
        gdtb_ref[...] += jnp.sum(d_dt_raw, axis=0, keepdims=True)
        d_act = jnp.concatenate(dxs + dbs + dcs, axis=1)
        d_conv = d_act * (sg * (1.0 + conv * (1.0 - sg)))
        gcb_ref[...] += jnp.sum(d_conv, axis=0, keepdims=True)
        nxt = d_conv_next[...]
        rows = lax.broadcasted_iota(jnp.int32, (t, D_XBC), 0)
        xraw = x_ref[...]
        d_x = d_conv * cw_ref[CONV_K - 1:CONV_K, :]
        gcw_ref[pl.ds(CONV_K - 1, 1), :] += jnp.sum(xraw * d_conv, axis=0, keepdims=True)
        for m in range(1, CONV_K):
            ahead = jnp.where(rows >= t - m, pltpu.roll(nxt, t - m, 0), pltpu.roll(d_conv, t - m, 0))
            d_x = d_x + ahead * cw_ref[CONV_K - 1 - m:CONV_K - m, :]
            gcw_ref[pl.ds(CONV_K - 1 - m, 1), :] += jnp.sum(xraw * ahead, axis=0, keepdims=True)
        d_conv_next[...] = d_conv
        dx_ref[...] = d_x.astype(bf16)

    rev = lambda c: (n_chunks - 1 - c, 0)
    fixed = lambda c: (0, 0)
    return pl.pallas_call(
        body, name="ssd_bwd", grid=(n_chunks,),
        in_specs=[pl.BlockSpec((t, D_SSM), rev), pl.BlockSpec((t, D_XBC), rev), pl.BlockSpec((t, D_XBC), rev),
                  pl.BlockSpec((t, LANES), rev), pl.BlockSpec((1, N_HEADS, HEAD_DIM, N_STATE), lambda c: (n_chunks - 1 - c, 0, 0, 0)),
                  pl.BlockSpec((CONV_K, D_XBC), fixed), pl.BlockSpec((1, LANES), fixed), pl.BlockSpec((1, LANES), fixed),
                  pl.BlockSpec((1, LANES), fixed)],
        out_specs=(pl.BlockSpec((t, D_XBC), rev), pl.BlockSpec((t, LANES), rev), pl.BlockSpec((CONV_K, D_XBC), fixed),
                   pl.BlockSpec((1, D_XBC), fixed), pl.BlockSpec((1, LANES), fixed), pl.BlockSpec((1, LANES), fixed),
                   pl.BlockSpec((1, LANES), fixed)),
        out_shape=(jax.ShapeDtypeStruct((seq, D_XBC), bf16), jax.ShapeDtypeStruct((seq, LANES), bf16),
                   jax.ShapeDtypeStruct((CONV_K, D_XBC), f32), jax.ShapeDtypeStruct((1, D_XBC), f32),
                   jax.ShapeDtypeStruct((1, LANES), f32), jax.ShapeDtypeStruct((1, LANES), f32), jax.ShapeDtypeStruct((1, LANES), f32)),
        scratch_shapes=[pltpu.VMEM((N_HEADS, HEAD_DIM, N_STATE), f32), pltpu.VMEM((t, D_XBC), f32)],
        compiler_params=_params(("arbitrary",)),
    )(dy, conv, xbc, dt_raw, states, conv_w, dtb, alog, dsk)


def _heads_to_cols(ref, width=HEAD_DIM):
    return jnp.concatenate([ref[h][:, :width] for h in range(N_HEADS)], axis=1)


def _silu_and_grad(z):
    sg = _sigmoid(z)
    return z * sg, sg * (1.0 + z * (1.0 - sg))


def _rms(v):
    return lax.rsqrt(jnp.mean(v * v, axis=-1, keepdims=True) + EPS)


def _rms_bwd(d_hat, hat, r):
    return r * (d_hat - hat * jnp.mean(d_hat * hat, axis=-1, keepdims=True))


def _post(x, target, o_tot, y, za, zs, w_out, gate, g_sb, g_ssm, g_f, seq):
    ts = 256

    def body(x_ref, t_ref, o_ref, y_ref, za_ref, zs_ref, w_ref, gate_ref, gsb_ref, gss_ref, gf_ref,
             ycat_t_ref, dmix_ref, dx2_ref, loss_ref, gnf_ref, dgate_ref, do_ref, dza_ref, dzs_ref, dy_ref, ggsb_ref, ggss_ref):
        @pl.when(pl.program_id(0) == 0)
        def _():
            for ref in (loss_ref, gnf_ref, dgate_ref, ggsb_ref, ggss_ref):
                ref[...] = jnp.zeros_like(ref)

        o = _heads_to_cols(o_ref)
        ro = _rms(o)
        sa, dsa = _silu_and_grad(za_ref[...])
        gsb = gsb_ref[...]
        ya = (o * ro * gsb) * sa
        yv = y_ref[...]
        sz, dsz = _silu_and_grad(zs_ref[...])
        u = yv * sz
        ru = _rms(u)
        gss = gss_ref[...]
        ys = u * ru * gss
        yab, ysb = ya.astype(bf16), ys.astype(bf16)
        _store_transposed(ycat_t_ref.at[:D_ATTN], yab)
        _store_transposed(ycat_t_ref.at[D_ATTN:], ysb)
        mixed = (jnp.dot(yab, w_ref[:D_ATTN, :], preferred_element_type=f32)
                 + jnp.dot(ysb, w_ref[D_ATTN:, :], preferred_element_type=f32))
        gate_v = gate_ref[...]
        x2 = x_ref[...] + gate_v * mixed
        r2 = _rms(x2)
        xh = x2 * r2
        gf = gf_ref[...]
        diff = xh * gf - t_ref[...]
        loss_ref[...] += jnp.sum(diff * diff) * (0.5 / D_MODEL)
        d_out = diff * (1.0 / D_MODEL)
        gnf_ref[...] += jnp.sum(d_out * xh, axis=0, keepdims=True)
        dx2 = _rms_bwd(d_out * gf, xh, r2)
        dx2_ref[...] = dx2
        dgate_ref[...] += jnp.sum(dx2 * mixed, axis=0, keepdims=True)
        dm = (dx2 * gate_v).astype(bf16)
        dmix_ref[...] = dm
        d_ya = _nt(dm, w_ref[:D_ATTN, :])
        d_ys = _nt(dm, w_ref[D_ATTN:, :])
        oh = o * ro
        dza_ref[...] = (d_ya * oh * gsb * dsa).astype(bf16)
        ggsb_ref[...] += jnp.sum(d_ya * oh * sa, axis=0, keepdims=True)
        d_o = _rms_bwd(d_ya * gsb * sa, oh, ro)
        for h in range(N_HEADS):
            do_ref[h] = d_o[:, h * HEAD_DIM:(h + 1) * HEAD_DIM].astype(bf16)
        uh = u * ru
        ggss_ref[...] += jnp.sum(d_ys * uh, axis=0, keepdims=True)
        du = _rms_bwd(d_ys * gss, uh, ru)
        dy_ref[...] = du * sz
        dzs_ref[...] = (du * yv * dsz).astype(bf16)

    row = lambda i: (i, 0)
    fixed = lambda i: (0, 0)
    vec = pl.BlockSpec((1, D_MODEL), fixed)
    return pl.pallas_call(
        body, name="post", grid=(seq // ts,),
        in_specs=[pl.BlockSpec((ts, D_MODEL), row), pl.BlockSpec((ts, D_MODEL), row),
                  pl.BlockSpec((N_HEADS, ts, 2 * HEAD_DIM), lambda i: (0, i, 0)), pl.BlockSpec((ts, D_SSM), row),
                  pl.BlockSpec((ts, D_ATTN), row), pl.BlockSpec((ts, D_SSM), row), pl.BlockSpec((D_ATTN + D_SSM, D_MODEL), fixed),
                  vec, vec, vec, vec],
        out_specs=(pl.BlockSpec((D_ATTN + D_SSM, ts), lambda i: (0, i)), pl.BlockSpec((ts, D_MODEL), row), pl.BlockSpec((ts, D_MODEL), row),
                   pl.BlockSpec((1, LANES), fixed), vec, vec,
                   pl.BlockSpec((N_HEADS, ts, HEAD_DIM), lambda i: (0, i, 0)), pl.BlockSpec((ts, D_ATTN), row),
                   pl.BlockSpec((ts, D_SSM), row), pl.BlockSpec((ts, D_SSM), row), vec, vec),
        out_shape=(jax.ShapeDtypeStruct((D_ATTN + D_SSM, seq), bf16), jax.ShapeDtypeStruct((seq, D_MODEL), bf16),
                   jax.ShapeDtypeStruct((seq, D_MODEL), f32), jax.ShapeDtypeStruct((1, LANES), f32),
                   jax.ShapeDtypeStruct((1, D_MODEL), f32), jax.ShapeDtypeStruct((1, D_MODEL), f32),
                   jax.ShapeDtypeStruct((N_HEADS, seq, HEAD_DIM), bf16), jax.ShapeDtypeStruct((seq, D_ATTN), bf16),
                   jax.ShapeDtypeStruct((seq, D_SSM), bf16), jax.ShapeDtypeStruct((seq, D_SSM), f32),
                   jax.ShapeDtypeStruct((1, D_MODEL), f32), jax.ShapeDtypeStruct((1, D_MODEL), f32)),
        compiler_params=_params(("arbitrary",)),
    )(x, target, o_tot, y, za, zs, w_out, gate, g_sb, g_ssm, g_f)


def _qkv_grads_to_cols(dq, dkt, dvt, seq):
    ts = 256
    nb = ts // CHUNK

    def body(dq_ref, dkt_ref, dvt_ref, out_ref):
        out_ref[:, :D_ATTN] = _heads_to_cols(dq_ref).astype(bf16)
        eye = (lax.broadcasted_iota(jnp.int32, (CHUNK, CHUNK), 0) == lax.broadcasted_iota(jnp.int32, (CHUNK, CHUNK), 1)).astype(bf16)
        for p, ref in ((1, dkt_ref), (2, dvt_ref)):
            for b in range(nb):
                cols = [_nt(eye, ref[h, b].astype(bf16)) for h in range(N_HEADS)]
                out_ref[b * CHUNK:(b + 1) * CHUNK, p * D_ATTN:(p + 1) * D_ATTN] = jnp.concatenate(cols, axis=1).astype(bf16)

    blk = pl.BlockSpec((N_HEADS, ts, HEAD_DIM), lambda i: (0, i, 0))
    blk_t = pl.BlockSpec((N_HEADS, nb, HEAD_DIM, CHUNK), lambda i: (0, i, 0, 0))
    return pl.pallas_call(
        body, name="qkv_grads_to_cols", grid=(seq // ts,), in_specs=[blk, blk_t, blk_t],
        out_specs=pl.BlockSpec((ts, 3 * D_ATTN), lambda i: (i, 0)),
        out_shape=jax.ShapeDtypeStruct((seq, 3 * D_ATTN), bf16), compiler_params=_params(("parallel",)),
    )(dq, dkt, dvt)


def _bwd_in(dqkv, dza, dzs, dxbc, ddt, wp, x, dx2, gain, scale, seq):
    ts = 256
    pieces = ((0, 0, 3 * D_ATTN), (1, OFF_ZA, D_ATTN), (2, OFF_ZS, D_SSM), (3, OFF_XBC, D_XBC), (4, OFF_DT, LANES))

    def body(dqkv_ref, dza_ref, dzs_ref, dxbc_ref, ddt_ref, w_ref, x_ref, dx2_ref, g_ref, sc_ref,
             gx_ref, dshift_ref, dscale_ref, ggain_ref):
        @pl.when(pl.program_id(0) == 0)
        def _():
            dshift_ref[...] = jnp.zeros_like(dshift_ref)
            dscale_ref[...] = jnp.zeros_like(dscale_ref)
            ggain_ref[...] = jnp.zeros_like(ggain_ref)

        refs = (dqkv_ref, dza_ref, dzs_ref, dxbc_ref, ddt_ref)
        dh = jnp.zeros((ts, D_MODEL), f32)
        for idx, off, width in pieces:
            for cc in range(0, width, 512):
                wd = min(512, width - cc)
                dh = dh + _nt(refs[idx][:, cc:cc + wd], w_ref[:, off + cc:off + cc + wd])
        xv = x_ref[...]
        r = _rms(xv)
        xh = xv * r
        g = g_ref[...]
        dshift_ref[...] += jnp.sum(dh, axis=0, keepdims=True)
        dscale_ref[...] += jnp.sum(dh * xh * g, axis=0, keepdims=True)
        tt = dh * (1.0 + sc_ref[...])
        ggain_ref[...] += jnp.sum(tt * xh, axis=0, keepdims=True)
        gx_ref[...] = dx2_ref[...] + _rms_bwd(tt * g, xh, r)

    row = lambda i: (i, 0)
    fixed = lambda i: (0, 0)
    vec = pl.BlockSpec((1, D_MODEL), fixed)
    return pl.pallas_call(
        body, name="bwd_in", grid=(seq // ts,),
        in_specs=[pl.BlockSpec((ts, 3 * D_ATTN), row), pl.BlockSpec((ts, D_ATTN), row), pl.BlockSpec((ts, D_SSM), row),
                  pl.BlockSpec((ts, D_XBC), row), pl.BlockSpec((ts, LANES), row), pl.BlockSpec((D_MODEL, D_PROJ_P), fixed),
                  pl.BlockSpec((ts, D_MODEL), row), pl.BlockSpec((ts, D_MODEL), row), vec, vec],
        out_specs=(pl.BlockSpec((ts, D_MODEL), row), vec, vec, vec),
        out_shape=(jax.ShapeDtypeStruct((seq, D_MODEL), f32), jax.ShapeDtypeStruct((1, D_MODEL), f32),
                   jax.ShapeDtypeStruct((1, D_MODEL), f32), jax.ShapeDtypeStruct((1, D_MODEL), f32)),
        compiler_params=_params(("arbitrary",)),
    )(dqkv, dza, dzs, dxbc, ddt, wp, x, dx2, gain, scale)


def _grad_w(a_t, b, tn, name):
    m, seq = a_t.shape
    n = b.shape[1]
    tk = min(1024, seq)
    n_k = seq // tk

    def body(a_ref, b_ref, o_ref, acc):
        @pl.when(pl.program_id(1) == 0)
        def _():
            acc[...] = jnp.zeros_like(acc)

        acc[...] += jnp.dot(a_ref[...], b_ref[...], preferred_element_type=f32)

        @pl.when(pl.program_id(1) == n_k - 1)
        def _():
            o_ref[...] = acc[...].astype(bf16)

    return pl.pallas_call(
        body, name=name, grid=(n // tn, n_k),
        in_specs=[pl.BlockSpec((m, tk), lambda j, k: (0, k)), pl.BlockSpec((tk, tn), lambda j, k: (k, j))],
        out_specs=pl.BlockSpec((m, tn), lambda j, k: (0, j)),
        out_shape=jax.ShapeDtypeStruct((m, n), bf16), scratch_shapes=[pltpu.VMEM((m, tn), f32)],
        compiler_params=_params(("parallel", "arbitrary")),
    )(a_t, b)


def _small_finish(g_all, c_all, dmod_mine):
    def body(g_ref, c_ref, dm_ref, tot_ref, gwada_ref):
        tot = g_ref[0:1, :]
        for j in range(1, N_DEV):
            tot = tot + g_ref[j:j + 1, :]
        tot_ref[...] = tot
        cv = c_ref[...]
        gwada_ref[...] = _tn(cv * _sigmoid(cv), dm_ref[...], HI)

    vmem = pl.BlockSpec(memory_space=pltpu.VMEM)
    return pl.pallas_call(
        body, name="small_finish", in_specs=[vmem, vmem, vmem], out_specs=(vmem, vmem),
        out_shape=(jax.ShapeDtypeStruct((1, N_PACK), f32), jax.ShapeDtypeStruct((D_MODEL, dmod_mine.shape[1]), f32)),
        compiler_params=_params(),
    )(g_all, c_all, dmod_mine)


def _adamw(w, g_parts, m, v, rows, name):
    r, c = w.shape
    n_parts = g_parts.shape[0]
    bc1 = 1.0 - ADAM_B1 ** ADAM_STEP
    bc2 = 1.0 - ADAM_B2 ** ADAM_STEP

    def body(w_ref, g_ref, m_ref, v_ref, go_ref, d_ref, mo_ref, vo_ref):
        g = g_ref[0].astype(f32)
        for j in range(1, n_parts):
            g = g + g_ref[j].astype(f32)
        go_ref[...] = g
        mn = ADAM_B1 * m_ref[...] + (1.0 - ADAM_B1) * g
        vn = ADAM_B2 * v_ref[...] + (1.0 - ADAM_B2) * (g * g)
        mo_ref[...] = mn
        vo_ref[...] = vn
        d_ref[...] = -ADAM_LR * ((mn / bc1) / (jnp.sqrt(vn / bc2) + ADAM_EPS) + ADAM_WD * w_ref[...])

    blk = pl.BlockSpec((rows, c), lambda i: (i, 0))
    return pl.pallas_call(
        body, name=name, grid=(r // rows,),
        in_specs=[blk, pl.BlockSpec((n_parts, rows, c), lambda i: (0, i, 0)), blk, blk],
        out_specs=(blk, blk, blk, blk), out_shape=(jax.ShapeDtypeStruct((r, c), f32),) * 4,
        compiler_params=_params(("parallel",)),
    )(w, g_parts, m, v)


def _pad_lanes(v):
    return jnp.pad(v, ((0, 0), (0, LANES - v.shape[1])))


def kernel(x, c, w_ada, b_ada, norm_in_gain, w_in, conv_w, conv_b, dt_bias, a_log, d_skip, sb_norm_gain, ssm_norm_gain, w_out, norm_f_gain, loss_target, m_w_ada, m_b_ada, m_norm_in_gain, m_w_in, m_conv_w, m_conv_b, m_dt_bias, m_a_log, m_d_skip, m_sb_norm_gain, m_ssm_norm_gain, m_w_out, m_norm_f_gain, v_w_ada, v_b_ada, v_norm_in_gain, v_w_in, v_conv_w, v_conv_b, v_dt_bias, v_a_log, v_d_skip, v_sb_norm_gain, v_ssm_norm_gain, v_w_out, v_norm_f_gain):
    seq = x.shape[1]
    xs = x[0]
    tgt = loss_target[0]
    _, my_slot = _me()

    mod, c_all = _mod_exchange(c, w_ada[0], b_ada)
    shift, scale, gate = mod[:, :D_MODEL], mod[:, D_MODEL:2 * D_MODEL], mod[:, 2 * D_MODEL:]
    w_in_g, w_out_g, conv_w_g = _all_gather_two_level(
        [_cast_bf16(w_in[0], 128), _cast_bf16(w_out[0], 128), conv_w[0]], "gather_weights")
    w_full = jnp.transpose(w_in_g, (1, 0, 2)).reshape(D_MODEL, D_PROJ)
    wp = jnp.concatenate([w_full[:, :4 * D_ATTN], w_full[:, D_PROJ - D_SSM:], w_full[:, 4 * D_ATTN:4 * D_ATTN + D_XBC],
                          _pad_lanes(w_full[:, 4 * D_ATTN + D_XBC:4 * D_ATTN + D_XBC + N_HEADS])], axis=1)
    w_out_full = w_out_g.reshape(D_ATTN + D_SSM, D_MODEL)
    conv_w_full = jnp.transpose(conv_w_g, (1, 0, 2)).reshape(CONV_K, D_XBC)
    dtb, alog, dsk = _pad_lanes(dt_bias), _pad_lanes(a_log), _pad_lanes(d_skip)

    h_t, qkv, za, zs, xbc, dt_raw = _proj(xs, norm_in_gain, scale, shift, wp, seq)
    o_tot = _attn_fwd(qkv, seq)
    conv, y, states = _ssd_fwd(xbc, dt_raw, conv_w_full, conv_b, dtb, alog, dsk, seq)
    ycat_t, dmix, dx2, loss_p, g_nf, d_gate, d_o, dza, dzs, dy, g_sb, g_ss = _post(
        xs, tgt, o_tot, y, za, zs, w_out_full, gate, sb_norm_gain, ssm_norm_gain, norm_f_gain.reshape(1, D_MODEL), seq)

    dq, dk, dv = _attn_bwd(qkv, o_tot, d_o, seq)
    dxbc, ddt, g_cw, g_cb, g_dtb, g_al, g_dsk = _ssd_bwd(dy, conv, xbc, dt_raw, states, conv_w_full, dtb, alog, dsk, seq)
    dqkv = _qkv_grads_to_cols(dq, dk, dv, seq)
    grad_x, d_shift, d_scale, g_in = _bwd_in(dqkv, dza, dzs, dxbc, ddt, wp, xs, dx2, norm_in_gain, scale, seq)
    gw_qkv = _grad_w(h_t, dqkv, D_XBC, "grad_w_qkv")
    gw_za = _grad_w(h_t, dza, D_ATTN, "grad_w_za")
    gw_zs = _grad_w(h_t, dzs, D_SSM, "grad_w_zs")
    gw_xbc = _grad_w(h_t, dxbc, D_XBC, "grad_w_xbc")
    gw_dt = _grad_w(h_t, ddt, LANES, "grad_w_dt")
    gw_out = _grad_w(ycat_t, dmix, D_MODEL, "grad_w_out")
    gw_in = jnp.concatenate([gw_qkv, gw_za, gw_xbc, gw_dt[:, :N_HEADS], gw_zs], axis=1)

    gw_in_parts, gw_out_parts = _reduce_scatter_two_level(
        [jnp.transpose(gw_in.reshape(D_MODEL, N_DEV, W_IN_SHARD), (1, 0, 2)),
         gw_out.reshape(N_DEV, (D_ATTN + D_SSM) // N_DEV, D_MODEL)], "scatter_grads")
    packed = jnp.concatenate([loss_p, d_shift, d_scale, d_gate, g_in, g_cb, g_dtb, g_al, g_dsk, g_sb, g_ss, g_nf,
                              g_cw.reshape(1, CONV_K * D_XBC)], axis=1)
    (packed_all,) = _all_gather([packed], "gather_small")
    packed_all = packed_all.reshape(N_DEV, N_PACK)
    n_ada = w_ada.shape[2]
    dmod_mine = lax.dynamic_slice(packed_all, (0, P_DMOD + my_slot * n_ada), (N_DEV, n_ada))
    tot, g_w_ada = _small_finish(packed_all, c_all.reshape(N_DEV, D_MODEL), dmod_mine)

    def big(w, parts, m, v, rows, name):
        return tuple(t[None] for t in _adamw(w[0], parts, m[0], v[0], rows, name))

    small_names = ["b_ada", "norm_in_gain", "conv_b", "dt_bias", "a_log", "d_skip", "sb_norm_gain", "ssm_norm_gain", "norm_f_gain"]
    given = {"b_ada": (b_ada, m_b_ada, v_b_ada), "norm_in_gain": (norm_in_gain, m_norm_in_gain, v_norm_in_gain),
             "conv_b": (conv_b, m_conv_b, v_conv_b), "dt_bias": (dt_bias, m_dt_bias, v_dt_bias), "a_log": (a_log, m_a_log, v_a_log),
             "d_skip": (d_skip, m_d_skip, v_d_skip), "sb_norm_gain": (sb_norm_gain, m_sb_norm_gain, v_sb_norm_gain),
             "ssm_norm_gain": (ssm_norm_gain, m_ssm_norm_gain, v_ssm_norm_gain), "norm_f_gain": (norm_f_gain, m_norm_f_gain, v_norm_f_gain)}

    def pack(which):
        cols = []
        for nm in small_names:
            t = given[nm][which].reshape(1, -1)
            cols.append(_pad_lanes(t) if t.shape[1] < LANES else t)
        return jnp.concatenate(cols, axis=1)

    packed_out = _adamw(pack(0), tot[:, P_DMOD:P_CW][None], pack(1), pack(2), 1, "adamw_small")
    res = {}
    off = 0
    for nm in small_names:
        shape = given[nm][0].shape
        size = given[nm][0].size
        res[nm] = tuple(t[:, off:off + size].reshape(shape) for t in packed_out)
        off += max(size, LANES)
    n_cw = conv_w.shape[2]
    g_cw_mine = lax.dynamic_slice(tot[:, P_CW:].reshape(CONV_K, D_XBC), (0, my_slot * n_cw), (CONV_K, n_cw))
    res["conv_w"] = tuple(t.reshape(conv_w.shape) for t in _adamw(conv_w.reshape(1, -1), g_cw_mine.reshape(1, 1, -1),
                                                                  m_conv_w.reshape(1, -1), v_conv_w.reshape(1, -1), 1, "adamw_conv_w"))
    res["w_ada"] = big(w_ada, g_w_ada[None], m_w_ada, v_w_ada, 128, "adamw_w_ada")
    res["w_in"] = big(w_in, gw_in_parts, m_w_in, v_w_in, 128, "adamw_w_in")
    res["w_out"] = big(w_out, gw_out_parts, m_w_out, v_w_out, 64, "adamw_w_out")
    names = ["w_ada", "b_ada", "norm_in_gain", "w_in", "conv_w", "conv_b", "dt_bias", "a_log", "d_skip", "sb_norm_gain",
             "ssm_norm_gain", "w_out", "norm_f_gain"]
    loss = tot[0, P_LOSS]
    return (loss, grad_x[None], *[res[n][0] for n in names], *[res[n][1] for n in names],
            *[res[n][2] for n in names], *[res[n][3] for n in names])
```

```python
import functools

import jax
import jax.numpy as jnp
from jax import lax
from jax.experimental import pallas as pl
from jax.experimental.pallas import tpu as pltpu

f32 = jnp.float32
bf16 = jnp.bfloat16
MESH = pl.DeviceIdType.MESH
HI = lax.Precision.HIGHEST

N_DEV = 8
D_MODEL = 1024
D_ATTN = 1024
D_SSM = 1024
N_HEADS = 16
HEAD_DIM = 64
N_GROUPS = 2
HEADS_PER_GROUP = 8
N_STATE = 128
D_XBC = D_SSM + 2 * N_GROUPS * N_STATE
D_PROJ = 4 * D_ATTN + D_XBC + N_HEADS + D_SSM
W_IN_SHARD = D_PROJ // N_DEV
CONV_K = 4
CHUNK = 128
ATTN_Q_ROWS = 2048
ATTN_UNROLL = 16
LANES = 128
EPS = 1e-6
OFF_ZA = 3072
OFF_ZS = 4096
OFF_XBC = 5120
OFF_DT = 6656
D_PROJ_P = 6784
VMEM_LIMIT_BYTES = 56 * 1024 * 1024

ADAM_LR = 0.001
ADAM_B1 = 0.9
ADAM_B2 = 0.999
ADAM_EPS = 1e-08
ADAM_WD = 0.01
ADAM_STEP = 10

P_LOSS = 0
P_DMOD = 128
P_GIN = 3200
P_CB = 4224
P_DTB = 5760
P_ALOG = 5888
P_DSK = 6016
P_GSB = 6144
P_GSS = 7168
P_GNF = 8192
P_CW = 9216
N_PACK = 15360


def _params(sem=None):
    return pltpu.CompilerParams(dimension_semantics=sem, vmem_limit_bytes=VMEM_LIMIT_BYTES)


def _sigmoid(v):
    return 1.0 / (1.0 + jnp.exp(-v))


def _softplus(v):
    return jnp.maximum(v, 0.0) + jnp.log(1.0 + jnp.exp(-jnp.abs(v)))


def _nt(a, b, precision=None):
    return lax.dot_general(a, b, (((1,), (1,)), ((), ())), preferred_element_type=f32, precision=precision)


def _tn(a, b, precision=None):
    return lax.dot_general(a, b, (((0,), (0,)), ((), ())), preferred_element_type=f32, precision=precision)


def _nn(a, b, precision=None):
    return lax.dot_general(a, b, (((1,), (0,)), ((), ())), preferred_element_type=f32, precision=precision)


def _me():
    x, y, c = lax.axis_index("x"), lax.axis_index("y"), lax.axis_index("c")
    return (x, y, c), 4 * x + 2 * y + c


def _peer(k):
    x, y, c = lax.axis_index("x"), lax.axis_index("y"), lax.axis_index("c")
    px = 1 - x if (k >> 2) & 1 else x
    py = 1 - y if (k >> 1) & 1 else y
    pc = 1 - c if k & 1 else c
    return (px, py, pc), 4 * px + 2 * py + pc


def _all_gather(arrs, name):
    n = len(arrs)

    def body(*refs):
        ins, outs = refs[:n], refs[n:2 * n]
        send_sems, recv_sems, local_sems = refs[2 * n:]
        _, my_slot = _me()
        sends = []
        locals_ = []
        for a in range(n):
            loc = pltpu.make_async_copy(ins[a], outs[a].at[my_slot], local_sems.at[a])
            loc.start()
            locals_.append(loc)
            for k in range(1, N_DEV):
                peer, _ = _peer(k)
                cp = pltpu.make_async_remote_copy(src_ref=ins[a], dst_ref=outs[a].at[my_slot], send_sem=send_sems.at[a, k - 1],
                                                  recv_sem=recv_sems.at[a, k - 1], device_id=peer, device_id_type=MESH)
                cp.start()
                sends.append(cp)
        for a in range(n):
            for k in range(1, N_DEV):
                peer, peer_slot = _peer(k)
                pltpu.make_async_remote_copy(src_ref=ins[a], dst_ref=outs[a].at[peer_slot], send_sem=send_sems.at[a, k - 1],
                                             recv_sem=recv_sems.at[a, k - 1], device_id=peer, device_id_type=MESH).wait_recv()
        for cp in sends:
            cp.wait_send()
        for loc in locals_:
            loc.wait()

    any_spec = pl.BlockSpec(memory_space=pl.ANY)
    return pl.pallas_call(
        body, name=name,
        out_shape=tuple(jax.ShapeDtypeStruct((N_DEV,) + a.shape, a.dtype) for a in arrs),
        in_specs=[any_spec] * n, out_specs=tuple([any_spec] * n),
        scratch_shapes=[pltpu.SemaphoreType.DMA((n, N_DEV - 1)), pltpu.SemaphoreType.DMA((n, N_DEV - 1)),
                        pltpu.SemaphoreType.DMA((n,))],
    )(*arrs)


def _all_gather_two_level(arrs, name):
    n = len(arrs)

    def body(*refs):
        ins, outs = refs[:n], refs[n:2 * n]
        send_sems, recv_sems, local_sems = refs[2 * n:]
        x, y, c = lax.axis_index("x"), lax.axis_index("y"), lax.axis_index("c")
        me, sibling = (x, y, c), (x, y, 1 - c)
        chips = [(1 - x, y), (x, 1 - y), (1 - x, 1 - y)]

        def copy(a, k, block, to, from_input=False):
            slot = 4 * block[0] + 2 * block[1] + block[2]
            return pltpu.make_async_remote_copy(src_ref=ins[a] if from_input else outs[a].at[slot], dst_ref=outs[a].at[slot],
                                                send_sem=send_sems.at[a, k], recv_sem=recv_sems.at[a, k], device_id=to,
                                                device_id_type=MESH)

        started = []
        locals_ = []
        for a in range(n):
            loc = pltpu.make_async_copy(ins[a], outs[a].at[4 * x + 2 * y + c], local_sems.at[a])
            loc.start()
            locals_.append(loc)
            first = [copy(a, 0, me, sibling, True)] + [copy(a, 1 + j, me, (*chip, c), True) for j, chip in enumerate(chips)]
            for cp in first:
                cp.start()
            started += first
        for a in range(n):
            for j, chip in enumerate(chips):
                copy(a, 1 + j, (*chip, c), me).wait_recv()
                onward = copy(a, 4 + j, (*chip, c), sibling)
                onward.start()
                started.append(onward)
        for a in range(n):
            copy(a, 0, sibling, me).wait_recv()
            for j, chip in enumerate(chips):
                copy(a, 4 + j, (*chip, 1 - c), me).wait_recv()
        for cp in started:
            cp.wait_send()
        for loc in locals_:
            loc.wait()

    any_spec = pl.BlockSpec(memory_space=pl.ANY)
    return pl.pallas_call(
        body, name=name,
        out_shape=tuple(jax.ShapeDtypeStruct((N_DEV,) + a.shape, a.dtype) for a in arrs),
        in_specs=[any_spec] * n, out_specs=tuple([any_spec] * n),
        scratch_shapes=[pltpu.SemaphoreType.DMA((n, N_DEV - 1)), pltpu.SemaphoreType.DMA((n, N_DEV - 1)),
                        pltpu.SemaphoreType.DMA((n,))],
    )(*arrs)


def _all_to_all(arrs, name):
    n = len(arrs)

    def body(*refs):
        ins, outs = refs[:n], refs[n:2 * n]
        send_sems, recv_sems, local_sems = refs[2 * n:]
        _, my_slot = _me()
        sends = []
        locals_ = []
        for a in range(n):
            loc = pltpu.make_async_copy(ins[a].at[my_slot], outs[a].at[my_slot], local_sems.at[a])
            loc.start()
            locals_.append(loc)
            for k in range(1, N_DEV):
                peer, peer_slot = _peer(k)
                cp = pltpu.make_async_remote_copy(src_ref=ins[a].at[peer_slot], dst_ref=outs[a].at[my_slot],
                                                  send_sem=send_sems.at[a, k - 1], recv_sem=recv_sems.at[a, k - 1],
                                                  device_id=peer, device_id_type=MESH)
                cp.start()
                sends.append(cp)
        for a in range(n):
            for k in range(1, N_DEV):
                peer, peer_slot = _peer(k)
                pltpu.make_async_remote_copy(src_ref=ins[a].at[peer_slot], dst_ref=outs[a].at[peer_slot],
                                             send_sem=send_sems.at[a, k - 1], recv_sem=recv_sems.at[a, k - 1],
                                             device_id=peer, device_id_type=MESH).wait_recv()
        for cp in sends:
            cp.wait_send()
        for loc in locals_:
            loc.wait()

    any_spec = pl.BlockSpec(memory_space=pl.ANY)
    return pl.pallas_call(
        body, name=name,
        out_shape=tuple(jax.ShapeDtypeStruct(a.shape, a.dtype) for a in arrs),
        in_specs=[any_spec] * n, out_specs=tuple([any_spec] * n),
        scratch_shapes=[pltpu.SemaphoreType.DMA((n, N_DEV - 1)), pltpu.SemaphoreType.DMA((n, N_DEV - 1)),
                        pltpu.SemaphoreType.DMA((n,))],
    )(*arrs)


def _reduce_scatter_two_level(arrs, name):
    n = len(arrs)
    n_chip = N_DEV // 2

    def body(*refs):
        ins, outs = refs[:n], refs[n:2 * n]
        mine_bufs, sib_bufs = refs[2 * n:3 * n], refs[3 * n:4 * n]
        send_sems, recv_sems, local_sems = refs[4 * n:]
        x, y, c = lax.axis_index("x"), lax.axis_index("y"), lax.axis_index("c")
        sibling = (x, y, 1 - c)

        def chip(r):
            return (1 - x if r & 2 else x), (1 - y if r & 1 else y)

        def slot(r, core):
            cx, cy = chip(r)
            return 4 * cx + 2 * cy + core

        def to_sibling(a, r):
            return pltpu.make_async_remote_copy(src_ref=ins[a].at[slot(r, 1 - c)], dst_ref=sib_bufs[a].at[r], send_sem=send_sems.at[a, r],
                                                recv_sem=recv_sems.at[a, r], device_id=sibling, device_id_type=MESH)

        def to_chip(a, r):
            return pltpu.make_async_remote_copy(src_ref=mine_bufs[a].at[r], dst_ref=outs[a].at[r], send_sem=send_sems.at[a, n_chip - 1 + r],
                                                recv_sem=recv_sems.at[a, n_chip - 1 + r], device_id=(*chip(r), c), device_id_type=MESH)

        def load_mine(a, r):
            return pltpu.make_async_copy(ins[a].at[slot(r, c)], mine_bufs[a].at[r], local_sems.at[a, r])

        started = []
        order = [1, 2, 3, 0]
        for a in range(n):
            for r in order:
                load_mine(a, r).start()
                cp = to_sibling(a, r)
                cp.start()
                started.append(cp)
        for a in range(n):
            for r in order:
                load_mine(a, r).wait()
                to_sibling(a, r).wait_recv()
                total = (mine_bufs[a][r].astype(f32) + sib_bufs[a][r].astype(f32)).astype(bf16)
                if r:
                    mine_bufs[a][r] = total
                    cp = to_chip(a, r)
                    cp.start()
                    started.append(cp)
                else:
                    outs[a][0] = total
        for a in range(n):
            for r in range(1, n_chip):
                to_chip(a, r).wait_recv()
        for cp in started:
            cp.wait_send()

    any_spec = pl.BlockSpec(memory_space=pl.ANY)
    vmem = pl.BlockSpec(memory_space=pltpu.VMEM)
    part = lambda a: (n_chip,) + a.shape[1:]
    return pl.pallas_call(
        body, name=name,
        out_shape=tuple(jax.ShapeDtypeStruct(part(a), a.dtype) for a in arrs),
        in_specs=[any_spec] * n, out_specs=tuple([vmem] * n),
        scratch_shapes=([pltpu.VMEM(part(a), a.dtype) for a in arrs] + [pltpu.VMEM(part(a), a.dtype) for a in arrs]
                        + [pltpu.SemaphoreType.DMA((n, N_DEV - 1)), pltpu.SemaphoreType.DMA((n, N_DEV - 1)),
                           pltpu.SemaphoreType.DMA((n, n_chip))]),
        compiler_params=_params(),
    )(*arrs)


def _mod_exchange(c_row, w_ada, b_ada):
    n_col = w_ada.shape[1]

    def body(c_ref, w_ref, b_ref, mod_ref, call_ref, part, modp, send_sems, recv_sems):
        _, my_slot = _me()
        call_ref[my_slot] = c_ref[...]
        sends = []
        for k in range(1, N_DEV):
            peer, _ = _peer(k)
            cp = pltpu.make_async_remote_copy(src_ref=c_ref, dst_ref=call_ref.at[my_slot], send_sem=send_sems.at[0, k - 1],
                                              recv_sem=recv_sems.at[0, k - 1], device_id=peer, device_id_type=MESH)
            cp.start()
            sends.append(cp)
        for k in range(1, N_DEV):
            peer, peer_slot = _peer(k)
            pltpu.make_async_remote_copy(src_ref=c_ref, dst_ref=call_ref.at[peer_slot], send_sem=send_sems.at[0, k - 1],
                                         recv_sem=recv_sems.at[0, k - 1], device_id=peer, device_id_type=MESH).wait_recv()
        for cp in sends:
            cp.wait_send()
        w = w_ref[...]
        for b in range(N_DEV):
            cb = call_ref[b]
            part[b] = _nn(cb * _sigmoid(cb), w, HI)
        modp[my_slot] = part[my_slot]
        sends = []
        for k in range(1, N_DEV):
            peer, peer_slot = _peer(k)
            cp = pltpu.make_async_remote_copy(src_ref=part.at[peer_slot], dst_ref=modp.at[my_slot], send_sem=send_sems.at[1, k - 1],
                                              recv_sem=recv_sems.at[1, k - 1], device_id=peer, device_id_type=MESH)
            cp.start()
            sends.append(cp)
        for k in range(1, N_DEV):
            peer, peer_slot = _peer(k)
            pltpu.make_async_remote_copy(src_ref=part.at[peer_slot], dst_ref=modp.at[peer_slot], send_sem=send_sems.at[1, k - 1],
                                         recv_sem=recv_sems.at[1, k - 1], device_id=peer, device_id_type=MESH).wait_recv()
        for cp in sends:
            cp.wait_send()
        for j in range(N_DEV):
            mod_ref[:, j * n_col:(j + 1) * n_col] = modp[j] + b_ref[:, j * n_col:(j + 1) * n_col]

    vmem = pl.BlockSpec(memory_space=pltpu.VMEM)
    return pl.pallas_call(
        body, name="mod_exchange",
        out_shape=(jax.ShapeDtypeStruct((1, N_DEV * n_col), f32), jax.ShapeDtypeStruct((N_DEV, 1, D_MODEL), f32)),
        in_specs=[vmem, vmem, vmem], out_specs=(vmem, vmem),
        scratch_shapes=[pltpu.VMEM((N_DEV, 1, n_col), f32), pltpu.VMEM((N_DEV, 1, n_col), f32),
                        pltpu.SemaphoreType.DMA((2, N_DEV - 1)), pltpu.SemaphoreType.DMA((2, N_DEV - 1))],
        compiler_params=_params(),
    )(c_row, w_ada, b_ada)


def _cast_bf16(a, rows):
    r, c = a.shape

    def body(a_ref, o_ref):
        o_ref[...] = a_ref[...].astype(bf16)

    return pl.pallas_call(
        body, name="cast_bf16", grid=(r // rows,),
        in_specs=[pl.BlockSpec((rows, c), lambda i: (i, 0))], out_specs=pl.BlockSpec((rows, c), lambda i: (i, 0)),
        out_shape=jax.ShapeDtypeStruct((r, c), bf16), compiler_params=_params(("parallel",)),
    )(a)


def _store_transposed(out_ref, v):
    blk = 256
    eye = (lax.broadcasted_iota(jnp.int32, (blk, blk), 0) == lax.broadcasted_iota(jnp.int32, (blk, blk), 1)).astype(bf16)
    for cb in range(0, v.shape[1], blk):
        out_ref[cb:cb + blk, :] = _nt(eye, v[:, cb:cb + blk]).astype(bf16)


def _proj(x, gain, scale, shift, wp, seq):
    ts = 256

    def body(x_ref, g_ref, sc_ref, sh_ref, w_ref, ht_ref, qkv_ref, za_ref, zs_ref, xbc_ref, dt_ref):
        xv = x_ref[...]
        r = lax.rsqrt(jnp.mean(xv * xv, axis=-1, keepdims=True) + EPS)
        hb = ((xv * r * g_ref[...]) * (1.0 + sc_ref[...]) + sh_ref[...]).astype(bf16)
        _store_transposed(ht_ref, hb)
        for cb in range(3 * D_ATTN // 256):
            res = jnp.dot(hb, w_ref[:, cb * 256:(cb + 1) * 256], preferred_element_type=f32)
            for u in range(4):
                qkv_ref[cb * 4 + u] = res[:, u * HEAD_DIM:(u + 1) * HEAD_DIM].astype(bf16)
        for out_ref, off, width in ((za_ref, OFF_ZA, D_ATTN), (zs_ref, OFF_ZS, D_SSM), (xbc_ref, OFF_XBC, D_XBC), (dt_ref, OFF_DT, LANES)):
            for cc in range(0, width, 512):
                wd = min(512, width - cc)
                out_ref[:, cc:cc + wd] = jnp.dot(hb, w_ref[:, off + cc:off + cc + wd], preferred_element_type=f32)

    row = lambda i: (i, 0)
    fixed = lambda i: (0, 0)
    return pl.pallas_call(
        body, name="proj", grid=(seq // ts,),
        in_specs=[pl.BlockSpec((ts, D_MODEL), row), pl.BlockSpec((1, D_MODEL), fixed), pl.BlockSpec((1, D_MODEL), fixed),
                  pl.BlockSpec((1, D_MODEL), fixed), pl.BlockSpec((D_MODEL, D_PROJ_P), fixed)],
        out_specs=(pl.BlockSpec((D_MODEL, ts), lambda i: (0, i)), pl.BlockSpec((3 * N_HEADS, ts, HEAD_DIM), lambda i: (0, i, 0)),
                   pl.BlockSpec((ts, D_ATTN), row), pl.BlockSpec((ts, D_SSM), row), pl.BlockSpec((ts, D_XBC), row),
                   pl.BlockSpec((ts, LANES), row)),
        out_shape=(jax.ShapeDtypeStruct((D_MODEL, seq), bf16), jax.ShapeDtypeStruct((3 * N_HEADS, seq, HEAD_DIM), bf16),
                   jax.ShapeDtypeStruct((seq, D_ATTN), f32), jax.ShapeDtypeStruct((seq, D_SSM), f32),
                   jax.ShapeDtypeStruct((seq, D_XBC), f32), jax.ShapeDtypeStruct((seq, LANES), f32)),
        compiler_params=_params(("arbitrary",)),
    )(x, gain, scale, shift, wp)


def _log_sigmoids(z):
    lb = jnp.minimum(z, 0.0) - jnp.log(1.0 + jnp.exp(-jnp.abs(z)))
    return lb, lb - z


def _split_bf16(v):
    hi = v.astype(bf16)
    return hi, (v - hi.astype(f32)).astype(bf16)


def _spread(v, sel):
    hi = v.astype(bf16)
    mid, lo = _split_bf16(v - hi.astype(f32))
    sel_b = sel.astype(bf16)
    return (jnp.dot(hi, sel_b, preferred_element_type=f32) + jnp.dot(mid, sel_b, preferred_element_type=f32)
            + jnp.dot(lo, sel_b, preferred_element_type=f32))


def _attn_fwd(qkv, seq):
    t = CHUNK
    tq = min(ATTN_Q_ROWS, seq)
    nd = tq // t
    unroll = min(ATTN_UNROLL, nd)
    assert nd % unroll == 0

    def body(q_ref, k_ref, v_ref, o_ref):
        i = pl.program_id(1)
        q = q_ref[0] * 0.125
        ur = lax.broadcasted_iota(jnp.int32, (2 * t, t), 0)
        upper = ((ur & (t - 1)) > lax.broadcasted_iota(jnp.int32, (2 * t, t), 1)).astype(bf16)

        def tile(j, q_s, acc, run, masked):
            n = q_s.shape[0]
            start = pl.multiple_of(j * t, t)
            k = k_ref[0, pl.ds(start, t), :]
            v = v_ref[0, pl.ds(start, t), :]
            z = _nt(q_s, k)
            lb, lom = _log_sigmoids(z)
            if masked:
                keep = lax.broadcasted_iota(jnp.int32, (n, t), 1) < lax.broadcasted_iota(jnp.int32, (n, t), 0)
                lom = jnp.where(keep, lom, 0.0)
            tail = jnp.dot(jnp.concatenate(_split_bf16(lom), axis=1), upper, preferred_element_type=f32)
            a = lb + tail + run
            if masked:
                a = jnp.where(keep, a, -jnp.inf)
            w = jnp.exp(a)
            acc = acc + jnp.dot(w.astype(bf16), v, preferred_element_type=f32)
            run = run + tail[:, 0:1] + lom[:, 0:1]
            return acc, run

        acc, run = jnp.zeros((tq, HEAD_DIM), f32), jnp.zeros((tq, 1), f32)
        for jj in reversed(range(nd)):
            r0 = jj * t
            acc_s, run_s = tile(i * nd + jj, q[r0:], acc[r0:], run[r0:], True)
            acc = acc_s if r0 == 0 else jnp.concatenate([acc[:r0], acc_s], axis=0)
            run = run_s if r0 == 0 else jnp.concatenate([run[:r0], run_s], axis=0)
        def group(n, cr):
            for u in range(unroll):
                cr = tile(i * nd - 1 - unroll * n - u, q, cr[0], cr[1], False)
            return cr

        acc, run = lax.fori_loop(0, i * (nd // unroll), group, (acc, run))
        o_ref[0] = jnp.concatenate([acc, jnp.broadcast_to(run, (tq, HEAD_DIM))], axis=1)

    return pl.pallas_call(
        body, name="attn_fwd", grid=(N_HEADS, seq // tq),
        in_specs=[pl.BlockSpec((1, tq, HEAD_DIM), lambda h, i: (h, i, 0)),
                  pl.BlockSpec((1, seq, HEAD_DIM), lambda h, i: (N_HEADS + h, 0, 0)),
                  pl.BlockSpec((1, seq, HEAD_DIM), lambda h, i: (2 * N_HEADS + h, 0, 0))],
        out_specs=pl.BlockSpec((1, tq, 2 * HEAD_DIM), lambda h, i: (h, i, 0)),
        out_shape=jax.ShapeDtypeStruct((N_HEADS, seq, 2 * HEAD_DIM), f32),
        compiler_params=_params(("parallel", "arbitrary")),
    )(qkv, qkv, qkv)


def _attn_bwd(qkv, o_tot, d_o, seq):
    t = CHUNK
    tq = min(ATTN_Q_ROWS, seq)
    nd = tq // t
    unroll = min(ATTN_UNROLL, nd)
    assert nd % unroll == 0
    nk = seq // t

    def body(q_ref, k_ref, v_ref, ot_ref, do_ref, dq_ref, dkt_ref, dvt_ref):
        i = pl.program_id(1)

        @pl.when(i == 0)
        def _():
            dkt_ref[...] = jnp.zeros_like(dkt_ref)
            dvt_ref[...] = jnp.zeros_like(dvt_ref)

        q = q_ref[0] * 0.125
        d_out = do_ref[0]
        total = ot_ref[0][:, HEAD_DIM:HEAD_DIM + 1]
        eye = (lax.broadcasted_iota(jnp.int32, (HEAD_DIM, HEAD_DIM), 0)
               == lax.broadcasted_iota(jnp.int32, (HEAD_DIM, HEAD_DIM), 1)).astype(bf16)
        q_t = _nt(eye, q).astype(bf16)
        do_t = _nt(eye, d_out).astype(bf16)
        ur = lax.broadcasted_iota(jnp.int32, (t, t), 0)
        uc = lax.broadcasted_iota(jnp.int32, (t, t), 1)
        ur2 = lax.broadcasted_iota(jnp.int32, (2 * t, t), 0) & (t - 1)
        incl = (ur2 <= lax.broadcasted_iota(jnp.int32, (2 * t, t), 1)).astype(bf16)
        before = (ur < uc).astype(bf16)

        def tile(j, r0, r1, dq, pre, dpre, masked):
            q_s, do_s, tot_s = q[r0:r1], d_out[r0:r1], total[r0:r1]
            n = q_s.shape[0]
            start = pl.multiple_of(j * t, t)
            k = k_ref[0, pl.ds(start, t), :]
            v = v_ref[0, pl.ds(start, t), :]
            z = _nt(q_s, k)
            lb, lom = _log_sigmoids(z)
            if masked:
                keep = lax.broadcasted_iota(jnp.int32, (n, t), 1) < lax.broadcasted_iota(jnp.int32, (n, t), 0)
                lom = jnp.where(keep, lom, 0.0)
            pin = jnp.dot(jnp.concatenate(_split_bf16(lom), axis=1), incl, preferred_element_type=f32)
            a = lb + ((tot_s - pre) - pin)
            if masked:
                a = jnp.where(keep, a, -jnp.inf)
            w = jnp.exp(a)
            d_a = _nt(do_s, v) * w
            d_lom_local = jnp.dot(d_a.astype(bf16), before, preferred_element_type=f32)
            d_lom = d_lom_local + dpre
            sig = jnp.exp(lb)
            dz = d_a * (1.0 - sig) - d_lom * sig
            if masked:
                dz = jnp.where(keep, dz, 0.0)
            dzb = dz.astype(bf16)
            dq = dq + jnp.dot(dzb, k, preferred_element_type=f32)
            dkt_ref[0, j] += jnp.dot(q_t[:, r0:r1], dzb, preferred_element_type=f32)
            dvt_ref[0, j] += jnp.dot(do_t[:, r0:r1], w.astype(bf16), preferred_element_type=f32)
            pre = pre + pin[:, t - 1:t]
            dpre = dpre + d_lom_local[:, t - 1:t] + d_a[:, t - 1:t]
            return dq, pre, dpre

        carry = (jnp.zeros((tq, HEAD_DIM), f32), jnp.zeros((tq, 1), f32), jnp.zeros((tq, 1), f32))
        def group(n, cr):
            for u in range(unroll):
                cr = tile(unroll * n + u, 0, tq, cr[0], cr[1], cr[2], False)
            return cr

        carry = lax.fori_loop(0, i * (nd // unroll), group, carry)
        for jj in range(nd):
            r0 = jj * t
            part = tile(i * nd + jj, r0, tq, *(c[r0:] for c in carry), True)
            carry = part if r0 == 0 else tuple(jnp.concatenate([c[:r0], p], axis=0) for c, p in zip(carry, part))
        dq_ref[0] = carry[0] * 0.125

    blk = pl.BlockSpec((1, tq, HEAD_DIM), lambda h, i: (h, i, 0))
    full_t = pl.BlockSpec((1, nk, HEAD_DIM, t), lambda h, i: (h, 0, 0, 0))
    return pl.pallas_call(
        body, name="attn_bwd", grid=(N_HEADS, seq // tq),
        in_specs=[blk, pl.BlockSpec((1, seq, HEAD_DIM), lambda h, i: (N_HEADS + h, 0, 0)),
                  pl.BlockSpec((1, seq, HEAD_DIM), lambda h, i: (2 * N_HEADS + h, 0, 0)),
                  pl.BlockSpec((1, tq, 2 * HEAD_DIM), lambda h, i: (h, i, 0)), blk],
        out_specs=(blk, full_t, full_t),
        out_shape=(jax.ShapeDtypeStruct((N_HEADS, seq, HEAD_DIM), f32),
                   jax.ShapeDtypeStruct((N_HEADS, nk, HEAD_DIM, t), f32), jax.ShapeDtypeStruct((N_HEADS, nk, HEAD_DIM, t), f32)),
        compiler_params=_params(("parallel", "arbitrary")),
    )(qkv, qkv, qkv, o_tot, d_o)


def _ssd_common(conv, dt_raw, dtb, alog):
    t = CHUNK
    sg = _sigmoid(conv)
    act = conv * sg
    dt_pre = dt_raw + dtb
    dt = _softplus(dt_pre)
    a = -jnp.exp(alog)
    row = lax.broadcasted_iota(jnp.int32, (t, t), 0)
    col = lax.broadcasted_iota(jnp.int32, (t, t), 1)
    causal = row >= col
    ac = _nn(causal.astype(f32), dt * a, HI)
    ac_t = _nt((row == col).astype(f32), ac, HI)
    ac_last = ac[t - 1:t, :]
    return sg, act, dt_pre, dt, a, causal, ac, ac_t, ac_last, jnp.exp(ac), jnp.exp(ac_last - ac), jnp.exp(ac_last)


def _ssd_fwd(xbc, dt_raw, conv_w, conv_b, dtb, alog, dsk, seq):
    t = CHUNK
    n_chunks = seq // t

    def body(x_ref, dt_ref, cw_ref, cb_ref, dtb_ref, al_ref, dsk_ref, conv_ref, y_ref, st_ref, prev, state):
        c = pl.program_id(0)

        @pl.when(c == 0)
        def _():
            prev[...] = jnp.zeros_like(prev)
            state[...] = jnp.zeros_like(state)

        cur = x_ref[...]
        pv = prev[...]
        rows = lax.broadcasted_iota(jnp.int32, (t, D_XBC), 0)
        conv = cur * cw_ref[CONV_K - 1:CONV_K, :] + cb_ref[...]
        for m in range(1, CONV_K):
            shifted = jnp.where(rows < m, pltpu.roll(pv, m, 0), pltpu.roll(cur, m, 0))
            conv = conv + shifted * cw_ref[CONV_K - 1 - m:CONV_K - m, :]
        prev[...] = cur
        conv_ref[...] = conv
        _, act, _, dt, _, _, ac, ac_t, _, e_ac, dte, cdec = _ssd_common(conv, dt_ref[...], dtb_ref[...], al_ref[...])
        xs = act[:, :D_SSM]
        wide = N_HEADS * t
        sel64 = (lax.broadcasted_iota(jnp.int32, (LANES, D_SSM), 1) // HEAD_DIM
                 == lax.broadcasted_iota(jnp.int32, (LANES, D_SSM), 0)).astype(f32)
        sel128 = (lax.broadcasted_iota(jnp.int32, (LANES, wide), 1) // t
                  == lax.broadcasted_iota(jnp.int32, (LANES, wide), 0)).astype(f32)
        xd_all = xs * _spread(dt, sel64)
        xdb = xd_all.astype(bf16)
        xdte_b = (xd_all * _spread(dte, sel64)).astype(bf16)
        seg_all = _spread(ac, sel128) - jnp.concatenate([jnp.broadcast_to(ac_t[h:h + 1, :], (t, t)) for h in range(N_HEADS)], axis=1)
        causal_all = (lax.broadcasted_iota(jnp.int32, (t, wide), 0) >= (lax.broadcasted_iota(jnp.int32, (t, wide), 1) & (t - 1)))
        lm_all = jnp.exp(jnp.where(causal_all, seg_all, -jnp.inf))
        bgs = [act[:, D_SSM + g * N_STATE:D_SSM + (g + 1) * N_STATE].astype(bf16) for g in range(N_GROUPS)]
        cgs = [act[:, D_SSM + (N_GROUPS + g) * N_STATE:D_SSM + (N_GROUPS + g + 1) * N_STATE].astype(bf16) for g in range(N_GROUPS)]
        gms = [_nt(cgs[g], bgs[g]) for g in range(N_GROUPS)]
        mm_b = (jnp.concatenate([gms[h // HEADS_PER_GROUP] for h in range(N_HEADS)], axis=1) * lm_all).astype(bf16)
        y_diags, zos = [], []
        for h in range(N_HEADS):
            g = h // HEADS_PER_GROUP
            hs = slice(h * HEAD_DIM, (h + 1) * HEAD_DIM)
            hp = state[h]
            st_ref[0, h] = hp
            y_diags.append(jnp.dot(mm_b[:, h * t:(h + 1) * t], xdb[:, hs], preferred_element_type=f32))
            zos.append(_nt(cgs[g], hp.astype(bf16)))
            state[h] = hp * cdec[:, h:h + 1] + _tn(xdte_b[:, hs], bgs[g])
        y_ref[...] = (jnp.concatenate(y_diags, axis=1) + jnp.concatenate(zos, axis=1) * _spread(e_ac, sel64)
                      + xs * _spread(dsk_ref[...], sel64))

    row = lambda c: (c, 0)
    fixed = lambda c: (0, 0)
    return pl.pallas_call(
        body, name="ssd_fwd", grid=(n_chunks,),
        in_specs=[pl.BlockSpec((t, D_XBC), row), pl.BlockSpec((t, LANES), row), pl.BlockSpec((CONV_K, D_XBC), fixed),
                  pl.BlockSpec((1, D_XBC), fixed), pl.BlockSpec((1, LANES), fixed), pl.BlockSpec((1, LANES), fixed),
                  pl.BlockSpec((1, LANES), fixed)],
        out_specs=(pl.BlockSpec((t, D_XBC), row), pl.BlockSpec((t, D_SSM), row),
                   pl.BlockSpec((1, N_HEADS, HEAD_DIM, N_STATE), lambda c: (c, 0, 0, 0))),
        out_shape=(jax.ShapeDtypeStruct((seq, D_XBC), f32), jax.ShapeDtypeStruct((seq, D_SSM), f32),
                   jax.ShapeDtypeStruct((n_chunks, N_HEADS, HEAD_DIM, N_STATE), f32)),
        scratch_shapes=[pltpu.VMEM((t, D_XBC), f32), pltpu.VMEM((N_HEADS, HEAD_DIM, N_STATE), f32)],
        compiler_params=_params(("arbitrary",)),
    )(xbc, dt_raw, conv_w, conv_b, dtb, alog, dsk)


def _ssd_bwd(dy, conv, xbc, dt_raw, states, conv_w, dtb, alog, dsk, seq):
    t = CHUNK
    n_chunks = seq // t

    def body(dy_ref, conv_ref, x_ref, dt_ref, st_ref, cw_ref, dtb_ref, al_ref, dsk_ref,
             dx_ref, ddt_ref, gcw_ref, gcb_ref, gdtb_ref, gal_ref, gdsk_ref, d_state, d_conv_next):
        c = pl.program_id(0)

        @pl.when(c == 0)
        def _():
            d_state[...] = jnp.zeros_like(d_state)
            d_conv_next[...] = jnp.zeros_like(d_conv_next)
            gcw_ref[...] = jnp.zeros_like(gcw_ref)
            gcb_ref[...] = jnp.zeros_like(gcb_ref)
            gdtb_ref[...] = jnp.zeros_like(gdtb_ref)
            gal_ref[...] = jnp.zeros_like(gal_ref)
            gdsk_ref[...] = jnp.zeros_like(gdsk_ref)

        conv = conv_ref[...]
        sg, act, dt_pre, dt, a, causal, ac, ac_t, _, e_ac, dte, cdec = _ssd_common(conv, dt_ref[...], dtb_ref[...], al_ref[...])
        dyv = dy_ref[...]
        xs = act[:, :D_SSM]
        sel64 = lax.broadcasted_iota(jnp.int32, (LANES, D_SSM), 1) // HEAD_DIM == lax.broadcasted_iota(jnp.int32, (LANES, D_SSM), 0)
        sel64_t = lax.broadcasted_iota(jnp.int32, (D_SSM, LANES), 0) // HEAD_DIM == lax.broadcasted_iota(jnp.int32, (D_SSM, LANES), 1)
        wide = N_HEADS * t
        sel128 = lax.broadcasted_iota(jnp.int32, (LANES, wide), 1) // t == lax.broadcasted_iota(jnp.int32, (LANES, wide), 0)
        sel128_t = lax.broadcasted_iota(jnp.int32, (wide, LANES), 0) // t == lax.broadcasted_iota(jnp.int32, (wide, LANES), 1)

        spread = _spread

        def lane_sums(v, sel_t):
            hi, lo = _split_bf16(v)
            sel_b = sel_t.astype(bf16)
            return jnp.dot(hi, sel_b, preferred_element_type=f32) + jnp.dot(lo, sel_b, preferred_element_type=f32)

        dt_x = spread(dt, sel64)
        e_x = spread(e_ac, sel64)
        dte_x = spread(dte, sel64)
        seg_all = spread(ac, sel128) - jnp.concatenate([jnp.broadcast_to(ac_t[h:h + 1, :], (t, t)) for h in range(N_HEADS)], axis=1)
        causal_all = (lax.broadcasted_iota(jnp.int32, (t, wide), 0) >= (lax.broadcasted_iota(jnp.int32, (t, wide), 1) & (t - 1)))
        lm_all = jnp.exp(jnp.where(causal_all, seg_all, -jnp.inf))
        xd_all = xs * dt_x
        xdb = xd_all.astype(bf16)
        xdte_b = (xd_all * dte_x).astype(bf16)
        d_yb = dyv.astype(bf16)
        d_zo_all = dyv * e_x
        d_zob = d_zo_all.astype(bf16)
        bgs = [act[:, D_SSM + g * N_STATE:D_SSM + (g + 1) * N_STATE].astype(bf16) for g in range(N_GROUPS)]
        cgs = [act[:, D_SSM + (N_GROUPS + g) * N_STATE:D_SSM + (N_GROUPS + g + 1) * N_STATE].astype(bf16) for g in range(N_GROUPS)]
        gms = [_nt(cgs[g], bgs[g]) for g in range(N_GROUPS)]
        mm_all = jnp.concatenate([gms[h // HEADS_PER_GROUP] for h in range(N_HEADS)], axis=1) * lm_all
        mm_b = mm_all.astype(bf16)
        d_hn_all = d_state[...].reshape(N_HEADS * HEAD_DIM, N_STATE)
        hp_all = st_ref[0].reshape(N_HEADS * HEAD_DIM, N_STATE)
        state_dot = lane_sums(d_hn_all * hp_all, jnp.ones((N_STATE, LANES), jnp.bool_))
        state_dot = jnp.sum(jnp.where(sel64_t, state_dot, 0.0), axis=0, keepdims=True)
        d_mms, d_xds, zos, d_ws = [], [], [], []
        d_bs = [jnp.zeros((t, N_STATE), f32) for _ in range(N_GROUPS)]
        d_cs = [jnp.zeros((t, N_STATE), f32) for _ in range(N_GROUPS)]
        for h in range(N_HEADS):
            g = h // HEADS_PER_GROUP
            hs = slice(h * HEAD_DIM, (h + 1) * HEAD_DIM)
            hpb = st_ref[0, h].astype(bf16)
            d_hn = d_state[h]
            d_hnb = d_hn.astype(bf16)
            d_mms.append(_nt(d_yb[:, hs], xdb[:, hs]))
            d_xds.append(_tn(mm_b[:, h * t:(h + 1) * t], d_yb[:, hs]))
            zos.append(_nt(cgs[g], hpb))
            d_cs[g] = d_cs[g] + jnp.dot(d_zob[:, hs], hpb, preferred_element_type=f32)
            d_state[h] = _tn(d_zob[:, hs], cgs[g]) + d_hn * cdec[:, h:h + 1]
            d_ws.append(_nt(bgs[g], d_hnb))
            d_bs[g] = d_bs[g] + jnp.dot(xdte_b[:, hs], d_hnb, preferred_element_type=f32)
        d_mm_all = jnp.concatenate(d_mms, axis=1)
        d_seg_all = d_mm_all * mm_all
        d_gm_all = d_mm_all * lm_all
        dwd_all = jnp.concatenate(d_ws, axis=1) * dte_x
        d_xd_all = jnp.concatenate(d_xds, axis=1) + dwd_all
        decay_part = dwd_all * xd_all
        decay_sums = lane_sums(decay_part, sel64_t)
        d_ac = lane_sums(d_seg_all, sel128_t) + lane_sums(d_zo_all * jnp.concatenate(zos, axis=1) - decay_part, sel64_t)
        d_dt = lane_sums(d_xd_all * xs, sel64_t)
        g_dsk = jnp.sum(lane_sums(dyv * xs, sel64_t), axis=0, keepdims=True)
        stacked = jnp.concatenate([d_seg_all[:, h * t:(h + 1) * t] for h in range(N_HEADS)], axis=0)
        s_hi, s_lo = _split_bf16(stacked)
        sel_b = sel128.astype(bf16)
        col_sums = jnp.dot(sel_b, s_hi, preferred_element_type=f32) + jnp.dot(sel_b, s_lo, preferred_element_type=f32)
        d_last = state_dot * cdec + jnp.sum(decay_sums, axis=0, keepdims=True)
        last_row = (lax.broadcasted_iota(jnp.int32, (t, 1), 0) == t - 1).astype(f32)
        sq_row = lax.broadcasted_iota(jnp.int32, (t, t), 0)
        sq_col = lax.broadcasted_iota(jnp.int32, (t, t), 1)
        d_ac = d_ac - _nt((sq_row == sq_col).astype(f32), col_sums, HI) + last_row * d_last
        d_ld = _nn((sq_col >= sq_row).astype(f32), d_ac, HI)
        dxs = [dyv * spread(dsk_ref[...], sel64) + d_xd_all * dt_x]
        dbs, dcs = [], []
        for g in range(N_GROUPS):
            d_gm = d_gm_all[:, g * HEADS_PER_GROUP * t:g * HEADS_PER_GROUP * t + t]
            for r in range(1, HEADS_PER_GROUP):
                d_gm = d_gm + d_gm_all[:, (g * HEADS_PER_GROUP + r) * t:(g * HEADS_PER_GROUP + r + 1) * t]
            d_gmb = d_gm.astype(bf16)
            dcs.append(d_cs[g] + jnp.dot(d_gmb, bgs[g], preferred_element_type=f32))
            dbs.append(d_bs[g] + _tn(d_gmb, cgs[g]))
        d_dt = d_dt + d_ld * a
        gal_ref[...] += jnp.sum(d_ld * dt, axis=0, keepdims=True) * a
        gdsk_ref[...] += g_dsk
        d_dt_raw = d_dt * _sigmoid(dt_pre)
        ddt_ref[...] = d_dt_raw.astype(bf16)
        gdtb_ref[...] += jnp.sum(d_dt_raw, axis=0, keepdims=True)
        d_act = jnp.concatenate(dxs + dbs + dcs, axis=1)
        d_conv = d_act * (sg * (1.0 + conv * (1.0 - sg)))
        gcb_ref[...] += jnp.sum(d_conv, axis=0, keepdims=True)
        nxt = d_conv_next[...]
        rows = lax.broadcasted_iota(jnp.int32, (t, D_XBC), 0)
        xraw = x_ref[...]
        d_x = d_conv * cw_ref[CONV_K - 1:CONV_K, :]
        gcw_ref[pl.ds(CONV_K - 1, 1), :] += jnp.sum(xraw * d_conv, axis=0, keepdims=True)
        for m in range(1, CONV_K):
            ahead = jnp.where(rows >= t - m, pltpu.roll(nxt, t - m, 0), pltpu.roll(d_conv, t - m, 0))
            d_x = d_x + ahead * cw_ref[CONV_K - 1 - m:CONV_K - m, :]
            gcw_ref[pl.ds(CONV_K - 1 - m, 1), :] += jnp.sum(xraw * ahead, axis=0, keepdims=True)
        d_conv_next[...] = d_conv
        dx_ref[...] = d_x.astype(bf16)

    rev = lambda c: (n_chunks - 1 - c, 0)
    fixed = lambda c: (0, 0)
    return pl.pallas_call(
        body, name="ssd_bwd", grid=(n_chunks,),
        in_specs=[pl.BlockSpec((t, D_SSM), rev), pl.BlockSpec((t, D_XBC), rev), pl.BlockSpec((t, D_XBC), rev),
                  pl.BlockSpec((t, LANES), rev), pl.BlockSpec((1, N_HEADS, HEAD_DIM, N_STATE), lambda c: (n_chunks - 1 - c, 0, 0, 0)),
                  pl.BlockSpec((CONV_K, D_XBC), fixed), pl.BlockSpec((1, LANES), fixed), pl.BlockSpec((1, LANES), fixed),
                  pl.BlockSpec((1, LANES), fixed)],
        out_specs=(pl.BlockSpec((t, D_XBC), rev), pl.BlockSpec((t, LANES), rev), pl.BlockSpec((CONV_K, D_XBC), fixed),
                   pl.BlockSpec((1, D_XBC), fixed), pl.BlockSpec((1, LANES), fixed), pl.BlockSpec((1, LANES), fixed),
                   pl.BlockSpec((1, LANES), fixed)),
        out_shape=(jax.ShapeDtypeStruct((seq, D_XBC), bf16), jax.ShapeDtypeStruct((seq, LANES), bf16),
                   jax.ShapeDtypeStruct((CONV_K, D_XBC), f32), jax.ShapeDtypeStruct((1, D_XBC), f32),
                   jax.ShapeDtypeStruct((1, LANES), f32), jax.ShapeDtypeStruct((1, LANES), f32), jax.ShapeDtypeStruct((1, LANES), f32)),
        scratch_shapes=[pltpu.VMEM((N_HEADS, HEAD_DIM, N_STATE), f32), pltpu.VMEM((t, D_XBC), f32)],
        compiler_params=_params(("arbitrary",)),
    )(dy, conv, xbc, dt_raw, states, conv_w, dtb, alog, dsk)


def _heads_to_cols(ref, width=HEAD_DIM):
    return jnp.concatenate([ref[h][:, :width] for h in range(N_HEADS)], axis=1)


def _silu_and_grad(z):
    sg = _sigmoid(z)
    return z * sg, sg * (1.0 + z * (1.0 - sg))


def _rms(v):
    return lax.rsqrt(jnp.mean(v * v, axis=-1, keepdims=True) + EPS)


def _rms_bwd(d_hat, hat, r):
    return r * (d_hat - hat * jnp.mean(d_hat * hat, axis=-1, keepdims=True))


def _post(x, target, o_tot, y, za, zs, w_out, gate, g_sb, g_ssm, g_f, seq):
    ts = 256

    def body(x_ref, t_ref, o_ref, y_ref, za_ref, zs_ref, w_ref, gate_ref, gsb_ref, gss_ref, gf_ref,
             ycat_t_ref, dmix_ref, dx2_ref, loss_ref, gnf_ref, dgate_ref, do_ref, dza_ref, dzs_ref, dy_ref, ggsb_ref, ggss_ref):
        @pl.when(pl.program_id(0) == 0)
        def _():
            for ref in (loss_ref, gnf_ref, dgate_ref, ggsb_ref, ggss_ref):
                ref[...] = jnp.zeros_like(ref)

        o = _heads_to_cols(o_ref)
        ro = _rms(o)
        sa, dsa = _silu_and_grad(za_ref[...])
        gsb = gsb_ref[...]
        ya = (o * ro * gsb) * sa
        yv = y_ref[...]
        sz, dsz = _silu_and_grad(zs_ref[...])
        u = yv * sz
        ru = _rms(u)
        gss = gss_ref[...]
        ys = u * ru * gss
        yab, ysb = ya.astype(bf16), ys.astype(bf16)
        _store_transposed(ycat_t_ref.at[:D_ATTN], yab)
        _store_transposed(ycat_t_ref.at[D_ATTN:], ysb)
        mixed = (jnp.dot(yab, w_ref[:D_ATTN, :], preferred_element_type=f32)
                 + jnp.dot(ysb, w_ref[D_ATTN:, :], preferred_element_type=f32))
        gate_v = gate_ref[...]
        x2 = x_ref[...] + gate_v * mixed
        r2 = _rms(x2)
        xh = x2 * r2
        gf = gf_ref[...]
        diff = xh * gf - t_ref[...]
        loss_ref[...] += jnp.sum(diff * diff) * (0.5 / D_MODEL)
        d_out = diff * (1.0 / D_MODEL)
        gnf_ref[...] += jnp.sum(d_out * xh, axis=0, keepdims=True)
        dx2 = _rms_bwd(d_out * gf, xh, r2)
        dx2_ref[...] = dx2
        dgate_ref[...] += jnp.sum(dx2 * mixed, axis=0, keepdims=True)
        dm = (dx2 * gate_v).astype(bf16)
        dmix_ref[...] = dm
        d_ya = _nt(dm, w_ref[:D_ATTN, :])
        d_ys = _nt(dm, w_ref[D_ATTN:, :])
        oh = o * ro
        dza_ref[...] = (d_ya * oh * gsb * dsa).astype(bf16)
        ggsb_ref[...] += jnp.sum(d_ya * oh * sa, axis=0, keepdims=True)
        d_o = _rms_bwd(d_ya * gsb * sa, oh, ro)
        for h in range(N_HEADS):
            do_ref[h] = d_o[:, h * HEAD_DIM:(h + 1) * HEAD_DIM].astype(bf16)
        uh = u * ru
        ggss_ref[...] += jnp.sum(d_ys * uh, axis=0, keepdims=True)
        du = _rms_bwd(d_ys * gss, uh, ru)
        dy_ref[...] = du * sz
        dzs_ref[...] = (du * yv * dsz).astype(bf16)

    row = lambda i: (i, 0)
    fixed = lambda i: (0, 0)
    vec = pl.BlockSpec((1, D_MODEL), fixed)
    return pl.pallas_call(
        body, name="post", grid=(seq // ts,),
        in_specs=[pl.BlockSpec((ts, D_MODEL), row), pl.BlockSpec((ts, D_MODEL), row),
                  pl.BlockSpec((N_HEADS, ts, 2 * HEAD_DIM), lambda i: (0, i, 0)), pl.BlockSpec((ts, D_SSM), row),
                  pl.BlockSpec((ts, D_ATTN), row), pl.BlockSpec((ts, D_SSM), row), pl.BlockSpec((D_ATTN + D_SSM, D_MODEL), fixed),
                  vec, vec, vec, vec],
        out_specs=(pl.BlockSpec((D_ATTN + D_SSM, ts), lambda i: (0, i)), pl.BlockSpec((ts, D_MODEL), row), pl.BlockSpec((ts, D_MODEL), row),
                   pl.BlockSpec((1, LANES), fixed), vec, vec,
                   pl.BlockSpec((N_HEADS, ts, HEAD_DIM), lambda i: (0, i, 0)), pl.BlockSpec((ts, D_ATTN), row),
                   pl.BlockSpec((ts, D_SSM), row), pl.BlockSpec((ts, D_SSM), row), vec, vec),
        out_shape=(jax.ShapeDtypeStruct((D_ATTN + D_SSM, seq), bf16), jax.ShapeDtypeStruct((seq, D_MODEL), bf16),
                   jax.ShapeDtypeStruct((seq, D_MODEL), f32), jax.ShapeDtypeStruct((1, LANES), f32),
                   jax.ShapeDtypeStruct((1, D_MODEL), f32), jax.ShapeDtypeStruct((1, D_MODEL), f32),
                   jax.ShapeDtypeStruct((N_HEADS, seq, HEAD_DIM), bf16), jax.ShapeDtypeStruct((seq, D_ATTN), bf16),
                   jax.ShapeDtypeStruct((seq, D_SSM), bf16), jax.ShapeDtypeStruct((seq, D_SSM), f32),
                   jax.ShapeDtypeStruct((1, D_MODEL), f32), jax.ShapeDtypeStruct((1, D_MODEL), f32)),
        compiler_params=_params(("arbitrary",)),
    )(x, target, o_tot, y, za, zs, w_out, gate, g_sb, g_ssm, g_f)


def _qkv_grads_to_cols(dq, dkt, dvt, seq):
    ts = 256
    nb = ts // CHUNK

    def body(dq_ref, dkt_ref, dvt_ref, out_ref):
        out_ref[:, :D_ATTN] = _heads_to_cols(dq_ref).astype(bf16)
        eye = (lax.broadcasted_iota(jnp.int32, (CHUNK, CHUNK), 0) == lax.broadcasted_iota(jnp.int32, (CHUNK, CHUNK), 1)).astype(bf16)
        for p, ref in ((1, dkt_ref), (2, dvt_ref)):
            for b in range(nb):
                cols = [_nt(eye, ref[h, b].astype(bf16)) for h in range(N_HEADS)]
                out_ref[b * CHUNK:(b + 1) * CHUNK, p * D_ATTN:(p + 1) * D_ATTN] = jnp.concatenate(cols, axis=1).astype(bf16)

    blk = pl.BlockSpec((N_HEADS, ts, HEAD_DIM), lambda i: (0, i, 0))
    blk_t = pl.BlockSpec((N_HEADS, nb, HEAD_DIM, CHUNK), lambda i: (0, i, 0, 0))
    return pl.pallas_call(
        body, name="qkv_grads_to_cols", grid=(seq // ts,), in_specs=[blk, blk_t, blk_t],
        out_specs=pl.BlockSpec((ts, 3 * D_ATTN), lambda i: (i, 0)),
        out_shape=jax.ShapeDtypeStruct((seq, 3 * D_ATTN), bf16), compiler_params=_params(("parallel",)),
    )(dq, dkt, dvt)


def _bwd_in(dqkv, dza, dzs, dxbc, ddt, wp, x, dx2, gain, scale, seq):
    ts = 256
    pieces = ((0, 0, 3 * D_ATTN), (1, OFF_ZA, D_ATTN), (2, OFF_ZS, D_SSM), (3, OFF_XBC, D_XBC), (4, OFF_DT, LANES))

    def body(dqkv_ref, dza_ref, dzs_ref, dxbc_ref, ddt_ref, w_ref, x_ref, dx2_ref, g_ref, sc_ref,
             gx_ref, dshift_ref, dscale_ref, ggain_ref):
        @pl.when(pl.program_id(0) == 0)
        def _():
            dshift_ref[...] = jnp.zeros_like(dshift_ref)
            dscale_ref[...] = jnp.zeros_like(dscale_ref)
            ggain_ref[...] = jnp.zeros_like(ggain_ref)

        refs = (dqkv_ref, dza_ref, dzs_ref, dxbc_ref, ddt_ref)
        dh = jnp.zeros((ts, D_MODEL), f32)
        for idx, off, width in pieces:
            for cc in range(0, width, 512):
                wd = min(512, width - cc)
                dh = dh + _nt(refs[idx][:, cc:cc + wd], w_ref[:, off + cc:off + cc + wd])
        xv = x_ref[...]
        r = _rms(xv)
        xh = xv * r
        g = g_ref[...]
        dshift_ref[...] += jnp.sum(dh, axis=0, keepdims=True)
        dscale_ref[...] += jnp.sum(dh * xh * g, axis=0, keepdims=True)
        tt = dh * (1.0 + sc_ref[...])
        ggain_ref[...] += jnp.sum(tt * xh, axis=0, keepdims=True)
        gx_ref[...] = dx2_ref[...] + _rms_bwd(tt * g, xh, r)

    row = lambda i: (i, 0)
    fixed = lambda i: (0, 0)
    vec = pl.BlockSpec((1, D_MODEL), fixed)
    return pl.pallas_call(
        body, name="bwd_in", grid=(seq // ts,),
        in_specs=[pl.BlockSpec((ts, 3 * D_ATTN), row), pl.BlockSpec((ts, D_ATTN), row), pl.BlockSpec((ts, D_SSM), row),
                  pl.BlockSpec((ts, D_XBC), row), pl.BlockSpec((ts, LANES), row), pl.BlockSpec((D_MODEL, D_PROJ_P), fixed),
                  pl.BlockSpec((ts, D_MODEL), row), pl.BlockSpec((ts, D_MODEL), row), vec, vec],
        out_specs=(pl.BlockSpec((ts, D_MODEL), row), vec, vec, vec),
        out_shape=(jax.ShapeDtypeStruct((seq, D_MODEL), f32), jax.ShapeDtypeStruct((1, D_MODEL), f32),
                   jax.ShapeDtypeStruct((1, D_MODEL), f32), jax.ShapeDtypeStruct((1, D_MODEL), f32)),
        compiler_params=_params(("arbitrary",)),
    )(dqkv, dza, dzs, dxbc, ddt, wp, x, dx2, gain, scale)


def _grad_w(a_t, b, tn, name):
    m, seq = a_t.shape
    n = b.shape[1]
    tk = min(1024, seq)
    n_k = seq // tk

    def body(a_ref, b_ref, o_ref, acc):
        @pl.when(pl.program_id(1) == 0)
        def _():
            acc[...] = jnp.zeros_like(acc)

        acc[...] += jnp.dot(a_ref[...], b_ref[...], preferred_element_type=f32)

        @pl.when(pl.program_id(1) == n_k - 1)
        def _():
            o_ref[...] = acc[...].astype(bf16)

    return pl.pallas_call(
        body, name=name, grid=(n // tn, n_k),
        in_specs=[pl.BlockSpec((m, tk), lambda j, k: (0, k)), pl.BlockSpec((tk, tn), lambda j, k: (k, j))],
        out_specs=pl.BlockSpec((m, tn), lambda j, k: (0, j)),
        out_shape=jax.ShapeDtypeStruct((m, n), bf16), scratch_shapes=[pltpu.VMEM((m, tn), f32)],
        compiler_params=_params(("parallel", "arbitrary")),
    )(a_t, b)


def _small_finish(g_all, c_all, dmod_mine):
    def body(g_ref, c_ref, dm_ref, tot_ref, gwada_ref):
        tot = g_ref[0:1, :]
        for j in range(1, N_DEV):
            tot = tot + g_ref[j:j + 1, :]
        tot_ref[...] = tot
        cv = c_ref[...]
        gwada_ref[...] = _tn(cv * _sigmoid(cv), dm_ref[...], HI)

    vmem = pl.BlockSpec(memory_space=pltpu.VMEM)
    return pl.pallas_call(
        body, name="small_finish", in_specs=[vmem, vmem, vmem], out_specs=(vmem, vmem),
        out_shape=(jax.ShapeDtypeStruct((1, N_PACK), f32), jax.ShapeDtypeStruct((D_MODEL, dmod_mine.shape[1]), f32)),
        compiler_params=_params(),
    )(g_all, c_all, dmod_mine)


def _adamw(w, g_parts, m, v, rows, name):
    r, c = w.shape
    n_parts = g_parts.shape[0]
    bc1 = 1.0 - ADAM_B1 ** ADAM_STEP
    bc2 = 1.0 - ADAM_B2 ** ADAM_STEP

    def body(w_ref, g_ref, m_ref, v_ref, go_ref, d_ref, mo_ref, vo_ref):
        g = g_ref[0].astype(f32)
        for j in range(1, n_parts):
            g = g + g_ref[j].astype(f32)
        go_ref[...] = g
        mn = ADAM_B1 * m_ref[...] + (1.0 - ADAM_B1) * g
        vn = ADAM_B2 * v_ref[...] + (1.0 - ADAM_B2) * (g * g)
        mo_ref[...] = mn
        vo_ref[...] = vn
        d_ref[...] = -ADAM_LR * ((mn / bc1) / (jnp.sqrt(vn / bc2) + ADAM_EPS) + ADAM_WD * w_ref[...])

    blk = pl.BlockSpec((rows, c), lambda i: (i, 0))
    return pl.pallas_call(
        body, name=name, grid=(r // rows,),
        in_specs=[blk, pl.BlockSpec((n_parts, rows, c), lambda i: (0, i, 0)), blk, blk],
        out_specs=(blk, blk, blk, blk), out_shape=(jax.ShapeDtypeStruct((r, c), f32),) * 4,
        compiler_params=_params(("parallel",)),
    )(w, g_parts, m, v)


def _pad_lanes(v):
    return jnp.pad(v, ((0, 0), (0, LANES - v.shape[1])))


def kernel(x, c, w_ada, b_ada, norm_in_gain, w_in, conv_w, conv_b, dt_bias, a_log, d_skip, sb_norm_gain, ssm_norm_gain, w_out, norm_f_gain, loss_target, m_w_ada, m_b_ada, m_norm_in_gain, m_w_in, m_conv_w, m_conv_b, m_dt_bias, m_a_log, m_d_skip, m_sb_norm_gain, m_ssm_norm_gain, m_w_out, m_norm_f_gain, v_w_ada, v_b_ada, v_norm_in_gain, v_w_in, v_conv_w, v_conv_b, v_dt_bias, v_a_log, v_d_skip, v_sb_norm_gain, v_ssm_norm_gain, v_w_out, v_norm_f_gain):
    seq = x.shape[1]
    xs = x[0]
    tgt = loss_target[0]
    _, my_slot = _me()

    mod, c_all = _mod_exchange(c, w_ada[0], b_ada)
    shift, scale, gate = mod[:, :D_MODEL], mod[:, D_MODEL:2 * D_MODEL], mod[:, 2 * D_MODEL:]
    w_in_g, w_out_g, conv_w_g = _all_gather_two_level(
        [_cast_bf16(w_in[0], 128), _cast_bf16(w_out[0], 128), conv_w[0]], "gather_weights")
    w_full = jnp.transpose(w_in_g, (1, 0, 2)).reshape(D_MODEL, D_PROJ)
    wp = jnp.concatenate([w_full[:, :4 * D_ATTN], w_full[:, D_PROJ - D_SSM:], w_full[:, 4 * D_ATTN:4 * D_ATTN + D_XBC],
                          _pad_lanes(w_full[:, 4 * D_ATTN + D_XBC:4 * D_ATTN + D_XBC + N_HEADS])], axis=1)
    w_out_full = w_out_g.reshape(D_ATTN + D_SSM, D_MODEL)
    conv_w_full = jnp.transpose(conv_w_g, (1, 0, 2)).reshape(CONV_K, D_XBC)
    dtb, alog, dsk = _pad_lanes(dt_bias), _pad_lanes(a_log), _pad_lanes(d_skip)

    h_t, qkv, za, zs, xbc, dt_raw = _proj(xs, norm_in_gain, scale, shift, wp, seq)
    o_tot = _attn_fwd(qkv, seq)
    conv, y, states = _ssd_fwd(xbc, dt_raw, conv_w_full, conv_b, dtb, alog, dsk, seq)
    ycat_t, dmix, dx2, loss_p, g_nf, d_gate, d_o, dza, dzs, dy, g_sb, g_ss = _post(
        xs, tgt, o_tot, y, za, zs, w_out_full, gate, sb_norm_gain, ssm_norm_gain, norm_f_gain.reshape(1, D_MODEL), seq)

    dq, dk, dv = _attn_bwd(qkv, o_tot, d_o, seq)
    dxbc, ddt, g_cw, g_cb, g_dtb, g_al, g_dsk = _ssd_bwd(dy, conv, xbc, dt_raw, states, conv_w_full, dtb, alog, dsk, seq)
    dqkv = _qkv_grads_to_cols(dq, dk, dv, seq)
    grad_x, d_shift, d_scale, g_in = _bwd_in(dqkv, dza, dzs, dxbc, ddt, wp, xs, dx2, norm_in_gain, scale, seq)
    gw_qkv = _grad_w(h_t, dqkv, D_XBC, "grad_w_qkv")
    gw_za = _grad_w(h_t, dza, D_ATTN, "grad_w_za")
    gw_zs = _grad_w(h_t, dzs, D_SSM, "grad_w_zs")
    gw_xbc = _grad_w(h_t, dxbc, D_XBC, "grad_w_xbc")
    gw_dt = _grad_w(h_t, ddt, LANES, "grad_w_dt")
    gw_out = _grad_w(ycat_t, dmix, D_MODEL, "grad_w_out")
    gw_in = jnp.concatenate([gw_qkv, gw_za, gw_xbc, gw_dt[:, :N_HEADS], gw_zs], axis=1)

    gw_in_parts, gw_out_parts = _reduce_scatter_two_level(
        [jnp.transpose(gw_in.reshape(D_MODEL, N_DEV, W_IN_SHARD), (1, 0, 2)),
         gw_out.reshape(N_DEV, (D_ATTN + D_SSM) // N_DEV, D_MODEL)], "scatter_grads")
    packed = jnp.concatenate([loss_p, d_shift, d_scale, d_gate, g_in, g_cb, g_dtb, g_al, g_dsk, g_sb, g_ss, g_nf,
                              g_cw.reshape(1, CONV_K * D_XBC)], axis=1)
    (packed_all,) = _all_gather([packed], "gather_small")
    packed_all = packed_all.reshape(N_DEV, N_PACK)
    n_ada = w_ada.shape[2]
    dmod_mine = lax.dynamic_slice(packed_all, (0, P_DMOD + my_slot * n_ada), (N_DEV, n_ada))
    tot, g_w_ada = _small_finish(packed_all, c_all.reshape(N_DEV, D_MODEL), dmod_mine)

    def big(w, parts, m, v, rows, name):
        return tuple(t[None] for t in _adamw(w[0], parts, m[0], v[0], rows, name))

    small_names = ["b_ada", "norm_in_gain", "conv_b", "dt_bias", "a_log", "d_skip", "sb_norm_gain", "ssm_norm_gain", "norm_f_gain"]
    given = {"b_ada": (b_ada, m_b_ada, v_b_ada), "norm_in_gain": (norm_in_gain, m_norm_in_gain, v_norm_in_gain),
             "conv_b": (conv_b, m_conv_b, v_conv_b), "dt_bias": (dt_bias, m_dt_bias, v_dt_bias), "a_log": (a_log, m_a_log, v_a_log),
             "d_skip": (d_skip, m_d_skip, v_d_skip), "sb_norm_gain": (sb_norm_gain, m_sb_norm_gain, v_sb_norm_gain),
             "ssm_norm_gain": (ssm_norm_gain, m_ssm_norm_gain, v_ssm_norm_gain), "norm_f_gain": (norm_f_gain, m_norm_f_gain, v_norm_f_gain)}

    def pack(which):
        cols = []
        for nm in small_names:
            t = given[nm][which].reshape(1, -1)
            cols.append(_pad_lanes(t) if t.shape[1] < LANES else t)
        return jnp.concatenate(cols, axis=1)

    packed_out = _adamw(pack(0), tot[:, P_DMOD:P_CW][None], pack(1), pack(2), 1, "adamw_small")
    res = {}
    off = 0
    for nm in small_names:
        shape = given[nm][0].shape
        size = given[nm][0].size
        res[nm] = tuple(t[:, off:off + size].reshape(shape) for t in packed_out)
        off += max(size, LANES)
    n_cw = conv_w.shape[2]
    g_cw_mine = lax.dynamic_slice(tot[:, P_CW:].reshape(CONV_K, D_XBC), (0, my_slot * n_cw), (CONV_K, n_cw))
    res["conv_w"] = tuple(t.reshape(conv_w.shape) for t in _adamw(conv_w.reshape(1, -1), g_cw_mine.reshape(1, 1, -1),
                                                                  m_conv_w.reshape(1, -1), v_conv_w.reshape(1, -1), 1, "adamw_conv_w"))
    res["w_ada"] = big(w_ada, g_w_ada[None], m_w_ada, v_w_ada, 128, "adamw_w_ada")
    res["w_in"] = big(w_in, gw_in_parts, m_w_in, v_w_in, 128, "adamw_w_in")
    res["w_out"] = big(w_out, gw_out_parts, m_w_out, v_w_out, 64, "adamw_w_out")
    names = ["w_ada", "b_ada", "norm_in_gain", "w_in", "conv_w", "conv_b", "dt_bias", "a_log", "d_skip", "sb_norm_gain",
             "ssm_norm_gain", "w_out", "norm_f_gain"]
    loss = tot[0, P_LOSS]
    return (loss, grad_x[None], *[res[n][0] for n in names], *[res[n][1] for n in names],
            *[res[n][2] for n in names], *[res[n][3] for n in names])
```
